```python
import jax
import jax.numpy as jnp
from jax import lax
import numpy as np

D_MODEL = 1024
BATCH = 16
SEQ = 256
DEPTH = 2
DEC_BATCH = 2
DEC_SEQ = 1024
PAST_LEN = 256

GRID_W = 64
N_Q_HEADS = 8
N_KV_HEADS = 2
HEAD_DIM = 64
Q_PER_KV = N_Q_HEADS // N_KV_HEADS
ATT_W = N_Q_HEADS * HEAD_DIM
KV_W = N_KV_HEADS * HEAD_DIM
ROPE_THETA = 10000.0
Q_BLOCK = 128
N_FOURIER_GROUPS = 4
FOURIER_GROUP_W = 64
FOURIER_W = N_FOURIER_GROUPS * FOURIER_GROUP_W
N_CHUNK_GROUPS = 4
CHUNK_GROUP_W = 64
CHUNK_W = N_CHUNK_GROUPS * CHUNK_GROUP_W
CHUNK = 128
N_BRANCHES = 3
IN_W = ATT_W + 2 * KV_W + FOURIER_W + 2 * CHUNK_W + N_BRANCHES * D_MODEL
N_EXPERTS = 16
N_EXPERT_GROUPS = 4
EXPERTS_PER_GROUP = N_EXPERTS // N_EXPERT_GROUPS
TOP_K = 2
D_EXPERT = 512
ALPHA = (2 * DEPTH) ** 0.25
BETA = (8 * DEPTH) ** -0.25
LN_EPS = 1e-6
RMS_EPS = 1e-6

kernel_name = 'hybrid_diffusion_prefix_trunk_step'


def layer_norm(x, g=None, b=None):
    xf = x.astype(jnp.float32)
    mu = jnp.mean(xf, axis=-1, keepdims=True)
    var = jnp.mean(jnp.square(xf - mu), axis=-1, keepdims=True)
    y = (xf - mu) * lax.rsqrt(var + LN_EPS)
    if g is not None:
        y = y * g.astype(jnp.float32) + b.astype(jnp.float32)
    return y.astype(x.dtype)


def rms_norm(x, g):
    xf = x.astype(jnp.float32)
    y = xf * lax.rsqrt(jnp.mean(jnp.square(xf), axis=-1, keepdims=True) + RMS_EPS)
    return (y * g.astype(jnp.float32)).astype(x.dtype)


def _rotate(xa, pos):
    quarter = xa.shape[-1] // 2
    inv = ROPE_THETA ** (-jnp.arange(quarter, dtype=jnp.float32) / quarter)
    ang = pos.astype(jnp.float32)[:, None] * inv[None, :]
    cos = jnp.cos(ang)[None, :, None, :]
    sin = jnp.sin(ang)[None, :, None, :]
    x1 = xa[..., :quarter].astype(jnp.float32)
    x2 = xa[..., quarter:].astype(jnp.float32)
    return jnp.concatenate([x1 * cos - x2 * sin, x1 * sin + x2 * cos], axis=-1)


def axial_rope(x):
    rows = x.shape[1] // GRID_W
    row = jnp.repeat(jnp.arange(rows, dtype=jnp.int32), GRID_W)
    col = jnp.tile(jnp.arange(GRID_W, dtype=jnp.int32), rows)
    half = HEAD_DIM // 2
    out = jnp.concatenate([_rotate(x[..., :half], row), _rotate(x[..., half:], col)], axis=-1)
    return out.astype(x.dtype)


def blocked_attention(q, k, v):
    b, sq = q.shape[0], q.shape[1]
    nb = sq // Q_BLOCK
    qb = q.reshape(b, nb, Q_BLOCK, N_KV_HEADS, Q_PER_KV, HEAD_DIM).transpose(1, 0, 2, 3, 4, 5)
    scale = HEAD_DIM ** -0.5

    def one_block(qblk):
        s = jnp.einsum('bqkgd,bskd->bkgqs', qblk, k, preferred_element_type=jnp.float32) * scale
        p = jax.nn.softmax(s, axis=-1).astype(v.dtype)
        return jnp.einsum('bkgqs,bskd->bqkgd', p, v)

    o = lax.map(one_block, qb)
    return o.transpose(1, 0, 2, 3, 4, 5).reshape(b, sq, ATT_W)


def fourier_mix(f):
    b, s, _ = f.shape
    fg = f.reshape(b, s, N_FOURIER_GROUPS, FOURIER_GROUP_W).astype(jnp.float32)
    out = jnp.real(jnp.fft.fft2(fg, axes=(1, 3), norm='ortho'))
    return out.reshape(b, s, FOURIER_W).astype(f.dtype)


def chunk_spatial_gate(u, v, ws, bs, g, bb):
    b, s, _ = u.shape
    nc = s // CHUNK
    vn = layer_norm(v, g, bb).reshape(b, nc, CHUNK, N_CHUNK_GROUPS, CHUNK_GROUP_W)
    sv = jnp.einsum('hpq,bcqhd->bcphd', ws, vn) + bs.T[None, None, :, :, None]
    return u * sv.reshape(b, s, CHUNK_W)


def token_mixer(h, ctx_k, ctx_v, latent, w_in, q_g, k_g, w_pa, w_pf, w_pc, w_o, cln_g, cln_b, cws, cbs):
    b, s, _ = h.shape
    o1 = ATT_W
    o2 = o1 + KV_W
    o3 = o2 + KV_W
    o4 = o3 + FOURIER_W
    o5 = o4 + CHUNK_W
    o6 = o5 + CHUNK_W
    proj = h @ w_in
    q, k, v, f, cu, cv, gl = jnp.split(proj, [o1, o2, o3, o4, o5, o6], axis=-1)
    q = rms_norm(q.reshape(b, s, N_Q_HEADS, HEAD_DIM), q_g)
    k = rms_norm(k.reshape(b, s, N_KV_HEADS, HEAD_DIM), k_g)
    v = v.reshape(b, s, N_KV_HEADS, HEAD_DIM)
    if latent:
        q = axial_rope(q)
        k = axial_rope(k)
        k_all = jnp.concatenate([ctx_k.astype(k.dtype), k], axis=1)
        v_all = jnp.concatenate([ctx_v.astype(v.dtype), v], axis=1)
    else:
        k_all, v_all = k, v
    att = blocked_attention(q, k_all, v_all)
    four = fourier_mix(f)
    chk = chunk_spatial_gate(jax.nn.gelu(cu), jax.nn.gelu(cv), cws, cbs, cln_g, cln_b)
    gates = jax.nn.sigmoid(gl).reshape(b, s, N_BRANCHES, D_MODEL)
    merged = (gates[:, :, 0] * (att @ w_pa) + gates[:, :, 1] * (four @ w_pf)
              + gates[:, :, 2] * (chk @ w_pc))
    return merged @ w_o, k, v


def grouped_moe(h, w_router, router_bias, w_ge, w_ue, w_de):
    logits = jnp.einsum('bsd,de->bse', h, w_router, preferred_element_type=jnp.float32)
    probs = jax.nn.softmax(logits, axis=-1)
    sel = probs + router_bias.astype(jnp.float32)
    b, s, _ = sel.shape
    grp = sel.reshape(b, s, N_EXPERT_GROUPS, EXPERTS_PER_GROUP)
    grp_score = jnp.sum(lax.top_k(grp, TOP_K)[0], axis=-1)
    best = jnp.argmax(grp_score, axis=-1)
    expert_group = jnp.arange(N_EXPERTS, dtype=jnp.int32) // EXPERTS_PER_GROUP
    in_group = expert_group[None, None, :] == best[..., None]
    _, eidx = lax.top_k(jnp.where(in_group, sel, -jnp.inf), TOP_K)
    w = jnp.take_along_axis(probs, eidx, axis=-1)
    w = w / jnp.sum(w, axis=-1, keepdims=True)
    combine = jnp.sum(jax.nn.one_hot(eidx, N_EXPERTS, dtype=jnp.float32) * w[..., None], axis=-2).astype(h.dtype)
    a = jnp.einsum('bsd,edf->bsef', h, w_ge)
    u = jnp.einsum('bsd,edf->bsef', h, w_ue)
    return jnp.einsum('bsef,efd->bsd', jax.nn.silu(a) * u * combine[..., None], w_de)


def trunk_layer(x, cond, ctx_k, ctx_v, latent, lp, w_router, router_bias):
    (w_in, q_g, k_g, w_pa, w_pf, w_pc, w_o, cln_g, cln_b, cws, cbs, w_ada, b_ada,
     ln1_g, ln1_b, ln2_g, ln2_b, w_ge, w_ue, w_de) = lp
    mod = (jax.nn.silu(cond) @ w_ada + b_ada)[:, None, :]
    sh1, sc1, g1, sh2, sc2, g2 = jnp.split(mod, 6, axis=-1)
    h = layer_norm(x) * (1 + sc1) + sh1
    mix, k, v = token_mixer(h, ctx_k, ctx_v, latent, w_in, q_g, k_g, w_pa, w_pf, w_pc, w_o,
                            cln_g, cln_b, cws, cbs)
    x = layer_norm(ALPHA * x + g1 * mix, ln1_g, ln1_b)
    h = layer_norm(x) * (1 + sc2) + sh2
    x = layer_norm(ALPHA * x + g2 * grouped_moe(h, w_router, router_bias, w_ge, w_ue, w_de), ln2_g, ln2_b)
    return x, k, v


def setup_inputs(seed: int = 0) -> dict:
    key = jax.random.key(seed)
    ks = jax.random.split(key, 32)
    D = D_MODEL

    def n(k, shape, s):
        return jax.random.normal(k, shape, jnp.float32) * s

    return {
        'x_prompt': n(ks[0], (BATCH, SEQ, D), 1.0),
        'x_sample': n(ks[1], (DEC_BATCH, DEC_SEQ, D), 1.0),
        'cache_k': n(ks[2], (DEC_BATCH, DEPTH, PAST_LEN, N_KV_HEADS, HEAD_DIM), 1.0),
        'cache_v': n(ks[3], (DEC_BATCH, DEPTH, PAST_LEN, N_KV_HEADS, HEAD_DIM), 1.0),
        'c': n(ks[4], (DEC_BATCH, D), 1.0),
        'c_ctx': n(ks[5], (D,), 1.0),
        'w_in': n(ks[6], (DEPTH, D, IN_W), D ** -0.5),
        'q_norm_g': 1.0 + n(ks[7], (DEPTH, HEAD_DIM), 0.02),
        'k_norm_g': 1.0 + n(ks[8], (DEPTH, HEAD_DIM), 0.02),
        'w_proj_att': n(ks[9], (DEPTH, ATT_W, D), ATT_W ** -0.5),
        'w_proj_fourier': n(ks[10], (DEPTH, FOURIER_W, D), FOURIER_W ** -0.5),
        'w_proj_chunk': n(ks[11], (DEPTH, CHUNK_W, D), CHUNK_W ** -0.5),
        'w_out': n(ks[12], (DEPTH, D, D), D ** -0.5 * BETA),
        'chunk_ln_g': 1.0 + n(ks[13], (DEPTH, CHUNK_W), 0.02),
        'chunk_ln_b': n(ks[14], (DEPTH, CHUNK_W), 0.02),
        'chunk_ws': n(ks[15], (DEPTH, N_CHUNK_GROUPS, CHUNK, CHUNK), CHUNK ** -0.5),
        'chunk_bs': 1.0 + n(ks[16], (DEPTH, N_CHUNK_GROUPS, CHUNK), 0.02),
        'w_ada': n(ks[17], (DEPTH, D, 6 * D), 0.5 * D ** -0.5),
        'b_ada': n(ks[18], (DEPTH, 6 * D), 0.02),
        'ln1_g': 1.0 + n(ks[19], (DEPTH, D), 0.02),
        'ln1_b': n(ks[20], (DEPTH, D), 0.02),
        'ln2_g': 1.0 + n(ks[21], (DEPTH, D), 0.02),
        'ln2_b': n(ks[22], (DEPTH, D), 0.02),
        'w_router': n(ks[23], (D, N_EXPERTS), D ** -0.5),
        'router_bias': n(ks[24], (N_EXPERTS,), 0.01),
        'w_gate_e': n(ks[25], (DEPTH, N_EXPERTS, D, D_EXPERT), D ** -0.5),
        'w_up_e': n(ks[26], (DEPTH, N_EXPERTS, D, D_EXPERT), D ** -0.5),
        'w_down_e': n(ks[27], (DEPTH, N_EXPERTS, D_EXPERT, D), D_EXPERT ** -0.5 * BETA),
    }


def reference(x_prompt, x_sample, cache_k, cache_v, c, c_ctx, w_in, q_norm_g, k_norm_g,
              w_proj_att, w_proj_fourier, w_proj_chunk, w_out, chunk_ln_g, chunk_ln_b,
              chunk_ws, chunk_bs, w_ada, b_ada, ln1_g, ln1_b, ln2_g, ln2_b,
              w_router, router_bias, w_gate_e, w_up_e, w_down_e):
    stacked = (w_in, q_norm_g, k_norm_g, w_proj_att, w_proj_fourier, w_proj_chunk, w_out,
               chunk_ln_g, chunk_ln_b, chunk_ws, chunk_bs, w_ada, b_ada,
               ln1_g, ln1_b, ln2_g, ln2_b, w_gate_e, w_up_e, w_down_e)
    xp = x_prompt
    xs = x_sample
    new_k = []
    new_v = []
    for l in range(DEPTH):
        lp = [a[l] for a in stacked]
        xp, k_ctx, v_ctx = trunk_layer(xp, c_ctx[None, :], None, None, False, lp, w_router, router_bias)
        new_k.append(k_ctx)
        new_v.append(v_ctx)
        xs, _, _ = trunk_layer(xs, c, cache_k[:, l], cache_v[:, l], True, lp, w_router, router_bias)
    y_prompt = xp
    y_sample = xs
    new_cache_k = jnp.stack(new_k, axis=1)
    new_cache_v = jnp.stack(new_v, axis=1)
    return (y_prompt, y_sample, new_cache_k, new_cache_v)
```

```python
import functools
import math

import jax
import jax.numpy as jnp
import numpy as np
from jax import lax
from jax.experimental import pallas as pl
from jax.experimental.pallas import tpu as pltpu

F32 = jnp.float32
BF16 = jnp.bfloat16

D_MODEL = 1024
BATCH = 16
SEQ = 256
DEPTH = 2
DEC_BATCH = 2
DEC_SEQ = 1024
PAST_LEN = 256
GRID_W = 64
N_Q_HEADS = 8
N_KV_HEADS = 2
HEAD_DIM = 64
Q_PER_KV = N_Q_HEADS // N_KV_HEADS
ATT_W = N_Q_HEADS * HEAD_DIM
KV_W = N_KV_HEADS * HEAD_DIM
ROPE_THETA = 10000.0
FOURIER_GROUP_W = 64
FOURIER_W = 256
N_CHUNK_GROUPS = 4
CHUNK_GROUP_W = 64
CHUNK_W = 256
CHUNK = 128
N_EXPERTS = 16
N_EXPERT_GROUPS = 4
EXPERTS_PER_GROUP = 4
D_EXPERT = 512
ALPHA = (2 * DEPTH) ** 0.25
LN_EPS = 1e-6
RMS_EPS = 1e-6

T_CTX = BATCH * SEQ
T_LAT = DEC_BATCH * DEC_SEQ
T_ALL = T_CTX + T_LAT
MIX_W = ATT_W + 2 * KV_W + FOURIER_W + 2 * CHUNK_W
GATE_W = 3 * D_MODEL
N_MOD = 6
N_COND = 8
ROUTER_PAD = 128
TM = 512
TQ = 256
TM_MOE = 1024
V7X_VMEM_LIMIT = 56 * 1024 * 1024


def _cparams(*sem):
    return pltpu.CompilerParams(dimension_semantics=sem, vmem_limit_bytes=V7X_VMEM_LIMIT)


def _cond_row(i, tm):
    n_ctx = T_CTX // tm
    return jnp.where(i < n_ctx, 0, 1 + (i - n_ctx) // (DEC_SEQ // tm))


def _ln(x):
    mu = jnp.mean(x, axis=-1, keepdims=True)
    xc = x - mu
    var = jnp.mean(xc * xc, axis=-1, keepdims=True)
    return xc * lax.rsqrt(var + LN_EPS)


def _split_bf16(x):
    hi = x.astype(BF16)
    lo = (x - hi.astype(F32)).astype(BF16)
    return hi, lo


def _dot(a, b):
    return jnp.dot(a, b, preferred_element_type=F32)


def _sigmoid(x):
    return 1.0 / (1.0 + jnp.exp(-x))


def _gelu_tanh(x):
    c = np.float32(np.sqrt(2 / np.pi))
    return x * (0.5 * (1.0 + jnp.tanh(c * (x + 0.044715 * (x * x * x)))))


def _mod_kernel(c_ref, w_ref, b_ref, o_ref):
    c = c_ref[...]
    s = c * _sigmoid(c)
    o_ref[0] = _dot(s.astype(BF16), w_ref[0].astype(BF16)) + b_ref[0]


def _mod_call(cond, w_ada, b_ada):
    tn = 1536
    n = N_MOD * D_MODEL
    return pl.pallas_call(
        _mod_kernel,
        out_shape=jax.ShapeDtypeStruct((DEPTH, N_COND, n), F32),
        grid=(DEPTH, n // tn),
        in_specs=[
            pl.BlockSpec((N_COND, D_MODEL), lambda l, j: (0, 0)),
            pl.BlockSpec((1, D_MODEL, tn), lambda l, j: (l, 0, j)),
            pl.BlockSpec((1, 1, tn), lambda l, j: (l, 0, j)),
        ],
        out_specs=pl.BlockSpec((1, N_COND, tn), lambda l, j: (l, 0, j)),
        compiler_params=_cparams("arbitrary", "arbitrary"),
        name="adaln_mod",
    )(cond, w_ada, b_ada.reshape(DEPTH, 1, n))


def _head_rms(q, bd, g):
    hi, lo = _split_bf16(q * q)
    ssum = _dot(hi, bd) + _dot(lo, bd)
    return q * lax.rsqrt(ssum * (1.0 / HEAD_DIM) + RMS_EPS) * g


def _rope(x, cos, sin_signed):
    w = x.shape[-1]
    lane = lax.broadcasted_iota(jnp.int32, x.shape, 1)
    swapped = jnp.where((lane % 32) < 16, pltpu.roll(x, w - 16, 1), pltpu.roll(x, 16, 1))
    return x * cos + swapped * sin_signed


def _proj_kernel(x_ref, mod_ref, w_ref, qg_ref, kg_ref, cos_ref, sin_ref, bd_ref, o_ref):
    i = pl.program_id(0)
    m = mod_ref[0]
    h = _ln(x_ref[...]) * (1.0 + m[1:2]) + m[0:1]
    proj = _dot(h.astype(BF16), w_ref[...])
    o_ref[:, ATT_W + KV_W:] = proj[:, ATT_W + KV_W:]
    bd = bd_ref[...]
    qn = _head_rms(proj[:, :ATT_W], bd, qg_ref[...])
    kn = _head_rms(proj[:, ATT_W:ATT_W + KV_W], bd[:KV_W, :KV_W], kg_ref[...])

    @pl.when(i < T_CTX // TM)
    def _():
        o_ref[:, :ATT_W] = qn
        o_ref[:, ATT_W:ATT_W + KV_W] = kn

    @pl.when(i >= T_CTX // TM)
    def _():
        cos = cos_ref[...]
        sin = sin_ref[...]
        o_ref[:, :ATT_W] = _rope(qn, cos, sin)
        o_ref[:, ATT_W:ATT_W + KV_W] = _rope(kn, cos[:, :KV_W], sin[:, :KV_W])


def _proj_call(x, modl, w_mix, qg, kg, cos_t, sin_t, bd):
    n_ctx = T_CTX // TM
    per_seq = DEC_SEQ // TM

    def rope_idx(i):
        return (jnp.where(i < n_ctx, 0, (i - n_ctx) % per_seq), 0)

    return pl.pallas_call(
        _proj_kernel,
        out_shape=jax.ShapeDtypeStruct((T_ALL, MIX_W), F32),
        grid=(T_ALL // TM,),
        in_specs=[
            pl.BlockSpec((TM, D_MODEL), lambda i: (i, 0)),
            pl.BlockSpec((1, N_MOD, D_MODEL), lambda i: (_cond_row(i, TM), 0, 0)),
            pl.BlockSpec((D_MODEL, MIX_W), lambda i: (0, 0)),
            pl.BlockSpec((1, ATT_W), lambda i: (0, 0)),
            pl.BlockSpec((1, KV_W), lambda i: (0, 0)),
            pl.BlockSpec((TM, ATT_W), rope_idx),
            pl.BlockSpec((TM, ATT_W), rope_idx),
            pl.BlockSpec((ATT_W, ATT_W), lambda i: (0, 0)),
        ],
        out_specs=pl.BlockSpec((TM, MIX_W), lambda i: (i, 0)),
        compiler_params=_cparams("arbitrary"),
        name="in_proj",
    )(x, modl, w_mix, qg, kg, cos_t, sin_t, bd)


def _attention_tile(q, k_parts, v_parts):
    scale = HEAD_DIM ** -0.5
    outs = []
    for g in range(N_KV_HEADS):
        ks = [k[:, g * HEAD_DIM:(g + 1) * HEAD_DIM] for k in k_parts]
        vs = [v[:, g * HEAD_DIM:(g + 1) * HEAD_DIM] for v in v_parts]
        for hh in range(Q_PER_KV):
            h = g * Q_PER_KV + hh
            qh = q[:, h * HEAD_DIM:(h + 1) * HEAD_DIM]
            ss = [lax.dot_general(qh, k, (((1,), (1,)), ((), ())), preferred_element_type=F32) * scale
                  for k in ks]
            m = functools.reduce(jnp.maximum, [jnp.max(s, axis=-1, keepdims=True) for s in ss])
            es = [jnp.exp(s - m) for s in ss]
            denom = functools.reduce(jnp.add, [jnp.sum(e, axis=-1, keepdims=True) for e in es])
            inv = 1.0 / denom
            o = functools.reduce(jnp.add, [_dot((e * inv).astype(BF16), v) for e, v in zip(es, vs)])
            outs.append(o)
    return jnp.concatenate(outs, axis=-1)


def _mixer_kernel(*refs, seq, latent):
    if latent:
        (mix_ref, ck_ref, cv_ref, dft_ref, bdc_ref, bds_ref, ws_ref, bsm_ref, clg_ref, clb_ref, o_ref) = refs
    else:
        (mix_ref, dft_ref, bdc_ref, bds_ref, ws_ref, bsm_ref, clg_ref, clb_ref, o_ref) = refs

    k_new = mix_ref[:, ATT_W:ATT_W + KV_W].astype(BF16)
    v_new = mix_ref[:, ATT_W + KV_W:ATT_W + 2 * KV_W].astype(BF16)
    if latent:
        k_parts = [ck_ref[...].astype(BF16), k_new]
        v_parts = [cv_ref[...].astype(BF16), v_new]
    else:
        k_parts, v_parts = [k_new], [v_new]

    def q_tile(t, carry):
        r0 = pl.multiple_of(t * TQ, TQ)
        q = mix_ref[pl.ds(r0, TQ), :ATT_W].astype(BF16)
        o_ref[pl.ds(r0, TQ), :ATT_W] = _attention_tile(q, k_parts, v_parts).astype(BF16)
        return carry

    lax.fori_loop(0, seq // TQ, q_tile, 0)

    f_hi, f_lo = _split_bf16(mix_ref[:, ATT_W + 2 * KV_W:ATT_W + 2 * KV_W + FOURIER_W])
    bdc = bdc_ref[...]
    bds = bds_ref[...]
    y = jnp.concatenate([_dot(f_hi, bdc) + _dot(f_lo, bdc), _dot(f_hi, bds) + _dot(f_lo, bds)], axis=0)
    y_hi, y_lo = _split_bf16(y)
    dft = dft_ref[...]
    four = _dot(dft, y_hi) + _dot(dft, y_lo)
    o_ref[:, ATT_W:ATT_W + FOURIER_W] = four.astype(BF16)

    c0 = ATT_W + 2 * KV_W + FOURIER_W
    u = _gelu_tanh(mix_ref[:, c0:c0 + CHUNK_W])
    vn = (_ln(_gelu_tanh(mix_ref[:, c0 + CHUNK_W:c0 + 2 * CHUNK_W])) * clg_ref[...] + clb_ref[...]).astype(BF16)
    lane = lax.broadcasted_iota(jnp.int32, (CHUNK, CHUNK_W), 1)
    bsm = bsm_ref[...]
    for c in range(seq // CHUNK):
        vc = vn[c * CHUNK:(c + 1) * CHUNK]
        sv = bsm
        for g in range(N_CHUNK_GROUPS):
            vg = jnp.where(lane // CHUNK_GROUP_W == g, vc, jnp.zeros_like(vc))
            sv = sv + _dot(ws_ref[g], vg)
        o_ref[c * CHUNK:(c + 1) * CHUNK, ATT_W + FOURIER_W:] = (u[c * CHUNK:(c + 1) * CHUNK] * sv).astype(BF16)


def _mixer_call(mix, mixed_prev, ctx_k, ctx_v, layer, dft, bdc, bds, ws, bsm, clg, clb, *, latent):
    seq = DEC_SEQ if latent else SEQ
    nb = DEC_BATCH if latent else BATCH
    row0 = (T_CTX // seq) if latent else 0
    const2 = lambda b: (0, 0)
    in_specs = [pl.BlockSpec((seq, MIX_W), lambda b: (row0 + b, 0))]
    args = [mix]
    if latent:
        in_specs += [pl.BlockSpec((None, None, PAST_LEN, KV_W), lambda b: (b, layer, 0, 0))] * 2
        args += [ctx_k, ctx_v]
    in_specs += [
        pl.BlockSpec((seq, 2 * seq), const2),
        pl.BlockSpec((FOURIER_W, FOURIER_W), const2),
        pl.BlockSpec((FOURIER_W, FOURIER_W), const2),
        pl.BlockSpec((N_CHUNK_GROUPS, CHUNK, CHUNK), lambda b: (0, 0, 0)),
        pl.BlockSpec((CHUNK, CHUNK_W), const2),
        pl.BlockSpec((1, CHUNK_W), const2),
        pl.BlockSpec((1, CHUNK_W), const2),
    ]
    args += [dft, bdc, bds, ws, bsm, clg, clb]
    aliases = {}
    if mixed_prev is not None:
        in_specs.append(pl.BlockSpec(memory_space=pl.ANY))
        args.append(mixed_prev)
        aliases = {len(args) - 1: 0}

    def body(*refs):
        if mixed_prev is not None:
            refs = refs[:-2] + refs[-1:]
        _mixer_kernel(*refs, seq=seq, latent=latent)

    return pl.pallas_call(
        body,
        out_shape=jax.ShapeDtypeStruct((T_ALL, D_MODEL), BF16),
        grid=(nb,),
        in_specs=in_specs,
        out_specs=pl.BlockSpec((seq, D_MODEL), lambda b: (row0 + b, 0)),
        input_output_aliases=aliases,
        compiler_params=_cparams("arbitrary"),
        name="mixer_latent" if latent else "mixer_context",
    )(*args)


def _route(logits, bias_t):
    lt = logits.T[:N_EXPERTS]
    ex = jnp.exp(lt - jnp.max(lt, axis=0, keepdims=True))
    probs = ex / jnp.sum(ex, axis=0, keepdims=True)
    sel = probs + bias_t
    p = [probs[e:e + 1] for e in range(N_EXPERTS)]
    s = [sel[e:e + 1] for e in range(N_EXPERTS)]
    n = EXPERTS_PER_GROUP
    scores = []
    for g in range(N_EXPERT_GROUPS):
        pair = [s[g * n + a] + s[g * n + b] for a in range(n) for b in range(a + 1, n)]
        scores.append(functools.reduce(jnp.maximum, pair))
    best = jnp.zeros_like(scores[0], dtype=jnp.int32)
    best_score = scores[0]
    for g in range(1, N_EXPERT_GROUPS):
        better = scores[g] > best_score
        best = jnp.where(better, g, best)
        best_score = jnp.where(better, scores[g], best_score)
    comb = []
    for g in range(N_EXPERT_GROUPS):
        in_g = best == g
        chosen = []
        for a in range(n):
            rank = jnp.zeros_like(best)
            for b in range(n):
                if b == a:
                    continue
                ahead = (s[g * n + b] > s[g * n + a]) if b > a else (s[g * n + b] >= s[g * n + a])
                rank = rank + ahead.astype(jnp.int32)
            chosen.append(jnp.logical_and(in_g, rank < 2))
        wsum = functools.reduce(jnp.add, [jnp.where(chosen[a], p[g * n + a], 0.0) for a in range(n)])
        for a in range(n):
            comb.append(jnp.where(chosen[a], p[g * n + a] / wsum, 0.0))
    comb_t = jnp.concatenate(comb + [jnp.zeros((ROUTER_PAD - N_EXPERTS, lt.shape[1]), F32)], axis=0)
    return comb_t.T


def _merge_kernel(x_ref, mixed_ref, mod_ref, wg_ref, wpa_ref, wpf_ref, wpc_ref, wo_ref, l1g_ref, l1b_ref,
                  wrh_ref, wrl_ref, bias_ref, x1_ref, h2_ref, comb_ref):
    x = x_ref[...]
    m = mod_ref[0]
    h = (_ln(x) * (1.0 + m[1:2]) + m[0:1]).astype(BF16)
    mixed = mixed_ref[...]
    merged = None
    branch_w = ((0, ATT_W, wpa_ref), (ATT_W, FOURIER_W, wpf_ref), (ATT_W + FOURIER_W, CHUNK_W, wpc_ref))
    for b, (c0, width, w_ref) in enumerate(branch_w):
        gate = _sigmoid(_dot(h, wg_ref[:, b * D_MODEL:(b + 1) * D_MODEL]))
        term = gate * _dot(mixed[:, c0:c0 + width], w_ref[...])
        merged = term if merged is None else merged + term
    mix = _dot(merged.astype(BF16), wo_ref[...])
    x1 = _ln(ALPHA * x + m[2:3] * mix) * l1g_ref[...] + l1b_ref[...]
    x1_ref[...] = x1
    h2 = _ln(x1) * (1.0 + m[4:5]) + m[3:4]
    h2_ref[...] = h2.astype(BF16)
    h_hi, h_lo = _split_bf16(h2)
    wrh = wrh_ref[...]
    logits = _dot(h_hi, wrh) + _dot(h_lo, wrh) + _dot(h_hi, wrl_ref[...])
    comb_ref[...] = _route(logits, bias_ref[...])


def _merge_call(x, mixed, modl, wg, wpa, wpf, wpc, wo, l1g, l1b, wrh, wrl, bias_t):
    const2 = lambda i: (0, 0)
    tok = lambda i: (i, 0)
    return pl.pallas_call(
        _merge_kernel,
        out_shape=(jax.ShapeDtypeStruct((T_ALL, D_MODEL), F32),
                   jax.ShapeDtypeStruct((T_ALL, D_MODEL), BF16),
                   jax.ShapeDtypeStruct((T_ALL, ROUTER_PAD), F32)),
        grid=(T_ALL // TM,),
        in_specs=[
            pl.BlockSpec((TM, D_MODEL), tok),
            pl.BlockSpec((TM, D_MODEL), tok),
            pl.BlockSpec((1, N_MOD, D_MODEL), lambda i: (_cond_row(i, TM), 0, 0)),
            pl.BlockSpec((D_MODEL, GATE_W), const2),
            pl.BlockSpec((ATT_W, D_MODEL), const2),
            pl.BlockSpec((FOURIER_W, D_MODEL), const2),
            pl.BlockSpec((CHUNK_W, D_MODEL), const2),
            pl.BlockSpec((D_MODEL, D_MODEL), const2),
            pl.BlockSpec((1, D_MODEL), const2),
            pl.BlockSpec((1, D_MODEL), const2),
            pl.BlockSpec((D_MODEL, ROUTER_PAD), const2),
            pl.BlockSpec((D_MODEL, ROUTER_PAD), const2),
            pl.BlockSpec((N_EXPERTS, TM), const2),
        ],
        out_specs=(pl.BlockSpec((TM, D_MODEL), tok),
                   pl.BlockSpec((TM, D_MODEL), tok),
                   pl.BlockSpec((TM, ROUTER_PAD), tok)),
        compiler_params=_cparams("arbitrary"),
        name="merge_route",
    )(x, mixed, modl, wg, wpa, wpf, wpc, wo, l1g, l1b, wrh, wrl, bias_t)


def _moe_kernel(h_ref, comb_ref, x1_ref, mod_ref, wge_ref, wue_ref, wde_ref, l2g_ref, l2b_ref, o_ref, acc_ref):
    e = pl.program_id(1)

    @pl.when(e == 0)
    def _():
        acc_ref[...] = jnp.zeros_like(acc_ref)

    h = h_ref[...]
    a = _dot(h, wge_ref[0])
    u = _dot(h, wue_ref[0])
    lane = lax.broadcasted_iota(jnp.int32, comb_ref.shape, 1)
    w_e = jnp.sum(jnp.where(lane == e, comb_ref[...], 0.0), axis=-1, keepdims=True)
    act = (a * _sigmoid(a)) * u * w_e
    acc_ref[...] += _dot(act.astype(BF16), wde_ref[0])

    @pl.when(e == N_EXPERTS - 1)
    def _():
        m = mod_ref[0]
        o_ref[...] = _ln(ALPHA * x1_ref[...] + m[5:6] * acc_ref[...]) * l2g_ref[...] + l2b_ref[...]


def _moe_call(h2, comb, x1, modl, wge, wue, wde, l2g, l2b):
    tm = TM_MOE
    tok = lambda i, e: (i, 0)
    const2 = lambda i, e: (0, 0)
    return pl.pallas_call(
        _moe_kernel,
        out_shape=jax.ShapeDtypeStruct((T_ALL, D_MODEL), F32),
        grid=(T_ALL // tm, N_EXPERTS),
        in_specs=[
            pl.BlockSpec((tm, D_MODEL), tok),
            pl.BlockSpec((tm, ROUTER_PAD), tok),
            pl.BlockSpec((tm, D_MODEL), tok),
            pl.BlockSpec((1, N_MOD, D_MODEL), lambda i, e: (_cond_row(i, tm), 0, 0)),
            pl.BlockSpec((1, D_MODEL, D_EXPERT), lambda i, e: (e, 0, 0)),
            pl.BlockSpec((1, D_MODEL, D_EXPERT), lambda i, e: (e, 0, 0)),
            pl.BlockSpec((1, D_EXPERT, D_MODEL), lambda i, e: (e, 0, 0)),
            pl.BlockSpec((1, D_MODEL), const2),
            pl.BlockSpec((1, D_MODEL), const2),
        ],
        out_specs=pl.BlockSpec((tm, D_MODEL), tok),
        scratch_shapes=[pltpu.VMEM((tm, D_MODEL), F32)],
        compiler_params=_cparams("arbitrary", "arbitrary"),
        name="moe_dense",
    )(h2, comb, x1, modl, wge, wue, wde, l2g, l2b)


def _rope_tables():
    pos = jnp.arange(DEC_SEQ, dtype=jnp.int32)
    row = (pos // GRID_W).astype(F32)
    col = (pos % GRID_W).astype(F32)
    quarter = HEAD_DIM // 4
    inv = ROPE_THETA ** (-jnp.arange(quarter, dtype=F32) / quarter)
    ang_r = row[:, None] * inv[None, :]
    ang_c = col[:, None] * inv[None, :]
    cos = jnp.concatenate([jnp.cos(ang_r)] * 2 + [jnp.cos(ang_c)] * 2, axis=-1)
    sin = jnp.concatenate([-jnp.sin(ang_r), jnp.sin(ang_r), -jnp.sin(ang_c), jnp.sin(ang_c)], axis=-1)
    return jnp.tile(cos, (1, N_Q_HEADS)), jnp.tile(sin, (1, N_Q_HEADS))


def _dft_mats(n, scale):
    j = jnp.arange(n, dtype=jnp.int32)
    ang = ((j[:, None] * j[None, :]) % n).astype(F32) * np.float32(2 * np.pi / n)
    return jnp.cos(ang) * scale, jnp.sin(ang) * scale


def _block_diag(m, reps):
    return jnp.kron(jnp.eye(reps, dtype=m.dtype), m)


def kernel(x_prompt, x_sample, cache_k, cache_v, c, c_ctx, w_in, q_norm_g, k_norm_g, w_proj_att,
           w_proj_fourier, w_proj_chunk, w_out, chunk_ln_g, chunk_ln_b, chunk_ws, chunk_bs, w_ada, b_ada,
           ln1_g, ln1_b, ln2_g, ln2_b, w_router, router_bias, w_gate_e, w_up_e, w_down_e):
    x = jnp.concatenate([x_prompt.reshape(T_CTX, D_MODEL), x_sample.reshape(T_LAT, D_MODEL)], axis=0)
    cond = jnp.concatenate([c_ctx[None, :], c, jnp.zeros((N_COND - 1 - DEC_BATCH, D_MODEL), F32)], axis=0)
    mod = _mod_call(cond, w_ada, b_ada)

    cos_t, sin_t = _rope_tables()
    bd_heads = _block_diag(jnp.ones((HEAD_DIM, HEAD_DIM), BF16), N_Q_HEADS)
    c64, s64 = _dft_mats(FOURIER_GROUP_W, 1.0)
    bdc = _block_diag(c64, FOURIER_W // FOURIER_GROUP_W).astype(BF16)
    bds = _block_diag(s64, FOURIER_W // FOURIER_GROUP_W).astype(BF16)
    dft = {}
    for seq in (SEQ, DEC_SEQ):
        cs, ss = _dft_mats(seq, np.float32(1.0 / math.sqrt(seq * FOURIER_GROUP_W)))
        dft[seq] = jnp.concatenate([cs, -ss], axis=1).astype(BF16)
    ctx_k = cache_k.reshape(DEC_BATCH, DEPTH, PAST_LEN, KV_W)
    ctx_v = cache_v.reshape(DEC_BATCH, DEPTH, PAST_LEN, KV_W)
    wr = jnp.pad(w_router, ((0, 0), (0, ROUTER_PAD - N_EXPERTS)))
    wr_hi = wr.astype(BF16)
    wr_lo = (wr - wr_hi.astype(F32)).astype(BF16)
    bias_t = jnp.broadcast_to(router_bias[:, None], (N_EXPERTS, TM))

    new_k, new_v = [], []
    for l in range(DEPTH):
        modl = mod[l].reshape(N_COND, N_MOD, D_MODEL)
        w_in_b = w_in[l].astype(BF16)
        qg = jnp.tile(q_norm_g[l], N_Q_HEADS)[None, :]
        kg = jnp.tile(k_norm_g[l], N_KV_HEADS)[None, :]
        mix = _proj_call(x, modl, w_in_b[:, :MIX_W], qg, kg, cos_t, sin_t, bd_heads)
        new_k.append(mix[:T_CTX, ATT_W:ATT_W + KV_W].reshape(BATCH, SEQ, N_KV_HEADS, HEAD_DIM))
        new_v.append(mix[:T_CTX, ATT_W + KV_W:ATT_W + 2 * KV_W].reshape(BATCH, SEQ, N_KV_HEADS, HEAD_DIM))

        ws = chunk_ws[l].astype(BF16)
        bsm = jnp.repeat(chunk_bs[l].T, CHUNK_GROUP_W, axis=1)
        clg = chunk_ln_g[l][None, :]
        clb = chunk_ln_b[l][None, :]
        mixed = _mixer_call(mix, None, None, None, l, dft[SEQ], bdc, bds, ws, bsm, clg, clb, latent=False)
        mixed = _mixer_call(mix, mixed, ctx_k, ctx_v, l, dft[DEC_SEQ], bdc, bds, ws, bsm, clg, clb, latent=True)

        x1, h2, comb = _merge_call(
            x, mixed, modl, w_in_b[:, MIX_W:], w_proj_att[l].astype(BF16), w_proj_fourier[l].astype(BF16),
            w_proj_chunk[l].astype(BF16), w_out[l].astype(BF16), ln1_g[l][None, :], ln1_b[l][None, :],
            wr_hi, wr_lo, bias_t)
        x = _moe_call(h2, comb, x1, modl, w_gate_e[l].astype(BF16), w_up_e[l].astype(BF16),
                      w_down_e[l].astype(BF16), ln2_g[l][None, :], ln2_b[l][None, :])

    y_prompt = x[:T_CTX].reshape(BATCH, SEQ, D_MODEL)
    y_sample = x[T_CTX:].reshape(DEC_BATCH, DEC_SEQ, D_MODEL)
    return (y_prompt, y_sample, jnp.stack(new_k, axis=1), jnp.stack(new_v, axis=1))
```

```python
import functools
import math

import jax
import jax.numpy as jnp
import numpy as np
from jax import lax
from jax.experimental import pallas as pl
from jax.experimental.pallas import tpu as pltpu

F32 = jnp.float32
BF16 = jnp.bfloat16

D_MODEL = 1024
BATCH = 16
SEQ = 256
DEPTH = 2
DEC_BATCH = 2
DEC_SEQ = 1024
PAST_LEN = 256
GRID_W = 64
N_Q_HEADS = 8
N_KV_HEADS = 2
HEAD_DIM = 64
Q_PER_KV = N_Q_HEADS // N_KV_HEADS
ATT_W = N_Q_HEADS * HEAD_DIM
KV_W = N_KV_HEADS * HEAD_DIM
ROPE_THETA = 10000.0
FOURIER_GROUP_W = 64
FOURIER_W = 256
N_CHUNK_GROUPS = 4
CHUNK_GROUP_W = 64
CHUNK_W = 256
CHUNK = 128
N_EXPERTS = 16
N_EXPERT_GROUPS = 4
EXPERTS_PER_GROUP = 4
D_EXPERT = 512
ALPHA = (2 * DEPTH) ** 0.25
LN_EPS = 1e-6
RMS_EPS = 1e-6

T_CTX = BATCH * SEQ
T_LAT = DEC_BATCH * DEC_SEQ
T_ALL = T_CTX + T_LAT
MIX_W = ATT_W + 2 * KV_W + FOURIER_W + 2 * CHUNK_W
GATE_W = 3 * D_MODEL
N_MOD = 6
N_COND = 8
ROUTER_PAD = 128
TM = 512
TQ = 256
HA_W = D_MODEL + ROUTER_PAD
V7X_VMEM_LIMIT = 56 * 1024 * 1024

PAIR_A = (0, 0, 0, 1, 1, 3)
PAIR_B = (1, 2, 3, 3, 2, 2)
N_PAIRS = len(PAIR_A)
N_CLASSES = N_EXPERT_GROUPS * N_PAIRS
CLASS_PAD = 32
ROUTE_ROWS = 8
TM_MOE = 256
N_TILES = T_ALL // TM_MOE + N_CLASSES
R_ROWS = N_TILES * TM_MOE
SORT_BLK = 512
TILE_LANES = 128
COPY_CHUNK = 256


def _cparams(*sem):
    return pltpu.CompilerParams(dimension_semantics=sem, vmem_limit_bytes=V7X_VMEM_LIMIT)


def _cond_row(i, tm):
    n_ctx = T_CTX // tm
    return jnp.where(i < n_ctx, 0, 1 + (i - n_ctx) // (DEC_SEQ // tm))


def _ln(x):
    mu = jnp.mean(x, axis=-1, keepdims=True)
    xc = x - mu
    var = jnp.mean(xc * xc, axis=-1, keepdims=True)
    return xc * lax.rsqrt(var + LN_EPS)


def _split_bf16(x):
    hi = x.astype(BF16)
    lo = (x - hi.astype(F32)).astype(BF16)
    return hi, lo


def _dot(a, b):
    return jnp.dot(a, b, preferred_element_type=F32)


def _sigmoid(x):
    return 1.0 / (1.0 + jnp.exp(-x))


def _gelu_tanh(x):
    c = np.float32(np.sqrt(2 / np.pi))
    return x * (0.5 * (1.0 + jnp.tanh(c * (x + 0.044715 * (x * x * x)))))


def _mod_kernel(c_ref, w_ref, b_ref, o_ref):
    c = c_ref[...]
    s = c * _sigmoid(c)
    o_ref[0] = _dot(s.astype(BF16), w_ref[0].astype(BF16)) + b_ref[0]


def _mod_call(cond, w_ada, b_ada):
    tn = 1536
    n = N_MOD * D_MODEL
    return pl.pallas_call(
        _mod_kernel,
        out_shape=jax.ShapeDtypeStruct((DEPTH, N_COND, n), F32),
        grid=(DEPTH, n // tn),
        in_specs=[
            pl.BlockSpec((N_COND, D_MODEL), lambda l, j: (0, 0)),
            pl.BlockSpec((1, D_MODEL, tn), lambda l, j: (l, 0, j)),
            pl.BlockSpec((1, 1, tn), lambda l, j: (l, 0, j)),
        ],
        out_specs=pl.BlockSpec((1, N_COND, tn), lambda l, j: (l, 0, j)),
        compiler_params=_cparams("arbitrary", "arbitrary"),
        name="adaln_mod",
    )(cond, w_ada, b_ada.reshape(DEPTH, 1, n))


def _head_rms(q, bd, g):
    hi, lo = _split_bf16(q * q)
    ssum = _dot(hi, bd) + _dot(lo, bd)
    return q * lax.rsqrt(ssum * (1.0 / HEAD_DIM) + RMS_EPS) * g


def _rope(x, cos, sin_signed):
    w = x.shape[-1]
    lane = lax.broadcasted_iota(jnp.int32, x.shape, 1)
    swapped = jnp.where((lane % 32) < 16, pltpu.roll(x, w - 16, 1), pltpu.roll(x, 16, 1))
    return x * cos + swapped * sin_signed


def _proj_kernel(x_ref, mod_ref, w_ref, qg_ref, kg_ref, cos_ref, sin_ref, bd_ref, o_ref):
    i = pl.program_id(0)
    m = mod_ref[0]
    h = _ln(x_ref[...]) * (1.0 + m[1:2]) + m[0:1]
    proj = _dot(h.astype(BF16), w_ref[...])
    o_ref[:, ATT_W + KV_W:] = proj[:, ATT_W + KV_W:]
    bd = bd_ref[...]
    qn = _head_rms(proj[:, :ATT_W], bd, qg_ref[...])
    kn = _head_rms(proj[:, ATT_W:ATT_W + KV_W], bd[:KV_W, :KV_W], kg_ref[...])

    @pl.when(i < T_CTX // TM)
    def _():
        o_ref[:, :ATT_W] = qn
        o_ref[:, ATT_W:ATT_W + KV_W] = kn

    @pl.when(i >= T_CTX // TM)
    def _():
        cos = cos_ref[...]
        sin = sin_ref[...]
        o_ref[:, :ATT_W] = _rope(qn, cos, sin)
        o_ref[:, ATT_W:ATT_W + KV_W] = _rope(kn, cos[:, :KV_W], sin[:, :KV_W])


def _proj_call(x, modl, w_mix, qg, kg, cos_t, sin_t, bd):
    n_ctx = T_CTX // TM
    per_seq = DEC_SEQ // TM

    def rope_idx(i):
        return (jnp.where(i < n_ctx, 0, (i - n_ctx) % per_seq), 0)

    return pl.pallas_call(
        _proj_kernel,
        out_shape=jax.ShapeDtypeStruct((T_ALL, MIX_W), F32),
        grid=(T_ALL // TM,),
        in_specs=[
            pl.BlockSpec((TM, D_MODEL), lambda i: (i, 0)),
            pl.BlockSpec((1, N_MOD, D_MODEL), lambda i: (_cond_row(i, TM), 0, 0)),
            pl.BlockSpec((D_MODEL, MIX_W), lambda i: (0, 0)),
            pl.BlockSpec((1, ATT_W), lambda i: (0, 0)),
            pl.BlockSpec((1, KV_W), lambda i: (0, 0)),
            pl.BlockSpec((TM, ATT_W), rope_idx),
            pl.BlockSpec((TM, ATT_W), rope_idx),
            pl.BlockSpec((ATT_W, ATT_W), lambda i: (0, 0)),
        ],
        out_specs=pl.BlockSpec((TM, MIX_W), lambda i: (i, 0)),
        compiler_params=_cparams("arbitrary"),
        name="in_proj",
    )(x, modl, w_mix, qg, kg, cos_t, sin_t, bd)


def _attention_tile(q, k_parts, v_parts):
    scale = HEAD_DIM ** -0.5
    outs = []
    for g in range(N_KV_HEADS):
        ks = [k[:, g * HEAD_DIM:(g + 1) * HEAD_DIM] for k in k_parts]
        vs = [v[:, g * HEAD_DIM:(g + 1) * HEAD_DIM] for v in v_parts]
        for hh in range(Q_PER_KV):
            h = g * Q_PER_KV + hh
            qh = q[:, h * HEAD_DIM:(h + 1) * HEAD_DIM]
            ss = [lax.dot_general(qh, k, (((1,), (1,)), ((), ())), preferred_element_type=F32) * scale
                  for k in ks]
            m = functools.reduce(jnp.maximum, [jnp.max(s, axis=-1, keepdims=True) for s in ss])
            es = [jnp.exp(s - m) for s in ss]
            denom = functools.reduce(jnp.add, [jnp.sum(e, axis=-1, keepdims=True) for e in es])
            inv = 1.0 / denom
            o = functools.reduce(jnp.add, [_dot((e * inv).astype(BF16), v) for e, v in zip(es, vs)])
            outs.append(o)
    return jnp.concatenate(outs, axis=-1)


def _mixer_kernel(*refs, seq, latent):
    if latent:
        (mix_ref, ck_ref, cv_ref, dft_ref, bdc_ref, bds_ref, ws_ref, bsm_ref, clg_ref, clb_ref, o_ref) = refs
    else:
        (mix_ref, dft_ref, bdc_ref, bds_ref, ws_ref, bsm_ref, clg_ref, clb_ref, o_ref) = refs

    k_new = mix_ref[:, ATT_W:ATT_W + KV_W].astype(BF16)
    v_new = mix_ref[:, ATT_W + KV_W:ATT_W + 2 * KV_W].astype(BF16)
    if latent:
        k_parts = [ck_ref[...].astype(BF16), k_new]
        v_parts = [cv_ref[...].astype(BF16), v_new]
    else:
        k_parts, v_parts = [k_new], [v_new]

    def q_tile(t, carry):
        r0 = pl.multiple_of(t * TQ, TQ)
        q = mix_ref[pl.ds(r0, TQ), :ATT_W].astype(BF16)
        o_ref[pl.ds(r0, TQ), :ATT_W] = _attention_tile(q, k_parts, v_parts).astype(BF16)
        return carry

    lax.fori_loop(0, seq // TQ, q_tile, 0)

    f_hi, f_lo = _split_bf16(mix_ref[:, ATT_W + 2 * KV_W:ATT_W + 2 * KV_W + FOURIER_W])
    bdc = bdc_ref[...]
    bds = bds_ref[...]
    y = jnp.concatenate([_dot(f_hi, bdc) + _dot(f_lo, bdc), _dot(f_hi, bds) + _dot(f_lo, bds)], axis=0)
    y_hi, y_lo = _split_bf16(y)
    dft = dft_ref[...]
    four = _dot(dft, y_hi) + _dot(dft, y_lo)
    o_ref[:, ATT_W:ATT_W + FOURIER_W] = four.astype(BF16)

    c0 = ATT_W + 2 * KV_W + FOURIER_W
    u = _gelu_tanh(mix_ref[:, c0:c0 + CHUNK_W])
    vn = (_ln(_gelu_tanh(mix_ref[:, c0 + CHUNK_W:c0 + 2 * CHUNK_W])) * clg_ref[...] + clb_ref[...]).astype(BF16)
    lane = lax.broadcasted_iota(jnp.int32, (CHUNK, CHUNK_W), 1)
    bsm = bsm_ref[...]
    for c in range(seq // CHUNK):
        vc = vn[c * CHUNK:(c + 1) * CHUNK]
        sv = bsm
        for g in range(N_CHUNK_GROUPS):
            vg = jnp.where(lane // CHUNK_GROUP_W == g, vc, jnp.zeros_like(vc))
            sv = sv + _dot(ws_ref[g], vg)
        o_ref[c * CHUNK:(c + 1) * CHUNK, ATT_W + FOURIER_W:] = (u[c * CHUNK:(c + 1) * CHUNK] * sv).astype(BF16)


def _mixer_call(mix, ctx_k, ctx_v, layer, dft, bdc, bds, ws, bsm, clg, clb, *, latent):
    seq = DEC_SEQ if latent else SEQ
    nb = DEC_BATCH if latent else BATCH
    row0 = (T_CTX // seq) if latent else 0
    const2 = lambda b: (0, 0)
    in_specs = [pl.BlockSpec((seq, MIX_W), lambda b: (row0 + b, 0))]
    args = [mix]
    if latent:
        in_specs += [pl.BlockSpec((None, None, PAST_LEN, KV_W), lambda b: (b, layer, 0, 0))] * 2
        args += [ctx_k, ctx_v]
    in_specs += [
        pl.BlockSpec((seq, 2 * seq), const2),
        pl.BlockSpec((FOURIER_W, FOURIER_W), const2),
        pl.BlockSpec((FOURIER_W, FOURIER_W), const2),
        pl.BlockSpec((N_CHUNK_GROUPS, CHUNK, CHUNK), lambda b: (0, 0, 0)),
        pl.BlockSpec((CHUNK, CHUNK_W), const2),
        pl.BlockSpec((1, CHUNK_W), const2),
        pl.BlockSpec((1, CHUNK_W), const2),
    ]
    args += [dft, bdc, bds, ws, bsm, clg, clb]
    return pl.pallas_call(
        functools.partial(_mixer_kernel, seq=seq, latent=latent),
        out_shape=jax.ShapeDtypeStruct((nb * seq, D_MODEL), BF16),
        grid=(nb,),
        in_specs=in_specs,
        out_specs=pl.BlockSpec((seq, D_MODEL), lambda b: (b, 0)),
        compiler_params=_cparams("arbitrary"),
        name="mixer_latent" if latent else "mixer_context",
    )(*args)


def _route(logits, bias_t):
    lt = logits.T[:N_EXPERTS]
    ex = jnp.exp(lt - jnp.max(lt, axis=0, keepdims=True))
    probs = ex / jnp.sum(ex, axis=0, keepdims=True)
    sel = probs + bias_t
    p = [probs[e:e + 1] for e in range(N_EXPERTS)]
    s = [sel[e:e + 1] for e in range(N_EXPERTS)]
    n = EXPERTS_PER_GROUP
    scores = []
    for g in range(N_EXPERT_GROUPS):
        pair = [s[g * n + a] + s[g * n + b] for a in range(n) for b in range(a + 1, n)]
        scores.append(functools.reduce(jnp.maximum, pair))
    best = jnp.zeros_like(scores[0], dtype=jnp.int32)
    best_score = scores[0]
    for g in range(1, N_EXPERT_GROUPS):
        better = scores[g] > best_score
        best = jnp.where(better, g, best)
        best_score = jnp.where(better, scores[g], best_score)
    cls = jnp.zeros_like(best_score)
    w_a = jnp.zeros_like(best_score)
    w_b = jnp.zeros_like(best_score)
    for g in range(N_EXPERT_GROUPS):
        in_g = best == g
        chosen = []
        for a in range(n):
            rank = jnp.zeros_like(best)
            for b in range(n):
                if b == a:
                    continue
                ahead = (s[g * n + b] > s[g * n + a]) if b > a else (s[g * n + b] >= s[g * n + a])
                rank = rank + ahead.astype(jnp.int32)
            chosen.append(jnp.logical_and(in_g, rank < 2))
        for k in range(N_PAIRS):
            pa, pb = p[g * n + PAIR_A[k]], p[g * n + PAIR_B[k]]
            hit = jnp.logical_and(chosen[PAIR_A[k]], chosen[PAIR_B[k]])
            wsum = pa + pb
            cls = jnp.where(hit, float(g * N_PAIRS + k), cls)
            w_a = jnp.where(hit, pa / wsum, w_a)
            w_b = jnp.where(hit, pb / wsum, w_b)
    return jnp.concatenate([cls, w_a, w_b, jnp.zeros((ROUTE_ROWS - 3, lt.shape[1]), F32)], axis=0)


def _merge_kernel(x_ref, mixc_ref, mixl_ref, mod_ref, wg_ref, wpa_ref, wpf_ref, wpc_ref, wo_ref, l1g_ref, l1b_ref,
                  wrh_ref, wrl_ref, bias_ref, x1_ref, ha_ref, rt_ref):
    x = x_ref[...]
    m = mod_ref[0]
    h = (_ln(x) * (1.0 + m[1:2]) + m[0:1]).astype(BF16)
    mixed = jnp.where(pl.program_id(0) < T_CTX // TM, mixc_ref[...], mixl_ref[...])
    merged = None
    branch_w = ((0, ATT_W, wpa_ref), (ATT_W, FOURIER_W, wpf_ref), (ATT_W + FOURIER_W, CHUNK_W, wpc_ref))
    for b, (c0, width, w_ref) in enumerate(branch_w):
        gate = _sigmoid(_dot(h, wg_ref[:, b * D_MODEL:(b + 1) * D_MODEL]))
        term = gate * _dot(mixed[:, c0:c0 + width], w_ref[...])
        merged = term if merged is None else merged + term
    mix = _dot(merged.astype(BF16), wo_ref[...])
    x1 = _ln(ALPHA * x + m[2:3] * mix) * l1g_ref[...] + l1b_ref[...]
    x1_ref[...] = x1
    h2 = _ln(x1) * (1.0 + m[4:5]) + m[3:4]
    ha_ref[:, :D_MODEL] = h2
    h_hi, h_lo = _split_bf16(h2)
    wrh = wrh_ref[...]
    logits = _dot(h_hi, wrh) + _dot(h_lo, wrh) + _dot(h_hi, wrl_ref[...])
    rt = _route(logits, bias_ref[...])
    rt_ref[...] = rt
    ha_ref[:, D_MODEL:] = jnp.concatenate([rt, jnp.zeros((ROUTER_PAD - ROUTE_ROWS, TM), F32)], axis=0).T


def _merge_call(x, mixed_ctx, mixed_lat, modl, wg, wpa, wpf, wpc, wo, l1g, l1b, wrh, wrl, bias_t):
    const2 = lambda i: (0, 0)
    tok = lambda i: (i, 0)
    n_ctx = T_CTX // TM
    return pl.pallas_call(
        _merge_kernel,
        out_shape=(jax.ShapeDtypeStruct((T_ALL, D_MODEL), F32),
                   jax.ShapeDtypeStruct((T_ALL, HA_W), F32),
                   jax.ShapeDtypeStruct((ROUTE_ROWS, T_ALL), F32)),
        grid=(T_ALL // TM,),
        in_specs=[
            pl.BlockSpec((TM, D_MODEL), tok),
            pl.BlockSpec((TM, D_MODEL), lambda i: (jnp.minimum(i, n_ctx - 1), 0)),
            pl.BlockSpec((TM, D_MODEL), lambda i: (jnp.maximum(i - n_ctx, 0), 0)),
            pl.BlockSpec((1, N_MOD, D_MODEL), lambda i: (_cond_row(i, TM), 0, 0)),
            pl.BlockSpec((D_MODEL, GATE_W), const2),
            pl.BlockSpec((ATT_W, D_MODEL), const2),
            pl.BlockSpec((FOURIER_W, D_MODEL), const2),
            pl.BlockSpec((CHUNK_W, D_MODEL), const2),
            pl.BlockSpec((D_MODEL, D_MODEL), const2),
            pl.BlockSpec((1, D_MODEL), const2),
            pl.BlockSpec((1, D_MODEL), const2),
            pl.BlockSpec((D_MODEL, ROUTER_PAD), const2),
            pl.BlockSpec((D_MODEL, ROUTER_PAD), const2),
            pl.BlockSpec((N_EXPERTS, TM), const2),
        ],
        out_specs=(pl.BlockSpec((TM, D_MODEL), tok),
                   pl.BlockSpec((TM, HA_W), tok),
                   pl.BlockSpec((ROUTE_ROWS, TM), lambda i: (0, i))),
        compiler_params=_cparams("arbitrary"),
        name="merge_route",
    )(x, mixed_ctx, mixed_lat, modl, wg, wpa, wpf, wpc, wo, l1g, l1b, wrh, wrl, bias_t)


def _sort_kernel(rt_ref, tri_ref, low_ref, pos_ref, tile_ref):
    crow = lax.broadcasted_iota(jnp.int32, (CLASS_PAD, SORT_BLK), 0)
    tri = tri_ref[...]
    n_blk = T_ALL // SORT_BLK
    carry = jnp.zeros((CLASS_PAD, 1), F32)
    ranks = []
    for b in range(n_blk):
        hot = rt_ref[0:1, b * SORT_BLK:(b + 1) * SORT_BLK].astype(jnp.int32) == crow
        hot_f = jnp.where(hot, 1.0, 0.0)
        before = _dot(hot_f.astype(BF16), tri) + carry
        ranks.append(jnp.sum(jnp.where(hot, before, 0.0), axis=0, keepdims=True))
        carry = carry + jnp.sum(hot_f, axis=1, keepdims=True)
    padded = jnp.floor((carry + (TM_MOE - 1.0)) * (1.0 / TM_MOE)) * TM_MOE
    padded = jnp.broadcast_to(padded, (CLASS_PAD, TILE_LANES))
    offs = _dot(low_ref[...], padded.astype(BF16))
    for b in range(n_blk):
        hot = rt_ref[0:1, b * SORT_BLK:(b + 1) * SORT_BLK].astype(jnp.int32) == crow
        base = jnp.sum(jnp.where(hot, offs[:, 0:1], 0.0), axis=0, keepdims=True)
        pos_ref[:, b * SORT_BLK:(b + 1) * SORT_BLK] = (base + ranks[b]).astype(jnp.int32)

    start = lax.broadcasted_iota(jnp.int32, (1, TILE_LANES), 1).astype(F32) * TM_MOE
    is_class = lax.broadcasted_iota(jnp.int32, (CLASS_PAD, TILE_LANES), 0) < N_CLASSES
    ends = jnp.where(is_class, offs + padded, 0.0)
    total = jnp.max(ends, axis=0, keepdims=True)
    valid = start < total
    tcls = jnp.sum(jnp.where(jnp.logical_and(is_class, ends <= start), 1.0, 0.0), axis=0, keepdims=True)
    last = jnp.max(jnp.where(valid, tcls, 0.0), axis=1, keepdims=True)
    tcls = jnp.where(valid, tcls, last)
    grp = functools.reduce(jnp.add, [jnp.where(tcls >= g * N_PAIRS, 1.0, 0.0) for g in range(1, N_EXPERT_GROUPS)])
    pair = tcls - grp * N_PAIRS
    slot_a = functools.reduce(jnp.add, [jnp.where(pair == k, float(PAIR_A[k]), 0.0) for k in range(N_PAIRS)])
    slot_b = functools.reduce(jnp.add, [jnp.where(pair == k, float(PAIR_B[k]), 0.0) for k in range(N_PAIRS)])
    rows = [grp * EXPERTS_PER_GROUP + slot_a, grp * EXPERTS_PER_GROUP + slot_b, jnp.where(valid, 1.0, 0.0)]
    tile_ref[...] = jnp.concatenate(rows + [jnp.zeros((8 - len(rows), TILE_LANES), F32)], axis=0).astype(jnp.int32)


def _sort_call(rt, tri, low):
    return pl.pallas_call(
        _sort_kernel,
        out_shape=(jax.ShapeDtypeStruct((1, T_ALL), jnp.int32),
                   jax.ShapeDtypeStruct((8, TILE_LANES), jnp.int32)),
        name="route_sort",
    )(rt, tri, low)


def _row_copy(idx_ref, src_ref, dst_ref, sem, t, *, scatter):
    r = idx_ref[t]
    if scatter:
        return pltpu.make_async_copy(src_ref.at[pl.ds(t, 1)], dst_ref.at[pl.ds(r, 1)], sem)
    return pltpu.make_async_copy(src_ref.at[pl.ds(r, 1)], dst_ref.at[pl.ds(t, 1)], sem)


def _row_move_kernel(idx_ref, src_ref, *rest, scatter):
    dst_ref, sem = rest[-2], rest[-1]
    copy = functools.partial(_row_copy, idx_ref, src_ref, dst_ref, sem, scatter=scatter)

    def wait_chunk():
        def wait(i, c):
            copy(0).wait()
            return c
        lax.fori_loop(0, COPY_CHUNK, wait, 0, unroll=8)

    def chunk(ci, carry):
        def issue(i, c):
            copy(ci * COPY_CHUNK + i).start()
            return c
        lax.fori_loop(0, COPY_CHUNK, issue, 0, unroll=8)

        @pl.when(ci > 0)
        def _():
            wait_chunk()
        return carry

    lax.fori_loop(0, T_ALL // COPY_CHUNK, chunk, 0)
    wait_chunk()


def _row_move_call(idx, src, dst_init, out_rows, *, scatter):
    in_specs = [pl.BlockSpec(memory_space=pltpu.SMEM), pl.BlockSpec(memory_space=pl.ANY)]
    args = [idx, src]
    aliases = {}
    if dst_init is not None:
        in_specs.append(pl.BlockSpec(memory_space=pl.ANY))
        args.append(dst_init)
        aliases = {2: 0}
    return pl.pallas_call(
        functools.partial(_row_move_kernel, scatter=scatter),
        out_shape=jax.ShapeDtypeStruct((out_rows, src.shape[1]), src.dtype),
        in_specs=in_specs,
        out_specs=pl.BlockSpec(memory_space=pl.ANY),
        scratch_shapes=[pltpu.SemaphoreType.DMA(())],
        input_output_aliases=aliases,
        compiler_params=pltpu.CompilerParams(has_side_effects=True),
        name="row_scatter" if scatter else "row_gather",
    )(*args)


def _moe_kernel(ea_ref, eb_ref, valid_ref, xs_ref, wga_ref, wua_ref, wda_ref, wgb_ref, wub_ref, wdb_ref, o_ref):
    j = pl.program_id(0)

    @pl.when(valid_ref[j] > 0)
    def _():
        h = xs_ref[:, :D_MODEL].astype(BF16)
        route = xs_ref[:, D_MODEL:]

        def expert(wg_ref, wu_ref, wd_ref, w):
            a = _dot(h, wg_ref[0])
            u = _dot(h, wu_ref[0])
            return _dot(((a * _sigmoid(a)) * u * w).astype(BF16), wd_ref[0])

        o_ref[...] = (expert(wga_ref, wua_ref, wda_ref, route[:, 1:2])
                      + expert(wgb_ref, wub_ref, wdb_ref, route[:, 2:3]))

    @pl.when(valid_ref[j] == 0)
    def _():
        o_ref[...] = jnp.zeros_like(o_ref)


def _moe_call(tile_a, tile_b, tile_valid, xs, wge, wue, wde):
    up = (1, D_MODEL, D_EXPERT)
    down = (1, D_EXPERT, D_MODEL)
    slot_a = lambda j, ea, eb, va: (ea[j], 0, 0)
    slot_b = lambda j, ea, eb, va: (eb[j], 0, 0)
    grid_spec = pltpu.PrefetchScalarGridSpec(
        num_scalar_prefetch=3,
        grid=(N_TILES,),
        in_specs=[
            pl.BlockSpec((TM_MOE, HA_W), lambda j, ea, eb, va: (j, 0)),
            pl.BlockSpec(up, slot_a), pl.BlockSpec(up, slot_a), pl.BlockSpec(down, slot_a),
            pl.BlockSpec(up, slot_b), pl.BlockSpec(up, slot_b), pl.BlockSpec(down, slot_b),
        ],
        out_specs=pl.BlockSpec((TM_MOE, D_MODEL), lambda j, ea, eb, va: (j, 0)),
    )
    return pl.pallas_call(
        _moe_kernel,
        out_shape=jax.ShapeDtypeStruct((R_ROWS, D_MODEL), F32),
        grid_spec=grid_spec,
        compiler_params=_cparams("arbitrary"),
        name="moe_pairs",
    )(tile_a, tile_b, tile_valid, xs, wge, wue, wde, wge, wue, wde)


def _post_kernel(x1_ref, moe_ref, mod_ref, l2g_ref, l2b_ref, o_ref):
    m = mod_ref[0]
    o_ref[...] = _ln(ALPHA * x1_ref[...] + m[5:6] * moe_ref[...]) * l2g_ref[...] + l2b_ref[...]


def _post_call(x1, moe, modl, l2g, l2b):
    tok = lambda i: (i, 0)
    const2 = lambda i: (0, 0)
    return pl.pallas_call(
        _post_kernel,
        out_shape=jax.ShapeDtypeStruct((T_ALL, D_MODEL), F32),
        grid=(T_ALL // TM,),
        in_specs=[
            pl.BlockSpec((TM, D_MODEL), tok),
            pl.BlockSpec((TM, D_MODEL), tok),
            pl.BlockSpec((1, N_MOD, D_MODEL), lambda i: (_cond_row(i, TM), 0, 0)),
            pl.BlockSpec((1, D_MODEL), const2),
            pl.BlockSpec((1, D_MODEL), const2),
        ],
        out_specs=pl.BlockSpec((TM, D_MODEL), tok),
        compiler_params=_cparams("arbitrary"),
        name="post_moe",
    )(x1, moe, modl, l2g, l2b)


def _rope_tables():
    pos = jnp.arange(DEC_SEQ, dtype=jnp.int32)
    row = (pos // GRID_W).astype(F32)
    col = (pos % GRID_W).astype(F32)
    quarter = HEAD_DIM // 4
    inv = ROPE_THETA ** (-jnp.arange(quarter, dtype=F32) / quarter)
    ang_r = row[:, None] * inv[None, :]
    ang_c = col[:, None] * inv[None, :]
    cos = jnp.concatenate([jnp.cos(ang_r)] * 2 + [jnp.cos(ang_c)] * 2, axis=-1)
    sin = jnp.concatenate([-jnp.sin(ang_r), jnp.sin(ang_r), -jnp.sin(ang_c), jnp.sin(ang_c)], axis=-1)
    return jnp.tile(cos, (1, N_Q_HEADS)), jnp.tile(sin, (1, N_Q_HEADS))


def _dft_mats(n, scale):
    j = jnp.arange(n, dtype=jnp.int32)
    ang = ((j[:, None] * j[None, :]) % n).astype(F32) * np.float32(2 * np.pi / n)
    return jnp.cos(ang) * scale, jnp.sin(ang) * scale


def _block_diag(m, reps):
    return jnp.kron(jnp.eye(reps, dtype=m.dtype), m)


def kernel(x_prompt, x_sample, cache_k, cache_v, c, c_ctx, w_in, q_norm_g, k_norm_g, w_proj_att,
           w_proj_fourier, w_proj_chunk, w_out, chunk_ln_g, chunk_ln_b, chunk_ws, chunk_bs, w_ada, b_ada,
           ln1_g, ln1_b, ln2_g, ln2_b, w_router, router_bias, w_gate_e, w_up_e, w_down_e):
    x = jnp.concatenate([x_prompt.reshape(T_CTX, D_MODEL), x_sample.reshape(T_LAT, D_MODEL)], axis=0)
    cond = jnp.concatenate([c_ctx[None, :], c, jnp.zeros((N_COND - 1 - DEC_BATCH, D_MODEL), F32)], axis=0)
    mod = _mod_call(cond, w_ada, b_ada)

    cos_t, sin_t = _rope_tables()
    bd_heads = _block_diag(jnp.ones((HEAD_DIM, HEAD_DIM), BF16), N_Q_HEADS)
    c64, s64 = _dft_mats(FOURIER_GROUP_W, 1.0)
    bdc = _block_diag(c64, FOURIER_W // FOURIER_GROUP_W).astype(BF16)
    bds = _block_diag(s64, FOURIER_W // FOURIER_GROUP_W).astype(BF16)
    dft = {}
    for seq in (SEQ, DEC_SEQ):
        cs, ss = _dft_mats(seq, np.float32(1.0 / math.sqrt(seq * FOURIER_GROUP_W)))
        dft[seq] = jnp.concatenate([cs, -ss], axis=1).astype(BF16)
    ctx_k = cache_k.reshape(DEC_BATCH, DEPTH, PAST_LEN, KV_W)
    ctx_v = cache_v.reshape(DEC_BATCH, DEPTH, PAST_LEN, KV_W)
    wr = jnp.pad(w_router, ((0, 0), (0, ROUTER_PAD - N_EXPERTS)))
    wr_hi = wr.astype(BF16)
    wr_lo = (wr - wr_hi.astype(F32)).astype(BF16)
    bias_t = jnp.broadcast_to(router_bias[:, None], (N_EXPERTS, TM))
    tri = jnp.triu(jnp.ones((SORT_BLK, SORT_BLK), BF16), 1)
    low = jnp.tril(jnp.ones((CLASS_PAD, CLASS_PAD), BF16), -1)

    new_k, new_v = [], []
    for l in range(DEPTH):
        modl = mod[l].reshape(N_COND, N_MOD, D_MODEL)
        w_in_b = w_in[l].astype(BF16)
        qg = jnp.tile(q_norm_g[l], N_Q_HEADS)[None, :]
        kg = jnp.tile(k_norm_g[l], N_KV_HEADS)[None, :]
        mix = _proj_call(x, modl, w_in_b[:, :MIX_W], qg, kg, cos_t, sin_t, bd_heads)
        new_k.append(mix[:T_CTX, ATT_W:ATT_W + KV_W].reshape(BATCH, SEQ, N_KV_HEADS, HEAD_DIM))
        new_v.append(mix[:T_CTX, ATT_W + KV_W:ATT_W + 2 * KV_W].reshape(BATCH, SEQ, N_KV_HEADS, HEAD_DIM))

        ws = chunk_ws[l].astype(BF16)
        bsm = jnp.repeat(chunk_bs[l].T, CHUNK_GROUP_W, axis=1)
        clg = chunk_ln_g[l][None, :]
        clb = chunk_ln_b[l][None, :]
        mixed_ctx = _mixer_call(mix, None, None, l, dft[SEQ], bdc, bds, ws, bsm, clg, clb, latent=False)
        mixed_lat = _mixer_call(mix, ctx_k, ctx_v, l, dft[DEC_SEQ], bdc, bds, ws, bsm, clg, clb, latent=True)

        x1, ha, rt = _merge_call(
            x, mixed_ctx, mixed_lat, modl, w_in_b[:, MIX_W:], w_proj_att[l].astype(BF16), w_proj_fourier[l].astype(BF16),
            w_proj_chunk[l].astype(BF16), w_out[l].astype(BF16), ln1_g[l][None, :], ln1_b[l][None, :],
            wr_hi, wr_lo, bias_t)
        pos2d, tiles = _sort_call(rt, tri, low)
        pos = pos2d[0]
        xs = _row_move_call(pos, ha, jnp.zeros((R_ROWS, HA_W), F32), R_ROWS, scatter=True)
        ys = _moe_call(tiles[0, :N_TILES], tiles[1, :N_TILES], tiles[2, :N_TILES], xs,
                       w_gate_e[l].astype(BF16), w_up_e[l].astype(BF16), w_down_e[l].astype(BF16))
        moe = _row_move_call(pos, ys, None, T_ALL, scatter=False)
        x = _post_call(x1, moe, modl, ln2_g[l][None, :], ln2_b[l][None, :])

    y_prompt = x[:T_CTX].reshape(BATCH, SEQ, D_MODEL)
    y_sample = x[T_CTX:].reshape(DEC_BATCH, DEC_SEQ, D_MODEL)
    return (y_prompt, y_sample, jnp.stack(new_k, axis=1), jnp.stack(new_v, axis=1))
```

```python
import functools
import math

import jax
import jax.numpy as jnp
import numpy as np
from jax import lax
from jax.experimental import pallas as pl
from jax.experimental.pallas import tpu as pltpu

F32 = jnp.float32
BF16 = jnp.bfloat16

D_MODEL = 1024
BATCH = 16
SEQ = 256
DEPTH = 2
DEC_BATCH = 2
DEC_SEQ = 1024
PAST_LEN = 256
GRID_W = 64
N_Q_HEADS = 8
N_KV_HEADS = 2
HEAD_DIM = 64
Q_PER_KV = N_Q_HEADS // N_KV_HEADS
ATT_W = N_Q_HEADS * HEAD_DIM
KV_W = N_KV_HEADS * HEAD_DIM
ROPE_THETA = 10000.0
FOURIER_GROUP_W = 64
FOURIER_W = 256
N_CHUNK_GROUPS = 4
CHUNK_GROUP_W = 64
CHUNK_W = 256
CHUNK = 128
N_EXPERTS = 16
N_EXPERT_GROUPS = 4
EXPERTS_PER_GROUP = 4
D_EXPERT = 512
ALPHA = (2 * DEPTH) ** 0.25
LN_EPS = 1e-6
RMS_EPS = 1e-6

LANES = 128
SUBLANES = 8
T_CTX = BATCH * SEQ
T_LAT = DEC_BATCH * DEC_SEQ
T_ALL = T_CTX + T_LAT
IN_W = ATT_W + 2 * KV_W + FOURIER_W + 2 * CHUNK_W + 3 * D_MODEL
MIX_W = ATT_W + 2 * KV_W + FOURIER_W + 2 * CHUNK_W
N_MOD = 6
N_COND = 8
ROUTER_PAD = LANES
TM = 512
TQ = 256
V7X_VMEM_LIMIT = 56 * 1024 * 1024

PAIR_A = (0, 0, 0, 1, 1, 3)
PAIR_B = (1, 2, 3, 3, 2, 2)
N_PAIRS = len(PAIR_A)
N_CLASSES = N_EXPERT_GROUPS * N_PAIRS
CLASS_PAD = 32
ROUTE_ROWS = 8
TM_MOE = 256
N_TILES = T_ALL // TM_MOE + N_CLASSES
R_ROWS = N_TILES * TM_MOE
SORT_BLK = 512
D_CHUNKS = D_MODEL // LANES
HA_ROWS = 2 * SUBLANES
ROUTE_ROW = D_CHUNKS
N_PARK = 2 * TM_MOE


def _cparams(*sem):
    return pltpu.CompilerParams(dimension_semantics=sem, vmem_limit_bytes=V7X_VMEM_LIMIT)


def _cond_row(i, tm):
    n_ctx = T_CTX // tm
    return jnp.where(i < n_ctx, 0, 1 + (i - n_ctx) // (DEC_SEQ // tm))


def _ln(x):
    mu = jnp.mean(x, axis=-1, keepdims=True)
    xc = x - mu
    var = jnp.mean(xc * xc, axis=-1, keepdims=True)
    return xc * lax.rsqrt(var + LN_EPS)


def _split_bf16(x):
    hi = x.astype(BF16)
    lo = (x - hi.astype(F32)).astype(BF16)
    return hi, lo


def _dot(a, b):
    return jnp.dot(a, b, preferred_element_type=F32)


def _sigmoid(x):
    return 1.0 / (1.0 + jnp.exp(-x))


def _gelu_tanh(x):
    c = np.float32(np.sqrt(2 / np.pi))
    return x * (0.5 * (1.0 + jnp.tanh(c * (x + 0.044715 * (x * x * x)))))


def _mod_kernel(c_ref, w_ref, b_ref, o_ref):
    c = c_ref[...]
    s = c * _sigmoid(c)
    o_ref[0] = _dot(s.astype(BF16), w_ref[0].astype(BF16)) + b_ref[0]


def _mod_call(cond, w_ada, b_ada):
    tn = 1536
    n = N_MOD * D_MODEL
    return pl.pallas_call(
        _mod_kernel,
        out_shape=jax.ShapeDtypeStruct((DEPTH, N_COND, n), F32),
        grid=(DEPTH, n // tn),
        in_specs=[
            pl.BlockSpec((N_COND, D_MODEL), lambda l, j: (0, 0)),
            pl.BlockSpec((1, D_MODEL, tn), lambda l, j: (l, 0, j)),
            pl.BlockSpec((1, 1, tn), lambda l, j: (l, 0, j)),
        ],
        out_specs=pl.BlockSpec((1, N_COND, tn), lambda l, j: (l, 0, j)),
        compiler_params=_cparams("arbitrary", "arbitrary"),
        name="adaln_mod",
    )(cond, w_ada, b_ada.reshape(DEPTH, 1, n))


def _head_rms(q, bd, g):
    hi, lo = _split_bf16(q * q)
    ssum = _dot(hi, bd) + _dot(lo, bd)
    return q * lax.rsqrt(ssum * (1.0 / HEAD_DIM) + RMS_EPS) * g


def _rope(x, cos, sin_signed):
    w = x.shape[-1]
    lane = lax.broadcasted_iota(jnp.int32, x.shape, 1)
    swapped = jnp.where((lane % 32) < 16, pltpu.roll(x, w - 16, 1), pltpu.roll(x, 16, 1))
    return x * cos + swapped * sin_signed


def _proj_kernel(x_ref, mod_ref, w_ref, qg_ref, kg_ref, cos_ref, sin_ref, bd_ref, o_ref):
    i = pl.program_id(0)
    m = mod_ref[0, 0]
    h = _ln(x_ref[...]) * (1.0 + m[1:2]) + m[0:1]
    proj = _dot(h.astype(BF16), w_ref[...])
    o_ref[:, ATT_W + KV_W:] = proj[:, ATT_W + KV_W:]
    bd = bd_ref[...]
    qn = _head_rms(proj[:, :ATT_W], bd, qg_ref[0])
    kn = _head_rms(proj[:, ATT_W:ATT_W + KV_W], bd[:KV_W, :KV_W], kg_ref[0])

    @pl.when(i < T_CTX // TM)
    def _():
        o_ref[:, :ATT_W] = qn
        o_ref[:, ATT_W:ATT_W + KV_W] = kn

    @pl.when(i >= T_CTX // TM)
    def _():
        cos = cos_ref[...]
        sin = sin_ref[...]
        o_ref[:, :ATT_W] = _rope(qn, cos, sin)
        o_ref[:, ATT_W:ATT_W + KV_W] = _rope(kn, cos[:, :KV_W], sin[:, :KV_W])


def _proj_call(layer, x, mod, w_in_b, qg, kg, cos_t, sin_t, bd):
    n_ctx = T_CTX // TM
    per_seq = DEC_SEQ // TM

    def rope_idx(i):
        return (jnp.where(i < n_ctx, 0, (i - n_ctx) % per_seq), 0)

    return pl.pallas_call(
        _proj_kernel,
        out_shape=jax.ShapeDtypeStruct((T_ALL, MIX_W), F32),
        grid=(T_ALL // TM,),
        in_specs=[
            pl.BlockSpec((TM, D_MODEL), lambda i: (i, 0)),
            pl.BlockSpec((1, 1, N_MOD, D_MODEL), lambda i: (layer, _cond_row(i, TM), 0, 0)),
            pl.BlockSpec((None, D_MODEL, MIX_W), lambda i: (layer, 0, 0)),
            pl.BlockSpec((1, 1, ATT_W), lambda i: (layer, 0, 0)),
            pl.BlockSpec((1, 1, KV_W), lambda i: (layer, 0, 0)),
            pl.BlockSpec((TM, ATT_W), rope_idx),
            pl.BlockSpec((TM, ATT_W), rope_idx),
            pl.BlockSpec((ATT_W, ATT_W), lambda i: (0, 0)),
        ],
        out_specs=pl.BlockSpec((TM, MIX_W), lambda i: (i, 0)),
        compiler_params=_cparams("arbitrary"),
        name="in_proj",
    )(x, mod, w_in_b, qg, kg, cos_t, sin_t, bd)


def _attention_tile(q, k_parts, v_parts):
    scale = HEAD_DIM ** -0.5
    outs = []
    for g in range(N_KV_HEADS):
        ks = [k[:, g * HEAD_DIM:(g + 1) * HEAD_DIM] for k in k_parts]
        vs = [v[:, g * HEAD_DIM:(g + 1) * HEAD_DIM] for v in v_parts]
        for hh in range(Q_PER_KV):
            h = g * Q_PER_KV + hh
            qh = q[:, h * HEAD_DIM:(h + 1) * HEAD_DIM]
            ss = [lax.dot_general(qh, k, (((1,), (1,)), ((), ())), preferred_element_type=F32) * scale
                  for k in ks]
            m = functools.reduce(jnp.maximum, [jnp.max(s, axis=-1, keepdims=True) for s in ss])
            es = [jnp.exp(s - m) for s in ss]
            denom = functools.reduce(jnp.add, [jnp.sum(e, axis=-1, keepdims=True) for e in es])
            inv = 1.0 / denom
            o = functools.reduce(jnp.add, [_dot((e * inv).astype(BF16), v) for e, v in zip(es, vs)])
            outs.append(o)
    return jnp.concatenate(outs, axis=-1)


def _mixer_kernel(*refs, seq, latent):
    if latent:
        (mix_ref, ck_ref, cv_ref, dft_ref, bdc_ref, bds_ref, ws_ref, bsm_ref, clg_ref, clb_ref, o_ref) = refs
    else:
        (mix_ref, dft_ref, bdc_ref, bds_ref, ws_ref, bsm_ref, clg_ref, clb_ref, o_ref) = refs

    k_new = mix_ref[:, ATT_W:ATT_W + KV_W].astype(BF16)
    v_new = mix_ref[:, ATT_W + KV_W:ATT_W + 2 * KV_W].astype(BF16)
    if latent:
        k_parts = [ck_ref[...].astype(BF16), k_new]
        v_parts = [cv_ref[...].astype(BF16), v_new]
    else:
        k_parts, v_parts = [k_new], [v_new]

    def q_tile(t, carry):
        r0 = pl.multiple_of(t * TQ, TQ)
        q = mix_ref[pl.ds(r0, TQ), :ATT_W].astype(BF16)
        o_ref[pl.ds(r0, TQ), :ATT_W] = _attention_tile(q, k_parts, v_parts).astype(BF16)
        return carry

    lax.fori_loop(0, seq // TQ, q_tile, 0)

    f_hi, f_lo = _split_bf16(mix_ref[:, ATT_W + 2 * KV_W:ATT_W + 2 * KV_W + FOURIER_W])
    bdc = bdc_ref[...]
    bds = bds_ref[...]
    y = jnp.concatenate([_dot(f_hi, bdc) + _dot(f_lo, bdc), _dot(f_hi, bds) + _dot(f_lo, bds)], axis=0)
    y_hi, y_lo = _split_bf16(y)
    dft = dft_ref[...]
    four = _dot(dft, y_hi) + _dot(dft, y_lo)
    o_ref[:, ATT_W:ATT_W + FOURIER_W] = four.astype(BF16)

    c0 = ATT_W + 2 * KV_W + FOURIER_W
    u = _gelu_tanh(mix_ref[:, c0:c0 + CHUNK_W])
    vn = (_ln(_gelu_tanh(mix_ref[:, c0 + CHUNK_W:c0 + 2 * CHUNK_W])) * clg_ref[0] + clb_ref[0]).astype(BF16)
    lane = lax.broadcasted_iota(jnp.int32, (CHUNK, CHUNK_W), 1)
    bsm = bsm_ref[...]
    for c in range(seq // CHUNK):
        vc = vn[c * CHUNK:(c + 1) * CHUNK]
        sv = bsm
        for g in range(N_CHUNK_GROUPS):
            vg = jnp.where(lane // CHUNK_GROUP_W == g, vc, jnp.zeros_like(vc))
            sv = sv + _dot(ws_ref[g], vg)
        o_ref[c * CHUNK:(c + 1) * CHUNK, ATT_W + FOURIER_W:] = (u[c * CHUNK:(c + 1) * CHUNK] * sv).astype(BF16)


def _mixer_call(layer, mix, ctx_k, ctx_v, dft, bdc, bds, ws, bsm, clg, clb, *, latent):
    seq = DEC_SEQ if latent else SEQ
    nb = DEC_BATCH if latent else BATCH
    row0 = (T_CTX // seq) if latent else 0
    const2 = lambda b: (0, 0)
    per_layer = lambda b: (layer, 0, 0)
    in_specs = [pl.BlockSpec((seq, MIX_W), lambda b: (row0 + b, 0))]
    args = [mix]
    if latent:
        in_specs += [pl.BlockSpec((None, None, PAST_LEN, KV_W), lambda b: (b, layer, 0, 0))] * 2
        args += [ctx_k, ctx_v]
    in_specs += [
        pl.BlockSpec((seq, 2 * seq), const2),
        pl.BlockSpec((FOURIER_W, FOURIER_W), const2),
        pl.BlockSpec((FOURIER_W, FOURIER_W), const2),
        pl.BlockSpec((None, N_CHUNK_GROUPS, CHUNK, CHUNK), lambda b: (layer, 0, 0, 0)),
        pl.BlockSpec((None, CHUNK, CHUNK_W), per_layer),
        pl.BlockSpec((1, 1, CHUNK_W), per_layer),
        pl.BlockSpec((1, 1, CHUNK_W), per_layer),
    ]
    args += [dft, bdc, bds, ws, bsm, clg, clb]
    return pl.pallas_call(
        functools.partial(_mixer_kernel, seq=seq, latent=latent),
        out_shape=jax.ShapeDtypeStruct((nb * seq, D_MODEL), BF16),
        grid=(nb,),
        in_specs=in_specs,
        out_specs=pl.BlockSpec((seq, D_MODEL), lambda b: (b, 0)),
        compiler_params=_cparams("arbitrary"),
        name="mixer_latent" if latent else "mixer_context",
    )(*args)


def _route(logits, bias_t):
    lt = logits.T[:N_EXPERTS]
    ex = jnp.exp(lt - jnp.max(lt, axis=0, keepdims=True))
    probs = ex / jnp.sum(ex, axis=0, keepdims=True)
    sel = probs + bias_t
    p = [probs[e:e + 1] for e in range(N_EXPERTS)]
    s = [sel[e:e + 1] for e in range(N_EXPERTS)]
    n = EXPERTS_PER_GROUP
    scores = []
    for g in range(N_EXPERT_GROUPS):
        pair = [s[g * n + a] + s[g * n + b] for a in range(n) for b in range(a + 1, n)]
        scores.append(functools.reduce(jnp.maximum, pair))
    best = jnp.zeros_like(scores[0], dtype=jnp.int32)
    best_score = scores[0]
    for g in range(1, N_EXPERT_GROUPS):
        better = scores[g] > best_score
        best = jnp.where(better, g, best)
        best_score = jnp.where(better, scores[g], best_score)
    cls = jnp.zeros_like(best_score)
    w_a = jnp.zeros_like(best_score)
    w_b = jnp.zeros_like(best_score)
    for g in range(N_EXPERT_GROUPS):
        in_g = best == g
        chosen = []
        for a in range(n):
            rank = jnp.zeros_like(best)
            for b in range(n):
                if b == a:
                    continue
                ahead = (s[g * n + b] > s[g * n + a]) if b > a else (s[g * n + b] >= s[g * n + a])
                rank = rank + ahead.astype(jnp.int32)
            chosen.append(jnp.logical_and(in_g, rank < 2))
        for k in range(N_PAIRS):
            pa, pb = p[g * n + PAIR_A[k]], p[g * n + PAIR_B[k]]
            hit = jnp.logical_and(chosen[PAIR_A[k]], chosen[PAIR_B[k]])
            wsum = pa + pb
            cls = jnp.where(hit, float(g * N_PAIRS + k), cls)
            w_a = jnp.where(hit, pa / wsum, w_a)
            w_b = jnp.where(hit, pb / wsum, w_b)
    return jnp.concatenate([cls, w_a, w_b, jnp.zeros((ROUTE_ROWS - 3, lt.shape[1]), F32)], axis=0)


def _merge_kernel(x_ref, mixc_ref, mixl_ref, mod_ref, win_ref, wpa_ref, wpf_ref, wpc_ref, wo_ref, l1g_ref, l1b_ref,
                  wrh_ref, wrl_ref, bias_ref, x1_ref, ha_ref, rt_ref):
    x = x_ref[...]
    m = mod_ref[0, 0]
    h = (_ln(x) * (1.0 + m[1:2]) + m[0:1]).astype(BF16)
    mixed = jnp.where(pl.program_id(0) < T_CTX // TM, mixc_ref[...], mixl_ref[...])
    merged = None
    branch_w = ((0, ATT_W, wpa_ref), (ATT_W, FOURIER_W, wpf_ref), (ATT_W + FOURIER_W, CHUNK_W, wpc_ref))
    for b, (c0, width, w_ref) in enumerate(branch_w):
        gate = _sigmoid(_dot(h, win_ref[:, MIX_W + b * D_MODEL:MIX_W + (b + 1) * D_MODEL]))
        term = gate * _dot(mixed[:, c0:c0 + width], w_ref[...])
        merged = term if merged is None else merged + term
    mix = _dot(merged.astype(BF16), wo_ref[...])
    x1 = _ln(ALPHA * x + m[2:3] * mix) * l1g_ref[0] + l1b_ref[0]
    x1_ref[...] = x1
    h2 = _ln(x1) * (1.0 + m[4:5]) + m[3:4]
    h_hi, h_lo = _split_bf16(h2)
    wrh = wrh_ref[...]
    logits = _dot(h_hi, wrh) + _dot(h_lo, wrh) + _dot(h_hi, wrl_ref[...])
    rt = _route(logits, bias_ref[...])
    rt_ref[...] = rt
    for k in range(D_CHUNKS):
        ha_ref[:, k, :] = h2[:, k * LANES:(k + 1) * LANES]
    ha_ref[:, ROUTE_ROW, :] = jnp.concatenate([rt, jnp.zeros((LANES - ROUTE_ROWS, TM), F32)], axis=0).T
    ha_ref[:, ROUTE_ROW + 1:, :] = jnp.zeros((TM, HA_ROWS - ROUTE_ROW - 1, LANES), F32)


def _merge_call(layer, x, mixed_ctx, mixed_lat, mod, w_in_b, wpa, wpf, wpc, wo, l1g, l1b, wrh, wrl, bias_t):
    const2 = lambda i: (0, 0)
    per_layer = lambda i: (layer, 0, 0)
    tok = lambda i: (i, 0)
    n_ctx = T_CTX // TM
    return pl.pallas_call(
        _merge_kernel,
        out_shape=(jax.ShapeDtypeStruct((T_ALL, D_MODEL), F32),
                   jax.ShapeDtypeStruct((T_ALL, HA_ROWS, LANES), F32),
                   jax.ShapeDtypeStruct((ROUTE_ROWS, T_ALL), F32)),
        grid=(T_ALL // TM,),
        in_specs=[
            pl.BlockSpec((TM, D_MODEL), tok),
            pl.BlockSpec((TM, D_MODEL), lambda i: (jnp.minimum(i, n_ctx - 1), 0)),
            pl.BlockSpec((TM, D_MODEL), lambda i: (jnp.maximum(i - n_ctx, 0), 0)),
            pl.BlockSpec((1, 1, N_MOD, D_MODEL), lambda i: (layer, _cond_row(i, TM), 0, 0)),
            pl.BlockSpec((None, D_MODEL, IN_W), per_layer, pipeline_mode=pl.Buffered(1)),
            pl.BlockSpec((None, ATT_W, D_MODEL), per_layer, pipeline_mode=pl.Buffered(1)),
            pl.BlockSpec((None, FOURIER_W, D_MODEL), per_layer, pipeline_mode=pl.Buffered(1)),
            pl.BlockSpec((None, CHUNK_W, D_MODEL), per_layer, pipeline_mode=pl.Buffered(1)),
            pl.BlockSpec((None, D_MODEL, D_MODEL), per_layer, pipeline_mode=pl.Buffered(1)),
            pl.BlockSpec((1, 1, D_MODEL), per_layer),
            pl.BlockSpec((1, 1, D_MODEL), per_layer),
            pl.BlockSpec((D_MODEL, ROUTER_PAD), const2),
            pl.BlockSpec((D_MODEL, ROUTER_PAD), const2),
            pl.BlockSpec((N_EXPERTS, TM), const2),
        ],
        out_specs=(pl.BlockSpec((TM, D_MODEL), tok),
                   pl.BlockSpec((TM, HA_ROWS, LANES), lambda i: (i, 0, 0)),
                   pl.BlockSpec((ROUTE_ROWS, TM), lambda i: (0, i))),
        compiler_params=_cparams("arbitrary"),
        name="merge_route",
    )(x, mixed_ctx, mixed_lat, mod, w_in_b, wpa, wpf, wpc, wo, l1g, l1b, wrh, wrl, bias_t)


def _sort_kernel(rt_ref, tri_ref, low_ref, pos_ref, tile_ref):
    crow = lax.broadcasted_iota(jnp.int32, (CLASS_PAD, SORT_BLK), 0)
    tri = tri_ref[...]
    n_blk = T_ALL // SORT_BLK
    carry = jnp.zeros((CLASS_PAD, 1), F32)
    ranks = []
    for b in range(n_blk):
        hot = rt_ref[0:1, b * SORT_BLK:(b + 1) * SORT_BLK].astype(jnp.int32) == crow
        hot_f = jnp.where(hot, 1.0, 0.0)
        before = _dot(hot_f.astype(BF16), tri) + carry
        ranks.append(jnp.sum(jnp.where(hot, before, 0.0), axis=0, keepdims=True))
        carry = carry + jnp.sum(hot_f, axis=1, keepdims=True)
    padded = jnp.floor((carry + (TM_MOE - 1.0)) * (1.0 / TM_MOE)) * TM_MOE
    padded = jnp.broadcast_to(padded, (CLASS_PAD, LANES))
    offs = _dot(low_ref[...], padded.astype(BF16))
    for b in range(n_blk):
        hot = rt_ref[0:1, b * SORT_BLK:(b + 1) * SORT_BLK].astype(jnp.int32) == crow
        base = jnp.sum(jnp.where(hot, offs[:, 0:1], 0.0), axis=0, keepdims=True)
        pos_ref[:, b * SORT_BLK:(b + 1) * SORT_BLK] = (base + ranks[b]).astype(jnp.int32)

    start = lax.broadcasted_iota(jnp.int32, (1, LANES), 1).astype(F32) * TM_MOE
    is_class = lax.broadcasted_iota(jnp.int32, (CLASS_PAD, LANES), 0) < N_CLASSES
    ends = jnp.where(is_class, offs + padded, 0.0)
    total = jnp.max(ends, axis=0, keepdims=True)
    valid = start < total
    tcls = jnp.sum(jnp.where(jnp.logical_and(is_class, ends <= start), 1.0, 0.0), axis=0, keepdims=True)
    last = jnp.max(jnp.where(valid, tcls, 0.0), axis=1, keepdims=True)
    tcls = jnp.where(valid, tcls, last)
    grp = functools.reduce(jnp.add, [jnp.where(tcls >= g * N_PAIRS, 1.0, 0.0) for g in range(1, N_EXPERT_GROUPS)])
    pair = tcls - grp * N_PAIRS
    slot_a = functools.reduce(jnp.add, [jnp.where(pair == k, float(PAIR_A[k]), 0.0) for k in range(N_PAIRS)])
    slot_b = functools.reduce(jnp.add, [jnp.where(pair == k, float(PAIR_B[k]), 0.0) for k in range(N_PAIRS)])
    rows = [grp * EXPERTS_PER_GROUP + slot_a, grp * EXPERTS_PER_GROUP + slot_b, jnp.where(valid, 1.0, 0.0)]
    tile_ref[...] = jnp.concatenate(rows + [jnp.zeros((8 - len(rows), LANES), F32)], axis=0).astype(jnp.int32)


def _sort_call(rt, tri, low):
    return pl.pallas_call(
        _sort_kernel,
        out_shape=(jax.ShapeDtypeStruct((1, T_ALL), jnp.int32),
                   jax.ShapeDtypeStruct((8, LANES), jnp.int32)),
        name="route_sort",
    )(rt, tri, low)


def _moe_kernel(ta_ref, tb_ref, valid_ref, pos_ref, ha_ref, wga_ref, wua_ref, wda_ref, wgb_ref, wub_ref, wdb_ref,
                out_ref, xbuf, ybuf, src_ref, gsem, ssem):
    j = pl.program_id(0)
    slot = j % 2
    del ta_ref, tb_ref

    def gather_copy(row, buf_slot, r):
        return pltpu.make_async_copy(ha_ref.at[jnp.maximum(src_ref[row], 0)], xbuf.at[buf_slot, r], gsem.at[buf_slot])

    def scatter_copy(row, buf_slot, r):
        tok = src_ref[row]
        dst = jnp.where(tok < 0, T_ALL + buf_slot * TM_MOE + r, tok)
        return pltpu.make_async_copy(ybuf.at[buf_slot, r], out_ref.at[dst], ssem.at[buf_slot])

    def for_rows(fn):
        def body(r, c):
            fn(r)
            return c
        lax.fori_loop(0, TM_MOE, body, 0, unroll=8)

    def start_gather(tile, buf_slot):
        for_rows(lambda r: gather_copy(tile * TM_MOE + r, buf_slot, r).start())

    def is_valid(tile):
        ok = jnp.logical_and(tile >= 0, tile < N_TILES)
        return jnp.logical_and(ok, valid_ref[jnp.clip(tile, 0, N_TILES - 1)] > 0)

    def retire_scatter(buf_slot):
        for_rows(lambda r: scatter_copy(0, buf_slot, 0).wait())

    @pl.when(j == 0)
    def _():
        def clear(r, c):
            src_ref[r] = -1
            return c
        lax.fori_loop(0, R_ROWS, clear, 0, unroll=8)

        def place(t, c):
            src_ref[pos_ref[t]] = t
            return c
        lax.fori_loop(0, T_ALL, place, 0, unroll=8)
        start_gather(0, 0)
        ybuf[...] = jnp.zeros(ybuf.shape, F32)
        for s in range(2):
            park = pltpu.make_async_copy(ybuf.at[s], out_ref.at[pl.ds(T_ALL + s * TM_MOE, TM_MOE)], ssem.at[s])
            park.start()
            park.wait()

    @pl.when(is_valid(j + 1))
    def _():
        start_gather(j + 1, 1 - slot)

    @pl.when(is_valid(j - 2))
    def _():
        retire_scatter(slot)

    @pl.when(is_valid(j))
    def _():
        for_rows(lambda r: gather_copy(0, slot, 0).wait())
        h = jnp.concatenate([xbuf[slot, :, k, :] for k in range(D_CHUNKS)], axis=-1).astype(BF16)
        route = xbuf[slot, :, ROUTE_ROW, :]

        def expert(wg_ref, wu_ref, wd_ref, w):
            a = _dot(h, wg_ref[...])
            u = _dot(h, wu_ref[...])
            return _dot(((a * _sigmoid(a)) * u * w).astype(BF16), wd_ref[...])

        o = expert(wga_ref, wua_ref, wda_ref, route[:, 1:2]) + expert(wgb_ref, wub_ref, wdb_ref, route[:, 2:3])
        for k in range(D_CHUNKS):
            ybuf[slot, :, k, :] = o[:, k * LANES:(k + 1) * LANES]
        for_rows(lambda r: scatter_copy(j * TM_MOE + r, slot, r).start())

    @pl.when(j == N_TILES - 1)
    def _():
        @pl.when(is_valid(j - 1))
        def _():
            retire_scatter(1 - slot)

        @pl.when(is_valid(j))
        def _():
            retire_scatter(slot)


def _moe_call(layer, tiles, pos, ha, wge, wue, wde):
    up = (None, None, D_MODEL, D_EXPERT)
    down = (None, None, D_EXPERT, D_MODEL)
    slot_a = lambda j, ta, tb, va, ps: (layer, ta[j], 0, 0)
    slot_b = lambda j, ta, tb, va, ps: (layer, tb[j], 0, 0)
    grid_spec = pltpu.PrefetchScalarGridSpec(
        num_scalar_prefetch=4,
        grid=(N_TILES,),
        in_specs=[
            pl.BlockSpec(memory_space=pl.ANY),
            pl.BlockSpec(up, slot_a), pl.BlockSpec(up, slot_a), pl.BlockSpec(down, slot_a),
            pl.BlockSpec(up, slot_b), pl.BlockSpec(up, slot_b), pl.BlockSpec(down, slot_b),
        ],
        out_specs=pl.BlockSpec(memory_space=pl.ANY),
        scratch_shapes=[
            pltpu.VMEM((2, TM_MOE, HA_ROWS, LANES), F32),
            pltpu.VMEM((2, TM_MOE, D_CHUNKS, LANES), F32),
            pltpu.SMEM((R_ROWS,), jnp.int32),
            pltpu.SemaphoreType.DMA((2,)),
            pltpu.SemaphoreType.DMA((2,)),
        ],
    )
    return pl.pallas_call(
        _moe_kernel,
        out_shape=jax.ShapeDtypeStruct((T_ALL + N_PARK, D_CHUNKS, LANES), F32),
        grid_spec=grid_spec,
        compiler_params=pltpu.CompilerParams(dimension_semantics=("arbitrary",), vmem_limit_bytes=V7X_VMEM_LIMIT,
                                             has_side_effects=True),
        name="moe_pairs",
    )(tiles[0, :N_TILES], tiles[1, :N_TILES], tiles[2, :N_TILES], pos, ha, wge, wue, wde, wge, wue, wde)


def _post_kernel(x1_ref, moe_ref, mod_ref, l2g_ref, l2b_ref, o_ref):
    m = mod_ref[0, 0]
    moe = jnp.concatenate([moe_ref[:, k, :] for k in range(D_CHUNKS)], axis=-1)
    o_ref[...] = _ln(ALPHA * x1_ref[...] + m[5:6] * moe) * l2g_ref[0] + l2b_ref[0]


def _post_call(layer, x1, moe, mod, l2g, l2b):
    tok = lambda i: (i, 0)
    per_layer = lambda i: (layer, 0, 0)
    return pl.pallas_call(
        _post_kernel,
        out_shape=jax.ShapeDtypeStruct((T_ALL, D_MODEL), F32),
        grid=(T_ALL // TM,),
        in_specs=[
            pl.BlockSpec((TM, D_MODEL), tok),
            pl.BlockSpec((TM, D_CHUNKS, LANES), lambda i: (i, 0, 0)),
            pl.BlockSpec((1, 1, N_MOD, D_MODEL), lambda i: (layer, _cond_row(i, TM), 0, 0)),
            pl.BlockSpec((1, 1, D_MODEL), per_layer),
            pl.BlockSpec((1, 1, D_MODEL), per_layer),
        ],
        out_specs=pl.BlockSpec((TM, D_MODEL), tok),
        compiler_params=_cparams("arbitrary"),
        name="post_moe",
    )(x1, moe, mod, l2g, l2b)


def _rope_tables():
    pos = jnp.arange(DEC_SEQ, dtype=jnp.int32)
    row = (pos // GRID_W).astype(F32)
    col = (pos % GRID_W).astype(F32)
    quarter = HEAD_DIM // 4
    inv = ROPE_THETA ** (-jnp.arange(quarter, dtype=F32) / quarter)
    ang_r = row[:, None] * inv[None, :]
    ang_c = col[:, None] * inv[None, :]
    cos = jnp.concatenate([jnp.cos(ang_r)] * 2 + [jnp.cos(ang_c)] * 2, axis=-1)
    sin = jnp.concatenate([-jnp.sin(ang_r), jnp.sin(ang_r), -jnp.sin(ang_c), jnp.sin(ang_c)], axis=-1)
    return jnp.tile(cos, (1, N_Q_HEADS)), jnp.tile(sin, (1, N_Q_HEADS))


def _dft_mats(n, scale):
    j = jnp.arange(n, dtype=jnp.int32)
    ang = ((j[:, None] * j[None, :]) % n).astype(F32) * np.float32(2 * np.pi / n)
    return jnp.cos(ang) * scale, jnp.sin(ang) * scale


def _block_diag(m, reps):
    return jnp.kron(jnp.eye(reps, dtype=m.dtype), m)


def kernel(x_prompt, x_sample, cache_k, cache_v, c, c_ctx, w_in, q_norm_g, k_norm_g, w_proj_att,
           w_proj_fourier, w_proj_chunk, w_out, chunk_ln_g, chunk_ln_b, chunk_ws, chunk_bs, w_ada, b_ada,
           ln1_g, ln1_b, ln2_g, ln2_b, w_router, router_bias, w_gate_e, w_up_e, w_down_e):
    x = jnp.concatenate([x_prompt.reshape(T_CTX, D_MODEL), x_sample.reshape(T_LAT, D_MODEL)], axis=0)
    cond = jnp.concatenate([c_ctx[None, :], c, jnp.zeros((N_COND - 1 - DEC_BATCH, D_MODEL), F32)], axis=0)
    mod = _mod_call(cond, w_ada, b_ada).reshape(DEPTH, N_COND, N_MOD, D_MODEL)

    cos_t, sin_t = _rope_tables()
    bd_heads = _block_diag(jnp.ones((HEAD_DIM, HEAD_DIM), BF16), N_Q_HEADS)
    c64, s64 = _dft_mats(FOURIER_GROUP_W, 1.0)
    bdc = _block_diag(c64, FOURIER_W // FOURIER_GROUP_W).astype(BF16)
    bds = _block_diag(s64, FOURIER_W // FOURIER_GROUP_W).astype(BF16)
    dft = {}
    for seq in (SEQ, DEC_SEQ):
        cs, ss = _dft_mats(seq, np.float32(1.0 / math.sqrt(seq * FOURIER_GROUP_W)))
        dft[seq] = jnp.concatenate([cs, -ss], axis=1).astype(BF16)
    ctx_k = cache_k.reshape(DEC_BATCH, DEPTH, PAST_LEN, KV_W)
    ctx_v = cache_v.reshape(DEC_BATCH, DEPTH, PAST_LEN, KV_W)
    wr = jnp.pad(w_router, ((0, 0), (0, ROUTER_PAD - N_EXPERTS)))
    wr_hi = wr.astype(BF16)
    wr_lo = (wr - wr_hi.astype(F32)).astype(BF16)
    bias_t = jnp.broadcast_to(router_bias[:, None], (N_EXPERTS, TM))
    tri = jnp.triu(jnp.ones((SORT_BLK, SORT_BLK), BF16), 1)
    low = jnp.tril(jnp.ones((CLASS_PAD, CLASS_PAD), BF16), -1)

    w_in_b = w_in.astype(BF16)
    wpa, wpf, wpc, wo = (w.astype(BF16) for w in (w_proj_att, w_proj_fourier, w_proj_chunk, w_out))
    wge, wue, wde = (w.astype(BF16) for w in (w_gate_e, w_up_e, w_down_e))
    qg = jnp.tile(q_norm_g, (1, N_Q_HEADS))[:, None, :]
    kg = jnp.tile(k_norm_g, (1, N_KV_HEADS))[:, None, :]
    ws = chunk_ws.astype(BF16)
    bsm = jnp.repeat(jnp.swapaxes(chunk_bs, 1, 2), CHUNK_GROUP_W, axis=2)
    row3 = lambda a: a[:, None, :]

    new_k, new_v = [], []
    for l in range(DEPTH):
        mix = _proj_call(l, x, mod, w_in_b, qg, kg, cos_t, sin_t, bd_heads)
        new_k.append(mix[:T_CTX, ATT_W:ATT_W + KV_W].reshape(BATCH, SEQ, N_KV_HEADS, HEAD_DIM))
        new_v.append(mix[:T_CTX, ATT_W + KV_W:ATT_W + 2 * KV_W].reshape(BATCH, SEQ, N_KV_HEADS, HEAD_DIM))
        mixer_args = (bdc, bds, ws, bsm, row3(chunk_ln_g), row3(chunk_ln_b))
        mixed_ctx = _mixer_call(l, mix, None, None, dft[SEQ], *mixer_args, latent=False)
        mixed_lat = _mixer_call(l, mix, ctx_k, ctx_v, dft[DEC_SEQ], *mixer_args, latent=True)
        x1, ha, rt = _merge_call(l, x, mixed_ctx, mixed_lat, mod, w_in_b, wpa, wpf, wpc, wo,
                                 row3(ln1_g), row3(ln1_b), wr_hi, wr_lo, bias_t)
        pos2d, tiles = _sort_call(rt, tri, low)
        moe = _moe_call(l, tiles, pos2d[0], ha, wge, wue, wde)
        x = _post_call(l, x1, moe, mod, row3(ln2_g), row3(ln2_b))

    y_prompt = x[:T_CTX].reshape(BATCH, SEQ, D_MODEL)
    y_sample = x[T_CTX:].reshape(DEC_BATCH, DEC_SEQ, D_MODEL)
    return (y_prompt, y_sample, jnp.stack(new_k, axis=1), jnp.stack(new_v, axis=1))
```

```python
import functools
import math

import jax
import jax.numpy as jnp
import numpy as np
from jax import lax
from jax.experimental import pallas as pl
from jax.experimental.pallas import tpu as pltpu

F32 = jnp.float32
BF16 = jnp.bfloat16

D_MODEL = 1024
BATCH = 16
SEQ = 256
DEPTH = 2
DEC_BATCH = 2
DEC_SEQ = 1024
PAST_LEN = 256
GRID_W = 64
N_Q_HEADS = 8
N_KV_HEADS = 2
HEAD_DIM = 64
Q_PER_KV = N_Q_HEADS // N_KV_HEADS
ATT_W = N_Q_HEADS * HEAD_DIM
KV_W = N_KV_HEADS * HEAD_DIM
ROPE_THETA = 10000.0
FOURIER_GROUP_W = 64
FOURIER_W = 256
N_CHUNK_GROUPS = 4
CHUNK_GROUP_W = 64
CHUNK_W = 256
CHUNK = 128
N_EXPERTS = 16
N_EXPERT_GROUPS = 4
EXPERTS_PER_GROUP = 4
D_EXPERT = 512
ALPHA = (2 * DEPTH) ** 0.25
LN_EPS = 1e-6
RMS_EPS = 1e-6

LANES = 128
SUBLANES = 8
T_CTX = BATCH * SEQ
T_LAT = DEC_BATCH * DEC_SEQ
T_ALL = T_CTX + T_LAT
IN_W = ATT_W + 2 * KV_W + FOURIER_W + 2 * CHUNK_W + 3 * D_MODEL
MIX_W = ATT_W + 2 * KV_W + FOURIER_W + 2 * CHUNK_W
N_MOD = 6
N_COND = 8
ROUTER_PAD = LANES
TM = 512
TQ = 256
V7X_VMEM_LIMIT = 56 * 1024 * 1024

PAIR_A = (0, 0, 0, 1, 1, 3)
PAIR_B = (1, 2, 3, 3, 2, 2)
N_PAIRS = len(PAIR_A)
N_CLASSES = N_EXPERT_GROUPS * N_PAIRS
CLASS_PAD = 32
ROUTE_ROWS = 8
TM_MOE = 256
N_TILES = T_ALL // TM_MOE + N_CLASSES
R_ROWS = N_TILES * TM_MOE
SORT_BLK = 512
D_CHUNKS = D_MODEL // LANES
HA_ROWS = D_CHUNKS + 1
ROUTE_ROW = D_CHUNKS
N_PARK = 2 * TM_MOE
COPY_GROUP = 8


def _cparams(*sem):
    return pltpu.CompilerParams(dimension_semantics=sem, vmem_limit_bytes=V7X_VMEM_LIMIT)


def _cond_row(i, tm):
    n_ctx = T_CTX // tm
    return jnp.where(i < n_ctx, 0, 1 + (i - n_ctx) // (DEC_SEQ // tm))


N_CTX_TILES = T_CTX // TM


def _group_specs():
    return [pl.BlockSpec((TM, D_MODEL), lambda i: (jnp.minimum(i, N_CTX_TILES - 1), 0)),
            pl.BlockSpec((TM, D_MODEL), lambda i: (jnp.maximum(i - N_CTX_TILES, 0), 0))]


def _group_tile(ctx_ref, lat_ref):
    return jnp.where(pl.program_id(0) < N_CTX_TILES, ctx_ref[...], lat_ref[...])


def _ln(x):
    mu = jnp.mean(x, axis=-1, keepdims=True)
    xc = x - mu
    var = jnp.mean(xc * xc, axis=-1, keepdims=True)
    return xc * lax.rsqrt(var + LN_EPS)


def _split_bf16(x):
    hi = x.astype(BF16)
    lo = (x - hi.astype(F32)).astype(BF16)
    return hi, lo


def _dot(a, b):
    return jnp.dot(a, b, preferred_element_type=F32)


def _sigmoid(x):
    return 1.0 / (1.0 + jnp.exp(-x))


def _gelu_tanh(x):
    c = np.float32(np.sqrt(2 / np.pi))
    return x * (0.5 * (1.0 + jnp.tanh(c * (x + 0.044715 * (x * x * x)))))


def _mod_kernel(c_ref, w_ref, b_ref, o_ref):
    c = c_ref[...]
    s = c * _sigmoid(c)
    o_ref[0] = _dot(s.astype(BF16), w_ref[0].astype(BF16)) + b_ref[0]


def _mod_call(cond, w_ada, b_ada):
    tn = 1536
    n = N_MOD * D_MODEL
    return pl.pallas_call(
        _mod_kernel,
        out_shape=jax.ShapeDtypeStruct((DEPTH, N_COND, n), F32),
        grid=(DEPTH, n // tn),
        in_specs=[
            pl.BlockSpec((N_COND, D_MODEL), lambda l, j: (0, 0)),
            pl.BlockSpec((1, D_MODEL, tn), lambda l, j: (l, 0, j)),
            pl.BlockSpec((1, 1, tn), lambda l, j: (l, 0, j)),
        ],
        out_specs=pl.BlockSpec((1, N_COND, tn), lambda l, j: (l, 0, j)),
        compiler_params=_cparams("arbitrary", "arbitrary"),
        name="adaln_mod",
    )(cond, w_ada, b_ada.reshape(DEPTH, 1, n))


def _head_rms(q, bd, g):
    hi, lo = _split_bf16(q * q)
    ssum = _dot(hi, bd) + _dot(lo, bd)
    return q * lax.rsqrt(ssum * (1.0 / HEAD_DIM) + RMS_EPS) * g


def _rope(x, cos, sin_signed):
    w = x.shape[-1]
    lane = lax.broadcasted_iota(jnp.int32, x.shape, 1)
    swapped = jnp.where((lane % 32) < 16, pltpu.roll(x, w - 16, 1), pltpu.roll(x, 16, 1))
    return x * cos + swapped * sin_signed


def _proj_kernel(xc_ref, xl_ref, mod_ref, w_ref, qg_ref, kg_ref, cos_ref, sin_ref, bd_ref, o_ref):
    i = pl.program_id(0)
    m = mod_ref[0, 0]
    h = _ln(_group_tile(xc_ref, xl_ref)) * (1.0 + m[1:2]) + m[0:1]
    proj = _dot(h.astype(BF16), w_ref[...])
    o_ref[:, ATT_W + KV_W:] = proj[:, ATT_W + KV_W:]
    bd = bd_ref[...]
    qn = _head_rms(proj[:, :ATT_W], bd, qg_ref[0])
    kn = _head_rms(proj[:, ATT_W:ATT_W + KV_W], bd[:KV_W, :KV_W], kg_ref[0])

    @pl.when(i < T_CTX // TM)
    def _():
        o_ref[:, :ATT_W] = qn
        o_ref[:, ATT_W:ATT_W + KV_W] = kn

    @pl.when(i >= T_CTX // TM)
    def _():
        cos = cos_ref[...]
        sin = sin_ref[...]
        o_ref[:, :ATT_W] = _rope(qn, cos, sin)
        o_ref[:, ATT_W:ATT_W + KV_W] = _rope(kn, cos[:, :KV_W], sin[:, :KV_W])


def _proj_call(layer, x, mod, w_in_b, qg, kg, cos_t, sin_t, bd):
    n_ctx = T_CTX // TM
    per_seq = DEC_SEQ // TM

    def rope_idx(i):
        return (jnp.where(i < n_ctx, 0, (i - n_ctx) % per_seq), 0)

    return pl.pallas_call(
        _proj_kernel,
        out_shape=jax.ShapeDtypeStruct((T_ALL, MIX_W), F32),
        grid=(T_ALL // TM,),
        in_specs=_group_specs() + [
            pl.BlockSpec((1, 1, N_MOD, D_MODEL), lambda i: (layer, _cond_row(i, TM), 0, 0)),
            pl.BlockSpec((None, D_MODEL, MIX_W), lambda i: (layer, 0, 0)),
            pl.BlockSpec((1, 1, ATT_W), lambda i: (layer, 0, 0)),
            pl.BlockSpec((1, 1, KV_W), lambda i: (layer, 0, 0)),
            pl.BlockSpec((TM, ATT_W), rope_idx),
            pl.BlockSpec((TM, ATT_W), rope_idx),
            pl.BlockSpec((ATT_W, ATT_W), lambda i: (0, 0)),
        ],
        out_specs=pl.BlockSpec((TM, MIX_W), lambda i: (i, 0)),
        compiler_params=_cparams("arbitrary"),
        name="in_proj",
    )(*x, mod, w_in_b, qg, kg, cos_t, sin_t, bd)


def _attention_tile(q, k_parts, v_parts):
    outs = []
    for g in range(N_KV_HEADS):
        ks = [k[:, g * HEAD_DIM:(g + 1) * HEAD_DIM] for k in k_parts]
        vs = [v[:, g * HEAD_DIM:(g + 1) * HEAD_DIM] for v in v_parts]
        for hh in range(Q_PER_KV):
            h = g * Q_PER_KV + hh
            qh = q[:, h * HEAD_DIM:(h + 1) * HEAD_DIM]
            ss = [lax.dot_general(qh, k, (((1,), (1,)), ((), ())), preferred_element_type=F32) for k in ks]
            m = functools.reduce(jnp.maximum, [jnp.max(s, axis=-1, keepdims=True) for s in ss])
            es = [jnp.exp(s - m) for s in ss]
            denom = functools.reduce(jnp.add, [jnp.sum(e, axis=-1, keepdims=True) for e in es])
            o = functools.reduce(jnp.add, [_dot(e.astype(BF16), v) for e, v in zip(es, vs)])
            outs.append(o * (1.0 / denom))
    return jnp.concatenate(outs, axis=-1)


def _mixer_kernel(*refs, seq, latent):
    if latent:
        (mix_ref, ck_ref, cv_ref, dft_ref, bdc_ref, bds_ref, ws_ref, bsm_ref, clg_ref, clb_ref, o_ref) = refs
    else:
        (mix_ref, dft_ref, bdc_ref, bds_ref, ws_ref, bsm_ref, clg_ref, clb_ref, o_ref) = refs

    k_new = mix_ref[:, ATT_W:ATT_W + KV_W].astype(BF16)
    v_new = mix_ref[:, ATT_W + KV_W:ATT_W + 2 * KV_W].astype(BF16)
    if latent:
        k_parts = [ck_ref[...].astype(BF16), k_new]
        v_parts = [cv_ref[...].astype(BF16), v_new]
    else:
        k_parts, v_parts = [k_new], [v_new]

    def q_tile(t, carry):
        r0 = pl.multiple_of(t * TQ, TQ)
        q = (mix_ref[pl.ds(r0, TQ), :ATT_W] * (HEAD_DIM ** -0.5)).astype(BF16)
        o_ref[pl.ds(r0, TQ), :ATT_W] = _attention_tile(q, k_parts, v_parts).astype(BF16)
        return carry

    lax.fori_loop(0, seq // TQ, q_tile, 0)

    f_hi, f_lo = _split_bf16(mix_ref[:, ATT_W + 2 * KV_W:ATT_W + 2 * KV_W + FOURIER_W])
    bdc = bdc_ref[...]
    bds = bds_ref[...]
    y = jnp.concatenate([_dot(f_hi, bdc) + _dot(f_lo, bdc), _dot(f_hi, bds) + _dot(f_lo, bds)], axis=0)
    y_hi, y_lo = _split_bf16(y)
    dft = dft_ref[...]
    four = _dot(dft, y_hi) + _dot(dft, y_lo)
    o_ref[:, ATT_W:ATT_W + FOURIER_W] = four.astype(BF16)

    c0 = ATT_W + 2 * KV_W + FOURIER_W
    u = _gelu_tanh(mix_ref[:, c0:c0 + CHUNK_W])
    vn = (_ln(_gelu_tanh(mix_ref[:, c0 + CHUNK_W:c0 + 2 * CHUNK_W])) * clg_ref[0] + clb_ref[0]).astype(BF16)
    lane = lax.broadcasted_iota(jnp.int32, (CHUNK, CHUNK_W), 1)
    bsm = bsm_ref[...]
    for c in range(seq // CHUNK):
        vc = vn[c * CHUNK:(c + 1) * CHUNK]
        sv = bsm
        for g in range(N_CHUNK_GROUPS):
            vg = jnp.where(lane // CHUNK_GROUP_W == g, vc, jnp.zeros_like(vc))
            sv = sv + _dot(ws_ref[g], vg)
        o_ref[c * CHUNK:(c + 1) * CHUNK, ATT_W + FOURIER_W:] = (u[c * CHUNK:(c + 1) * CHUNK] * sv).astype(BF16)


def _mixer_call(layer, mix, ctx_k, ctx_v, dft, bdc, bds, ws, bsm, clg, clb, *, latent):
    seq = DEC_SEQ if latent else SEQ
    nb = DEC_BATCH if latent else BATCH
    row0 = (T_CTX // seq) if latent else 0
    const2 = lambda b: (0, 0)
    per_layer = lambda b: (layer, 0, 0)
    in_specs = [pl.BlockSpec((seq, MIX_W), lambda b: (row0 + b, 0))]
    args = [mix]
    if latent:
        in_specs += [pl.BlockSpec((None, None, PAST_LEN, KV_W), lambda b: (b, layer, 0, 0))] * 2
        args += [ctx_k, ctx_v]
    in_specs += [
        pl.BlockSpec((seq, 2 * seq), const2),
        pl.BlockSpec((FOURIER_W, FOURIER_W), const2),
        pl.BlockSpec((FOURIER_W, FOURIER_W), const2),
        pl.BlockSpec((None, N_CHUNK_GROUPS, CHUNK, CHUNK), lambda b: (layer, 0, 0, 0)),
        pl.BlockSpec((None, CHUNK, CHUNK_W), per_layer),
        pl.BlockSpec((1, 1, CHUNK_W), per_layer),
        pl.BlockSpec((1, 1, CHUNK_W), per_layer),
    ]
    args += [dft, bdc, bds, ws, bsm, clg, clb]
    return pl.pallas_call(
        functools.partial(_mixer_kernel, seq=seq, latent=latent),
        out_shape=jax.ShapeDtypeStruct((nb * seq, D_MODEL), BF16),
        grid=(nb,),
        in_specs=in_specs,
        out_specs=pl.BlockSpec((seq, D_MODEL), lambda b: (b, 0)),
        compiler_params=_cparams("arbitrary"),
        name="mixer_latent" if latent else "mixer_context",
    )(*args)


def _route(logits, bias_t):
    lt = logits.T[:N_EXPERTS]
    ex = jnp.exp(lt - jnp.max(lt, axis=0, keepdims=True))
    probs = ex / jnp.sum(ex, axis=0, keepdims=True)
    sel = probs + bias_t
    p = [probs[e:e + 1] for e in range(N_EXPERTS)]
    s = [sel[e:e + 1] for e in range(N_EXPERTS)]
    n = EXPERTS_PER_GROUP
    scores = []
    for g in range(N_EXPERT_GROUPS):
        pair = [s[g * n + a] + s[g * n + b] for a in range(n) for b in range(a + 1, n)]
        scores.append(functools.reduce(jnp.maximum, pair))
    best = jnp.zeros_like(scores[0], dtype=jnp.int32)
    best_score = scores[0]
    for g in range(1, N_EXPERT_GROUPS):
        better = scores[g] > best_score
        best = jnp.where(better, g, best)
        best_score = jnp.where(better, scores[g], best_score)
    cls = jnp.zeros_like(best_score)
    w_a = jnp.zeros_like(best_score)
    w_b = jnp.zeros_like(best_score)
    for g in range(N_EXPERT_GROUPS):
        in_g = best == g
        chosen = []
        for a in range(n):
            rank = jnp.zeros_like(best)
            for b in range(n):
                if b == a:
                    continue
                ahead = (s[g * n + b] > s[g * n + a]) if b > a else (s[g * n + b] >= s[g * n + a])
                rank = rank + ahead.astype(jnp.int32)
            chosen.append(jnp.logical_and(in_g, rank < 2))
        for k in range(N_PAIRS):
            pa, pb = p[g * n + PAIR_A[k]], p[g * n + PAIR_B[k]]
            hit = jnp.logical_and(chosen[PAIR_A[k]], chosen[PAIR_B[k]])
            wsum = pa + pb
            cls = jnp.where(hit, float(g * N_PAIRS + k), cls)
            w_a = jnp.where(hit, pa / wsum, w_a)
            w_b = jnp.where(hit, pb / wsum, w_b)
    return jnp.concatenate([cls, w_a, w_b, jnp.zeros((ROUTE_ROWS - 3, lt.shape[1]), F32)], axis=0)


def _merge_kernel(xc_ref, xl_ref, mixc_ref, mixl_ref, mod_ref, win_ref, wpa_ref, wpf_ref, wpc_ref, wo_ref,
                  l1g_ref, l1b_ref, wrh_ref, wrl_ref, bias_ref, x1_ref, ha_ref, rt_ref):
    x = _group_tile(xc_ref, xl_ref)
    m = mod_ref[0, 0]
    h = (_ln(x) * (1.0 + m[1:2]) + m[0:1]).astype(BF16)
    mixed = _group_tile(mixc_ref, mixl_ref)
    merged = None
    branch_w = ((0, ATT_W, wpa_ref), (ATT_W, FOURIER_W, wpf_ref), (ATT_W + FOURIER_W, CHUNK_W, wpc_ref))
    for b, (c0, width, w_ref) in enumerate(branch_w):
        gate = _sigmoid(_dot(h, win_ref[:, MIX_W + b * D_MODEL:MIX_W + (b + 1) * D_MODEL]))
        term = gate * _dot(mixed[:, c0:c0 + width], w_ref[...])
        merged = term if merged is None else merged + term
    mix = _dot(merged.astype(BF16), wo_ref[...])
    x1 = _ln(ALPHA * x + m[2:3] * mix) * l1g_ref[0] + l1b_ref[0]
    x1_ref[...] = x1
    h2 = _ln(x1) * (1.0 + m[4:5]) + m[3:4]
    h_hi, h_lo = _split_bf16(h2)
    wrh = wrh_ref[...]
    logits = _dot(h_hi, wrh) + _dot(h_lo, wrh) + _dot(h_hi, wrl_ref[...])
    rt = _route(logits, bias_ref[...])
    rt_ref[...] = rt
    for k in range(D_CHUNKS):
        ha_ref[:, k, :] = h2[:, k * LANES:(k + 1) * LANES]
    ha_ref[:, ROUTE_ROW, :] = jnp.concatenate([rt, jnp.zeros((LANES - ROUTE_ROWS, TM), F32)], axis=0).T


def _merge_call(layer, x, mixed_ctx, mixed_lat, mod, w_in_b, wpa, wpf, wpc, wo, l1g, l1b, wrh, wrl, bias_t):
    const2 = lambda i: (0, 0)
    per_layer = lambda i: (layer, 0, 0)
    tok = lambda i: (i, 0)
    return pl.pallas_call(
        _merge_kernel,
        out_shape=(jax.ShapeDtypeStruct((T_ALL, D_MODEL), F32),
                   jax.ShapeDtypeStruct((T_ALL, HA_ROWS, LANES), F32),
                   jax.ShapeDtypeStruct((ROUTE_ROWS, T_ALL), F32)),
        grid=(T_ALL // TM,),
        in_specs=_group_specs() + _group_specs() + [
            pl.BlockSpec((1, 1, N_MOD, D_MODEL), lambda i: (layer, _cond_row(i, TM), 0, 0)),
            pl.BlockSpec((None, D_MODEL, IN_W), per_layer, pipeline_mode=pl.Buffered(1)),
            pl.BlockSpec((None, ATT_W, D_MODEL), per_layer, pipeline_mode=pl.Buffered(1)),
            pl.BlockSpec((None, FOURIER_W, D_MODEL), per_layer, pipeline_mode=pl.Buffered(1)),
            pl.BlockSpec((None, CHUNK_W, D_MODEL), per_layer, pipeline_mode=pl.Buffered(1)),
            pl.BlockSpec((None, D_MODEL, D_MODEL), per_layer, pipeline_mode=pl.Buffered(1)),
            pl.BlockSpec((1, 1, D_MODEL), per_layer),
            pl.BlockSpec((1, 1, D_MODEL), per_layer),
            pl.BlockSpec((D_MODEL, ROUTER_PAD), const2),
            pl.BlockSpec((D_MODEL, ROUTER_PAD), const2),
            pl.BlockSpec((N_EXPERTS, TM), const2),
        ],
        out_specs=(pl.BlockSpec((TM, D_MODEL), tok),
                   pl.BlockSpec((TM, HA_ROWS, LANES), lambda i: (i, 0, 0)),
                   pl.BlockSpec((ROUTE_ROWS, TM), lambda i: (0, i))),
        compiler_params=_cparams("arbitrary"),
        name="merge_route",
    )(*x, mixed_ctx, mixed_lat, mod, w_in_b, wpa, wpf, wpc, wo, l1g, l1b, wrh, wrl, bias_t)


def _sort_kernel(rt_ref, tri_ref, low_ref, pos_ref, tile_ref):
    crow = lax.broadcasted_iota(jnp.int32, (CLASS_PAD, SORT_BLK), 0)
    tri = tri_ref[...]
    n_blk = T_ALL // SORT_BLK
    carry = jnp.zeros((CLASS_PAD, 1), F32)
    ranks = []
    for b in range(n_blk):
        hot = rt_ref[0:1, b * SORT_BLK:(b + 1) * SORT_BLK].astype(jnp.int32) == crow
        hot_f = jnp.where(hot, 1.0, 0.0)
        before = _dot(hot_f.astype(BF16), tri) + carry
        ranks.append(jnp.sum(jnp.where(hot, before, 0.0), axis=0, keepdims=True))
        carry = carry + jnp.sum(hot_f, axis=1, keepdims=True)
    padded = jnp.floor((carry + (TM_MOE - 1.0)) * (1.0 / TM_MOE)) * TM_MOE
    padded = jnp.broadcast_to(padded, (CLASS_PAD, LANES))
    offs = _dot(low_ref[...], padded.astype(BF16))
    for b in range(n_blk):
        hot = rt_ref[0:1, b * SORT_BLK:(b + 1) * SORT_BLK].astype(jnp.int32) == crow
        base = jnp.sum(jnp.where(hot, offs[:, 0:1], 0.0), axis=0, keepdims=True)
        pos_ref[:, b * SORT_BLK:(b + 1) * SORT_BLK] = (base + ranks[b]).astype(jnp.int32)

    start = lax.broadcasted_iota(jnp.int32, (1, LANES), 1).astype(F32) * TM_MOE
    is_class = lax.broadcasted_iota(jnp.int32, (CLASS_PAD, LANES), 0) < N_CLASSES
    ends = jnp.where(is_class, offs + padded, 0.0)
    total = jnp.max(ends, axis=0, keepdims=True)
    valid = start < total
    tcls = jnp.sum(jnp.where(jnp.logical_and(is_class, ends <= start), 1.0, 0.0), axis=0, keepdims=True)
    last = jnp.max(jnp.where(valid, tcls, 0.0), axis=1, keepdims=True)
    tcls = jnp.where(valid, tcls, last)
    grp = functools.reduce(jnp.add, [jnp.where(tcls >= g * N_PAIRS, 1.0, 0.0) for g in range(1, N_EXPERT_GROUPS)])
    pair = tcls - grp * N_PAIRS
    slot_a = functools.reduce(jnp.add, [jnp.where(pair == k, float(PAIR_A[k]), 0.0) for k in range(N_PAIRS)])
    slot_b = functools.reduce(jnp.add, [jnp.where(pair == k, float(PAIR_B[k]), 0.0) for k in range(N_PAIRS)])
    crow_t = lax.broadcasted_iota(jnp.int32, (CLASS_PAD, LANES), 0).astype(F32)
    real_end = jnp.sum(jnp.where(crow_t == tcls, offs + carry, 0.0), axis=0, keepdims=True)
    n_rows = jnp.where(valid, jnp.clip(real_end - start, 0.0, float(TM_MOE)), 0.0)
    rows = [grp * EXPERTS_PER_GROUP + slot_a, grp * EXPERTS_PER_GROUP + slot_b, n_rows]
    tile_ref[...] = jnp.concatenate(rows + [jnp.zeros((8 - len(rows), LANES), F32)], axis=0).astype(jnp.int32)


def _sort_call(rt, tri, low):
    return pl.pallas_call(
        _sort_kernel,
        out_shape=(jax.ShapeDtypeStruct((1, T_ALL), jnp.int32),
                   jax.ShapeDtypeStruct((8, LANES), jnp.int32)),
        name="route_sort",
    )(rt, tri, low)


def _moe_kernel(ta_ref, tb_ref, nrow_ref, pos_ref, ha_ref, wga_ref, wua_ref, wda_ref, wgb_ref, wub_ref, wdb_ref,
                out_ref, xbuf, ybuf, src_ref, gsem, ssem):
    j = pl.program_id(0)
    slot = j % 2
    del ta_ref, tb_ref

    def n_rows(tile):
        inside = jnp.logical_and(tile >= 0, tile < N_TILES)
        return jnp.where(inside, nrow_ref[jnp.clip(tile, 0, N_TILES - 1)], 0)

    def token_of(tile, r):
        return src_ref[tile * TM_MOE + jnp.minimum(r, nrow_ref[tile] - 1)]

    def gather_copy(tile, buf_slot, r):
        return pltpu.make_async_copy(ha_ref.at[token_of(tile, r)], xbuf.at[buf_slot, r], gsem.at[buf_slot])

    def scatter_copy(tile, buf_slot, r):
        dst = jnp.where(r < nrow_ref[tile], token_of(tile, r), T_ALL + buf_slot * TM_MOE + r)
        return pltpu.make_async_copy(ybuf.at[buf_slot, r], out_ref.at[dst], ssem.at[buf_slot])

    def for_groups(tile, fn):
        def body(g, c):
            for k in range(COPY_GROUP):
                fn(g * COPY_GROUP + k, k % 2)
            return c
        lax.fori_loop(0, (n_rows(tile) + COPY_GROUP - 1) // COPY_GROUP, body, 0)

    def start_gather(tile, buf_slot):
        for_groups(tile, lambda r, pri: gather_copy(tile, buf_slot, r).start(priority=pri))

    def retire(tile, copy_of):
        for_groups(tile, lambda r, pri: copy_of(0).wait())

    @pl.when(j == 0)
    def _():
        def place(t, c):
            src_ref[pos_ref[t]] = t
            return c
        lax.fori_loop(0, T_ALL, place, 0, unroll=8)
        xbuf[...] = jnp.zeros(xbuf.shape, F32)
        start_gather(0, 0)
        ybuf[...] = jnp.zeros(ybuf.shape, F32)
        for s in range(2):
            park = pltpu.make_async_copy(ybuf.at[s], out_ref.at[pl.ds(T_ALL + s * TM_MOE, TM_MOE)], ssem.at[s])
            park.start()
            park.wait()

    @pl.when(n_rows(j + 1) > 0)
    def _():
        start_gather(j + 1, 1 - slot)

    @pl.when(n_rows(j - 2) > 0)
    def _():
        retire(j - 2, lambda r: scatter_copy(j - 2, slot, r))

    @pl.when(n_rows(j) > 0)
    def _():
        retire(j, lambda r: gather_copy(j, slot, r))
        h = jnp.concatenate([xbuf[slot, :, k, :] for k in range(D_CHUNKS)], axis=-1).astype(BF16)
        route = xbuf[slot, :, ROUTE_ROW, :]

        def expert(wg_ref, wu_ref, wd_ref, w):
            a = _dot(h, wg_ref[...])
            u = _dot(h, wu_ref[...])
            return _dot(((a * _sigmoid(a)) * u * w).astype(BF16), wd_ref[...])

        o = expert(wga_ref, wua_ref, wda_ref, route[:, 1:2]) + expert(wgb_ref, wub_ref, wdb_ref, route[:, 2:3])
        for k in range(D_CHUNKS):
            ybuf[slot, :, k, :] = o[:, k * LANES:(k + 1) * LANES]
        for_groups(j, lambda r, pri: scatter_copy(j, slot, r).start(priority=pri))

    @pl.when(j == N_TILES - 1)
    def _():
        @pl.when(n_rows(j - 1) > 0)
        def _():
            retire(j - 1, lambda r: scatter_copy(j - 1, 1 - slot, r))

        @pl.when(n_rows(j) > 0)
        def _():
            retire(j, lambda r: scatter_copy(j, slot, r))


def _moe_call(layer, tiles, pos, ha, wge, wue, wde):
    up = (None, None, D_MODEL, D_EXPERT)
    down = (None, None, D_EXPERT, D_MODEL)
    slot_a = lambda j, ta, tb, va, ps: (layer, ta[j], 0, 0)
    slot_b = lambda j, ta, tb, va, ps: (layer, tb[j], 0, 0)
    grid_spec = pltpu.PrefetchScalarGridSpec(
        num_scalar_prefetch=4,
        grid=(N_TILES,),
        in_specs=[
            pl.BlockSpec(memory_space=pl.ANY),
            pl.BlockSpec(up, slot_a), pl.BlockSpec(up, slot_a), pl.BlockSpec(down, slot_a),
            pl.BlockSpec(up, slot_b), pl.BlockSpec(up, slot_b), pl.BlockSpec(down, slot_b),
        ],
        out_specs=pl.BlockSpec(memory_space=pl.ANY),
        scratch_shapes=[
            pltpu.VMEM((2, TM_MOE, HA_ROWS, LANES), F32),
            pltpu.VMEM((2, TM_MOE, D_CHUNKS, LANES), F32),
            pltpu.SMEM((R_ROWS,), jnp.int32),
            pltpu.SemaphoreType.DMA((2,)),
            pltpu.SemaphoreType.DMA((2,)),
        ],
    )
    return pl.pallas_call(
        _moe_kernel,
        out_shape=jax.ShapeDtypeStruct((T_ALL + N_PARK, D_CHUNKS, LANES), F32),
        grid_spec=grid_spec,
        compiler_params=pltpu.CompilerParams(dimension_semantics=("arbitrary",), vmem_limit_bytes=V7X_VMEM_LIMIT,
                                             has_side_effects=True),
        name="moe_pairs",
    )(tiles[0, :N_TILES], tiles[1, :N_TILES], tiles[2, :N_TILES], pos, ha, wge, wue, wde, wge, wue, wde)


def _post_kernel(x1_ref, moe_ref, mod_ref, l2g_ref, l2b_ref, oc_ref, ol_ref):
    m = mod_ref[0, 0]
    moe = jnp.concatenate([moe_ref[:, k, :] for k in range(D_CHUNKS)], axis=-1)
    y = _ln(ALPHA * x1_ref[...] + m[5:6] * moe) * l2g_ref[0] + l2b_ref[0]
    i = pl.program_id(0)

    @pl.when(i < N_CTX_TILES)
    def _():
        oc_ref[...] = y

    @pl.when(i >= N_CTX_TILES)
    def _():
        ol_ref[...] = y


def _post_call(layer, x1, moe, mod, l2g, l2b):
    tok = lambda i: (i, 0)
    per_layer = lambda i: (layer, 0, 0)
    return pl.pallas_call(
        _post_kernel,
        out_shape=(jax.ShapeDtypeStruct((T_CTX, D_MODEL), F32), jax.ShapeDtypeStruct((T_LAT, D_MODEL), F32)),
        grid=(T_ALL // TM,),
        in_specs=[
            pl.BlockSpec((TM, D_MODEL), tok),
            pl.BlockSpec((TM, D_CHUNKS, LANES), lambda i: (i, 0, 0)),
            pl.BlockSpec((1, 1, N_MOD, D_MODEL), lambda i: (layer, _cond_row(i, TM), 0, 0)),
            pl.BlockSpec((1, 1, D_MODEL), per_layer),
            pl.BlockSpec((1, 1, D_MODEL), per_layer),
        ],
        out_specs=tuple(_group_specs()),
        compiler_params=_cparams("arbitrary"),
        name="post_moe",
    )(x1, moe, mod, l2g, l2b)


def _rope_tables():
    pos = np.arange(DEC_SEQ)
    quarter = HEAD_DIM // 4
    inv = ROPE_THETA ** (-np.arange(quarter, dtype=np.float64) / quarter)
    ang_r = (pos // GRID_W)[:, None] * inv[None, :]
    ang_c = (pos % GRID_W)[:, None] * inv[None, :]
    cos = np.concatenate([np.cos(ang_r)] * 2 + [np.cos(ang_c)] * 2, axis=-1)
    sin = np.concatenate([-np.sin(ang_r), np.sin(ang_r), -np.sin(ang_c), np.sin(ang_c)], axis=-1)
    return (jnp.asarray(np.tile(cos, (1, N_Q_HEADS)).astype(np.float32)),
            jnp.asarray(np.tile(sin, (1, N_Q_HEADS)).astype(np.float32)))


def _dft_mats(n, scale):
    j = np.arange(n)
    ang = ((j[:, None] * j[None, :]) % n) * (2 * np.pi / n)
    return np.cos(ang) * scale, np.sin(ang) * scale


def _block_diag(m, reps):
    return np.kron(np.eye(reps), m)


def _const_bf16(a):
    return jnp.asarray(np.asarray(a, np.float32)).astype(BF16)


def kernel(x_prompt, x_sample, cache_k, cache_v, c, c_ctx, w_in, q_norm_g, k_norm_g, w_proj_att,
           w_proj_fourier, w_proj_chunk, w_out, chunk_ln_g, chunk_ln_b, chunk_ws, chunk_bs, w_ada, b_ada,
           ln1_g, ln1_b, ln2_g, ln2_b, w_router, router_bias, w_gate_e, w_up_e, w_down_e):
    x = (x_prompt.reshape(T_CTX, D_MODEL), x_sample.reshape(T_LAT, D_MODEL))
    cond = jnp.concatenate([c_ctx[None, :], c, jnp.zeros((N_COND - 1 - DEC_BATCH, D_MODEL), F32)], axis=0)
    mod = _mod_call(cond, w_ada, b_ada).reshape(DEPTH, N_COND, N_MOD, D_MODEL)

    cos_t, sin_t = _rope_tables()
    bd_heads = _const_bf16(_block_diag(np.ones((HEAD_DIM, HEAD_DIM)), N_Q_HEADS))
    c64, s64 = _dft_mats(FOURIER_GROUP_W, 1.0)
    bdc = _const_bf16(_block_diag(c64, FOURIER_W // FOURIER_GROUP_W))
    bds = _const_bf16(_block_diag(s64, FOURIER_W // FOURIER_GROUP_W))
    dft = {}
    for seq in (SEQ, DEC_SEQ):
        cs, ss = _dft_mats(seq, 1.0 / math.sqrt(seq * FOURIER_GROUP_W))
        dft[seq] = _const_bf16(np.concatenate([cs, -ss], axis=1))
    ctx_k = cache_k.reshape(DEC_BATCH, DEPTH, PAST_LEN, KV_W)
    ctx_v = cache_v.reshape(DEC_BATCH, DEPTH, PAST_LEN, KV_W)
    wr = jnp.pad(w_router, ((0, 0), (0, ROUTER_PAD - N_EXPERTS)))
    wr_hi = wr.astype(BF16)
    wr_lo = (wr - wr_hi.astype(F32)).astype(BF16)
    bias_t = jnp.broadcast_to(router_bias[:, None], (N_EXPERTS, TM))
    tri = _const_bf16(np.triu(np.ones((SORT_BLK, SORT_BLK)), 1))
    low = _const_bf16(np.tril(np.ones((CLASS_PAD, CLASS_PAD)), -1))

    w_in_b = w_in.astype(BF16)
    wpa, wpf, wpc, wo = (w.astype(BF16) for w in (w_proj_att, w_proj_fourier, w_proj_chunk, w_out))
    wge, wue, wde = (w.astype(BF16) for w in (w_gate_e, w_up_e, w_down_e))
    qg = jnp.tile(q_norm_g, (1, N_Q_HEADS))[:, None, :]
    kg = jnp.tile(k_norm_g, (1, N_KV_HEADS))[:, None, :]
    ws = chunk_ws.astype(BF16)
    bsm = jnp.repeat(jnp.swapaxes(chunk_bs, 1, 2), CHUNK_GROUP_W, axis=2)
    row3 = lambda a: a[:, None, :]

    new_k, new_v = [], []
    for l in range(DEPTH):
        mix = _proj_call(l, x, mod, w_in_b, qg, kg, cos_t, sin_t, bd_heads)
        new_k.append(mix[:T_CTX, ATT_W:ATT_W + KV_W].reshape(BATCH, SEQ, N_KV_HEADS, HEAD_DIM))
        new_v.append(mix[:T_CTX, ATT_W + KV_W:ATT_W + 2 * KV_W].reshape(BATCH, SEQ, N_KV_HEADS, HEAD_DIM))
        mixer_args = (bdc, bds, ws, bsm, row3(chunk_ln_g), row3(chunk_ln_b))
        mixed_ctx = _mixer_call(l, mix, None, None, dft[SEQ], *mixer_args, latent=False)
        mixed_lat = _mixer_call(l, mix, ctx_k, ctx_v, dft[DEC_SEQ], *mixer_args, latent=True)
        x1, ha, rt = _merge_call(l, x, mixed_ctx, mixed_lat, mod, w_in_b, wpa, wpf, wpc, wo,
                                 row3(ln1_g), row3(ln1_b), wr_hi, wr_lo, bias_t)
        pos2d, tiles = _sort_call(rt, tri, low)
        moe = _moe_call(l, tiles, pos2d[0], ha, wge, wue, wde)
        x = _post_call(l, x1, moe, mod, row3(ln2_g), row3(ln2_b))

    y_prompt = x[0].reshape(BATCH, SEQ, D_MODEL)
    y_sample = x[1].reshape(DEC_BATCH, DEC_SEQ, D_MODEL)
    return (y_prompt, y_sample, jnp.stack(new_k, axis=1), jnp.stack(new_v, axis=1))
```

```python
import functools
import math

import jax
import jax.numpy as jnp
import numpy as np
from jax import lax
from jax.experimental import pallas as pl
from jax.experimental.pallas import tpu as pltpu

F32 = jnp.float32
BF16 = jnp.bfloat16

D_MODEL = 1024
BATCH = 16
SEQ = 256
DEPTH = 2
DEC_BATCH = 2
DEC_SEQ = 1024
PAST_LEN = 256
GRID_W = 64
N_Q_HEADS = 8
N_KV_HEADS = 2
HEAD_DIM = 64
Q_PER_KV = N_Q_HEADS // N_KV_HEADS
ATT_W = N_Q_HEADS * HEAD_DIM
KV_W = N_KV_HEADS * HEAD_DIM
ROPE_THETA = 10000.0
FOURIER_GROUP_W = 64
FOURIER_W = 256
N_CHUNK_GROUPS = 4
CHUNK_GROUP_W = 64
CHUNK_W = 256
CHUNK = 128
N_EXPERTS = 16
N_EXPERT_GROUPS = 4
EXPERTS_PER_GROUP = 4
D_EXPERT = 512
ALPHA = (2 * DEPTH) ** 0.25
LN_EPS = 1e-6
RMS_EPS = 1e-6

LANES = 128
SUBLANES = 8
T_CTX = BATCH * SEQ
T_LAT = DEC_BATCH * DEC_SEQ
T_ALL = T_CTX + T_LAT
IN_W = ATT_W + 2 * KV_W + FOURIER_W + 2 * CHUNK_W + 3 * D_MODEL
MIX_W = ATT_W + 2 * KV_W + FOURIER_W + 2 * CHUNK_W
N_MOD = 6
N_COND = 8
ROUTER_PAD = LANES
TM = 512
TQ = 256
MERGE_SUB = 256
V7X_VMEM_LIMIT = 56 * 1024 * 1024

PAIR_A = (0, 0, 0, 1, 1, 3)
PAIR_B = (1, 2, 3, 3, 2, 2)
N_PAIRS = len(PAIR_A)
N_CLASSES = N_EXPERT_GROUPS * N_PAIRS
CLASS_PAD = 32
ROUTE_ROWS = 8
TM_MOE = 256
N_TILES = T_ALL // TM_MOE + N_CLASSES
R_ROWS = N_TILES * TM_MOE
SORT_BLK = 512
D_CHUNKS = D_MODEL // LANES
N_PARK = 2 * TM_MOE
COPY_GROUP = 8


def _cparams(*sem):
    return pltpu.CompilerParams(dimension_semantics=sem, vmem_limit_bytes=V7X_VMEM_LIMIT)


def _cond_row(i, tm):
    n_ctx = T_CTX // tm
    return jnp.where(i < n_ctx, 0, 1 + (i - n_ctx) // (DEC_SEQ // tm))


N_CTX_TILES = T_CTX // TM


def _group_specs():
    return [pl.BlockSpec((TM, D_MODEL), lambda i: (jnp.minimum(i, N_CTX_TILES - 1), 0)),
            pl.BlockSpec((TM, D_MODEL), lambda i: (jnp.maximum(i - N_CTX_TILES, 0), 0))]


def _group_tile(ctx_ref, lat_ref):
    return jnp.where(pl.program_id(0) < N_CTX_TILES, ctx_ref[...], lat_ref[...])


def _ln(x):
    mu = jnp.mean(x, axis=-1, keepdims=True)
    xc = x - mu
    var = jnp.mean(xc * xc, axis=-1, keepdims=True)
    return xc * lax.rsqrt(var + LN_EPS)


def _split_bf16(x):
    hi = x.astype(BF16)
    lo = (x - hi.astype(F32)).astype(BF16)
    return hi, lo


def _dot(a, b):
    return jnp.dot(a, b, preferred_element_type=F32)


def _sigmoid(x):
    return 1.0 / (1.0 + jnp.exp(-x))


def _gelu_tanh(x):
    c = np.float32(np.sqrt(2 / np.pi))
    return x * (0.5 * (1.0 + jnp.tanh(c * (x + 0.044715 * (x * x * x)))))


def _mod_kernel(c_ref, w_ref, b_ref, o_ref):
    c = c_ref[...]
    s = c * _sigmoid(c)
    o_ref[0] = _dot(s.astype(BF16), w_ref[0].astype(BF16)) + b_ref[0]


def _mod_call(cond, w_ada, b_ada):
    tn = 1536
    n = N_MOD * D_MODEL
    return pl.pallas_call(
        _mod_kernel,
        out_shape=jax.ShapeDtypeStruct((DEPTH, N_COND, n), F32),
        grid=(DEPTH, n // tn),
        in_specs=[
            pl.BlockSpec((N_COND, D_MODEL), lambda l, j: (0, 0)),
            pl.BlockSpec((1, D_MODEL, tn), lambda l, j: (l, 0, j)),
            pl.BlockSpec((1, 1, tn), lambda l, j: (l, 0, j)),
        ],
        out_specs=pl.BlockSpec((1, N_COND, tn), lambda l, j: (l, 0, j)),
        compiler_params=_cparams("arbitrary", "arbitrary"),
        name="adaln_mod",
    )(cond, w_ada, b_ada.reshape(DEPTH, 1, n))


def _head_rms(q, bd, g):
    hi, lo = _split_bf16(q * q)
    ssum = _dot(hi, bd) + _dot(lo, bd)
    return q * lax.rsqrt(ssum * (1.0 / HEAD_DIM) + RMS_EPS) * g


def _rope(x, cos, sin_signed):
    w = x.shape[-1]
    lane = lax.broadcasted_iota(jnp.int32, x.shape, 1)
    swapped = jnp.where((lane % 32) < 16, pltpu.roll(x, w - 16, 1), pltpu.roll(x, 16, 1))
    return x * cos + swapped * sin_signed


def _proj_kernel(xc_ref, xl_ref, mod_ref, w_ref, qg_ref, kg_ref, cos_ref, sin_ref, bd_ref, o_ref):
    i = pl.program_id(0)
    m = mod_ref[0, 0]
    h = _ln(_group_tile(xc_ref, xl_ref)) * (1.0 + m[1:2]) + m[0:1]
    proj = _dot(h.astype(BF16), w_ref[...])
    o_ref[:, ATT_W + KV_W:] = proj[:, ATT_W + KV_W:]
    bd = bd_ref[...]
    qn = _head_rms(proj[:, :ATT_W], bd, qg_ref[0])
    kn = _head_rms(proj[:, ATT_W:ATT_W + KV_W], bd[:KV_W, :KV_W], kg_ref[0])

    @pl.when(i < T_CTX // TM)
    def _():
        o_ref[:, :ATT_W] = qn
        o_ref[:, ATT_W:ATT_W + KV_W] = kn

    @pl.when(i >= T_CTX // TM)
    def _():
        cos = cos_ref[...]
        sin = sin_ref[...]
        o_ref[:, :ATT_W] = _rope(qn, cos, sin)
        o_ref[:, ATT_W:ATT_W + KV_W] = _rope(kn, cos[:, :KV_W], sin[:, :KV_W])


def _proj_call(layer, x, mod, w_in_b, qg, kg, cos_t, sin_t, bd):
    n_ctx = T_CTX // TM
    per_seq = DEC_SEQ // TM

    def rope_idx(i):
        return (jnp.where(i < n_ctx, 0, (i - n_ctx) % per_seq), 0)

    return pl.pallas_call(
        _proj_kernel,
        out_shape=jax.ShapeDtypeStruct((T_ALL, MIX_W), F32),
        grid=(T_ALL // TM,),
        in_specs=_group_specs() + [
            pl.BlockSpec((1, 1, N_MOD, D_MODEL), lambda i: (layer, _cond_row(i, TM), 0, 0)),
            pl.BlockSpec((None, D_MODEL, MIX_W), lambda i: (layer, 0, 0)),
            pl.BlockSpec((1, 1, ATT_W), lambda i: (layer, 0, 0)),
            pl.BlockSpec((1, 1, KV_W), lambda i: (layer, 0, 0)),
            pl.BlockSpec((TM, ATT_W), rope_idx),
            pl.BlockSpec((TM, ATT_W), rope_idx),
            pl.BlockSpec((ATT_W, ATT_W), lambda i: (0, 0)),
        ],
        out_specs=pl.BlockSpec((TM, MIX_W), lambda i: (i, 0)),
        compiler_params=_cparams("arbitrary"),
        name="in_proj",
    )(*x, mod, w_in_b, qg, kg, cos_t, sin_t, bd)


def _attention_tile(q, k_parts, v_parts):
    outs = []
    for g in range(N_KV_HEADS):
        ks = [k[:, g * HEAD_DIM:(g + 1) * HEAD_DIM] for k in k_parts]
        vs = [v[:, g * HEAD_DIM:(g + 1) * HEAD_DIM] for v in v_parts]
        for hh in range(Q_PER_KV):
            h = g * Q_PER_KV + hh
            qh = q[:, h * HEAD_DIM:(h + 1) * HEAD_DIM]
            ss = [lax.dot_general(qh, k, (((1,), (1,)), ((), ())), preferred_element_type=F32) for k in ks]
            m = functools.reduce(jnp.maximum, [jnp.max(s, axis=-1, keepdims=True) for s in ss])
            es = [jnp.exp(s - m) for s in ss]
            denom = functools.reduce(jnp.add, [jnp.sum(e, axis=-1, keepdims=True) for e in es])
            o = functools.reduce(jnp.add, [_dot(e.astype(BF16), v) for e, v in zip(es, vs)])
            outs.append(o * (1.0 / denom))
    return jnp.concatenate(outs, axis=-1)


def _mixer_kernel(*refs, seq, latent):
    if latent:
        (mix_ref, ck_ref, cv_ref, dft_ref, bdc_ref, bds_ref, ws_ref, bsm_ref, clg_ref, clb_ref, o_ref) = refs
    else:
        (mix_ref, dft_ref, bdc_ref, bds_ref, ws_ref, bsm_ref, clg_ref, clb_ref, o_ref) = refs

    k_new = mix_ref[:, ATT_W:ATT_W + KV_W].astype(BF16)
    v_new = mix_ref[:, ATT_W + KV_W:ATT_W + 2 * KV_W].astype(BF16)
    if latent:
        k_parts = [ck_ref[...].astype(BF16), k_new]
        v_parts = [cv_ref[...].astype(BF16), v_new]
    else:
        k_parts, v_parts = [k_new], [v_new]

    def q_tile(t, carry):
        r0 = pl.multiple_of(t * TQ, TQ)
        q = (mix_ref[pl.ds(r0, TQ), :ATT_W] * (HEAD_DIM ** -0.5)).astype(BF16)
        o_ref[pl.ds(r0, TQ), :ATT_W] = _attention_tile(q, k_parts, v_parts).astype(BF16)
        return carry

    lax.fori_loop(0, seq // TQ, q_tile, 0)

    f_hi, f_lo = _split_bf16(mix_ref[:, ATT_W + 2 * KV_W:ATT_W + 2 * KV_W + FOURIER_W])
    bdc = bdc_ref[...]
    bds = bds_ref[...]
    y = jnp.concatenate([_dot(f_hi, bdc) + _dot(f_lo, bdc), _dot(f_hi, bds) + _dot(f_lo, bds)], axis=0)
    y_hi, y_lo = _split_bf16(y)
    dft = dft_ref[...]
    four = _dot(dft, y_hi) + _dot(dft, y_lo)
    o_ref[:, ATT_W:ATT_W + FOURIER_W] = four.astype(BF16)

    c0 = ATT_W + 2 * KV_W + FOURIER_W
    u = _gelu_tanh(mix_ref[:, c0:c0 + CHUNK_W])
    vn = (_ln(_gelu_tanh(mix_ref[:, c0 + CHUNK_W:c0 + 2 * CHUNK_W])) * clg_ref[0] + clb_ref[0]).astype(BF16)
    lane = lax.broadcasted_iota(jnp.int32, (CHUNK, CHUNK_W), 1)
    bsm = bsm_ref[...]
    for c in range(seq // CHUNK):
        vc = vn[c * CHUNK:(c + 1) * CHUNK]
        sv = bsm
        for g in range(N_CHUNK_GROUPS):
            vg = jnp.where(lane // CHUNK_GROUP_W == g, vc, jnp.zeros_like(vc))
            sv = sv + _dot(ws_ref[g], vg)
        o_ref[c * CHUNK:(c + 1) * CHUNK, ATT_W + FOURIER_W:] = (u[c * CHUNK:(c + 1) * CHUNK] * sv).astype(BF16)


def _mixer_call(layer, mix, ctx_k, ctx_v, dft, bdc, bds, ws, bsm, clg, clb, *, latent):
    seq = DEC_SEQ if latent else SEQ
    nb = DEC_BATCH if latent else BATCH
    row0 = (T_CTX // seq) if latent else 0
    const2 = lambda b: (0, 0)
    per_layer = lambda b: (layer, 0, 0)
    in_specs = [pl.BlockSpec((seq, MIX_W), lambda b: (row0 + b, 0))]
    args = [mix]
    if latent:
        in_specs += [pl.BlockSpec((None, None, PAST_LEN, KV_W), lambda b: (b, layer, 0, 0))] * 2
        args += [ctx_k, ctx_v]
    in_specs += [
        pl.BlockSpec((seq, 2 * seq), const2),
        pl.BlockSpec((FOURIER_W, FOURIER_W), const2),
        pl.BlockSpec((FOURIER_W, FOURIER_W), const2),
        pl.BlockSpec((None, N_CHUNK_GROUPS, CHUNK, CHUNK), lambda b: (layer, 0, 0, 0)),
        pl.BlockSpec((None, CHUNK, CHUNK_W), per_layer),
        pl.BlockSpec((1, 1, CHUNK_W), per_layer),
        pl.BlockSpec((1, 1, CHUNK_W), per_layer),
    ]
    args += [dft, bdc, bds, ws, bsm, clg, clb]
    return pl.pallas_call(
        functools.partial(_mixer_kernel, seq=seq, latent=latent),
        out_shape=jax.ShapeDtypeStruct((nb * seq, D_MODEL), BF16),
        grid=(nb,),
        in_specs=in_specs,
        out_specs=pl.BlockSpec((seq, D_MODEL), lambda b: (b, 0)),
        compiler_params=_cparams("arbitrary"),
        name="mixer_latent" if latent else "mixer_context",
    )(*args)


def _route(logits, bias_t):
    lt = logits.T[:N_EXPERTS]
    ex = jnp.exp(lt - jnp.max(lt, axis=0, keepdims=True))
    probs = ex / jnp.sum(ex, axis=0, keepdims=True)
    sel = probs + bias_t
    p = [probs[e:e + 1] for e in range(N_EXPERTS)]
    s = [sel[e:e + 1] for e in range(N_EXPERTS)]
    n = EXPERTS_PER_GROUP
    scores = []
    for g in range(N_EXPERT_GROUPS):
        pair = [s[g * n + a] + s[g * n + b] for a in range(n) for b in range(a + 1, n)]
        scores.append(functools.reduce(jnp.maximum, pair))
    best = jnp.zeros_like(scores[0], dtype=jnp.int32)
    best_score = scores[0]
    for g in range(1, N_EXPERT_GROUPS):
        better = scores[g] > best_score
        best = jnp.where(better, g, best)
        best_score = jnp.where(better, scores[g], best_score)
    cls = jnp.zeros_like(best_score)
    w_a = jnp.zeros_like(best_score)
    w_b = jnp.zeros_like(best_score)
    for g in range(N_EXPERT_GROUPS):
        in_g = best == g
        chosen = []
        for a in range(n):
            rank = jnp.zeros_like(best)
            for b in range(n):
                if b == a:
                    continue
                ahead = (s[g * n + b] > s[g * n + a]) if b > a else (s[g * n + b] >= s[g * n + a])
                rank = rank + ahead.astype(jnp.int32)
            chosen.append(jnp.logical_and(in_g, rank < 2))
        for k in range(N_PAIRS):
            pa, pb = p[g * n + PAIR_A[k]], p[g * n + PAIR_B[k]]
            hit = jnp.logical_and(chosen[PAIR_A[k]], chosen[PAIR_B[k]])
            wsum = pa + pb
            cls = jnp.where(hit, float(g * N_PAIRS + k), cls)
            w_a = jnp.where(hit, pa / wsum, w_a)
            w_b = jnp.where(hit, pb / wsum, w_b)
    return jnp.concatenate([cls, w_a, w_b, jnp.zeros((ROUTE_ROWS - 3, lt.shape[1]), F32)], axis=0)


def _merge_kernel(xc_ref, xl_ref, mixc_ref, mixl_ref, mod_ref, win_ref, wpa_ref, wpf_ref, wpc_ref, wo_ref,
                  l1g_ref, l1b_ref, wrh_ref, wrl_ref, bias_ref, x1_ref, h2_ref, rt_ref, rc_ref):
    m = mod_ref[0, 0]
    in_ctx = pl.program_id(0) < N_CTX_TILES
    branch_w = ((0, ATT_W, wpa_ref), (ATT_W, FOURIER_W, wpf_ref), (ATT_W + FOURIER_W, CHUNK_W, wpc_ref))
    for s in range(TM // MERGE_SUB):
        rows = slice(s * MERGE_SUB, (s + 1) * MERGE_SUB)
        x = jnp.where(in_ctx, xc_ref[rows, :], xl_ref[rows, :])
        mixed = jnp.where(in_ctx, mixc_ref[rows, :], mixl_ref[rows, :])
        h = (_ln(x) * (1.0 + m[1:2]) + m[0:1]).astype(BF16)
        merged = None
        for b, (c0, width, w_ref) in enumerate(branch_w):
            gate = _sigmoid(_dot(h, win_ref[:, MIX_W + b * D_MODEL:MIX_W + (b + 1) * D_MODEL]))
            term = gate * _dot(mixed[:, c0:c0 + width], w_ref[...])
            merged = term if merged is None else merged + term
        mix = _dot(merged.astype(BF16), wo_ref[...])
        x1 = _ln(ALPHA * x + m[2:3] * mix) * l1g_ref[0] + l1b_ref[0]
        x1_ref[rows, :] = x1
        h2 = _ln(x1) * (1.0 + m[4:5]) + m[3:4]
        h_hi, h_lo = _split_bf16(h2)
        wrh = wrh_ref[...]
        logits = _dot(h_hi, wrh) + _dot(h_lo, wrh) + _dot(h_hi, wrl_ref[...])
        rt = _route(logits, bias_ref[:, rows])
        rt_ref[:, rows] = rt
        h2_ref[rows, :] = h2
        rc_ref[rows, :] = jnp.concatenate([rt, jnp.zeros((LANES - ROUTE_ROWS, MERGE_SUB), F32)], axis=0).T


def _merge_call(layer, x, mixed_ctx, mixed_lat, mod, w_in_b, wpa, wpf, wpc, wo, l1g, l1b, wrh, wrl, bias_t):
    const2 = lambda i: (0, 0)
    per_layer = lambda i: (layer, 0, 0)
    tok = lambda i: (i, 0)
    return pl.pallas_call(
        _merge_kernel,
        out_shape=(jax.ShapeDtypeStruct((T_ALL, D_MODEL), F32),
                   jax.ShapeDtypeStruct((T_ALL, D_MODEL), F32),
                   jax.ShapeDtypeStruct((ROUTE_ROWS, T_ALL), F32),
                   jax.ShapeDtypeStruct((T_ALL, LANES), F32)),
        grid=(T_ALL // TM,),
        in_specs=_group_specs() + _group_specs() + [
            pl.BlockSpec((1, 1, N_MOD, D_MODEL), lambda i: (layer, _cond_row(i, TM), 0, 0)),
            pl.BlockSpec((None, D_MODEL, IN_W), per_layer, pipeline_mode=pl.Buffered(1)),
            pl.BlockSpec((None, ATT_W, D_MODEL), per_layer, pipeline_mode=pl.Buffered(1)),
            pl.BlockSpec((None, FOURIER_W, D_MODEL), per_layer, pipeline_mode=pl.Buffered(1)),
            pl.BlockSpec((None, CHUNK_W, D_MODEL), per_layer, pipeline_mode=pl.Buffered(1)),
            pl.BlockSpec((None, D_MODEL, D_MODEL), per_layer, pipeline_mode=pl.Buffered(1)),
            pl.BlockSpec((1, 1, D_MODEL), per_layer),
            pl.BlockSpec((1, 1, D_MODEL), per_layer),
            pl.BlockSpec((D_MODEL, ROUTER_PAD), const2),
            pl.BlockSpec((D_MODEL, ROUTER_PAD), const2),
            pl.BlockSpec((N_EXPERTS, TM), const2),
        ],
        out_specs=(pl.BlockSpec((TM, D_MODEL), tok),
                   pl.BlockSpec((TM, D_MODEL), tok),
                   pl.BlockSpec((ROUTE_ROWS, TM), lambda i: (0, i)),
                   pl.BlockSpec((TM, LANES), tok)),
        compiler_params=_cparams("arbitrary"),
        name="merge_route",
    )(*x, mixed_ctx, mixed_lat, mod, w_in_b, wpa, wpf, wpc, wo, l1g, l1b, wrh, wrl, bias_t)


def _sort_kernel(rt_ref, tri_ref, low_ref, pos_ref, tile_ref):
    crow = lax.broadcasted_iota(jnp.int32, (CLASS_PAD, SORT_BLK), 0)
    tri = tri_ref[...]
    n_blk = T_ALL // SORT_BLK
    carry = jnp.zeros((CLASS_PAD, 1), F32)
    ranks = []
    for b in range(n_blk):
        hot = rt_ref[0:1, b * SORT_BLK:(b + 1) * SORT_BLK].astype(jnp.int32) == crow
        hot_f = jnp.where(hot, 1.0, 0.0)
        before = _dot(hot_f.astype(BF16), tri) + carry
        ranks.append(jnp.sum(jnp.where(hot, before, 0.0), axis=0, keepdims=True))
        carry = carry + jnp.sum(hot_f, axis=1, keepdims=True)
    padded = jnp.floor((carry + (TM_MOE - 1.0)) * (1.0 / TM_MOE)) * TM_MOE
    padded = jnp.broadcast_to(padded, (CLASS_PAD, LANES))
    offs = _dot(low_ref[...], padded.astype(BF16))
    for b in range(n_blk):
        hot = rt_ref[0:1, b * SORT_BLK:(b + 1) * SORT_BLK].astype(jnp.int32) == crow
        base = jnp.sum(jnp.where(hot, offs[:, 0:1], 0.0), axis=0, keepdims=True)
        pos_ref[:, b * SORT_BLK:(b + 1) * SORT_BLK] = (base + ranks[b]).astype(jnp.int32)

    start = lax.broadcasted_iota(jnp.int32, (1, LANES), 1).astype(F32) * TM_MOE
    is_class = lax.broadcasted_iota(jnp.int32, (CLASS_PAD, LANES), 0) < N_CLASSES
    ends = jnp.where(is_class, offs + padded, 0.0)
    total = jnp.max(ends, axis=0, keepdims=True)
    valid = start < total
    tcls = jnp.sum(jnp.where(jnp.logical_and(is_class, ends <= start), 1.0, 0.0), axis=0, keepdims=True)
    last = jnp.max(jnp.where(valid, tcls, 0.0), axis=1, keepdims=True)
    tcls = jnp.where(valid, tcls, last)
    grp = functools.reduce(jnp.add, [jnp.where(tcls >= g * N_PAIRS, 1.0, 0.0) for g in range(1, N_EXPERT_GROUPS)])
    pair = tcls - grp * N_PAIRS
    slot_a = functools.reduce(jnp.add, [jnp.where(pair == k, float(PAIR_A[k]), 0.0) for k in range(N_PAIRS)])
    slot_b = functools.reduce(jnp.add, [jnp.where(pair == k, float(PAIR_B[k]), 0.0) for k in range(N_PAIRS)])
    crow_t = lax.broadcasted_iota(jnp.int32, (CLASS_PAD, LANES), 0).astype(F32)
    real_end = jnp.sum(jnp.where(crow_t == tcls, offs + carry, 0.0), axis=0, keepdims=True)
    n_rows = jnp.where(valid, jnp.clip(real_end - start, 0.0, float(TM_MOE)), 0.0)
    rows = [grp * EXPERTS_PER_GROUP + slot_a, grp * EXPERTS_PER_GROUP + slot_b, n_rows]
    tile_ref[...] = jnp.concatenate(rows + [jnp.zeros((8 - len(rows), LANES), F32)], axis=0).astype(jnp.int32)


def _sort_call(rt, tri, low):
    return pl.pallas_call(
        _sort_kernel,
        out_shape=(jax.ShapeDtypeStruct((1, T_ALL), jnp.int32),
                   jax.ShapeDtypeStruct((8, LANES), jnp.int32)),
        name="route_sort",
    )(rt, tri, low)


def _moe_kernel(ta_ref, tb_ref, nrow_ref, pos_ref, h2_ref, rc_ref, wga_ref, wua_ref, wda_ref, wgb_ref, wub_ref,
                wdb_ref, out_ref, hbuf, rbuf, ybuf, src_ref, ssem):
    j = pl.program_id(0)
    slot = j % 2
    del ta_ref, tb_ref

    def n_rows(tile):
        inside = jnp.logical_and(tile >= 0, tile < N_TILES)
        return jnp.where(inside, nrow_ref[jnp.clip(tile, 0, N_TILES - 1)], 0)

    def token_of(tile, r):
        return src_ref[tile * TM_MOE + jnp.minimum(r, nrow_ref[tile] - 1)]

    def scatter_copy(tile, buf_slot, r):
        dst = jnp.where(r < nrow_ref[tile], token_of(tile, r), T_ALL + buf_slot * TM_MOE + r)
        return pltpu.make_async_copy(ybuf.at[buf_slot, r], out_ref.at[dst], ssem.at[buf_slot])

    def for_groups(tile, fn):
        def body(g, c):
            for k in range(COPY_GROUP):
                fn(g * COPY_GROUP + k, k)
            return c
        lax.fori_loop(0, (n_rows(tile) + COPY_GROUP - 1) // COPY_GROUP, body, 0)

    def retire(tile, copy_of):
        for_groups(tile, lambda r, k: copy_of(0).wait())

    @pl.when(j == 0)
    def _():
        def place(t, c):
            src_ref[pos_ref[t]] = t
            return c
        lax.fori_loop(0, T_ALL, place, 0, unroll=8)
        hbuf[...] = jnp.zeros(hbuf.shape, F32)
        rbuf[...] = jnp.zeros(rbuf.shape, F32)
        ybuf[...] = jnp.zeros(ybuf.shape, F32)
        for s in range(2):
            park = pltpu.make_async_copy(ybuf.at[s], out_ref.at[pl.ds(T_ALL + s * TM_MOE, TM_MOE)], ssem.at[s])
            park.start()
            park.wait()

    @pl.when(n_rows(j - 2) > 0)
    def _():
        retire(j - 2, lambda r: scatter_copy(j - 2, slot, r))

    @pl.when(n_rows(j) > 0)
    def _():
        def gather_row(r, k):
            tok = token_of(j, r)
            hbuf[pl.ds(r, 1), :] = h2_ref[pl.ds(tok, 1), :]
            rbuf[pl.ds(r, 1), :] = rc_ref[pl.ds(tok, 1), :]
        for_groups(j, gather_row)
        h = hbuf[...].astype(BF16)
        route = rbuf[...]

        def expert(wg_ref, wu_ref, wd_ref, w):
            a = _dot(h, wg_ref[...])
            u = _dot(h, wu_ref[...])
            return _dot(((a * _sigmoid(a)) * u * w).astype(BF16), wd_ref[...])

        o = expert(wga_ref, wua_ref, wda_ref, route[:, 1:2]) + expert(wgb_ref, wub_ref, wdb_ref, route[:, 2:3])
        for k in range(D_CHUNKS):
            ybuf[slot, :, k, :] = o[:, k * LANES:(k + 1) * LANES]
        for_groups(j, lambda r, k: scatter_copy(j, slot, r).start(priority=k % 2))

    @pl.when(j == N_TILES - 1)
    def _():
        @pl.when(n_rows(j - 1) > 0)
        def _():
            retire(j - 1, lambda r: scatter_copy(j - 1, 1 - slot, r))

        @pl.when(n_rows(j) > 0)
        def _():
            retire(j, lambda r: scatter_copy(j, slot, r))


def _moe_call(layer, tiles, pos, h2, rc, wge, wue, wde):
    up = (None, None, D_MODEL, D_EXPERT)
    down = (None, None, D_EXPERT, D_MODEL)
    slot_a = lambda j, ta, tb, va, ps: (layer, ta[j], 0, 0)
    slot_b = lambda j, ta, tb, va, ps: (layer, tb[j], 0, 0)
    whole = lambda j, ta, tb, va, ps: (0, 0)
    grid_spec = pltpu.PrefetchScalarGridSpec(
        num_scalar_prefetch=4,
        grid=(N_TILES,),
        in_specs=[
            pl.BlockSpec((T_ALL, D_MODEL), whole, pipeline_mode=pl.Buffered(1)),
            pl.BlockSpec((T_ALL, LANES), whole, pipeline_mode=pl.Buffered(1)),
            pl.BlockSpec(up, slot_a), pl.BlockSpec(up, slot_a), pl.BlockSpec(down, slot_a),
            pl.BlockSpec(up, slot_b), pl.BlockSpec(up, slot_b), pl.BlockSpec(down, slot_b),
        ],
        out_specs=pl.BlockSpec(memory_space=pl.ANY),
        scratch_shapes=[
            pltpu.VMEM((TM_MOE, D_MODEL), F32),
            pltpu.VMEM((TM_MOE, LANES), F32),
            pltpu.VMEM((2, TM_MOE, D_CHUNKS, LANES), F32),
            pltpu.SMEM((R_ROWS,), jnp.int32),
            pltpu.SemaphoreType.DMA((2,)),
        ],
    )
    return pl.pallas_call(
        _moe_kernel,
        out_shape=jax.ShapeDtypeStruct((T_ALL + N_PARK, D_CHUNKS, LANES), F32),
        grid_spec=grid_spec,
        compiler_params=pltpu.CompilerParams(dimension_semantics=("arbitrary",), vmem_limit_bytes=V7X_VMEM_LIMIT,
                                             has_side_effects=True),
        name="moe_pairs",
    )(tiles[0, :N_TILES], tiles[1, :N_TILES], tiles[2, :N_TILES], pos, h2, rc, wge, wue, wde, wge, wue, wde)


def _post_kernel(x1_ref, moe_ref, mod_ref, l2g_ref, l2b_ref, oc_ref, ol_ref):
    m = mod_ref[0, 0]
    moe = jnp.concatenate([moe_ref[:, k, :] for k in range(D_CHUNKS)], axis=-1)
    y = _ln(ALPHA * x1_ref[...] + m[5:6] * moe) * l2g_ref[0] + l2b_ref[0]
    i = pl.program_id(0)

    @pl.when(i < N_CTX_TILES)
    def _():
        oc_ref[...] = y

    @pl.when(i >= N_CTX_TILES)
    def _():
        ol_ref[...] = y


def _post_call(layer, x1, moe, mod, l2g, l2b):
    tok = lambda i: (i, 0)
    per_layer = lambda i: (layer, 0, 0)
    return pl.pallas_call(
        _post_kernel,
        out_shape=(jax.ShapeDtypeStruct((T_CTX, D_MODEL), F32), jax.ShapeDtypeStruct((T_LAT, D_MODEL), F32)),
        grid=(T_ALL // TM,),
        in_specs=[
            pl.BlockSpec((TM, D_MODEL), tok),
            pl.BlockSpec((TM, D_CHUNKS, LANES), lambda i: (i, 0, 0)),
            pl.BlockSpec((1, 1, N_MOD, D_MODEL), lambda i: (layer, _cond_row(i, TM), 0, 0)),
            pl.BlockSpec((1, 1, D_MODEL), per_layer),
            pl.BlockSpec((1, 1, D_MODEL), per_layer),
        ],
        out_specs=tuple(_group_specs()),
        compiler_params=_cparams("arbitrary"),
        name="post_moe",
    )(x1, moe, mod, l2g, l2b)


def _rope_tables():
    pos = np.arange(DEC_SEQ)
    quarter = HEAD_DIM // 4
    inv = ROPE_THETA ** (-np.arange(quarter, dtype=np.float64) / quarter)
    ang_r = (pos // GRID_W)[:, None] * inv[None, :]
    ang_c = (pos % GRID_W)[:, None] * inv[None, :]
    cos = np.concatenate([np.cos(ang_r)] * 2 + [np.cos(ang_c)] * 2, axis=-1)
    sin = np.concatenate([-np.sin(ang_r), np.sin(ang_r), -np.sin(ang_c), np.sin(ang_c)], axis=-1)
    return (jnp.asarray(np.tile(cos, (1, N_Q_HEADS)).astype(np.float32)),
            jnp.asarray(np.tile(sin, (1, N_Q_HEADS)).astype(np.float32)))


def _dft_mats(n, scale):
    j = np.arange(n)
    ang = ((j[:, None] * j[None, :]) % n) * (2 * np.pi / n)
    return np.cos(ang) * scale, np.sin(ang) * scale


def _block_diag(m, reps):
    return np.kron(np.eye(reps), m)


def _const_bf16(a):
    return jnp.asarray(np.asarray(a, np.float32)).astype(BF16)


def kernel(x_prompt, x_sample, cache_k, cache_v, c, c_ctx, w_in, q_norm_g, k_norm_g, w_proj_att,
           w_proj_fourier, w_proj_chunk, w_out, chunk_ln_g, chunk_ln_b, chunk_ws, chunk_bs, w_ada, b_ada,
           ln1_g, ln1_b, ln2_g, ln2_b, w_router, router_bias, w_gate_e, w_up_e, w_down_e):
    x = (x_prompt.reshape(T_CTX, D_MODEL), x_sample.reshape(T_LAT, D_MODEL))
    cond = jnp.concatenate([c_ctx[None, :], c, jnp.zeros((N_COND - 1 - DEC_BATCH, D_MODEL), F32)], axis=0)
    mod = _mod_call(cond, w_ada, b_ada).reshape(DEPTH, N_COND, N_MOD, D_MODEL)

    cos_t, sin_t = _rope_tables()
    bd_heads = _const_bf16(_block_diag(np.ones((HEAD_DIM, HEAD_DIM)), N_Q_HEADS))
    c64, s64 = _dft_mats(FOURIER_GROUP_W, 1.0)
    bdc = _const_bf16(_block_diag(c64, FOURIER_W // FOURIER_GROUP_W))
    bds = _const_bf16(_block_diag(s64, FOURIER_W // FOURIER_GROUP_W))
    dft = {}
    for seq in (SEQ, DEC_SEQ):
        cs, ss = _dft_mats(seq, 1.0 / math.sqrt(seq * FOURIER_GROUP_W))
        dft[seq] = _const_bf16(np.concatenate([cs, -ss], axis=1))
    ctx_k = cache_k.reshape(DEC_BATCH, DEPTH, PAST_LEN, KV_W)
    ctx_v = cache_v.reshape(DEC_BATCH, DEPTH, PAST_LEN, KV_W)
    wr = jnp.pad(w_router, ((0, 0), (0, ROUTER_PAD - N_EXPERTS)))
    wr_hi = wr.astype(BF16)
    wr_lo = (wr - wr_hi.astype(F32)).astype(BF16)
    bias_t = jnp.broadcast_to(router_bias[:, None], (N_EXPERTS, TM))
    tri = _const_bf16(np.triu(np.ones((SORT_BLK, SORT_BLK)), 1))
    low = _const_bf16(np.tril(np.ones((CLASS_PAD, CLASS_PAD)), -1))

    w_in_b = w_in.astype(BF16)
    wpa, wpf, wpc, wo = (w.astype(BF16) for w in (w_proj_att, w_proj_fourier, w_proj_chunk, w_out))
    wge, wue, wde = (w.astype(BF16) for w in (w_gate_e, w_up_e, w_down_e))
    qg = jnp.tile(q_norm_g, (1, N_Q_HEADS))[:, None, :]
    kg = jnp.tile(k_norm_g, (1, N_KV_HEADS))[:, None, :]
    ws = chunk_ws.astype(BF16)
    bsm = jnp.repeat(jnp.swapaxes(chunk_bs, 1, 2), CHUNK_GROUP_W, axis=2)
    row3 = lambda a: a[:, None, :]

    new_k, new_v = [], []
    for l in range(DEPTH):
        mix = _proj_call(l, x, mod, w_in_b, qg, kg, cos_t, sin_t, bd_heads)
        new_k.append(mix[:T_CTX, ATT_W:ATT_W + KV_W].reshape(BATCH, SEQ, N_KV_HEADS, HEAD_DIM))
        new_v.append(mix[:T_CTX, ATT_W + KV_W:ATT_W + 2 * KV_W].reshape(BATCH, SEQ, N_KV_HEADS, HEAD_DIM))
        mixer_args = (bdc, bds, ws, bsm, row3(chunk_ln_g), row3(chunk_ln_b))
        mixed_ctx = _mixer_call(l, mix, None, None, dft[SEQ], *mixer_args, latent=False)
        mixed_lat = _mixer_call(l, mix, ctx_k, ctx_v, dft[DEC_SEQ], *mixer_args, latent=True)
        x1, h2, rt, rc = _merge_call(l, x, mixed_ctx, mixed_lat, mod, w_in_b, wpa, wpf, wpc, wo,
                                 row3(ln1_g), row3(ln1_b), wr_hi, wr_lo, bias_t)
        pos2d, tiles = _sort_call(rt, tri, low)
        moe = _moe_call(l, tiles, pos2d[0], h2, rc, wge, wue, wde)
        x = _post_call(l, x1, moe, mod, row3(ln2_g), row3(ln2_b))

    y_prompt = x[0].reshape(BATCH, SEQ, D_MODEL)
    y_sample = x[1].reshape(DEC_BATCH, DEC_SEQ, D_MODEL)
    return (y_prompt, y_sample, jnp.stack(new_k, axis=1), jnp.stack(new_v, axis=1))
```

```python
import functools
import math

import jax
import jax.numpy as jnp
import numpy as np
from jax import lax
from jax.experimental import pallas as pl
from jax.experimental.pallas import tpu as pltpu

F32 = jnp.float32
BF16 = jnp.bfloat16

D_MODEL = 1024
BATCH = 16
SEQ = 256
DEPTH = 2
DEC_BATCH = 2
DEC_SEQ = 1024
PAST_LEN = 256
GRID_W = 64
N_Q_HEADS = 8
N_KV_HEADS = 2
HEAD_DIM = 64
Q_PER_KV = N_Q_HEADS // N_KV_HEADS
ATT_W = N_Q_HEADS * HEAD_DIM
KV_W = N_KV_HEADS * HEAD_DIM
ROPE_THETA = 10000.0
FOURIER_GROUP_W = 64
FOURIER_W = 256
N_CHUNK_GROUPS = 4
CHUNK_GROUP_W = 64
CHUNK_W = 256
CHUNK = 128
N_EXPERTS = 16
N_EXPERT_GROUPS = 4
EXPERTS_PER_GROUP = 4
D_EXPERT = 512
ALPHA = (2 * DEPTH) ** 0.25
LN_EPS = 1e-6
RMS_EPS = 1e-6

LANES = 128
SUBLANES = 8
T_CTX = BATCH * SEQ
T_LAT = DEC_BATCH * DEC_SEQ
T_ALL = T_CTX + T_LAT
IN_W = ATT_W + 2 * KV_W + FOURIER_W + 2 * CHUNK_W + 3 * D_MODEL
MIX_W = ATT_W + 2 * KV_W + FOURIER_W + 2 * CHUNK_W
N_MOD = 6
N_COND = 8
ROUTER_PAD = LANES
TM = 512
TQ = 256
MERGE_SUB = 256
V7X_VMEM_LIMIT = 56 * 1024 * 1024

PAIR_A = (0, 0, 0, 1, 1, 3)
PAIR_B = (1, 2, 3, 3, 2, 2)
N_PAIRS = len(PAIR_A)
N_CLASSES = N_EXPERT_GROUPS * N_PAIRS
CLASS_PAD = 32
ROUTE_ROWS = 8
TM_MOE = 256
N_TILES = T_ALL // TM_MOE + N_CLASSES
R_ROWS = N_TILES * TM_MOE
SORT_BLK = 512
D_CHUNKS = D_MODEL // LANES
HA_W = D_MODEL + LANES
SMALL_TILE = 64
N_PARK = 2 * TM_MOE
COPY_GROUP = 8


def _cparams(*sem):
    return pltpu.CompilerParams(dimension_semantics=sem, vmem_limit_bytes=V7X_VMEM_LIMIT)


def _cond_row(i, tm):
    n_ctx = T_CTX // tm
    return jnp.where(i < n_ctx, 0, 1 + (i - n_ctx) // (DEC_SEQ // tm))


N_CTX_TILES = T_CTX // TM


def _group_specs():
    return [pl.BlockSpec((TM, D_MODEL), lambda i: (jnp.minimum(i, N_CTX_TILES - 1), 0)),
            pl.BlockSpec((TM, D_MODEL), lambda i: (jnp.maximum(i - N_CTX_TILES, 0), 0))]


def _group_tile(ctx_ref, lat_ref):
    return jnp.where(pl.program_id(0) < N_CTX_TILES, ctx_ref[...], lat_ref[...])


def _ln(x):
    mu = jnp.mean(x, axis=-1, keepdims=True)
    xc = x - mu
    var = jnp.mean(xc * xc, axis=-1, keepdims=True)
    return xc * lax.rsqrt(var + LN_EPS)


def _split_bf16(x):
    hi = x.astype(BF16)
    lo = (x - hi.astype(F32)).astype(BF16)
    return hi, lo


def _dot(a, b):
    return jnp.dot(a, b, preferred_element_type=F32)


def _sigmoid(x):
    return 1.0 / (1.0 + jnp.exp(-x))


def _gelu_tanh(x):
    c = np.float32(np.sqrt(2 / np.pi))
    return x * (0.5 * (1.0 + jnp.tanh(c * (x + 0.044715 * (x * x * x)))))


def _mod_kernel(c_ref, w_ref, b_ref, o_ref):
    c = c_ref[...]
    s = c * _sigmoid(c)
    o_ref[0] = _dot(s.astype(BF16), w_ref[0].astype(BF16)) + b_ref[0]


def _mod_call(cond, w_ada, b_ada):
    tn = 1536
    n = N_MOD * D_MODEL
    return pl.pallas_call(
        _mod_kernel,
        out_shape=jax.ShapeDtypeStruct((DEPTH, N_COND, n), F32),
        grid=(DEPTH, n // tn),
        in_specs=[
            pl.BlockSpec((N_COND, D_MODEL), lambda l, j: (0, 0)),
            pl.BlockSpec((1, D_MODEL, tn), lambda l, j: (l, 0, j)),
            pl.BlockSpec((1, 1, tn), lambda l, j: (l, 0, j)),
        ],
        out_specs=pl.BlockSpec((1, N_COND, tn), lambda l, j: (l, 0, j)),
        compiler_params=_cparams("arbitrary", "arbitrary"),
        name="adaln_mod",
    )(cond, w_ada, b_ada.reshape(DEPTH, 1, n))


def _head_rms(q, bd, g):
    hi, lo = _split_bf16(q * q)
    ssum = _dot(hi, bd) + _dot(lo, bd)
    return q * lax.rsqrt(ssum * (1.0 / HEAD_DIM) + RMS_EPS) * g


def _rope(x, cos, sin_signed):
    w = x.shape[-1]
    lane = lax.broadcasted_iota(jnp.int32, x.shape, 1)
    swapped = jnp.where((lane % 32) < 16, pltpu.roll(x, w - 16, 1), pltpu.roll(x, 16, 1))
    return x * cos + swapped * sin_signed


def _proj_kernel(xc_ref, xl_ref, mod_ref, w_ref, qg_ref, kg_ref, cos_ref, sin_ref, bd_ref, o_ref):
    i = pl.program_id(0)
    m = mod_ref[0, 0]
    h = _ln(_group_tile(xc_ref, xl_ref)) * (1.0 + m[1:2]) + m[0:1]
    proj = _dot(h.astype(BF16), w_ref[...])
    o_ref[:, ATT_W + KV_W:] = proj[:, ATT_W + KV_W:]
    bd = bd_ref[...]
    qn = _head_rms(proj[:, :ATT_W], bd, qg_ref[0])
    kn = _head_rms(proj[:, ATT_W:ATT_W + KV_W], bd[:KV_W, :KV_W], kg_ref[0])

    @pl.when(i < T_CTX // TM)
    def _():
        o_ref[:, :ATT_W] = qn
        o_ref[:, ATT_W:ATT_W + KV_W] = kn

    @pl.when(i >= T_CTX // TM)
    def _():
        cos = cos_ref[...]
        sin = sin_ref[...]
        o_ref[:, :ATT_W] = _rope(qn, cos, sin)
        o_ref[:, ATT_W:ATT_W + KV_W] = _rope(kn, cos[:, :KV_W], sin[:, :KV_W])


def _proj_call(layer, x, mod, w_in_b, qg, kg, cos_t, sin_t, bd):
    n_ctx = T_CTX // TM
    per_seq = DEC_SEQ // TM

    def rope_idx(i):
        return (jnp.where(i < n_ctx, 0, (i - n_ctx) % per_seq), 0)

    return pl.pallas_call(
        _proj_kernel,
        out_shape=jax.ShapeDtypeStruct((T_ALL, MIX_W), F32),
        grid=(T_ALL // TM,),
        in_specs=_group_specs() + [
            pl.BlockSpec((1, 1, N_MOD, D_MODEL), lambda i: (layer, _cond_row(i, TM), 0, 0)),
            pl.BlockSpec((None, D_MODEL, MIX_W), lambda i: (layer, 0, 0)),
            pl.BlockSpec((1, 1, ATT_W), lambda i: (layer, 0, 0)),
            pl.BlockSpec((1, 1, KV_W), lambda i: (layer, 0, 0)),
            pl.BlockSpec((TM, ATT_W), rope_idx),
            pl.BlockSpec((TM, ATT_W), rope_idx),
            pl.BlockSpec((ATT_W, ATT_W), lambda i: (0, 0)),
        ],
        out_specs=pl.BlockSpec((TM, MIX_W), lambda i: (i, 0)),
        compiler_params=_cparams("arbitrary"),
        name="in_proj",
    )(*x, mod, w_in_b, qg, kg, cos_t, sin_t, bd)


def _attention_tile(q, k_parts, v_parts):
    outs = []
    for g in range(N_KV_HEADS):
        ks = [k[:, g * HEAD_DIM:(g + 1) * HEAD_DIM] for k in k_parts]
        vs = [v[:, g * HEAD_DIM:(g + 1) * HEAD_DIM] for v in v_parts]
        for hh in range(Q_PER_KV):
            h = g * Q_PER_KV + hh
            qh = q[:, h * HEAD_DIM:(h + 1) * HEAD_DIM]
            ss = [lax.dot_general(qh, k, (((1,), (1,)), ((), ())), preferred_element_type=F32) for k in ks]
            m = functools.reduce(jnp.maximum, [jnp.max(s, axis=-1, keepdims=True) for s in ss])
            es = [jnp.exp(s - m) for s in ss]
            denom = functools.reduce(jnp.add, [jnp.sum(e, axis=-1, keepdims=True) for e in es])
            o = functools.reduce(jnp.add, [_dot(e.astype(BF16), v) for e, v in zip(es, vs)])
            outs.append(o * (1.0 / denom))
    return jnp.concatenate(outs, axis=-1)


def _mixer_kernel(*refs, seq, latent):
    if latent:
        (mix_ref, ck_ref, cv_ref, dft_ref, bdc_ref, bds_ref, ws_ref, bsm_ref, clg_ref, clb_ref, o_ref) = refs
    else:
        (mix_ref, dft_ref, bdc_ref, bds_ref, ws_ref, bsm_ref, clg_ref, clb_ref, o_ref) = refs

    k_new = mix_ref[:, ATT_W:ATT_W + KV_W].astype(BF16)
    v_new = mix_ref[:, ATT_W + KV_W:ATT_W + 2 * KV_W].astype(BF16)
    if latent:
        k_parts = [ck_ref[...].astype(BF16), k_new]
        v_parts = [cv_ref[...].astype(BF16), v_new]
    else:
        k_parts, v_parts = [k_new], [v_new]

    def q_tile(t, carry):
        r0 = pl.multiple_of(t * TQ, TQ)
        q = (mix_ref[pl.ds(r0, TQ), :ATT_W] * (HEAD_DIM ** -0.5)).astype(BF16)
        o_ref[pl.ds(r0, TQ), :ATT_W] = _attention_tile(q, k_parts, v_parts).astype(BF16)
        return carry

    lax.fori_loop(0, seq // TQ, q_tile, 0)

    f_hi, f_lo = _split_bf16(mix_ref[:, ATT_W + 2 * KV_W:ATT_W + 2 * KV_W + FOURIER_W])
    bdc = bdc_ref[...]
    bds = bds_ref[...]
    y = jnp.concatenate([_dot(f_hi, bdc) + _dot(f_lo, bdc), _dot(f_hi, bds) + _dot(f_lo, bds)], axis=0)
    y_hi, y_lo = _split_bf16(y)
    dft = dft_ref[...]
    four = _dot(dft, y_hi) + _dot(dft, y_lo)
    o_ref[:, ATT_W:ATT_W + FOURIER_W] = four.astype(BF16)

    c0 = ATT_W + 2 * KV_W + FOURIER_W
    u = _gelu_tanh(mix_ref[:, c0:c0 + CHUNK_W])
    vn = (_ln(_gelu_tanh(mix_ref[:, c0 + CHUNK_W:c0 + 2 * CHUNK_W])) * clg_ref[0] + clb_ref[0]).astype(BF16)
    lane = lax.broadcasted_iota(jnp.int32, (CHUNK, CHUNK_W), 1)
    bsm = bsm_ref[...]
    for c in range(seq // CHUNK):
        vc = vn[c * CHUNK:(c + 1) * CHUNK]
        sv = bsm
        for g in range(N_CHUNK_GROUPS):
            vg = jnp.where(lane // CHUNK_GROUP_W == g, vc, jnp.zeros_like(vc))
            sv = sv + _dot(ws_ref[g], vg)
        o_ref[c * CHUNK:(c + 1) * CHUNK, ATT_W + FOURIER_W:] = (u[c * CHUNK:(c + 1) * CHUNK] * sv).astype(BF16)


def _mixer_call(layer, mix, ctx_k, ctx_v, dft, bdc, bds, ws, bsm, clg, clb, *, latent):
    seq = DEC_SEQ if latent else SEQ
    nb = DEC_BATCH if latent else BATCH
    row0 = (T_CTX // seq) if latent else 0
    const2 = lambda b: (0, 0)
    per_layer = lambda b: (layer, 0, 0)
    in_specs = [pl.BlockSpec((seq, MIX_W), lambda b: (row0 + b, 0))]
    args = [mix]
    if latent:
        in_specs += [pl.BlockSpec((None, None, PAST_LEN, KV_W), lambda b: (b, layer, 0, 0))] * 2
        args += [ctx_k, ctx_v]
    in_specs += [
        pl.BlockSpec((seq, 2 * seq), const2),
        pl.BlockSpec((FOURIER_W, FOURIER_W), const2),
        pl.BlockSpec((FOURIER_W, FOURIER_W), const2),
        pl.BlockSpec((None, N_CHUNK_GROUPS, CHUNK, CHUNK), lambda b: (layer, 0, 0, 0)),
        pl.BlockSpec((None, CHUNK, CHUNK_W), per_layer),
        pl.BlockSpec((1, 1, CHUNK_W), per_layer),
        pl.BlockSpec((1, 1, CHUNK_W), per_layer),
    ]
    args += [dft, bdc, bds, ws, bsm, clg, clb]
    return pl.pallas_call(
        functools.partial(_mixer_kernel, seq=seq, latent=latent),
        out_shape=jax.ShapeDtypeStruct((nb * seq, D_MODEL), BF16),
        grid=(nb,),
        in_specs=in_specs,
        out_specs=pl.BlockSpec((seq, D_MODEL), lambda b: (b, 0)),
        compiler_params=_cparams("arbitrary"),
        name="mixer_latent" if latent else "mixer_context",
    )(*args)


def _route(logits, bias_t):
    lt = logits.T[:N_EXPERTS]
    ex = jnp.exp(lt - jnp.max(lt, axis=0, keepdims=True))
    probs = ex / jnp.sum(ex, axis=0, keepdims=True)
    sel = probs + bias_t
    p = [probs[e:e + 1] for e in range(N_EXPERTS)]
    s = [sel[e:e + 1] for e in range(N_EXPERTS)]
    n = EXPERTS_PER_GROUP
    scores = []
    for g in range(N_EXPERT_GROUPS):
        pair = [s[g * n + a] + s[g * n + b] for a in range(n) for b in range(a + 1, n)]
        scores.append(functools.reduce(jnp.maximum, pair))
    best = jnp.zeros_like(scores[0], dtype=jnp.int32)
    best_score = scores[0]
    for g in range(1, N_EXPERT_GROUPS):
        better = scores[g] > best_score
        best = jnp.where(better, g, best)
        best_score = jnp.where(better, scores[g], best_score)
    cls = jnp.zeros_like(best_score)
    w_a = jnp.zeros_like(best_score)
    w_b = jnp.zeros_like(best_score)
    for g in range(N_EXPERT_GROUPS):
        in_g = best == g
        chosen = []
        for a in range(n):
            rank = jnp.zeros_like(best)
            for b in range(n):
                if b == a:
                    continue
                ahead = (s[g * n + b] > s[g * n + a]) if b > a else (s[g * n + b] >= s[g * n + a])
                rank = rank + ahead.astype(jnp.int32)
            chosen.append(jnp.logical_and(in_g, rank < 2))
        for k in range(N_PAIRS):
            pa, pb = p[g * n + PAIR_A[k]], p[g * n + PAIR_B[k]]
            hit = jnp.logical_and(chosen[PAIR_A[k]], chosen[PAIR_B[k]])
            wsum = pa + pb
            cls = jnp.where(hit, float(g * N_PAIRS + k), cls)
            w_a = jnp.where(hit, pa / wsum, w_a)
            w_b = jnp.where(hit, pb / wsum, w_b)
    return jnp.concatenate([cls, w_a, w_b, jnp.zeros((ROUTE_ROWS - 3, lt.shape[1]), F32)], axis=0)


def _merge_kernel(xc_ref, xl_ref, mixc_ref, mixl_ref, mod_ref, win_ref, wpa_ref, wpf_ref, wpc_ref, wo_ref,
                  l1g_ref, l1b_ref, wrh_ref, wrl_ref, bias_ref, x1_ref, ha_ref, rt_ref):
    m = mod_ref[0, 0]
    in_ctx = pl.program_id(0) < N_CTX_TILES
    branch_w = ((0, ATT_W, wpa_ref), (ATT_W, FOURIER_W, wpf_ref), (ATT_W + FOURIER_W, CHUNK_W, wpc_ref))
    for s in range(TM // MERGE_SUB):
        rows = slice(s * MERGE_SUB, (s + 1) * MERGE_SUB)
        x = jnp.where(in_ctx, xc_ref[rows, :], xl_ref[rows, :])
        mixed = jnp.where(in_ctx, mixc_ref[rows, :], mixl_ref[rows, :])
        h = (_ln(x) * (1.0 + m[1:2]) + m[0:1]).astype(BF16)
        merged = None
        for b, (c0, width, w_ref) in enumerate(branch_w):
            gate = _sigmoid(_dot(h, win_ref[:, MIX_W + b * D_MODEL:MIX_W + (b + 1) * D_MODEL]))
            term = gate * _dot(mixed[:, c0:c0 + width], w_ref[...])
            merged = term if merged is None else merged + term
        mix = _dot(merged.astype(BF16), wo_ref[...])
        x1 = _ln(ALPHA * x + m[2:3] * mix) * l1g_ref[0] + l1b_ref[0]
        x1_ref[rows, :] = x1
        h2 = _ln(x1) * (1.0 + m[4:5]) + m[3:4]
        h_hi, h_lo = _split_bf16(h2)
        wrh = wrh_ref[...]
        logits = _dot(h_hi, wrh) + _dot(h_lo, wrh) + _dot(h_hi, wrl_ref[...])
        rt = _route(logits, bias_ref[:, rows])
        rt_ref[:, rows] = rt
        ha_ref[rows, :D_MODEL] = h2
        ha_ref[rows, D_MODEL:] = jnp.concatenate([rt, jnp.zeros((LANES - ROUTE_ROWS, MERGE_SUB), F32)], axis=0).T


def _merge_call(layer, x, mixed_ctx, mixed_lat, mod, w_in_b, wpa, wpf, wpc, wo, l1g, l1b, wrh, wrl, bias_t):
    const2 = lambda i: (0, 0)
    per_layer = lambda i: (layer, 0, 0)
    tok = lambda i: (i, 0)
    return pl.pallas_call(
        _merge_kernel,
        out_shape=(jax.ShapeDtypeStruct((T_ALL, D_MODEL), F32),
                   jax.ShapeDtypeStruct((T_ALL, HA_W), F32),
                   jax.ShapeDtypeStruct((ROUTE_ROWS, T_ALL), F32)),
        grid=(T_ALL // TM,),
        in_specs=_group_specs() + _group_specs() + [
            pl.BlockSpec((1, 1, N_MOD, D_MODEL), lambda i: (layer, _cond_row(i, TM), 0, 0)),
            pl.BlockSpec((None, D_MODEL, IN_W), per_layer, pipeline_mode=pl.Buffered(1)),
            pl.BlockSpec((None, ATT_W, D_MODEL), per_layer, pipeline_mode=pl.Buffered(1)),
            pl.BlockSpec((None, FOURIER_W, D_MODEL), per_layer, pipeline_mode=pl.Buffered(1)),
            pl.BlockSpec((None, CHUNK_W, D_MODEL), per_layer, pipeline_mode=pl.Buffered(1)),
            pl.BlockSpec((None, D_MODEL, D_MODEL), per_layer, pipeline_mode=pl.Buffered(1)),
            pl.BlockSpec((1, 1, D_MODEL), per_layer),
            pl.BlockSpec((1, 1, D_MODEL), per_layer),
            pl.BlockSpec((D_MODEL, ROUTER_PAD), const2),
            pl.BlockSpec((D_MODEL, ROUTER_PAD), const2),
            pl.BlockSpec((N_EXPERTS, TM), const2),
        ],
        out_specs=(pl.BlockSpec((TM, D_MODEL), tok),
                   pl.BlockSpec((TM, HA_W), tok),
                   pl.BlockSpec((ROUTE_ROWS, TM), lambda i: (0, i))),
        compiler_params=_cparams("arbitrary"),
        name="merge_route",
    )(*x, mixed_ctx, mixed_lat, mod, w_in_b, wpa, wpf, wpc, wo, l1g, l1b, wrh, wrl, bias_t)


def _sort_kernel(rt_ref, tri_ref, low_ref, pos_ref, tile_ref):
    crow = lax.broadcasted_iota(jnp.int32, (CLASS_PAD, SORT_BLK), 0)
    tri = tri_ref[...]
    n_blk = T_ALL // SORT_BLK
    carry = jnp.zeros((CLASS_PAD, 1), F32)
    ranks = []
    for b in range(n_blk):
        hot = rt_ref[0:1, b * SORT_BLK:(b + 1) * SORT_BLK].astype(jnp.int32) == crow
        hot_f = jnp.where(hot, 1.0, 0.0)
        before = _dot(hot_f.astype(BF16), tri) + carry
        ranks.append(jnp.sum(jnp.where(hot, before, 0.0), axis=0, keepdims=True))
        carry = carry + jnp.sum(hot_f, axis=1, keepdims=True)
    padded = jnp.floor((carry + (TM_MOE - 1.0)) * (1.0 / TM_MOE)) * TM_MOE
    padded = jnp.broadcast_to(padded, (CLASS_PAD, LANES))
    offs = _dot(low_ref[...], padded.astype(BF16))
    for b in range(n_blk):
        hot = rt_ref[0:1, b * SORT_BLK:(b + 1) * SORT_BLK].astype(jnp.int32) == crow
        base = jnp.sum(jnp.where(hot, offs[:, 0:1], 0.0), axis=0, keepdims=True)
        pos_ref[:, b * SORT_BLK:(b + 1) * SORT_BLK] = (base + ranks[b]).astype(jnp.int32)

    start = lax.broadcasted_iota(jnp.int32, (1, LANES), 1).astype(F32) * TM_MOE
    is_class = lax.broadcasted_iota(jnp.int32, (CLASS_PAD, LANES), 0) < N_CLASSES
    ends = jnp.where(is_class, offs + padded, 0.0)
    total = jnp.max(ends, axis=0, keepdims=True)
    valid = start < total
    tcls = jnp.sum(jnp.where(jnp.logical_and(is_class, ends <= start), 1.0, 0.0), axis=0, keepdims=True)
    last = jnp.max(jnp.where(valid, tcls, 0.0), axis=1, keepdims=True)
    tcls = jnp.where(valid, tcls, last)
    grp = functools.reduce(jnp.add, [jnp.where(tcls >= g * N_PAIRS, 1.0, 0.0) for g in range(1, N_EXPERT_GROUPS)])
    pair = tcls - grp * N_PAIRS
    slot_a = functools.reduce(jnp.add, [jnp.where(pair == k, float(PAIR_A[k]), 0.0) for k in range(N_PAIRS)])
    slot_b = functools.reduce(jnp.add, [jnp.where(pair == k, float(PAIR_B[k]), 0.0) for k in range(N_PAIRS)])
    crow_t = lax.broadcasted_iota(jnp.int32, (CLASS_PAD, LANES), 0).astype(F32)
    real_end = jnp.sum(jnp.where(crow_t == tcls, offs + carry, 0.0), axis=0, keepdims=True)
    n_rows = jnp.where(valid, jnp.clip(real_end - start, 0.0, float(TM_MOE)), 0.0)
    rows = [grp * EXPERTS_PER_GROUP + slot_a, grp * EXPERTS_PER_GROUP + slot_b, n_rows]
    tile_ref[...] = jnp.concatenate(rows + [jnp.zeros((8 - len(rows), LANES), F32)], axis=0).astype(jnp.int32)


def _sort_call(rt, tri, low):
    return pl.pallas_call(
        _sort_kernel,
        out_shape=(jax.ShapeDtypeStruct((1, T_ALL), jnp.int32),
                   jax.ShapeDtypeStruct((8, LANES), jnp.int32)),
        name="route_sort",
    )(rt, tri, low)


def _moe_kernel(ta_ref, tb_ref, nrow_ref, pos_ref, ha_ref, wga_ref, wua_ref, wda_ref, wgb_ref, wub_ref,
                wdb_ref, out_ref, hbuf, ybuf, src_ref, ssem):
    j = pl.program_id(0)
    slot = j % 2
    del ta_ref, tb_ref

    def n_rows(tile):
        inside = jnp.logical_and(tile >= 0, tile < N_TILES)
        return jnp.where(inside, nrow_ref[jnp.clip(tile, 0, N_TILES - 1)], 0)

    def token_of(tile, r):
        return src_ref[tile * TM_MOE + jnp.minimum(r, nrow_ref[tile] - 1)]

    def scatter_copy(tile, buf_slot, r):
        dst = jnp.where(r < nrow_ref[tile], token_of(tile, r), T_ALL + buf_slot * TM_MOE + r)
        return pltpu.make_async_copy(ybuf.at[buf_slot, r], out_ref.at[dst], ssem.at[buf_slot])

    def for_groups(tile, fn):
        def body(g, c):
            for k in range(COPY_GROUP):
                fn(g, k)
            return c
        lax.fori_loop(0, (n_rows(tile) + COPY_GROUP - 1) // COPY_GROUP, body, 0)

    def retire(tile, copy_of):
        for_groups(tile, lambda g, k: copy_of(0).wait())

    @pl.when(j == 0)
    def _():
        def place(t, c):
            src_ref[pos_ref[t]] = t
            return c
        lax.fori_loop(0, T_ALL, place, 0, unroll=8)
        hbuf[...] = jnp.zeros(hbuf.shape, F32)
        ybuf[...] = jnp.zeros(ybuf.shape, F32)
        for s in range(2):
            park = pltpu.make_async_copy(ybuf.at[s], out_ref.at[pl.ds(T_ALL + s * TM_MOE, TM_MOE)], ssem.at[s])
            park.start()
            park.wait()

    @pl.when(n_rows(j - 2) > 0)
    def _():
        retire(j - 2, lambda r: scatter_copy(j - 2, slot, r))

    def run_experts(rows):
        rec = hbuf[:rows // SUBLANES].reshape(rows, HA_W)
        h = rec[:, :D_MODEL].astype(BF16)

        def expert(wg_ref, wu_ref, wd_ref, w):
            a = _dot(h, wg_ref[...])
            u = _dot(h, wu_ref[...])
            return _dot(((a * _sigmoid(a)) * u * w).astype(BF16), wd_ref[...])

        o = (expert(wga_ref, wua_ref, wda_ref, rec[:, D_MODEL + 1:D_MODEL + 2])
             + expert(wgb_ref, wub_ref, wdb_ref, rec[:, D_MODEL + 2:D_MODEL + 3]))
        for k in range(D_CHUNKS):
            ybuf[slot, :rows, k, :] = o[:, k * LANES:(k + 1) * LANES]

    @pl.when(n_rows(j) > 0)
    def _():
        def gather_row(g, k):
            tok = token_of(j, g * COPY_GROUP + k)
            grp, sub = lax.shift_right_logical(tok, 3), jnp.bitwise_and(tok, SUBLANES - 1)
            hbuf[g, pl.ds(k, 1), :] = ha_ref[grp, pl.ds(sub, 1), :]
        for_groups(j, gather_row)

        @pl.when(n_rows(j) > SMALL_TILE)
        def _():
            run_experts(TM_MOE)

        @pl.when(n_rows(j) <= SMALL_TILE)
        def _():
            run_experts(SMALL_TILE)

        for_groups(j, lambda g, k: scatter_copy(j, slot, g * COPY_GROUP + k).start(priority=k % 2))

    @pl.when(j == N_TILES - 1)
    def _():
        @pl.when(n_rows(j - 1) > 0)
        def _():
            retire(j - 1, lambda r: scatter_copy(j - 1, 1 - slot, r))

        @pl.when(n_rows(j) > 0)
        def _():
            retire(j, lambda r: scatter_copy(j, slot, r))


def _moe_call(layer, tiles, pos, ha, wge, wue, wde):
    up = (None, None, D_MODEL, D_EXPERT)
    down = (None, None, D_EXPERT, D_MODEL)
    slot_a = lambda j, ta, tb, va, ps: (layer, ta[j], 0, 0)
    slot_b = lambda j, ta, tb, va, ps: (layer, tb[j], 0, 0)
    grid_spec = pltpu.PrefetchScalarGridSpec(
        num_scalar_prefetch=4,
        grid=(N_TILES,),
        in_specs=[
            pl.BlockSpec((T_ALL // SUBLANES, SUBLANES, HA_W), lambda j, ta, tb, va, ps: (0, 0, 0),
                         pipeline_mode=pl.Buffered(1)),
            pl.BlockSpec(up, slot_a), pl.BlockSpec(up, slot_a), pl.BlockSpec(down, slot_a),
            pl.BlockSpec(up, slot_b), pl.BlockSpec(up, slot_b), pl.BlockSpec(down, slot_b),
        ],
        out_specs=pl.BlockSpec(memory_space=pl.ANY),
        scratch_shapes=[
            pltpu.VMEM((TM_MOE // SUBLANES, SUBLANES, HA_W), F32),
            pltpu.VMEM((2, TM_MOE, D_CHUNKS, LANES), F32),
            pltpu.SMEM((R_ROWS,), jnp.int32),
            pltpu.SemaphoreType.DMA((2,)),
        ],
    )
    return pl.pallas_call(
        _moe_kernel,
        out_shape=jax.ShapeDtypeStruct((T_ALL + N_PARK, D_CHUNKS, LANES), F32),
        grid_spec=grid_spec,
        compiler_params=pltpu.CompilerParams(dimension_semantics=("arbitrary",), vmem_limit_bytes=V7X_VMEM_LIMIT,
                                             has_side_effects=True),
        name="moe_pairs",
    )(tiles[0, :N_TILES], tiles[1, :N_TILES], tiles[2, :N_TILES], pos,
      ha.reshape(T_ALL // SUBLANES, SUBLANES, HA_W), wge, wue, wde, wge, wue, wde)


def _post_kernel(x1_ref, moe_ref, mod_ref, l2g_ref, l2b_ref, oc_ref, ol_ref):
    m = mod_ref[0, 0]
    moe = jnp.concatenate([moe_ref[:, k, :] for k in range(D_CHUNKS)], axis=-1)
    y = _ln(ALPHA * x1_ref[...] + m[5:6] * moe) * l2g_ref[0] + l2b_ref[0]
    i = pl.program_id(0)

    @pl.when(i < N_CTX_TILES)
    def _():
        oc_ref[...] = y

    @pl.when(i >= N_CTX_TILES)
    def _():
        ol_ref[...] = y


def _post_call(layer, x1, moe, mod, l2g, l2b):
    tok = lambda i: (i, 0)
    per_layer = lambda i: (layer, 0, 0)
    return pl.pallas_call(
        _post_kernel,
        out_shape=(jax.ShapeDtypeStruct((T_CTX, D_MODEL), F32), jax.ShapeDtypeStruct((T_LAT, D_MODEL), F32)),
        grid=(T_ALL // TM,),
        in_specs=[
            pl.BlockSpec((TM, D_MODEL), tok),
            pl.BlockSpec((TM, D_CHUNKS, LANES), lambda i: (i, 0, 0)),
            pl.BlockSpec((1, 1, N_MOD, D_MODEL), lambda i: (layer, _cond_row(i, TM), 0, 0)),
            pl.BlockSpec((1, 1, D_MODEL), per_layer),
            pl.BlockSpec((1, 1, D_MODEL), per_layer),
        ],
        out_specs=tuple(_group_specs()),
        compiler_params=_cparams("arbitrary"),
        name="post_moe",
    )(x1, moe, mod, l2g, l2b)


def _rope_tables():
    pos = np.arange(DEC_SEQ)
    quarter = HEAD_DIM // 4
    inv = ROPE_THETA ** (-np.arange(quarter, dtype=np.float64) / quarter)
    ang_r = (pos // GRID_W)[:, None] * inv[None, :]
    ang_c = (pos % GRID_W)[:, None] * inv[None, :]
    cos = np.concatenate([np.cos(ang_r)] * 2 + [np.cos(ang_c)] * 2, axis=-1)
    sin = np.concatenate([-np.sin(ang_r), np.sin(ang_r), -np.sin(ang_c), np.sin(ang_c)], axis=-1)
    return (jnp.asarray(np.tile(cos, (1, N_Q_HEADS)).astype(np.float32)),
            jnp.asarray(np.tile(sin, (1, N_Q_HEADS)).astype(np.float32)))


def _dft_mats(n, scale):
    j = np.arange(n)
    ang = ((j[:, None] * j[None, :]) % n) * (2 * np.pi / n)
    return np.cos(ang) * scale, np.sin(ang) * scale


def _block_diag(m, reps):
    return np.kron(np.eye(reps), m)


def _const_bf16(a):
    return jnp.asarray(np.asarray(a, np.float32)).astype(BF16)


def kernel(x_prompt, x_sample, cache_k, cache_v, c, c_ctx, w_in, q_norm_g, k_norm_g, w_proj_att,
           w_proj_fourier, w_proj_chunk, w_out, chunk_ln_g, chunk_ln_b, chunk_ws, chunk_bs, w_ada, b_ada,
           ln1_g, ln1_b, ln2_g, ln2_b, w_router, router_bias, w_gate_e, w_up_e, w_down_e):
    x = (x_prompt.reshape(T_CTX, D_MODEL), x_sample.reshape(T_LAT, D_MODEL))
    cond = jnp.concatenate([c_ctx[None, :], c, jnp.zeros((N_COND - 1 - DEC_BATCH, D_MODEL), F32)], axis=0)
    mod = _mod_call(cond, w_ada, b_ada).reshape(DEPTH, N_COND, N_MOD, D_MODEL)

    cos_t, sin_t = _rope_tables()
    bd_heads = _const_bf16(_block_diag(np.ones((HEAD_DIM, HEAD_DIM)), N_Q_HEADS))
    c64, s64 = _dft_mats(FOURIER_GROUP_W, 1.0)
    bdc = _const_bf16(_block_diag(c64, FOURIER_W // FOURIER_GROUP_W))
    bds = _const_bf16(_block_diag(s64, FOURIER_W // FOURIER_GROUP_W))
    dft = {}
    for seq in (SEQ, DEC_SEQ):
        cs, ss = _dft_mats(seq, 1.0 / math.sqrt(seq * FOURIER_GROUP_W))
        dft[seq] = _const_bf16(np.concatenate([cs, -ss], axis=1))
    ctx_k = cache_k.reshape(DEC_BATCH, DEPTH, PAST_LEN, KV_W)
    ctx_v = cache_v.reshape(DEC_BATCH, DEPTH, PAST_LEN, KV_W)
    wr = jnp.pad(w_router, ((0, 0), (0, ROUTER_PAD - N_EXPERTS)))
    wr_hi = wr.astype(BF16)
    wr_lo = (wr - wr_hi.astype(F32)).astype(BF16)
    bias_t = jnp.broadcast_to(router_bias[:, None], (N_EXPERTS, TM))
    tri = _const_bf16(np.triu(np.ones((SORT_BLK, SORT_BLK)), 1))
    low = _const_bf16(np.tril(np.ones((CLASS_PAD, CLASS_PAD)), -1))

    w_in_b = w_in.astype(BF16)
    wpa, wpf, wpc, wo = (w.astype(BF16) for w in (w_proj_att, w_proj_fourier, w_proj_chunk, w_out))
    wge, wue, wde = (w.astype(BF16) for w in (w_gate_e, w_up_e, w_down_e))
    qg = jnp.tile(q_norm_g, (1, N_Q_HEADS))[:, None, :]
    kg = jnp.tile(k_norm_g, (1, N_KV_HEADS))[:, None, :]
    ws = chunk_ws.astype(BF16)
    bsm = jnp.repeat(jnp.swapaxes(chunk_bs, 1, 2), CHUNK_GROUP_W, axis=2)
    row3 = lambda a: a[:, None, :]

    new_k, new_v = [], []
    for l in range(DEPTH):
        mix = _proj_call(l, x, mod, w_in_b, qg, kg, cos_t, sin_t, bd_heads)
        new_k.append(mix[:T_CTX, ATT_W:ATT_W + KV_W].reshape(BATCH, SEQ, N_KV_HEADS, HEAD_DIM))
        new_v.append(mix[:T_CTX, ATT_W + KV_W:ATT_W + 2 * KV_W].reshape(BATCH, SEQ, N_KV_HEADS, HEAD_DIM))
        mixer_args = (bdc, bds, ws, bsm, row3(chunk_ln_g), row3(chunk_ln_b))
        mixed_ctx = _mixer_call(l, mix, None, None, dft[SEQ], *mixer_args, latent=False)
        mixed_lat = _mixer_call(l, mix, ctx_k, ctx_v, dft[DEC_SEQ], *mixer_args, latent=True)
        x1, ha, rt = _merge_call(l, x, mixed_ctx, mixed_lat, mod, w_in_b, wpa, wpf, wpc, wo,
                                 row3(ln1_g), row3(ln1_b), wr_hi, wr_lo, bias_t)
        pos2d, tiles = _sort_call(rt, tri, low)
        moe = _moe_call(l, tiles, pos2d[0], ha, wge, wue, wde)
        x = _post_call(l, x1, moe, mod, row3(ln2_g), row3(ln2_b))

    y_prompt = x[0].reshape(BATCH, SEQ, D_MODEL)
    y_sample = x[1].reshape(DEC_BATCH, DEC_SEQ, D_MODEL)
    return (y_prompt, y_sample, jnp.stack(new_k, axis=1), jnp.stack(new_v, axis=1))
```

```python
import functools
import math

import jax
import jax.numpy as jnp
import numpy as np
from jax import lax
from jax.experimental import pallas as pl
from jax.experimental.pallas import tpu as pltpu

F32 = jnp.float32
BF16 = jnp.bfloat16

D_MODEL = 1024
BATCH = 16
SEQ = 256
DEPTH = 2
DEC_BATCH = 2
DEC_SEQ = 1024
PAST_LEN = 256
GRID_W = 64
N_Q_HEADS = 8
N_KV_HEADS = 2
HEAD_DIM = 64
Q_PER_KV = N_Q_HEADS // N_KV_HEADS
ATT_W = N_Q_HEADS * HEAD_DIM
KV_W = N_KV_HEADS * HEAD_DIM
ROPE_THETA = 10000.0
FOURIER_GROUP_W = 64
FOURIER_W = 256
N_CHUNK_GROUPS = 4
CHUNK_GROUP_W = 64
CHUNK_W = 256
CHUNK = 128
N_EXPERTS = 16
N_EXPERT_GROUPS = 4
EXPERTS_PER_GROUP = 4
D_EXPERT = 512
ALPHA = (2 * DEPTH) ** 0.25
LN_EPS = 1e-6
RMS_EPS = 1e-6

LANES = 128
SUBLANES = 8
T_CTX = BATCH * SEQ
T_LAT = DEC_BATCH * DEC_SEQ
T_ALL = T_CTX + T_LAT
IN_W = ATT_W + 2 * KV_W + FOURIER_W + 2 * CHUNK_W + 3 * D_MODEL
MIX_W = ATT_W + 2 * KV_W + FOURIER_W + 2 * CHUNK_W
N_MOD = 6
N_COND = 8
ROUTER_PAD = LANES
TM = 512
TQ = 256
MERGE_SUB = 256
PROJ_SUB = 256
V7X_VMEM_LIMIT = 56 * 1024 * 1024

PAIR_A = (0, 0, 0, 1, 1, 3)
PAIR_B = (1, 2, 3, 3, 2, 2)
N_PAIRS = len(PAIR_A)
N_CLASSES = N_EXPERT_GROUPS * N_PAIRS
CLASS_PAD = 32
ROUTE_ROWS = 8
TM_MOE = 256
N_TILES = T_ALL // TM_MOE + N_CLASSES
R_ROWS = N_TILES * TM_MOE
SORT_BLK = 512
D_CHUNKS = D_MODEL // LANES
HA_W = D_MODEL + LANES
SMALL_TILE = 64
N_PARK = 2 * TM_MOE
COPY_GROUP = 8


def _cparams(*sem):
    return pltpu.CompilerParams(dimension_semantics=sem, vmem_limit_bytes=V7X_VMEM_LIMIT)


def _cond_row(i, tm):
    n_ctx = T_CTX // tm
    return jnp.where(i < n_ctx, 0, 1 + (i - n_ctx) // (DEC_SEQ // tm))


N_CTX_TILES = T_CTX // TM


def _group_specs():
    return [pl.BlockSpec((TM, D_MODEL), lambda i: (jnp.minimum(i, N_CTX_TILES - 1), 0)),
            pl.BlockSpec((TM, D_MODEL), lambda i: (jnp.maximum(i - N_CTX_TILES, 0), 0))]


def _group_tile(ctx_ref, lat_ref):
    return jnp.where(pl.program_id(0) < N_CTX_TILES, ctx_ref[...], lat_ref[...])


def _ln(x):
    mu = jnp.mean(x, axis=-1, keepdims=True)
    xc = x - mu
    var = jnp.mean(xc * xc, axis=-1, keepdims=True)
    return xc * lax.rsqrt(var + LN_EPS)


def _split_bf16(x):
    hi = x.astype(BF16)
    lo = (x - hi.astype(F32)).astype(BF16)
    return hi, lo


def _dot(a, b):
    return jnp.dot(a, b, preferred_element_type=F32)


def _sigmoid(x):
    return 1.0 / (1.0 + jnp.exp(-x))


def _gelu_tanh(x):
    c = np.float32(np.sqrt(2 / np.pi))
    return x * (0.5 * (1.0 + jnp.tanh(c * (x + 0.044715 * (x * x * x)))))


def _mod_kernel(c_ref, w_ref, b_ref, o_ref):
    c = c_ref[...]
    s = c * _sigmoid(c)
    o_ref[0] = _dot(s.astype(BF16), w_ref[0].astype(BF16)) + b_ref[0]


def _mod_call(cond, w_ada, b_ada):
    tn = 1536
    n = N_MOD * D_MODEL
    return pl.pallas_call(
        _mod_kernel,
        out_shape=jax.ShapeDtypeStruct((DEPTH, N_COND, n), F32),
        grid=(DEPTH, n // tn),
        in_specs=[
            pl.BlockSpec((N_COND, D_MODEL), lambda l, j: (0, 0)),
            pl.BlockSpec((1, D_MODEL, tn), lambda l, j: (l, 0, j)),
            pl.BlockSpec((1, 1, tn), lambda l, j: (l, 0, j)),
        ],
        out_specs=pl.BlockSpec((1, N_COND, tn), lambda l, j: (l, 0, j)),
        compiler_params=_cparams("arbitrary", "arbitrary"),
        name="adaln_mod",
    )(cond, w_ada, b_ada.reshape(DEPTH, 1, n))


def _head_rms(q, bd, g):
    hi, lo = _split_bf16(q * q)
    ssum = _dot(hi, bd) + _dot(lo, bd)
    return q * lax.rsqrt(ssum * (1.0 / HEAD_DIM) + RMS_EPS) * g


def _rope(x, cos, sin_signed):
    w = x.shape[-1]
    lane = lax.broadcasted_iota(jnp.int32, x.shape, 1)
    swapped = jnp.where((lane % 32) < 16, pltpu.roll(x, w - 16, 1), pltpu.roll(x, 16, 1))
    return x * cos + swapped * sin_signed


def _proj_kernel(xc_ref, xl_ref, mod_ref, w_ref, qg_ref, kg_ref, cos_ref, sin_ref, bd_ref, o_ref):
    m = mod_ref[0, 0]
    in_ctx = pl.program_id(0) < N_CTX_TILES
    bd = bd_ref[...]
    sub_rows = [slice(s * PROJ_SUB, (s + 1) * PROJ_SUB) for s in range(TM // PROJ_SUB)]
    projs = []
    for rows in sub_rows:
        x = jnp.where(in_ctx, xc_ref[rows, :], xl_ref[rows, :])
        h = _ln(x) * (1.0 + m[1:2]) + m[0:1]
        proj = _dot(h.astype(BF16), w_ref[...])
        o_ref[rows, ATT_W + KV_W:] = proj[:, ATT_W + KV_W:]
        projs.append(proj[:, :ATT_W + KV_W])
    for rows, proj in zip(sub_rows, projs):
        qn = _head_rms(proj[:, :ATT_W], bd, qg_ref[0])
        kn = _head_rms(proj[:, ATT_W:], bd[:KV_W, :KV_W], kg_ref[0])
        cos = cos_ref[rows, :]
        sin = sin_ref[rows, :]
        o_ref[rows, :ATT_W] = jnp.where(in_ctx, qn, _rope(qn, cos, sin))
        o_ref[rows, ATT_W:ATT_W + KV_W] = jnp.where(in_ctx, kn, _rope(kn, cos[:, :KV_W], sin[:, :KV_W]))


def _proj_call(layer, x, mod, w_in_b, qg, kg, cos_t, sin_t, bd):
    n_ctx = T_CTX // TM
    per_seq = DEC_SEQ // TM

    def rope_idx(i):
        return (jnp.where(i < n_ctx, 0, (i - n_ctx) % per_seq), 0)

    return pl.pallas_call(
        _proj_kernel,
        out_shape=jax.ShapeDtypeStruct((T_ALL, MIX_W), F32),
        grid=(T_ALL // TM,),
        in_specs=_group_specs() + [
            pl.BlockSpec((1, 1, N_MOD, D_MODEL), lambda i: (layer, _cond_row(i, TM), 0, 0)),
            pl.BlockSpec((None, D_MODEL, MIX_W), lambda i: (layer, 0, 0)),
            pl.BlockSpec((1, 1, ATT_W), lambda i: (layer, 0, 0)),
            pl.BlockSpec((1, 1, KV_W), lambda i: (layer, 0, 0)),
            pl.BlockSpec((TM, ATT_W), rope_idx),
            pl.BlockSpec((TM, ATT_W), rope_idx),
            pl.BlockSpec((ATT_W, ATT_W), lambda i: (0, 0)),
        ],
        out_specs=pl.BlockSpec((TM, MIX_W), lambda i: (i, 0)),
        compiler_params=_cparams("arbitrary"),
        name="in_proj",
    )(*x, mod, w_in_b, qg, kg, cos_t, sin_t, bd)


def _attention_tile(q, k_parts, v_parts):
    def kv_of(h, parts):
        g = h // Q_PER_KV
        return [p[:, g * HEAD_DIM:(g + 1) * HEAD_DIM] for p in parts]

    def scores(h):
        qh = q[:, h * HEAD_DIM:(h + 1) * HEAD_DIM]
        return [lax.dot_general(qh, k, (((1,), (1,)), ((), ())), preferred_element_type=F32)
                for k in kv_of(h, k_parts)]

    outs = []
    ss_next = scores(0)
    for h in range(N_Q_HEADS):
        ss = ss_next
        if h + 1 < N_Q_HEADS:
            ss_next = scores(h + 1)
        m = functools.reduce(jnp.maximum, [jnp.max(s, axis=-1, keepdims=True) for s in ss])
        es = [jnp.exp(s - m) for s in ss]
        denom = functools.reduce(jnp.add, [jnp.sum(e, axis=-1, keepdims=True) for e in es])
        o = functools.reduce(jnp.add, [_dot(e.astype(BF16), v) for e, v in zip(es, kv_of(h, v_parts))])
        outs.append(o * (1.0 / denom))
    return jnp.concatenate(outs, axis=-1)


def _mixer_kernel(*refs, seq, latent):
    if latent:
        (mix_ref, ck_ref, cv_ref, dft_ref, bdc_ref, bds_ref, ws_ref, bsm_ref, clg_ref, clb_ref, o_ref) = refs
    else:
        (mix_ref, dft_ref, bdc_ref, bds_ref, ws_ref, bsm_ref, clg_ref, clb_ref, o_ref) = refs

    k_new = mix_ref[:, ATT_W:ATT_W + KV_W].astype(BF16)
    v_new = mix_ref[:, ATT_W + KV_W:ATT_W + 2 * KV_W].astype(BF16)
    if latent:
        k_parts = [ck_ref[...].astype(BF16), k_new]
        v_parts = [cv_ref[...].astype(BF16), v_new]
    else:
        k_parts, v_parts = [k_new], [v_new]

    def q_tile(t, carry):
        r0 = pl.multiple_of(t * TQ, TQ)
        q = (mix_ref[pl.ds(r0, TQ), :ATT_W] * (HEAD_DIM ** -0.5)).astype(BF16)
        o_ref[pl.ds(r0, TQ), :ATT_W] = _attention_tile(q, k_parts, v_parts).astype(BF16)
        return carry

    lax.fori_loop(0, seq // TQ, q_tile, 0)

    f_hi, f_lo = _split_bf16(mix_ref[:, ATT_W + 2 * KV_W:ATT_W + 2 * KV_W + FOURIER_W])
    bdc = bdc_ref[...]
    bds = bds_ref[...]
    y = jnp.concatenate([_dot(f_hi, bdc) + _dot(f_lo, bdc), _dot(f_hi, bds) + _dot(f_lo, bds)], axis=0)
    y_hi, y_lo = _split_bf16(y)
    dft = dft_ref[...]
    four = _dot(dft, y_hi) + _dot(dft, y_lo)
    o_ref[:, ATT_W:ATT_W + FOURIER_W] = four.astype(BF16)

    c0 = ATT_W + 2 * KV_W + FOURIER_W
    u = _gelu_tanh(mix_ref[:, c0:c0 + CHUNK_W])
    vn = (_ln(_gelu_tanh(mix_ref[:, c0 + CHUNK_W:c0 + 2 * CHUNK_W])) * clg_ref[0] + clb_ref[0]).astype(BF16)
    lane = lax.broadcasted_iota(jnp.int32, (CHUNK, CHUNK_W), 1)
    bsm = bsm_ref[...]
    for c in range(seq // CHUNK):
        vc = vn[c * CHUNK:(c + 1) * CHUNK]
        sv = bsm
        for g in range(N_CHUNK_GROUPS):
            vg = jnp.where(lane // CHUNK_GROUP_W == g, vc, jnp.zeros_like(vc))
            sv = sv + _dot(ws_ref[g], vg)
        o_ref[c * CHUNK:(c + 1) * CHUNK, ATT_W + FOURIER_W:] = (u[c * CHUNK:(c + 1) * CHUNK] * sv).astype(BF16)


def _mixer_call(layer, mix, ctx_k, ctx_v, dft, bdc, bds, ws, bsm, clg, clb, *, latent):
    seq = DEC_SEQ if latent else SEQ
    nb = DEC_BATCH if latent else BATCH
    row0 = (T_CTX // seq) if latent else 0
    const2 = lambda b: (0, 0)
    per_layer = lambda b: (layer, 0, 0)
    in_specs = [pl.BlockSpec((seq, MIX_W), lambda b: (row0 + b, 0))]
    args = [mix]
    if latent:
        in_specs += [pl.BlockSpec((None, None, PAST_LEN, KV_W), lambda b: (b, layer, 0, 0))] * 2
        args += [ctx_k, ctx_v]
    in_specs += [
        pl.BlockSpec((seq, 2 * seq), const2),
        pl.BlockSpec((FOURIER_W, FOURIER_W), const2),
        pl.BlockSpec((FOURIER_W, FOURIER_W), const2),
        pl.BlockSpec((None, N_CHUNK_GROUPS, CHUNK, CHUNK), lambda b: (layer, 0, 0, 0)),
        pl.BlockSpec((None, CHUNK, CHUNK_W), per_layer),
        pl.BlockSpec((1, 1, CHUNK_W), per_layer),
        pl.BlockSpec((1, 1, CHUNK_W), per_layer),
    ]
    args += [dft, bdc, bds, ws, bsm, clg, clb]
    return pl.pallas_call(
        functools.partial(_mixer_kernel, seq=seq, latent=latent),
        out_shape=jax.ShapeDtypeStruct((nb * seq, D_MODEL), BF16),
        grid=(nb,),
        in_specs=in_specs,
        out_specs=pl.BlockSpec((seq, D_MODEL), lambda b: (b, 0)),
        compiler_params=_cparams("arbitrary"),
        name="mixer_latent" if latent else "mixer_context",
    )(*args)


def _route(logits, bias_t):
    lt = logits.T[:N_EXPERTS]
    ex = jnp.exp(lt - jnp.max(lt, axis=0, keepdims=True))
    probs = ex / jnp.sum(ex, axis=0, keepdims=True)
    sel = probs + bias_t
    p = [probs[e:e + 1] for e in range(N_EXPERTS)]
    s = [sel[e:e + 1] for e in range(N_EXPERTS)]
    n = EXPERTS_PER_GROUP
    scores = []
    for g in range(N_EXPERT_GROUPS):
        pair = [s[g * n + a] + s[g * n + b] for a in range(n) for b in range(a + 1, n)]
        scores.append(functools.reduce(jnp.maximum, pair))
    best = jnp.zeros_like(scores[0], dtype=jnp.int32)
    best_score = scores[0]
    for g in range(1, N_EXPERT_GROUPS):
        better = scores[g] > best_score
        best = jnp.where(better, g, best)
        best_score = jnp.where(better, scores[g], best_score)
    cls = jnp.zeros_like(best_score)
    w_a = jnp.zeros_like(best_score)
    w_b = jnp.zeros_like(best_score)
    for g in range(N_EXPERT_GROUPS):
        in_g = best == g
        chosen = []
        for a in range(n):
            rank = jnp.zeros_like(best)
            for b in range(n):
                if b == a:
                    continue
                ahead = (s[g * n + b] > s[g * n + a]) if b > a else (s[g * n + b] >= s[g * n + a])
                rank = rank + ahead.astype(jnp.int32)
            chosen.append(jnp.logical_and(in_g, rank < 2))
        for k in range(N_PAIRS):
            pa, pb = p[g * n + PAIR_A[k]], p[g * n + PAIR_B[k]]
            hit = jnp.logical_and(chosen[PAIR_A[k]], chosen[PAIR_B[k]])
            wsum = pa + pb
            cls = jnp.where(hit, float(g * N_PAIRS + k), cls)
            w_a = jnp.where(hit, pa / wsum, w_a)
            w_b = jnp.where(hit, pb / wsum, w_b)
    return jnp.concatenate([cls, w_a, w_b, jnp.zeros((ROUTE_ROWS - 3, lt.shape[1]), F32)], axis=0)


def _merge_kernel(xc_ref, xl_ref, mixc_ref, mixl_ref, mod_ref, win_ref, wpa_ref, wpf_ref, wpc_ref, wo_ref,
                  l1g_ref, l1b_ref, wrh_ref, wrl_ref, bias_ref, x1_ref, ha_ref, rt_ref):
    m = mod_ref[0, 0]
    in_ctx = pl.program_id(0) < N_CTX_TILES
    branch_w = ((0, ATT_W, wpa_ref), (ATT_W, FOURIER_W, wpf_ref), (ATT_W + FOURIER_W, CHUNK_W, wpc_ref))
    sub_rows = [slice(s * MERGE_SUB, (s + 1) * MERGE_SUB) for s in range(TM // MERGE_SUB)]
    residual = []
    for rows in sub_rows:
        x = jnp.where(in_ctx, xc_ref[rows, :], xl_ref[rows, :])
        mixed = jnp.where(in_ctx, mixc_ref[rows, :], mixl_ref[rows, :])
        h = (_ln(x) * (1.0 + m[1:2]) + m[0:1]).astype(BF16)
        merged = None
        for b, (c0, width, w_ref) in enumerate(branch_w):
            gate = _sigmoid(_dot(h, win_ref[:, MIX_W + b * D_MODEL:MIX_W + (b + 1) * D_MODEL]))
            term = gate * _dot(mixed[:, c0:c0 + width], w_ref[...])
            merged = term if merged is None else merged + term
        mix = _dot(merged.astype(BF16), wo_ref[...])
        residual.append(ALPHA * x + m[2:3] * mix)
    for rows, pre in zip(sub_rows, residual):
        x1 = _ln(pre) * l1g_ref[0] + l1b_ref[0]
        x1_ref[rows, :] = x1
        h2 = _ln(x1) * (1.0 + m[4:5]) + m[3:4]
        h_hi, h_lo = _split_bf16(h2)
        wrh = wrh_ref[...]
        logits = _dot(h_hi, wrh) + _dot(h_lo, wrh) + _dot(h_hi, wrl_ref[...])
        rt = _route(logits, bias_ref[:, rows])
        rt_ref[:, rows] = rt
        ha_ref[rows, :D_MODEL] = h2
        ha_ref[rows, D_MODEL:] = jnp.concatenate([rt, jnp.zeros((LANES - ROUTE_ROWS, MERGE_SUB), F32)], axis=0).T


def _merge_call(layer, x, mixed_ctx, mixed_lat, mod, w_in_b, wpa, wpf, wpc, wo, l1g, l1b, wrh, wrl, bias_t):
    const2 = lambda i: (0, 0)
    per_layer = lambda i: (layer, 0, 0)
    tok = lambda i: (i, 0)
    return pl.pallas_call(
        _merge_kernel,
        out_shape=(jax.ShapeDtypeStruct((T_ALL, D_MODEL), F32),
                   jax.ShapeDtypeStruct((T_ALL, HA_W), F32),
                   jax.ShapeDtypeStruct((ROUTE_ROWS, T_ALL), F32)),
        grid=(T_ALL // TM,),
        in_specs=_group_specs() + _group_specs() + [
            pl.BlockSpec((1, 1, N_MOD, D_MODEL), lambda i: (layer, _cond_row(i, TM), 0, 0)),
            pl.BlockSpec((None, D_MODEL, IN_W), per_layer, pipeline_mode=pl.Buffered(1)),
            pl.BlockSpec((None, ATT_W, D_MODEL), per_layer, pipeline_mode=pl.Buffered(1)),
            pl.BlockSpec((None, FOURIER_W, D_MODEL), per_layer, pipeline_mode=pl.Buffered(1)),
            pl.BlockSpec((None, CHUNK_W, D_MODEL), per_layer, pipeline_mode=pl.Buffered(1)),
            pl.BlockSpec((None, D_MODEL, D_MODEL), per_layer, pipeline_mode=pl.Buffered(1)),
            pl.BlockSpec((1, 1, D_MODEL), per_layer),
            pl.BlockSpec((1, 1, D_MODEL), per_layer),
            pl.BlockSpec((D_MODEL, ROUTER_PAD), const2),
            pl.BlockSpec((D_MODEL, ROUTER_PAD), const2),
            pl.BlockSpec((N_EXPERTS, TM), const2),
        ],
        out_specs=(pl.BlockSpec((TM, D_MODEL), tok),
                   pl.BlockSpec((TM, HA_W), tok),
                   pl.BlockSpec((ROUTE_ROWS, TM), lambda i: (0, i))),
        compiler_params=_cparams("arbitrary"),
        name="merge_route",
    )(*x, mixed_ctx, mixed_lat, mod, w_in_b, wpa, wpf, wpc, wo, l1g, l1b, wrh, wrl, bias_t)


def _sort_kernel(rt_ref, tri_ref, low_ref, pos_ref, tile_ref):
    crow = lax.broadcasted_iota(jnp.int32, (CLASS_PAD, SORT_BLK), 0)
    tri = tri_ref[...]
    n_blk = T_ALL // SORT_BLK
    carry = jnp.zeros((CLASS_PAD, 1), F32)
    ranks = []
    for b in range(n_blk):
        hot = rt_ref[0:1, b * SORT_BLK:(b + 1) * SORT_BLK].astype(jnp.int32) == crow
        hot_f = jnp.where(hot, 1.0, 0.0)
        before = _dot(hot_f.astype(BF16), tri) + carry
        ranks.append(jnp.sum(jnp.where(hot, before, 0.0), axis=0, keepdims=True))
        carry = carry + jnp.sum(hot_f, axis=1, keepdims=True)
    padded = jnp.floor((carry + (TM_MOE - 1.0)) * (1.0 / TM_MOE)) * TM_MOE
    padded = jnp.broadcast_to(padded, (CLASS_PAD, LANES))
    offs = _dot(low_ref[...], padded.astype(BF16))
    for b in range(n_blk):
        hot = rt_ref[0:1, b * SORT_BLK:(b + 1) * SORT_BLK].astype(jnp.int32) == crow
        base = jnp.sum(jnp.where(hot, offs[:, 0:1], 0.0), axis=0, keepdims=True)
        pos_ref[:, b * SORT_BLK:(b + 1) * SORT_BLK] = (base + ranks[b]).astype(jnp.int32)

    start = lax.broadcasted_iota(jnp.int32, (1, LANES), 1).astype(F32) * TM_MOE
    is_class = lax.broadcasted_iota(jnp.int32, (CLASS_PAD, LANES), 0) < N_CLASSES
    ends = jnp.where(is_class, offs + padded, 0.0)
    total = jnp.max(ends, axis=0, keepdims=True)
    valid = start < total
    tcls = jnp.sum(jnp.where(jnp.logical_and(is_class, ends <= start), 1.0, 0.0), axis=0, keepdims=True)
    last = jnp.max(jnp.where(valid, tcls, 0.0), axis=1, keepdims=True)
    tcls = jnp.where(valid, tcls, last)
    grp = functools.reduce(jnp.add, [jnp.where(tcls >= g * N_PAIRS, 1.0, 0.0) for g in range(1, N_EXPERT_GROUPS)])
    pair = tcls - grp * N_PAIRS
    slot_a = functools.reduce(jnp.add, [jnp.where(pair == k, float(PAIR_A[k]), 0.0) for k in range(N_PAIRS)])
    slot_b = functools.reduce(jnp.add, [jnp.where(pair == k, float(PAIR_B[k]), 0.0) for k in range(N_PAIRS)])
    crow_t = lax.broadcasted_iota(jnp.int32, (CLASS_PAD, LANES), 0).astype(F32)
    real_end = jnp.sum(jnp.where(crow_t == tcls, offs + carry, 0.0), axis=0, keepdims=True)
    n_rows = jnp.where(valid, jnp.clip(real_end - start, 0.0, float(TM_MOE)), 0.0)
    rows = [grp * EXPERTS_PER_GROUP + slot_a, grp * EXPERTS_PER_GROUP + slot_b, n_rows]
    tile_ref[...] = jnp.concatenate(rows + [jnp.zeros((8 - len(rows), LANES), F32)], axis=0).astype(jnp.int32)


def _sort_call(rt, tri, low):
    return pl.pallas_call(
        _sort_kernel,
        out_shape=(jax.ShapeDtypeStruct((1, T_ALL), jnp.int32),
                   jax.ShapeDtypeStruct((8, LANES), jnp.int32)),
        name="route_sort",
    )(rt, tri, low)


def _moe_kernel(ta_ref, tb_ref, nrow_ref, pos_ref, ha_ref, wga_ref, wua_ref, wda_ref, wgb_ref, wub_ref,
                wdb_ref, out_ref, hbuf, ybuf, src_ref, ssem):
    j = pl.program_id(0)
    slot = j % 2
    del ta_ref, tb_ref

    def n_rows(tile):
        inside = jnp.logical_and(tile >= 0, tile < N_TILES)
        return jnp.where(inside, nrow_ref[jnp.clip(tile, 0, N_TILES - 1)], 0)

    def token_of(tile, r):
        return src_ref[tile * TM_MOE + jnp.minimum(r, nrow_ref[tile] - 1)]

    def scatter_copy(tile, buf_slot, r):
        dst = jnp.where(r < nrow_ref[tile], token_of(tile, r), T_ALL + buf_slot * TM_MOE + r)
        return pltpu.make_async_copy(ybuf.at[buf_slot, r], out_ref.at[dst], ssem.at[buf_slot])

    def for_groups(tile, fn):
        def body(g, c):
            for k in range(COPY_GROUP):
                fn(g, k)
            return c
        lax.fori_loop(0, (n_rows(tile) + COPY_GROUP - 1) // COPY_GROUP, body, 0)

    def retire(tile, copy_of):
        for_groups(tile, lambda g, k: copy_of(0).wait())

    @pl.when(j == 0)
    def _():
        def place(t, c):
            src_ref[pos_ref[t]] = t
            return c
        lax.fori_loop(0, T_ALL, place, 0, unroll=8)
        hbuf[...] = jnp.zeros(hbuf.shape, F32)
        ybuf[...] = jnp.zeros(ybuf.shape, F32)
        for s in range(2):
            park = pltpu.make_async_copy(ybuf.at[s], out_ref.at[pl.ds(T_ALL + s * TM_MOE, TM_MOE)], ssem.at[s])
            park.start()
            park.wait()

    @pl.when(n_rows(j - 2) > 0)
    def _():
        retire(j - 2, lambda r: scatter_copy(j - 2, slot, r))

    def run_experts(rows):
        rec = hbuf[:rows // SUBLANES].reshape(rows, HA_W)
        h = rec[:, :D_MODEL].astype(BF16)

        def expert(wg_ref, wu_ref, wd_ref, w):
            a = _dot(h, wg_ref[...])
            u = _dot(h, wu_ref[...])
            return _dot(((a * _sigmoid(a)) * u * w).astype(BF16), wd_ref[...])

        o = (expert(wga_ref, wua_ref, wda_ref, rec[:, D_MODEL + 1:D_MODEL + 2])
             + expert(wgb_ref, wub_ref, wdb_ref, rec[:, D_MODEL + 2:D_MODEL + 3]))
        for k in range(D_CHUNKS):
            ybuf[slot, :rows, k, :] = o[:, k * LANES:(k + 1) * LANES]

    @pl.when(n_rows(j) > 0)
    def _():
        def gather_row(g, k):
            tok = token_of(j, g * COPY_GROUP + k)
            grp, sub = lax.shift_right_logical(tok, 3), jnp.bitwise_and(tok, SUBLANES - 1)
            hbuf[g, pl.ds(k, 1), :] = ha_ref[grp, pl.ds(sub, 1), :]
        for_groups(j, gather_row)

        @pl.when(n_rows(j) > SMALL_TILE)
        def _():
            run_experts(TM_MOE)

        @pl.when(n_rows(j) <= SMALL_TILE)
        def _():
            run_experts(SMALL_TILE)

        for_groups(j, lambda g, k: scatter_copy(j, slot, g * COPY_GROUP + k).start(priority=k % 2))

    @pl.when(j == N_TILES - 1)
    def _():
        @pl.when(n_rows(j - 1) > 0)
        def _():
            retire(j - 1, lambda r: scatter_copy(j - 1, 1 - slot, r))

        @pl.when(n_rows(j) > 0)
        def _():
            retire(j, lambda r: scatter_copy(j, slot, r))


def _moe_call(layer, tiles, pos, ha, wge, wue, wde):
    up = (None, None, D_MODEL, D_EXPERT)
    down = (None, None, D_EXPERT, D_MODEL)
    slot_a = lambda j, ta, tb, va, ps: (layer, ta[j], 0, 0)
    slot_b = lambda j, ta, tb, va, ps: (layer, tb[j], 0, 0)
    grid_spec = pltpu.PrefetchScalarGridSpec(
        num_scalar_prefetch=4,
        grid=(N_TILES,),
        in_specs=[
            pl.BlockSpec((T_ALL // SUBLANES, SUBLANES, HA_W), lambda j, ta, tb, va, ps: (0, 0, 0),
                         pipeline_mode=pl.Buffered(1)),
            pl.BlockSpec(up, slot_a), pl.BlockSpec(up, slot_a), pl.BlockSpec(down, slot_a),
            pl.BlockSpec(up, slot_b), pl.BlockSpec(up, slot_b), pl.BlockSpec(down, slot_b),
        ],
        out_specs=pl.BlockSpec(memory_space=pl.ANY),
        scratch_shapes=[
            pltpu.VMEM((TM_MOE // SUBLANES, SUBLANES, HA_W), F32),
            pltpu.VMEM((2, TM_MOE, D_CHUNKS, LANES), F32),
            pltpu.SMEM((R_ROWS,), jnp.int32),
            pltpu.SemaphoreType.DMA((2,)),
        ],
    )
    return pl.pallas_call(
        _moe_kernel,
        out_shape=jax.ShapeDtypeStruct((T_ALL + N_PARK, D_CHUNKS, LANES), F32),
        grid_spec=grid_spec,
        compiler_params=pltpu.CompilerParams(dimension_semantics=("arbitrary",), vmem_limit_bytes=V7X_VMEM_LIMIT,
                                             has_side_effects=True),
        name="moe_pairs",
    )(tiles[0, :N_TILES], tiles[1, :N_TILES], tiles[2, :N_TILES], pos,
      ha.reshape(T_ALL // SUBLANES, SUBLANES, HA_W), wge, wue, wde, wge, wue, wde)


def _post_kernel(x1_ref, moe_ref, mod_ref, l2g_ref, l2b_ref, oc_ref, ol_ref):
    m = mod_ref[0, 0]
    moe = jnp.concatenate([moe_ref[:, k, :] for k in range(D_CHUNKS)], axis=-1)
    y = _ln(ALPHA * x1_ref[...] + m[5:6] * moe) * l2g_ref[0] + l2b_ref[0]
    i = pl.program_id(0)

    @pl.when(i < N_CTX_TILES)
    def _():
        oc_ref[...] = y

    @pl.when(i >= N_CTX_TILES)
    def _():
        ol_ref[...] = y


def _post_call(layer, x1, moe, mod, l2g, l2b):
    tok = lambda i: (i, 0)
    per_layer = lambda i: (layer, 0, 0)
    return pl.pallas_call(
        _post_kernel,
        out_shape=(jax.ShapeDtypeStruct((T_CTX, D_MODEL), F32), jax.ShapeDtypeStruct((T_LAT, D_MODEL), F32)),
        grid=(T_ALL // TM,),
        in_specs=[
            pl.BlockSpec((TM, D_MODEL), tok),
            pl.BlockSpec((TM, D_CHUNKS, LANES), lambda i: (i, 0, 0)),
            pl.BlockSpec((1, 1, N_MOD, D_MODEL), lambda i: (layer, _cond_row(i, TM), 0, 0)),
            pl.BlockSpec((1, 1, D_MODEL), per_layer),
            pl.BlockSpec((1, 1, D_MODEL), per_layer),
        ],
        out_specs=tuple(_group_specs()),
        compiler_params=_cparams("arbitrary"),
        name="post_moe",
    )(x1, moe, mod, l2g, l2b)


def _rope_tables():
    pos = np.arange(DEC_SEQ)
    quarter = HEAD_DIM // 4
    inv = ROPE_THETA ** (-np.arange(quarter, dtype=np.float64) / quarter)
    ang_r = (pos // GRID_W)[:, None] * inv[None, :]
    ang_c = (pos % GRID_W)[:, None] * inv[None, :]
    cos = np.concatenate([np.cos(ang_r)] * 2 + [np.cos(ang_c)] * 2, axis=-1)
    sin = np.concatenate([-np.sin(ang_r), np.sin(ang_r), -np.sin(ang_c), np.sin(ang_c)], axis=-1)
    return (jnp.asarray(np.tile(cos, (1, N_Q_HEADS)).astype(np.float32)),
            jnp.asarray(np.tile(sin, (1, N_Q_HEADS)).astype(np.float32)))


def _dft_mats(n, scale):
    j = np.arange(n)
    ang = ((j[:, None] * j[None, :]) % n) * (2 * np.pi / n)
    return np.cos(ang) * scale, np.sin(ang) * scale


def _block_diag(m, reps):
    return np.kron(np.eye(reps), m)


def _const_bf16(a):
    return jnp.asarray(np.asarray(a, np.float32)).astype(BF16)


def kernel(x_prompt, x_sample, cache_k, cache_v, c, c_ctx, w_in, q_norm_g, k_norm_g, w_proj_att,
           w_proj_fourier, w_proj_chunk, w_out, chunk_ln_g, chunk_ln_b, chunk_ws, chunk_bs, w_ada, b_ada,
           ln1_g, ln1_b, ln2_g, ln2_b, w_router, router_bias, w_gate_e, w_up_e, w_down_e):
    x = (x_prompt.reshape(T_CTX, D_MODEL), x_sample.reshape(T_LAT, D_MODEL))
    cond = jnp.concatenate([c_ctx[None, :], c, jnp.zeros((N_COND - 1 - DEC_BATCH, D_MODEL), F32)], axis=0)
    mod = _mod_call(cond, w_ada, b_ada).reshape(DEPTH, N_COND, N_MOD, D_MODEL)

    cos_t, sin_t = _rope_tables()
    bd_heads = _const_bf16(_block_diag(np.ones((HEAD_DIM, HEAD_DIM)), N_Q_HEADS))
    c64, s64 = _dft_mats(FOURIER_GROUP_W, 1.0)
    bdc = _const_bf16(_block_diag(c64, FOURIER_W // FOURIER_GROUP_W))
    bds = _const_bf16(_block_diag(s64, FOURIER_W // FOURIER_GROUP_W))
    dft = {}
    for seq in (SEQ, DEC_SEQ):
        cs, ss = _dft_mats(seq, 1.0 / math.sqrt(seq * FOURIER_GROUP_W))
        dft[seq] = _const_bf16(np.concatenate([cs, -ss], axis=1))
    ctx_k = cache_k.reshape(DEC_BATCH, DEPTH, PAST_LEN, KV_W)
    ctx_v = cache_v.reshape(DEC_BATCH, DEPTH, PAST_LEN, KV_W)
    wr = jnp.pad(w_router, ((0, 0), (0, ROUTER_PAD - N_EXPERTS)))
    wr_hi = wr.astype(BF16)
    wr_lo = (wr - wr_hi.astype(F32)).astype(BF16)
    bias_t = jnp.broadcast_to(router_bias[:, None], (N_EXPERTS, TM))
    tri = _const_bf16(np.triu(np.ones((SORT_BLK, SORT_BLK)), 1))
    low = _const_bf16(np.tril(np.ones((CLASS_PAD, CLASS_PAD)), -1))

    w_in_b = w_in.astype(BF16)
    wpa, wpf, wpc, wo = (w.astype(BF16) for w in (w_proj_att, w_proj_fourier, w_proj_chunk, w_out))
    wge, wue, wde = (w.astype(BF16) for w in (w_gate_e, w_up_e, w_down_e))
    qg = jnp.tile(q_norm_g, (1, N_Q_HEADS))[:, None, :]
    kg = jnp.tile(k_norm_g, (1, N_KV_HEADS))[:, None, :]
    ws = chunk_ws.astype(BF16)
    bsm = jnp.repeat(jnp.swapaxes(chunk_bs, 1, 2), CHUNK_GROUP_W, axis=2)
    row3 = lambda a: a[:, None, :]

    new_k, new_v = [], []
    for l in range(DEPTH):
        mix = _proj_call(l, x, mod, w_in_b, qg, kg, cos_t, sin_t, bd_heads)
        new_k.append(mix[:T_CTX, ATT_W:ATT_W + KV_W].reshape(BATCH, SEQ, N_KV_HEADS, HEAD_DIM))
        new_v.append(mix[:T_CTX, ATT_W + KV_W:ATT_W + 2 * KV_W].reshape(BATCH, SEQ, N_KV_HEADS, HEAD_DIM))
        mixer_args = (bdc, bds, ws, bsm, row3(chunk_ln_g), row3(chunk_ln_b))
        mixed_ctx = _mixer_call(l, mix, None, None, dft[SEQ], *mixer_args, latent=False)
        mixed_lat = _mixer_call(l, mix, ctx_k, ctx_v, dft[DEC_SEQ], *mixer_args, latent=True)
        x1, ha, rt = _merge_call(l, x, mixed_ctx, mixed_lat, mod, w_in_b, wpa, wpf, wpc, wo,
                                 row3(ln1_g), row3(ln1_b), wr_hi, wr_lo, bias_t)
        pos2d, tiles = _sort_call(rt, tri, low)
        moe = _moe_call(l, tiles, pos2d[0], ha, wge, wue, wde)
        x = _post_call(l, x1, moe, mod, row3(ln2_g), row3(ln2_b))

    y_prompt = x[0].reshape(BATCH, SEQ, D_MODEL)
    y_sample = x[1].reshape(DEC_BATCH, DEC_SEQ, D_MODEL)
    return (y_prompt, y_sample, jnp.stack(new_k, axis=1), jnp.stack(new_v, axis=1))
```

```python
import functools
import math

import jax
import jax.numpy as jnp
import numpy as np
from jax import lax
from jax.experimental import pallas as pl
from jax.experimental.pallas import tpu as pltpu

F32 = jnp.float32
BF16 = jnp.bfloat16

D_MODEL = 1024
BATCH = 16
SEQ = 256
DEPTH = 2
DEC_BATCH = 2
DEC_SEQ = 1024
PAST_LEN = 256
GRID_W = 64
N_Q_HEADS = 8
N_KV_HEADS = 2
HEAD_DIM = 64
Q_PER_KV = N_Q_HEADS // N_KV_HEADS
ATT_W = N_Q_HEADS * HEAD_DIM
KV_W = N_KV_HEADS * HEAD_DIM
ROPE_THETA = 10000.0
FOURIER_GROUP_W = 64
FOURIER_W = 256
N_CHUNK_GROUPS = 4
CHUNK_GROUP_W = 64
CHUNK_W = 256
CHUNK = 128
N_EXPERTS = 16
N_EXPERT_GROUPS = 4
EXPERTS_PER_GROUP = 4
D_EXPERT = 512
ALPHA = (2 * DEPTH) ** 0.25
LN_EPS = 1e-6
RMS_EPS = 1e-6

LANES = 128
SUBLANES = 8
T_CTX = BATCH * SEQ
T_LAT = DEC_BATCH * DEC_SEQ
T_ALL = T_CTX + T_LAT
IN_W = ATT_W + 2 * KV_W + FOURIER_W + 2 * CHUNK_W + 3 * D_MODEL
MIX_W = ATT_W + 2 * KV_W + FOURIER_W + 2 * CHUNK_W
N_MOD = 6
N_COND = 8
ROUTER_PAD = LANES
TM = 512
TQ = 256
MERGE_SUB = 256
PROJ_SUB = 256
V7X_VMEM_LIMIT = 56 * 1024 * 1024

PAIR_A = (0, 0, 0, 1, 1, 3)
PAIR_B = (1, 2, 3, 3, 2, 2)
N_PAIRS = len(PAIR_A)
N_CLASSES = N_EXPERT_GROUPS * N_PAIRS
CLASS_PAD = 32
ROUTE_ROWS = 8
TM_MOE = 256
N_TILES = T_ALL // TM_MOE + N_CLASSES
R_ROWS = N_TILES * TM_MOE
SORT_BLK = 512
D_CHUNKS = D_MODEL // LANES
HA_W = D_MODEL + LANES
TILE_SLOT_A, TILE_SLOT_B, TILE_ROWS = 0, 1, 2
SMALL_TILE = 64
N_PARK = 2 * TM_MOE
COPY_GROUP = 8


def _cparams(*sem):
    return pltpu.CompilerParams(dimension_semantics=sem, vmem_limit_bytes=V7X_VMEM_LIMIT)


def _cond_row(i, tm):
    n_ctx = T_CTX // tm
    return jnp.where(i < n_ctx, 0, 1 + (i - n_ctx) // (DEC_SEQ // tm))


N_CTX_TILES = T_CTX // TM


def _group_specs():
    return [pl.BlockSpec((TM, D_MODEL), lambda i: (jnp.minimum(i, N_CTX_TILES - 1), 0)),
            pl.BlockSpec((TM, D_MODEL), lambda i: (jnp.maximum(i - N_CTX_TILES, 0), 0))]


def _group_tile(ctx_ref, lat_ref):
    return jnp.where(pl.program_id(0) < N_CTX_TILES, ctx_ref[...], lat_ref[...])


def _ln(x):
    mu = jnp.mean(x, axis=-1, keepdims=True)
    xc = x - mu
    var = jnp.mean(xc * xc, axis=-1, keepdims=True)
    return xc * lax.rsqrt(var + LN_EPS)


def _split_bf16(x):
    hi = x.astype(BF16)
    lo = (x - hi.astype(F32)).astype(BF16)
    return hi, lo


def _dot(a, b):
    return jnp.dot(a, b, preferred_element_type=F32)


def _sigmoid(x):
    return 1.0 / (1.0 + jnp.exp(-x))


def _gelu_tanh(x):
    c = np.float32(np.sqrt(2 / np.pi))
    return x * (0.5 * (1.0 + jnp.tanh(c * (x + 0.044715 * (x * x * x)))))


def _mod_kernel(c_ref, w_ref, b_ref, o_ref):
    c = c_ref[...]
    s = c * _sigmoid(c)
    o_ref[0] = _dot(s.astype(BF16), w_ref[0].astype(BF16)) + b_ref[0]


def _mod_call(cond, w_ada, b_ada):
    tn = 1536
    n = N_MOD * D_MODEL
    return pl.pallas_call(
        _mod_kernel,
        out_shape=jax.ShapeDtypeStruct((DEPTH, N_COND, n), F32),
        grid=(DEPTH, n // tn),
        in_specs=[
            pl.BlockSpec((N_COND, D_MODEL), lambda l, j: (0, 0)),
            pl.BlockSpec((1, D_MODEL, tn), lambda l, j: (l, 0, j)),
            pl.BlockSpec((1, 1, tn), lambda l, j: (l, 0, j)),
        ],
        out_specs=pl.BlockSpec((1, N_COND, tn), lambda l, j: (l, 0, j)),
        compiler_params=_cparams("arbitrary", "arbitrary"),
        name="adaln_mod",
    )(cond, w_ada, b_ada.reshape(DEPTH, 1, n))


def _head_rms(q, bd, g):
    hi, lo = _split_bf16(q * q)
    ssum = _dot(hi, bd) + _dot(lo, bd)
    return q * lax.rsqrt(ssum * (1.0 / HEAD_DIM) + RMS_EPS) * g


def _rope(x, cos, sin_signed):
    w = x.shape[-1]
    lane = lax.broadcasted_iota(jnp.int32, x.shape, 1)
    swapped = jnp.where((lane % 32) < 16, pltpu.roll(x, w - 16, 1), pltpu.roll(x, 16, 1))
    return x * cos + swapped * sin_signed


def _proj_kernel(xc_ref, xl_ref, mod_ref, w_ref, qg_ref, kg_ref, cos_ref, sin_ref, bd_ref, o_ref):
    m = mod_ref[0, 0]
    in_ctx = pl.program_id(0) < N_CTX_TILES
    bd = bd_ref[...]
    sub_rows = [slice(s * PROJ_SUB, (s + 1) * PROJ_SUB) for s in range(TM // PROJ_SUB)]
    projs = []
    for rows in sub_rows:
        x = jnp.where(in_ctx, xc_ref[rows, :], xl_ref[rows, :])
        h = _ln(x) * (1.0 + m[1:2]) + m[0:1]
        proj = _dot(h.astype(BF16), w_ref[...])
        o_ref[rows, ATT_W + KV_W:] = proj[:, ATT_W + KV_W:]
        projs.append(proj[:, :ATT_W + KV_W])
    for rows, proj in zip(sub_rows, projs):
        qn = _head_rms(proj[:, :ATT_W], bd, qg_ref[0])
        kn = _head_rms(proj[:, ATT_W:], bd[:KV_W, :KV_W], kg_ref[0])
        cos = cos_ref[rows, :]
        sin = sin_ref[rows, :]
        o_ref[rows, :ATT_W] = jnp.where(in_ctx, qn, _rope(qn, cos, sin))
        o_ref[rows, ATT_W:ATT_W + KV_W] = jnp.where(in_ctx, kn, _rope(kn, cos[:, :KV_W], sin[:, :KV_W]))


def _proj_call(layer, x, mod, w_in_b, qg, kg, cos_t, sin_t, bd):
    n_ctx = T_CTX // TM
    per_seq = DEC_SEQ // TM

    def rope_idx(i):
        return (jnp.where(i < n_ctx, 0, (i - n_ctx) % per_seq), 0)

    return pl.pallas_call(
        _proj_kernel,
        out_shape=jax.ShapeDtypeStruct((T_ALL, MIX_W), F32),
        grid=(T_ALL // TM,),
        in_specs=_group_specs() + [
            pl.BlockSpec((1, 1, N_MOD, D_MODEL), lambda i: (layer, _cond_row(i, TM), 0, 0)),
            pl.BlockSpec((None, D_MODEL, MIX_W), lambda i: (layer, 0, 0)),
            pl.BlockSpec((1, 1, ATT_W), lambda i: (layer, 0, 0)),
            pl.BlockSpec((1, 1, KV_W), lambda i: (layer, 0, 0)),
            pl.BlockSpec((TM, ATT_W), rope_idx),
            pl.BlockSpec((TM, ATT_W), rope_idx),
            pl.BlockSpec((ATT_W, ATT_W), lambda i: (0, 0)),
        ],
        out_specs=pl.BlockSpec((TM, MIX_W), lambda i: (i, 0)),
        compiler_params=_cparams("arbitrary"),
        name="in_proj",
    )(*x, mod, w_in_b, qg, kg, cos_t, sin_t, bd)


def _attention_tile(q, k_parts, v_parts):
    def kv_of(h, parts):
        g = h // Q_PER_KV
        return [p[:, g * HEAD_DIM:(g + 1) * HEAD_DIM] for p in parts]

    def scores(h):
        qh = q[:, h * HEAD_DIM:(h + 1) * HEAD_DIM]
        return [lax.dot_general(qh, k, (((1,), (1,)), ((), ())), preferred_element_type=F32)
                for k in kv_of(h, k_parts)]

    outs = []
    ss_next = scores(0)
    for h in range(N_Q_HEADS):
        ss = ss_next
        if h + 1 < N_Q_HEADS:
            ss_next = scores(h + 1)
        m = functools.reduce(jnp.maximum, [jnp.max(s, axis=-1, keepdims=True) for s in ss])
        es = [jnp.exp(s - m) for s in ss]
        denom = functools.reduce(jnp.add, [jnp.sum(e, axis=-1, keepdims=True) for e in es])
        o = functools.reduce(jnp.add, [_dot(e.astype(BF16), v) for e, v in zip(es, kv_of(h, v_parts))])
        outs.append(o * (1.0 / denom))
    return jnp.concatenate(outs, axis=-1)


def _mixer_kernel(*refs, seq, latent):
    if latent:
        (mix_ref, ck_ref, cv_ref, dft_ref, bdc_ref, bds_ref, ws_ref, bsm_ref, clg_ref, clb_ref, o_ref) = refs
    else:
        (mix_ref, dft_ref, bdc_ref, bds_ref, ws_ref, bsm_ref, clg_ref, clb_ref, o_ref) = refs

    k_new = mix_ref[:, ATT_W:ATT_W + KV_W].astype(BF16)
    v_new = mix_ref[:, ATT_W + KV_W:ATT_W + 2 * KV_W].astype(BF16)
    if latent:
        k_parts = [ck_ref[...].astype(BF16), k_new]
        v_parts = [cv_ref[...].astype(BF16), v_new]
    else:
        k_parts, v_parts = [k_new], [v_new]

    def q_tile(t, carry):
        r0 = pl.multiple_of(t * TQ, TQ)
        q = (mix_ref[pl.ds(r0, TQ), :ATT_W] * (HEAD_DIM ** -0.5)).astype(BF16)
        o_ref[pl.ds(r0, TQ), :ATT_W] = _attention_tile(q, k_parts, v_parts).astype(BF16)
        return carry

    lax.fori_loop(0, seq // TQ, q_tile, 0)

    f_hi, f_lo = _split_bf16(mix_ref[:, ATT_W + 2 * KV_W:ATT_W + 2 * KV_W + FOURIER_W])
    bdc = bdc_ref[...]
    bds = bds_ref[...]
    y = jnp.concatenate([_dot(f_hi, bdc) + _dot(f_lo, bdc), _dot(f_hi, bds) + _dot(f_lo, bds)], axis=0)
    y_hi, y_lo = _split_bf16(y)
    dft = dft_ref[...]
    four = _dot(dft, y_hi) + _dot(dft, y_lo)
    o_ref[:, ATT_W:ATT_W + FOURIER_W] = four.astype(BF16)

    c0 = ATT_W + 2 * KV_W + FOURIER_W
    u = _gelu_tanh(mix_ref[:, c0:c0 + CHUNK_W])
    vn = (_ln(_gelu_tanh(mix_ref[:, c0 + CHUNK_W:c0 + 2 * CHUNK_W])) * clg_ref[0] + clb_ref[0]).astype(BF16)
    lane = lax.broadcasted_iota(jnp.int32, (CHUNK, CHUNK_W), 1)
    bsm = bsm_ref[...]
    for c in range(seq // CHUNK):
        vc = vn[c * CHUNK:(c + 1) * CHUNK]
        sv = bsm
        for g in range(N_CHUNK_GROUPS):
            vg = jnp.where(lane // CHUNK_GROUP_W == g, vc, jnp.zeros_like(vc))
            sv = sv + _dot(ws_ref[g], vg)
        o_ref[c * CHUNK:(c + 1) * CHUNK, ATT_W + FOURIER_W:] = (u[c * CHUNK:(c + 1) * CHUNK] * sv).astype(BF16)


def _mixer_call(layer, mix, ctx_k, ctx_v, dft, bdc, bds, ws, bsm, clg, clb, *, latent):
    seq = DEC_SEQ if latent else SEQ
    nb = DEC_BATCH if latent else BATCH
    row0 = (T_CTX // seq) if latent else 0
    const2 = lambda b: (0, 0)
    per_layer = lambda b: (layer, 0, 0)
    in_specs = [pl.BlockSpec((seq, MIX_W), lambda b: (row0 + b, 0))]
    args = [mix]
    if latent:
        in_specs += [pl.BlockSpec((None, None, PAST_LEN, KV_W), lambda b: (b, layer, 0, 0))] * 2
        args += [ctx_k, ctx_v]
    in_specs += [
        pl.BlockSpec((seq, 2 * seq), const2),
        pl.BlockSpec((FOURIER_W, FOURIER_W), const2),
        pl.BlockSpec((FOURIER_W, FOURIER_W), const2),
        pl.BlockSpec((None, N_CHUNK_GROUPS, CHUNK, CHUNK), lambda b: (layer, 0, 0, 0)),
        pl.BlockSpec((None, CHUNK, CHUNK_W), per_layer),
        pl.BlockSpec((1, 1, CHUNK_W), per_layer),
        pl.BlockSpec((1, 1, CHUNK_W), per_layer),
    ]
    args += [dft, bdc, bds, ws, bsm, clg, clb]
    return pl.pallas_call(
        functools.partial(_mixer_kernel, seq=seq, latent=latent),
        out_shape=jax.ShapeDtypeStruct((nb * seq, D_MODEL), BF16),
        grid=(nb,),
        in_specs=in_specs,
        out_specs=pl.BlockSpec((seq, D_MODEL), lambda b: (b, 0)),
        compiler_params=_cparams("arbitrary"),
        name="mixer_latent" if latent else "mixer_context",
    )(*args)


def _route(logits, bias_t):
    lt = logits.T[:N_EXPERTS]
    ex = jnp.exp(lt - jnp.max(lt, axis=0, keepdims=True))
    probs = ex / jnp.sum(ex, axis=0, keepdims=True)
    sel = probs + bias_t
    p = [probs[e:e + 1] for e in range(N_EXPERTS)]
    s = [sel[e:e + 1] for e in range(N_EXPERTS)]
    n = EXPERTS_PER_GROUP
    scores = []
    for g in range(N_EXPERT_GROUPS):
        pair = [s[g * n + a] + s[g * n + b] for a in range(n) for b in range(a + 1, n)]
        scores.append(functools.reduce(jnp.maximum, pair))
    best = jnp.zeros_like(scores[0], dtype=jnp.int32)
    best_score = scores[0]
    for g in range(1, N_EXPERT_GROUPS):
        better = scores[g] > best_score
        best = jnp.where(better, g, best)
        best_score = jnp.where(better, scores[g], best_score)
    cls = jnp.zeros_like(best_score)
    w_a = jnp.zeros_like(best_score)
    w_b = jnp.zeros_like(best_score)
    for g in range(N_EXPERT_GROUPS):
        in_g = best == g
        chosen = []
        for a in range(n):
            rank = jnp.zeros_like(best)
            for b in range(n):
                if b == a:
                    continue
                ahead = (s[g * n + b] > s[g * n + a]) if b > a else (s[g * n + b] >= s[g * n + a])
                rank = rank + ahead.astype(jnp.int32)
            chosen.append(jnp.logical_and(in_g, rank < 2))
        for k in range(N_PAIRS):
            pa, pb = p[g * n + PAIR_A[k]], p[g * n + PAIR_B[k]]
            hit = jnp.logical_and(chosen[PAIR_A[k]], chosen[PAIR_B[k]])
            wsum = pa + pb
            cls = jnp.where(hit, float(g * N_PAIRS + k), cls)
            w_a = jnp.where(hit, pa / wsum, w_a)
            w_b = jnp.where(hit, pb / wsum, w_b)
    return jnp.concatenate([cls, w_a, w_b, jnp.zeros((ROUTE_ROWS - 3, lt.shape[1]), F32)], axis=0)


def _merge_kernel(xc_ref, xl_ref, mixc_ref, mixl_ref, mod_ref, win_ref, wpa_ref, wpf_ref, wpc_ref, wo_ref,
                  l1g_ref, l1b_ref, wrh_ref, wrl_ref, bias_ref, x1_ref, ha_ref, rt_ref):
    m = mod_ref[0, 0]
    in_ctx = pl.program_id(0) < N_CTX_TILES
    branch_w = ((0, ATT_W, wpa_ref), (ATT_W, FOURIER_W, wpf_ref), (ATT_W + FOURIER_W, CHUNK_W, wpc_ref))
    sub_rows = [slice(s * MERGE_SUB, (s + 1) * MERGE_SUB) for s in range(TM // MERGE_SUB)]
    residual = []
    for rows in sub_rows:
        x = jnp.where(in_ctx, xc_ref[rows, :], xl_ref[rows, :])
        mixed = jnp.where(in_ctx, mixc_ref[rows, :], mixl_ref[rows, :])
        h = (_ln(x) * (1.0 + m[1:2]) + m[0:1]).astype(BF16)
        merged = None
        for b, (c0, width, w_ref) in enumerate(branch_w):
            gate = _sigmoid(_dot(h, win_ref[:, MIX_W + b * D_MODEL:MIX_W + (b + 1) * D_MODEL]))
            term = gate * _dot(mixed[:, c0:c0 + width], w_ref[...])
            merged = term if merged is None else merged + term
        mix = _dot(merged.astype(BF16), wo_ref[...])
        residual.append(ALPHA * x + m[2:3] * mix)
    for rows, pre in zip(sub_rows, residual):
        x1 = _ln(pre) * l1g_ref[0] + l1b_ref[0]
        x1_ref[rows, :] = x1
        h2 = _ln(x1) * (1.0 + m[4:5]) + m[3:4]
        h_hi, h_lo = _split_bf16(h2)
        wrh = wrh_ref[...]
        logits = _dot(h_hi, wrh) + _dot(h_lo, wrh) + _dot(h_hi, wrl_ref[...])
        rt = _route(logits, bias_ref[:, rows])
        rt_ref[:, rows] = rt
        ha_ref[rows, :D_MODEL] = h2
        ha_ref[rows, D_MODEL:] = jnp.concatenate([rt, jnp.zeros((LANES - ROUTE_ROWS, MERGE_SUB), F32)], axis=0).T


def _merge_call(layer, x, mixed_ctx, mixed_lat, mod, w_in_b, wpa, wpf, wpc, wo, l1g, l1b, wrh, wrl, bias_t):
    const2 = lambda i: (0, 0)
    per_layer = lambda i: (layer, 0, 0)
    tok = lambda i: (i, 0)
    return pl.pallas_call(
        _merge_kernel,
        out_shape=(jax.ShapeDtypeStruct((T_ALL, D_MODEL), F32),
                   jax.ShapeDtypeStruct((T_ALL, HA_W), F32),
                   jax.ShapeDtypeStruct((ROUTE_ROWS, T_ALL), F32)),
        grid=(T_ALL // TM,),
        in_specs=_group_specs() + _group_specs() + [
            pl.BlockSpec((1, 1, N_MOD, D_MODEL), lambda i: (layer, _cond_row(i, TM), 0, 0)),
            pl.BlockSpec((None, D_MODEL, IN_W), per_layer, pipeline_mode=pl.Buffered(1)),
            pl.BlockSpec((None, ATT_W, D_MODEL), per_layer, pipeline_mode=pl.Buffered(1)),
            pl.BlockSpec((None, FOURIER_W, D_MODEL), per_layer, pipeline_mode=pl.Buffered(1)),
            pl.BlockSpec((None, CHUNK_W, D_MODEL), per_layer, pipeline_mode=pl.Buffered(1)),
            pl.BlockSpec((None, D_MODEL, D_MODEL), per_layer, pipeline_mode=pl.Buffered(1)),
            pl.BlockSpec((1, 1, D_MODEL), per_layer),
            pl.BlockSpec((1, 1, D_MODEL), per_layer),
            pl.BlockSpec((D_MODEL, ROUTER_PAD), const2),
            pl.BlockSpec((D_MODEL, ROUTER_PAD), const2),
            pl.BlockSpec((N_EXPERTS, TM), const2),
        ],
        out_specs=(pl.BlockSpec((TM, D_MODEL), tok),
                   pl.BlockSpec((TM, HA_W), tok),
                   pl.BlockSpec((ROUTE_ROWS, TM), lambda i: (0, i))),
        compiler_params=_cparams("arbitrary"),
        name="merge_route",
    )(*x, mixed_ctx, mixed_lat, mod, w_in_b, wpa, wpf, wpc, wo, l1g, l1b, wrh, wrl, bias_t)


def _sort_kernel(rt_ref, tri_ref, low_ref, pos_ref, tile_ref):
    crow = lax.broadcasted_iota(jnp.int32, (CLASS_PAD, SORT_BLK), 0)
    tri = tri_ref[...]
    n_blk = T_ALL // SORT_BLK
    carry = jnp.zeros((CLASS_PAD, 1), F32)
    ranks = []
    for b in range(n_blk):
        hot = rt_ref[0:1, b * SORT_BLK:(b + 1) * SORT_BLK].astype(jnp.int32) == crow
        hot_f = jnp.where(hot, 1.0, 0.0)
        before = _dot(hot_f.astype(BF16), tri) + carry
        ranks.append(jnp.sum(jnp.where(hot, before, 0.0), axis=0, keepdims=True))
        carry = carry + jnp.sum(hot_f, axis=1, keepdims=True)
    padded = jnp.floor((carry + (TM_MOE - 1.0)) * (1.0 / TM_MOE)) * TM_MOE
    padded = jnp.broadcast_to(padded, (CLASS_PAD, LANES))
    offs = _dot(low_ref[...], padded.astype(BF16))
    for b in range(n_blk):
        hot = rt_ref[0:1, b * SORT_BLK:(b + 1) * SORT_BLK].astype(jnp.int32) == crow
        base = jnp.sum(jnp.where(hot, offs[:, 0:1], 0.0), axis=0, keepdims=True)
        pos_ref[:, b * SORT_BLK:(b + 1) * SORT_BLK] = (base + ranks[b]).astype(jnp.int32)

    start = lax.broadcasted_iota(jnp.int32, (1, LANES), 1).astype(F32) * TM_MOE
    is_class = lax.broadcasted_iota(jnp.int32, (CLASS_PAD, LANES), 0) < N_CLASSES
    ends = jnp.where(is_class, offs + padded, 0.0)
    total = jnp.max(ends, axis=0, keepdims=True)
    valid = start < total
    tcls = jnp.sum(jnp.where(jnp.logical_and(is_class, ends <= start), 1.0, 0.0), axis=0, keepdims=True)
    last = jnp.max(jnp.where(valid, tcls, 0.0), axis=1, keepdims=True)
    tcls = jnp.where(valid, tcls, last)
    grp = functools.reduce(jnp.add, [jnp.where(tcls >= g * N_PAIRS, 1.0, 0.0) for g in range(1, N_EXPERT_GROUPS)])
    pair = tcls - grp * N_PAIRS
    slot_a = functools.reduce(jnp.add, [jnp.where(pair == k, float(PAIR_A[k]), 0.0) for k in range(N_PAIRS)])
    slot_b = functools.reduce(jnp.add, [jnp.where(pair == k, float(PAIR_B[k]), 0.0) for k in range(N_PAIRS)])
    crow_t = lax.broadcasted_iota(jnp.int32, (CLASS_PAD, LANES), 0).astype(F32)
    real_end = jnp.sum(jnp.where(crow_t == tcls, offs + carry, 0.0), axis=0, keepdims=True)
    n_rows = jnp.where(valid, jnp.clip(real_end - start, 0.0, float(TM_MOE)), 0.0)
    rows = [grp * EXPERTS_PER_GROUP + slot_a, grp * EXPERTS_PER_GROUP + slot_b, n_rows]
    tile_ref[...] = jnp.concatenate(rows + [jnp.zeros((8 - len(rows), LANES), F32)], axis=0).astype(jnp.int32)


def _sort_call(rt, tri, low):
    return pl.pallas_call(
        _sort_kernel,
        out_shape=(jax.ShapeDtypeStruct((1, T_ALL), jnp.int32),
                   jax.ShapeDtypeStruct((8, LANES), jnp.int32)),
        name="route_sort",
    )(rt, tri, low)


def _moe_kernel(tile_ref, pos_ref, ha_ref, wga_ref, wua_ref, wda_ref, wgb_ref, wub_ref,
                wdb_ref, out_ref, hbuf, ybuf, src_ref, ssem):
    j = pl.program_id(0)
    slot = j % 2

    def n_rows(tile):
        inside = jnp.logical_and(tile >= 0, tile < N_TILES)
        return jnp.where(inside, tile_ref[TILE_ROWS, jnp.clip(tile, 0, N_TILES - 1)], 0)

    def token_of(tile, r):
        return src_ref[tile * TM_MOE + jnp.minimum(r, tile_ref[TILE_ROWS, tile] - 1)]

    def scatter_copy(tile, buf_slot, r):
        dst = jnp.where(r < tile_ref[TILE_ROWS, tile], token_of(tile, r), T_ALL + buf_slot * TM_MOE + r)
        return pltpu.make_async_copy(ybuf.at[buf_slot, r], out_ref.at[dst], ssem.at[buf_slot])

    def for_groups(tile, fn):
        def body(g, c):
            for k in range(COPY_GROUP):
                fn(g, k)
            return c
        lax.fori_loop(0, (n_rows(tile) + COPY_GROUP - 1) // COPY_GROUP, body, 0)

    def retire(tile, copy_of):
        for_groups(tile, lambda g, k: copy_of(0).wait())

    @pl.when(j == 0)
    def _():
        def place(t, c):
            src_ref[pos_ref[t]] = t
            return c
        lax.fori_loop(0, T_ALL, place, 0, unroll=8)
        hbuf[...] = jnp.zeros(hbuf.shape, F32)
        ybuf[...] = jnp.zeros(ybuf.shape, F32)
        for s in range(2):
            park = pltpu.make_async_copy(ybuf.at[s], out_ref.at[pl.ds(T_ALL + s * TM_MOE, TM_MOE)], ssem.at[s])
            park.start()
            park.wait()

    @pl.when(n_rows(j - 2) > 0)
    def _():
        retire(j - 2, lambda r: scatter_copy(j - 2, slot, r))

    def run_experts(rows):
        rec = hbuf[:rows // SUBLANES].reshape(rows, HA_W)
        h = rec[:, :D_MODEL].astype(BF16)

        def expert(wg_ref, wu_ref, wd_ref, w):
            a = _dot(h, wg_ref[...])
            u = _dot(h, wu_ref[...])
            return _dot(((a * _sigmoid(a)) * u * w).astype(BF16), wd_ref[...])

        o = (expert(wga_ref, wua_ref, wda_ref, rec[:, D_MODEL + 1:D_MODEL + 2])
             + expert(wgb_ref, wub_ref, wdb_ref, rec[:, D_MODEL + 2:D_MODEL + 3]))
        ybuf[slot, :rows] = o.reshape(rows, D_CHUNKS, LANES)

    @pl.when(n_rows(j) > 0)
    def _():
        def gather_row(g, k):
            tok = token_of(j, g * COPY_GROUP + k)
            grp, sub = lax.shift_right_logical(tok, 3), jnp.bitwise_and(tok, SUBLANES - 1)
            hbuf[g, pl.ds(k, 1), :] = ha_ref[grp, pl.ds(sub, 1), :]
        for_groups(j, gather_row)

        @pl.when(n_rows(j) > SMALL_TILE)
        def _():
            run_experts(TM_MOE)

        @pl.when(n_rows(j) <= SMALL_TILE)
        def _():
            run_experts(SMALL_TILE)

        for_groups(j, lambda g, k: scatter_copy(j, slot, g * COPY_GROUP + k).start(priority=k % 2))

    @pl.when(j == N_TILES - 1)
    def _():
        @pl.when(n_rows(j - 1) > 0)
        def _():
            retire(j - 1, lambda r: scatter_copy(j - 1, 1 - slot, r))

        @pl.when(n_rows(j) > 0)
        def _():
            retire(j, lambda r: scatter_copy(j, slot, r))


def _moe_call(layer, tiles, pos, ha, wge, wue, wde):
    up = (None, None, D_MODEL, D_EXPERT)
    down = (None, None, D_EXPERT, D_MODEL)
    slot_a = lambda j, tl, ps: (layer, tl[TILE_SLOT_A, j], 0, 0)
    slot_b = lambda j, tl, ps: (layer, tl[TILE_SLOT_B, j], 0, 0)
    expert_w = pl.BlockSpec
    grid_spec = pltpu.PrefetchScalarGridSpec(
        num_scalar_prefetch=2,
        grid=(N_TILES,),
        in_specs=[
            pl.BlockSpec((T_ALL // SUBLANES, SUBLANES, HA_W), lambda j, tl, ps: (0, 0, 0),
                         pipeline_mode=pl.Buffered(1)),
            expert_w(up, slot_a), expert_w(up, slot_a), expert_w(down, slot_a),
            expert_w(up, slot_b), expert_w(up, slot_b), expert_w(down, slot_b),
        ],
        out_specs=pl.BlockSpec(memory_space=pl.ANY),
        scratch_shapes=[
            pltpu.VMEM((TM_MOE // SUBLANES, SUBLANES, HA_W), F32),
            pltpu.VMEM((2, TM_MOE, D_CHUNKS, LANES), F32),
            pltpu.SMEM((R_ROWS,), jnp.int32),
            pltpu.SemaphoreType.DMA((2,)),
        ],
    )
    return pl.pallas_call(
        _moe_kernel,
        out_shape=jax.ShapeDtypeStruct((T_ALL + N_PARK, D_CHUNKS, LANES), F32),
        grid_spec=grid_spec,
        compiler_params=pltpu.CompilerParams(dimension_semantics=("arbitrary",), vmem_limit_bytes=V7X_VMEM_LIMIT,
                                             has_side_effects=True),
        name="moe_pairs",
    )(tiles, pos,
      ha.reshape(T_ALL // SUBLANES, SUBLANES, HA_W), wge, wue, wde, wge, wue, wde)


def _post_kernel(x1_ref, moe_ref, mod_ref, l2g_ref, l2b_ref, oc_ref, ol_ref):
    m = mod_ref[0, 0]
    moe = moe_ref[...].reshape(TM, D_MODEL)
    y = _ln(ALPHA * x1_ref[...] + m[5:6] * moe) * l2g_ref[0] + l2b_ref[0]
    i = pl.program_id(0)

    @pl.when(i < N_CTX_TILES)
    def _():
        oc_ref[...] = y

    @pl.when(i >= N_CTX_TILES)
    def _():
        ol_ref[...] = y


def _post_call(layer, x1, moe, mod, l2g, l2b):
    tok = lambda i: (i, 0)
    per_layer = lambda i: (layer, 0, 0)
    return pl.pallas_call(
        _post_kernel,
        out_shape=(jax.ShapeDtypeStruct((T_CTX, D_MODEL), F32), jax.ShapeDtypeStruct((T_LAT, D_MODEL), F32)),
        grid=(T_ALL // TM,),
        in_specs=[
            pl.BlockSpec((TM, D_MODEL), tok),
            pl.BlockSpec((TM, D_CHUNKS, LANES), lambda i: (i, 0, 0)),
            pl.BlockSpec((1, 1, N_MOD, D_MODEL), lambda i: (layer, _cond_row(i, TM), 0, 0)),
            pl.BlockSpec((1, 1, D_MODEL), per_layer),
            pl.BlockSpec((1, 1, D_MODEL), per_layer),
        ],
        out_specs=tuple(_group_specs()),
        compiler_params=_cparams("arbitrary"),
        name="post_moe",
    )(x1, moe, mod, l2g, l2b)


def _rope_tables():
    pos = np.arange(DEC_SEQ)
    quarter = HEAD_DIM // 4
    inv = ROPE_THETA ** (-np.arange(quarter, dtype=np.float64) / quarter)
    ang_r = (pos // GRID_W)[:, None] * inv[None, :]
    ang_c = (pos % GRID_W)[:, None] * inv[None, :]
    cos = np.concatenate([np.cos(ang_r)] * 2 + [np.cos(ang_c)] * 2, axis=-1)
    sin = np.concatenate([-np.sin(ang_r), np.sin(ang_r), -np.sin(ang_c), np.sin(ang_c)], axis=-1)
    return (jnp.asarray(np.tile(cos, (1, N_Q_HEADS)).astype(np.float32)),
            jnp.asarray(np.tile(sin, (1, N_Q_HEADS)).astype(np.float32)))


def _dft_mats(n, scale):
    j = np.arange(n)
    ang = ((j[:, None] * j[None, :]) % n) * (2 * np.pi / n)
    return np.cos(ang) * scale, np.sin(ang) * scale


def _block_diag(m, reps):
    return np.kron(np.eye(reps), m)


def _const_bf16(a):
    return jnp.asarray(np.asarray(a, np.float32)).astype(BF16)


def _const_01(a):
    return jnp.asarray(np.asarray(a, np.float32).astype(BF16))


def kernel(x_prompt, x_sample, cache_k, cache_v, c, c_ctx, w_in, q_norm_g, k_norm_g, w_proj_att,
           w_proj_fourier, w_proj_chunk, w_out, chunk_ln_g, chunk_ln_b, chunk_ws, chunk_bs, w_ada, b_ada,
           ln1_g, ln1_b, ln2_g, ln2_b, w_router, router_bias, w_gate_e, w_up_e, w_down_e):
    x = (x_prompt.reshape(T_CTX, D_MODEL), x_sample.reshape(T_LAT, D_MODEL))
    cond = jnp.concatenate([c_ctx[None, :], c, jnp.zeros((N_COND - 1 - DEC_BATCH, D_MODEL), F32)], axis=0)
    mod = _mod_call(cond, w_ada, b_ada).reshape(DEPTH, N_COND, N_MOD, D_MODEL)

    cos_t, sin_t = _rope_tables()
    bd_heads = _const_01(_block_diag(np.ones((HEAD_DIM, HEAD_DIM)), N_Q_HEADS))
    c64, s64 = _dft_mats(FOURIER_GROUP_W, 1.0)
    bdc = _const_bf16(_block_diag(c64, FOURIER_W // FOURIER_GROUP_W))
    bds = _const_bf16(_block_diag(s64, FOURIER_W // FOURIER_GROUP_W))
    dft = {}
    for seq in (SEQ, DEC_SEQ):
        cs, ss = _dft_mats(seq, 1.0 / math.sqrt(seq * FOURIER_GROUP_W))
        dft[seq] = _const_bf16(np.concatenate([cs, -ss], axis=1))
    ctx_k = cache_k.reshape(DEC_BATCH, DEPTH, PAST_LEN, KV_W)
    ctx_v = cache_v.reshape(DEC_BATCH, DEPTH, PAST_LEN, KV_W)
    wr = jnp.pad(w_router, ((0, 0), (0, ROUTER_PAD - N_EXPERTS)))
    wr_hi = wr.astype(BF16)
    wr_lo = (wr - wr_hi.astype(F32)).astype(BF16)
    bias_t = jnp.broadcast_to(router_bias[:, None], (N_EXPERTS, TM))
    tri = _const_01(np.triu(np.ones((SORT_BLK, SORT_BLK)), 1))
    low = _const_01(np.tril(np.ones((CLASS_PAD, CLASS_PAD)), -1))

    w_in_b = w_in.astype(BF16)
    wpa, wpf, wpc, wo = (w.astype(BF16) for w in (w_proj_att, w_proj_fourier, w_proj_chunk, w_out))
    wge, wue, wde = (w.astype(BF16) for w in (w_gate_e, w_up_e, w_down_e))
    qg = jnp.tile(q_norm_g, (1, N_Q_HEADS))[:, None, :]
    kg = jnp.tile(k_norm_g, (1, N_KV_HEADS))[:, None, :]
    ws = chunk_ws.astype(BF16)
    bsm = jnp.repeat(jnp.swapaxes(chunk_bs, 1, 2), CHUNK_GROUP_W, axis=2)
    row3 = lambda a: a[:, None, :]

    new_k, new_v = [], []
    for l in range(DEPTH):
        mix = _proj_call(l, x, mod, w_in_b, qg, kg, cos_t, sin_t, bd_heads)
        new_k.append(mix[:T_CTX, ATT_W:ATT_W + KV_W].reshape(BATCH, SEQ, N_KV_HEADS, HEAD_DIM))
        new_v.append(mix[:T_CTX, ATT_W + KV_W:ATT_W + 2 * KV_W].reshape(BATCH, SEQ, N_KV_HEADS, HEAD_DIM))
        mixer_args = (bdc, bds, ws, bsm, row3(chunk_ln_g), row3(chunk_ln_b))
        mixed_ctx = _mixer_call(l, mix, None, None, dft[SEQ], *mixer_args, latent=False)
        mixed_lat = _mixer_call(l, mix, ctx_k, ctx_v, dft[DEC_SEQ], *mixer_args, latent=True)
        x1, ha, rt = _merge_call(l, x, mixed_ctx, mixed_lat, mod, w_in_b, wpa, wpf, wpc, wo,
                                 row3(ln1_g), row3(ln1_b), wr_hi, wr_lo, bias_t)
        pos2d, tiles = _sort_call(rt, tri, low)
        moe = _moe_call(l, tiles, pos2d[0], ha, wge, wue, wde)
        x = _post_call(l, x1, moe, mod, row3(ln2_g), row3(ln2_b))

    y_prompt = x[0].reshape(BATCH, SEQ, D_MODEL)
    y_sample = x[1].reshape(DEC_BATCH, DEC_SEQ, D_MODEL)
    return (y_prompt, y_sample, jnp.stack(new_k, axis=1), jnp.stack(new_v, axis=1))
```

```python
import functools
import math

import jax
import jax.numpy as jnp
import numpy as np
from jax import lax
from jax.experimental import pallas as pl
from jax.experimental.pallas import tpu as pltpu

F32 = jnp.float32
BF16 = jnp.bfloat16

D_MODEL = 1024
BATCH = 16
SEQ = 256
DEPTH = 2
DEC_BATCH = 2
DEC_SEQ = 1024
PAST_LEN = 256
GRID_W = 64
N_Q_HEADS = 8
N_KV_HEADS = 2
HEAD_DIM = 64
Q_PER_KV = N_Q_HEADS // N_KV_HEADS
ATT_W = N_Q_HEADS * HEAD_DIM
KV_W = N_KV_HEADS * HEAD_DIM
ROPE_THETA = 10000.0
FOURIER_GROUP_W = 64
FOURIER_W = 256
N_CHUNK_GROUPS = 4
CHUNK_GROUP_W = 64
CHUNK_W = 256
CHUNK = 128
N_EXPERTS = 16
N_EXPERT_GROUPS = 4
EXPERTS_PER_GROUP = 4
D_EXPERT = 512
ALPHA = (2 * DEPTH) ** 0.25
LN_EPS = 1e-6
RMS_EPS = 1e-6

LANES = 128
SUBLANES = 8
T_CTX = BATCH * SEQ
T_LAT = DEC_BATCH * DEC_SEQ
T_ALL = T_CTX + T_LAT
IN_W = ATT_W + 2 * KV_W + FOURIER_W + 2 * CHUNK_W + 3 * D_MODEL
MIX_W = ATT_W + 2 * KV_W + FOURIER_W + 2 * CHUNK_W
N_MOD = 6
N_COND = 8
ROUTER_PAD = LANES
TM = 512
TQ = 256
MERGE_SUB = 256
PROJ_SUB = 256
V7X_VMEM_LIMIT = 56 * 1024 * 1024

PAIR_A = (0, 0, 0, 1, 1, 3)
PAIR_B = (1, 2, 3, 3, 2, 2)
N_PAIRS = len(PAIR_A)
N_CLASSES = N_EXPERT_GROUPS * N_PAIRS
CLASS_PAD = 32
ROUTE_ROWS = 8
TM_MOE = 256
N_TILES = T_ALL // TM_MOE + N_CLASSES
R_ROWS = N_TILES * TM_MOE
SORT_BLK = 512
D_CHUNKS = D_MODEL // LANES
HA_W = D_MODEL + LANES
TILE_SLOT_A, TILE_SLOT_B, TILE_ROWS = 0, 1, 2
SMALL_TILE = 64
N_PARK = 2 * TM_MOE
COPY_GROUP = 8


def _cparams(*sem):
    return pltpu.CompilerParams(dimension_semantics=sem, vmem_limit_bytes=V7X_VMEM_LIMIT)


def _cond_row(i, tm):
    n_ctx = T_CTX // tm
    return jnp.where(i < n_ctx, 0, 1 + (i - n_ctx) // (DEC_SEQ // tm))


N_CTX_TILES = T_CTX // TM


def _group_specs():
    return [pl.BlockSpec((TM, D_MODEL), lambda i: (jnp.minimum(i, N_CTX_TILES - 1), 0)),
            pl.BlockSpec((TM, D_MODEL), lambda i: (jnp.maximum(i - N_CTX_TILES, 0), 0))]


def _group_tile(ctx_ref, lat_ref):
    return jnp.where(pl.program_id(0) < N_CTX_TILES, ctx_ref[...], lat_ref[...])


def _ln(x):
    mu = jnp.mean(x, axis=-1, keepdims=True)
    xc = x - mu
    var = jnp.mean(xc * xc, axis=-1, keepdims=True)
    return xc * lax.rsqrt(var + LN_EPS)


def _split_bf16(x):
    hi = x.astype(BF16)
    lo = (x - hi.astype(F32)).astype(BF16)
    return hi, lo


def _dot(a, b):
    return jnp.dot(a, b, preferred_element_type=F32)


def _sigmoid(x):
    return 1.0 / (1.0 + jnp.exp(-x))


def _gelu_tanh(x):
    c = np.float32(np.sqrt(2 / np.pi))
    return x * (0.5 * (1.0 + jnp.tanh(c * (x + 0.044715 * (x * x * x)))))


def _mod_kernel(c_ref, w_ref, b_ref, o_ref):
    c = c_ref[...]
    s = c * _sigmoid(c)
    o_ref[0] = _dot(s.astype(BF16), w_ref[0].astype(BF16)) + b_ref[0]


def _mod_call(cond, w_ada, b_ada):
    tn = 1536
    n = N_MOD * D_MODEL
    return pl.pallas_call(
        _mod_kernel,
        out_shape=jax.ShapeDtypeStruct((DEPTH, N_COND, n), F32),
        grid=(DEPTH, n // tn),
        in_specs=[
            pl.BlockSpec((N_COND, D_MODEL), lambda l, j: (0, 0)),
            pl.BlockSpec((1, D_MODEL, tn), lambda l, j: (l, 0, j)),
            pl.BlockSpec((1, 1, tn), lambda l, j: (l, 0, j)),
        ],
        out_specs=pl.BlockSpec((1, N_COND, tn), lambda l, j: (l, 0, j)),
        compiler_params=_cparams("arbitrary", "arbitrary"),
        name="adaln_mod",
    )(cond, w_ada, b_ada.reshape(DEPTH, 1, n))


def _head_rms(q, bd, g):
    hi, lo = _split_bf16(q * q)
    ssum = _dot(hi, bd) + _dot(lo, bd)
    return q * lax.rsqrt(ssum * (1.0 / HEAD_DIM) + RMS_EPS) * g


def _rope(x, cos, sin_signed):
    w = x.shape[-1]
    lane = lax.broadcasted_iota(jnp.int32, x.shape, 1)
    swapped = jnp.where((lane % 32) < 16, pltpu.roll(x, w - 16, 1), pltpu.roll(x, 16, 1))
    return x * cos + swapped * sin_signed


def _proj_kernel(xc_ref, xl_ref, mod_ref, w_ref, qg_ref, kg_ref, cos_ref, sin_ref, bd_ref, kc_in, vc_in,
                 o_ref, kc_ref, vc_ref):
    del kc_in, vc_in
    m = mod_ref[0, 0]
    in_ctx = pl.program_id(0) < N_CTX_TILES
    bd = bd_ref[...]
    sub_rows = [slice(s * PROJ_SUB, (s + 1) * PROJ_SUB) for s in range(TM // PROJ_SUB)]
    projs = []
    for rows in sub_rows:
        x = jnp.where(in_ctx, xc_ref[rows, :], xl_ref[rows, :])
        h = _ln(x) * (1.0 + m[1:2]) + m[0:1]
        proj = _dot(h.astype(BF16), w_ref[...])
        o_ref[rows, ATT_W + KV_W:] = proj[:, ATT_W + KV_W:]
        projs.append(proj[:, :ATT_W + 2 * KV_W])
    keys = []
    for rows, proj in zip(sub_rows, projs):
        qn = _head_rms(proj[:, :ATT_W], bd, qg_ref[0])
        kn = _head_rms(proj[:, ATT_W:ATT_W + KV_W], bd[:KV_W, :KV_W], kg_ref[0])
        cos = cos_ref[rows, :]
        sin = sin_ref[rows, :]
        o_ref[rows, :ATT_W] = jnp.where(in_ctx, qn, _rope(qn, cos, sin))
        o_ref[rows, ATT_W:ATT_W + KV_W] = jnp.where(in_ctx, kn, _rope(kn, cos[:, :KV_W], sin[:, :KV_W]))
        keys.append(kn)

    @pl.when(in_ctx)
    def _():
        for s in range(len(sub_rows)):
            kc_ref[s] = keys[s]
            vc_ref[s] = projs[s][:, ATT_W + KV_W:]


def _proj_call(layer, x, mod, w_in_b, qg, kg, cos_t, sin_t, bd, k_cache, v_cache):
    assert PROJ_SUB == SEQ
    n_ctx = T_CTX // TM
    per_seq = DEC_SEQ // TM

    def rope_idx(i):
        return (jnp.where(i < n_ctx, 0, (i - n_ctx) % per_seq), 0)

    cache_spec = pl.BlockSpec((TM // SEQ, None, SEQ, KV_W), lambda i: (jnp.minimum(i, n_ctx - 1), layer, 0, 0))
    cache_shape = jax.ShapeDtypeStruct((BATCH, DEPTH, SEQ, KV_W), F32)
    n_in = 11
    return pl.pallas_call(
        _proj_kernel,
        out_shape=(jax.ShapeDtypeStruct((T_ALL, MIX_W), F32), cache_shape, cache_shape),
        grid=(T_ALL // TM,),
        in_specs=_group_specs() + [
            pl.BlockSpec((1, 1, N_MOD, D_MODEL), lambda i: (layer, _cond_row(i, TM), 0, 0)),
            pl.BlockSpec((None, D_MODEL, MIX_W), lambda i: (layer, 0, 0)),
            pl.BlockSpec((1, 1, ATT_W), lambda i: (layer, 0, 0)),
            pl.BlockSpec((1, 1, KV_W), lambda i: (layer, 0, 0)),
            pl.BlockSpec((TM, ATT_W), rope_idx),
            pl.BlockSpec((TM, ATT_W), rope_idx),
            pl.BlockSpec((ATT_W, ATT_W), lambda i: (0, 0)),
            pl.BlockSpec(memory_space=pl.ANY),
            pl.BlockSpec(memory_space=pl.ANY),
        ],
        out_specs=(pl.BlockSpec((TM, MIX_W), lambda i: (i, 0)), cache_spec, cache_spec),
        input_output_aliases={n_in - 2: 1, n_in - 1: 2},
        compiler_params=_cparams("arbitrary"),
        name="in_proj",
    )(*x, mod, w_in_b, qg, kg, cos_t, sin_t, bd, k_cache, v_cache)


def _attention_tile(q, k_parts, v_parts, phase_major):
    def kv_of(h, parts):
        g = h // Q_PER_KV
        return [p[:, g * HEAD_DIM:(g + 1) * HEAD_DIM] for p in parts]

    def scores(h):
        qh = q[:, h * HEAD_DIM:(h + 1) * HEAD_DIM]
        return [lax.dot_general(qh, k, (((1,), (1,)), ((), ())), preferred_element_type=F32)
                for k in kv_of(h, k_parts)]

    if phase_major:
        all_ss = [scores(h) for h in range(N_Q_HEADS)]
        all_m = [functools.reduce(jnp.maximum, [jnp.max(s, axis=-1, keepdims=True) for s in ss]) for ss in all_ss]
        all_es = [[jnp.exp(s - m) for s in ss] for ss, m in zip(all_ss, all_m)]
        all_den = [functools.reduce(jnp.add, [jnp.sum(e, axis=-1, keepdims=True) for e in es]) for es in all_es]
        outs = [functools.reduce(jnp.add, [_dot(e.astype(BF16), v) for e, v in zip(es, kv_of(h, v_parts))])
                * (1.0 / den) for h, (es, den) in enumerate(zip(all_es, all_den))]
        return jnp.concatenate(outs, axis=-1)
    outs = []
    ss_next = scores(0)
    for h in range(N_Q_HEADS):
        ss = ss_next
        if h + 1 < N_Q_HEADS:
            ss_next = scores(h + 1)
        m = functools.reduce(jnp.maximum, [jnp.max(s, axis=-1, keepdims=True) for s in ss])
        es = [jnp.exp(s - m) for s in ss]
        denom = functools.reduce(jnp.add, [jnp.sum(e, axis=-1, keepdims=True) for e in es])
        o = functools.reduce(jnp.add, [_dot(e.astype(BF16), v) for e, v in zip(es, kv_of(h, v_parts))])
        outs.append(o * (1.0 / denom))
    return jnp.concatenate(outs, axis=-1)


def _mixer_kernel(*refs, seq, latent):
    if latent:
        (mix_ref, ck_ref, cv_ref, dft_ref, bdc_ref, bds_ref, ws_ref, bsm_ref, clg_ref, clb_ref, o_ref) = refs
    else:
        (mix_ref, dft_ref, bdc_ref, bds_ref, ws_ref, bsm_ref, clg_ref, clb_ref, o_ref) = refs

    k_new = mix_ref[:, ATT_W:ATT_W + KV_W].astype(BF16)
    v_new = mix_ref[:, ATT_W + KV_W:ATT_W + 2 * KV_W].astype(BF16)
    if latent:
        k_parts = [ck_ref[...].astype(BF16), k_new]
        v_parts = [cv_ref[...].astype(BF16), v_new]
    else:
        k_parts, v_parts = [k_new], [v_new]

    def q_tile(t, carry):
        r0 = pl.multiple_of(t * TQ, TQ)
        q = (mix_ref[pl.ds(r0, TQ), :ATT_W] * (HEAD_DIM ** -0.5)).astype(BF16)
        o_ref[pl.ds(r0, TQ), :ATT_W] = _attention_tile(q, k_parts, v_parts, phase_major=not latent).astype(BF16)
        return carry

    lax.fori_loop(0, seq // TQ, q_tile, 0)

    f_hi, f_lo = _split_bf16(mix_ref[:, ATT_W + 2 * KV_W:ATT_W + 2 * KV_W + FOURIER_W])
    bdc = bdc_ref[...]
    bds = bds_ref[...]
    y = jnp.concatenate([_dot(f_hi, bdc) + _dot(f_lo, bdc), _dot(f_hi, bds) + _dot(f_lo, bds)], axis=0)
    y_hi, y_lo = _split_bf16(y)
    dft = dft_ref[...]
    four = _dot(dft, y_hi) + _dot(dft, y_lo)
    o_ref[:, ATT_W:ATT_W + FOURIER_W] = four.astype(BF16)

    c0 = ATT_W + 2 * KV_W + FOURIER_W
    u = _gelu_tanh(mix_ref[:, c0:c0 + CHUNK_W])
    vn = (_ln(_gelu_tanh(mix_ref[:, c0 + CHUNK_W:c0 + 2 * CHUNK_W])) * clg_ref[0] + clb_ref[0]).astype(BF16)
    lane = lax.broadcasted_iota(jnp.int32, (CHUNK, CHUNK_W), 1)
    bsm = bsm_ref[...]
    for c in range(seq // CHUNK):
        vc = vn[c * CHUNK:(c + 1) * CHUNK]
        sv = bsm
        for g in range(N_CHUNK_GROUPS):
            vg = jnp.where(lane // CHUNK_GROUP_W == g, vc, jnp.zeros_like(vc))
            sv = sv + _dot(ws_ref[g], vg)
        o_ref[c * CHUNK:(c + 1) * CHUNK, ATT_W + FOURIER_W:] = (u[c * CHUNK:(c + 1) * CHUNK] * sv).astype(BF16)


def _mixer_call(layer, mix, ctx_k, ctx_v, dft, bdc, bds, ws, bsm, clg, clb, *, latent):
    seq = DEC_SEQ if latent else SEQ
    nb = DEC_BATCH if latent else BATCH
    row0 = (T_CTX // seq) if latent else 0
    const2 = lambda b: (0, 0)
    per_layer = lambda b: (layer, 0, 0)
    in_specs = [pl.BlockSpec((seq, MIX_W), lambda b: (row0 + b, 0))]
    args = [mix]
    if latent:
        in_specs += [pl.BlockSpec((None, None, PAST_LEN, KV_W), lambda b: (b, layer, 0, 0))] * 2
        args += [ctx_k, ctx_v]
    in_specs += [
        pl.BlockSpec((seq, 2 * seq), const2),
        pl.BlockSpec((FOURIER_W, FOURIER_W), const2),
        pl.BlockSpec((FOURIER_W, FOURIER_W), const2),
        pl.BlockSpec((None, N_CHUNK_GROUPS, CHUNK, CHUNK), lambda b: (layer, 0, 0, 0)),
        pl.BlockSpec((None, CHUNK, CHUNK_W), per_layer),
        pl.BlockSpec((1, 1, CHUNK_W), per_layer),
        pl.BlockSpec((1, 1, CHUNK_W), per_layer),
    ]
    args += [dft, bdc, bds, ws, bsm, clg, clb]
    return pl.pallas_call(
        functools.partial(_mixer_kernel, seq=seq, latent=latent),
        out_shape=jax.ShapeDtypeStruct((nb * seq, D_MODEL), BF16),
        grid=(nb,),
        in_specs=in_specs,
        out_specs=pl.BlockSpec((seq, D_MODEL), lambda b: (b, 0)),
        compiler_params=_cparams("arbitrary"),
        name="mixer_latent" if latent else "mixer_context",
    )(*args)


def _route(logits, bias_t):
    lt = logits.T[:N_EXPERTS]
    ex = jnp.exp(lt - jnp.max(lt, axis=0, keepdims=True))
    probs = ex / jnp.sum(ex, axis=0, keepdims=True)
    sel = probs + bias_t
    p = [probs[e:e + 1] for e in range(N_EXPERTS)]
    s = [sel[e:e + 1] for e in range(N_EXPERTS)]
    n = EXPERTS_PER_GROUP
    scores = []
    for g in range(N_EXPERT_GROUPS):
        pair = [s[g * n + a] + s[g * n + b] for a in range(n) for b in range(a + 1, n)]
        scores.append(functools.reduce(jnp.maximum, pair))
    best = jnp.zeros_like(scores[0], dtype=jnp.int32)
    best_score = scores[0]
    for g in range(1, N_EXPERT_GROUPS):
        better = scores[g] > best_score
        best = jnp.where(better, g, best)
        best_score = jnp.where(better, scores[g], best_score)
    cls = jnp.zeros_like(best_score)
    w_a = jnp.zeros_like(best_score)
    w_b = jnp.zeros_like(best_score)
    for g in range(N_EXPERT_GROUPS):
        in_g = best == g
        chosen = []
        for a in range(n):
            rank = jnp.zeros_like(best)
            for b in range(n):
                if b == a:
                    continue
                ahead = (s[g * n + b] > s[g * n + a]) if b > a else (s[g * n + b] >= s[g * n + a])
                rank = rank + ahead.astype(jnp.int32)
            chosen.append(jnp.logical_and(in_g, rank < 2))
        for k in range(N_PAIRS):
            pa, pb = p[g * n + PAIR_A[k]], p[g * n + PAIR_B[k]]
            hit = jnp.logical_and(chosen[PAIR_A[k]], chosen[PAIR_B[k]])
            wsum = pa + pb
            cls = jnp.where(hit, float(g * N_PAIRS + k), cls)
            w_a = jnp.where(hit, pa / wsum, w_a)
            w_b = jnp.where(hit, pb / wsum, w_b)
    return jnp.concatenate([cls, w_a, w_b, jnp.zeros((ROUTE_ROWS - 3, lt.shape[1]), F32)], axis=0)


def _merge_kernel(xc_ref, xl_ref, mixc_ref, mixl_ref, mod_ref, win_ref, wpa_ref, wpf_ref, wpc_ref, wo_ref,
                  l1g_ref, l1b_ref, wrh_ref, wrl_ref, bias_ref, x1_ref, ha_ref, rt_ref):
    m = mod_ref[0, 0]
    in_ctx = pl.program_id(0) < N_CTX_TILES
    branch_w = ((0, ATT_W, wpa_ref), (ATT_W, FOURIER_W, wpf_ref), (ATT_W + FOURIER_W, CHUNK_W, wpc_ref))
    sub_rows = [slice(s * MERGE_SUB, (s + 1) * MERGE_SUB) for s in range(TM // MERGE_SUB)]
    residual = []
    for rows in sub_rows:
        x = jnp.where(in_ctx, xc_ref[rows, :], xl_ref[rows, :])
        mixed = jnp.where(in_ctx, mixc_ref[rows, :], mixl_ref[rows, :])
        h = (_ln(x) * (1.0 + m[1:2]) + m[0:1]).astype(BF16)
        merged = None
        for b, (c0, width, w_ref) in enumerate(branch_w):
            gate = _sigmoid(_dot(h, win_ref[:, MIX_W + b * D_MODEL:MIX_W + (b + 1) * D_MODEL]))
            term = gate * _dot(mixed[:, c0:c0 + width], w_ref[...])
            merged = term if merged is None else merged + term
        mix = _dot(merged.astype(BF16), wo_ref[...])
        residual.append(ALPHA * x + m[2:3] * mix)
    for rows, pre in zip(sub_rows, residual):
        x1 = _ln(pre) * l1g_ref[0] + l1b_ref[0]
        x1_ref[rows, :] = x1
        h2 = _ln(x1) * (1.0 + m[4:5]) + m[3:4]
        h_hi, h_lo = _split_bf16(h2)
        wrh = wrh_ref[...]
        logits = _dot(h_hi, wrh) + _dot(h_lo, wrh) + _dot(h_hi, wrl_ref[...])
        rt = _route(logits, bias_ref[:, rows])
        rt_ref[:, rows] = rt
        ha_ref[rows, :D_MODEL] = h2
        ha_ref[rows, D_MODEL:] = jnp.concatenate([rt, jnp.zeros((LANES - ROUTE_ROWS, MERGE_SUB), F32)], axis=0).T


def _merge_call(layer, x, mixed_ctx, mixed_lat, mod, w_in_b, wpa, wpf, wpc, wo, l1g, l1b, wrh, wrl, bias_t):
    const2 = lambda i: (0, 0)
    per_layer = lambda i: (layer, 0, 0)
    tok = lambda i: (i, 0)
    return pl.pallas_call(
        _merge_kernel,
        out_shape=(jax.ShapeDtypeStruct((T_ALL, D_MODEL), F32),
                   jax.ShapeDtypeStruct((T_ALL, HA_W), F32),
                   jax.ShapeDtypeStruct((ROUTE_ROWS, T_ALL), F32)),
        grid=(T_ALL // TM,),
        in_specs=_group_specs() + _group_specs() + [
            pl.BlockSpec((1, 1, N_MOD, D_MODEL), lambda i: (layer, _cond_row(i, TM), 0, 0)),
            pl.BlockSpec((None, D_MODEL, IN_W), per_layer, pipeline_mode=pl.Buffered(1)),
            pl.BlockSpec((None, ATT_W, D_MODEL), per_layer, pipeline_mode=pl.Buffered(1)),
            pl.BlockSpec((None, FOURIER_W, D_MODEL), per_layer, pipeline_mode=pl.Buffered(1)),
            pl.BlockSpec((None, CHUNK_W, D_MODEL), per_layer, pipeline_mode=pl.Buffered(1)),
            pl.BlockSpec((None, D_MODEL, D_MODEL), per_layer, pipeline_mode=pl.Buffered(1)),
            pl.BlockSpec((1, 1, D_MODEL), per_layer),
            pl.BlockSpec((1, 1, D_MODEL), per_layer),
            pl.BlockSpec((D_MODEL, ROUTER_PAD), const2),
            pl.BlockSpec((D_MODEL, ROUTER_PAD), const2),
            pl.BlockSpec((N_EXPERTS, TM), const2),
        ],
        out_specs=(pl.BlockSpec((TM, D_MODEL), tok),
                   pl.BlockSpec((TM, HA_W), tok),
                   pl.BlockSpec((ROUTE_ROWS, TM), lambda i: (0, i))),
        compiler_params=_cparams("arbitrary"),
        name="merge_route",
    )(*x, mixed_ctx, mixed_lat, mod, w_in_b, wpa, wpf, wpc, wo, l1g, l1b, wrh, wrl, bias_t)


def _sort_kernel(rt_ref, tri_ref, low_ref, pos_ref, tile_ref):
    crow = lax.broadcasted_iota(jnp.int32, (CLASS_PAD, SORT_BLK), 0)
    tri = tri_ref[...]
    n_blk = T_ALL // SORT_BLK
    carry = jnp.zeros((CLASS_PAD, 1), F32)
    ranks = []
    for b in range(n_blk):
        hot = rt_ref[0:1, b * SORT_BLK:(b + 1) * SORT_BLK].astype(jnp.int32) == crow
        hot_f = jnp.where(hot, 1.0, 0.0)
        before = _dot(hot_f.astype(BF16), tri) + carry
        ranks.append(jnp.sum(jnp.where(hot, before, 0.0), axis=0, keepdims=True))
        carry = carry + jnp.sum(hot_f, axis=1, keepdims=True)
    padded = jnp.floor((carry + (TM_MOE - 1.0)) * (1.0 / TM_MOE)) * TM_MOE
    padded = jnp.broadcast_to(padded, (CLASS_PAD, LANES))
    offs = _dot(low_ref[...], padded.astype(BF16))
    for b in range(n_blk):
        hot = rt_ref[0:1, b * SORT_BLK:(b + 1) * SORT_BLK].astype(jnp.int32) == crow
        base = jnp.sum(jnp.where(hot, offs[:, 0:1], 0.0), axis=0, keepdims=True)
        pos_ref[:, b * SORT_BLK:(b + 1) * SORT_BLK] = (base + ranks[b]).astype(jnp.int32)

    start = lax.broadcasted_iota(jnp.int32, (1, LANES), 1).astype(F32) * TM_MOE
    is_class = lax.broadcasted_iota(jnp.int32, (CLASS_PAD, LANES), 0) < N_CLASSES
    ends = jnp.where(is_class, offs + padded, 0.0)
    total = jnp.max(ends, axis=0, keepdims=True)
    valid = start < total
    tcls = jnp.sum(jnp.where(jnp.logical_and(is_class, ends <= start), 1.0, 0.0), axis=0, keepdims=True)
    last = jnp.max(jnp.where(valid, tcls, 0.0), axis=1, keepdims=True)
    tcls = jnp.where(valid, tcls, last)
    grp = functools.reduce(jnp.add, [jnp.where(tcls >= g * N_PAIRS, 1.0, 0.0) for g in range(1, N_EXPERT_GROUPS)])
    pair = tcls - grp * N_PAIRS
    slot_a = functools.reduce(jnp.add, [jnp.where(pair == k, float(PAIR_A[k]), 0.0) for k in range(N_PAIRS)])
    slot_b = functools.reduce(jnp.add, [jnp.where(pair == k, float(PAIR_B[k]), 0.0) for k in range(N_PAIRS)])
    crow_t = lax.broadcasted_iota(jnp.int32, (CLASS_PAD, LANES), 0).astype(F32)
    real_end = jnp.sum(jnp.where(crow_t == tcls, offs + carry, 0.0), axis=0, keepdims=True)
    n_rows = jnp.where(valid, jnp.clip(real_end - start, 0.0, float(TM_MOE)), 0.0)
    rows = [grp * EXPERTS_PER_GROUP + slot_a, grp * EXPERTS_PER_GROUP + slot_b, n_rows]
    tile_ref[...] = jnp.concatenate(rows + [jnp.zeros((8 - len(rows), LANES), F32)], axis=0).astype(jnp.int32)


def _sort_call(rt, tri, low):
    return pl.pallas_call(
        _sort_kernel,
        out_shape=(jax.ShapeDtypeStruct((1, T_ALL), jnp.int32),
                   jax.ShapeDtypeStruct((8, LANES), jnp.int32)),
        name="route_sort",
    )(rt, tri, low)


def _moe_kernel(tile_ref, pos_ref, ha_ref, wga_ref, wua_ref, wda_ref, wgb_ref, wub_ref,
                wdb_ref, out_ref, hbuf, ybuf, src_ref, ssem):
    j = pl.program_id(0)
    slot = j % 2

    def n_rows(tile):
        inside = jnp.logical_and(tile >= 0, tile < N_TILES)
        return jnp.where(inside, tile_ref[TILE_ROWS, jnp.clip(tile, 0, N_TILES - 1)], 0)

    def token_of(tile, r):
        return src_ref[tile * TM_MOE + jnp.minimum(r, tile_ref[TILE_ROWS, tile] - 1)]

    def scatter_copy(tile, buf_slot, r):
        dst = jnp.where(r < tile_ref[TILE_ROWS, tile], token_of(tile, r), T_ALL + buf_slot * TM_MOE + r)
        return pltpu.make_async_copy(ybuf.at[buf_slot, r], out_ref.at[dst], ssem.at[buf_slot])

    def for_groups(tile, fn):
        def body(g, c):
            for k in range(COPY_GROUP):
                fn(g, k)
            return c
        lax.fori_loop(0, (n_rows(tile) + COPY_GROUP - 1) // COPY_GROUP, body, 0)

    def retire(tile, copy_of):
        for_groups(tile, lambda g, k: copy_of(0).wait())

    @pl.when(j == 0)
    def _():
        def place(t, c):
            src_ref[pos_ref[t]] = t
            return c
        lax.fori_loop(0, T_ALL, place, 0, unroll=8)
        hbuf[...] = jnp.zeros(hbuf.shape, F32)
        ybuf[...] = jnp.zeros(ybuf.shape, F32)
        for s in range(2):
            park = pltpu.make_async_copy(ybuf.at[s], out_ref.at[pl.ds(T_ALL + s * TM_MOE, TM_MOE)], ssem.at[s])
            park.start()
            park.wait()

    @pl.when(n_rows(j - 2) > 0)
    def _():
        retire(j - 2, lambda r: scatter_copy(j - 2, slot, r))

    def run_experts(rows):
        rec = hbuf[:rows // SUBLANES].reshape(rows, HA_W)
        h = rec[:, :D_MODEL].astype(BF16)

        def expert(wg_ref, wu_ref, wd_ref, w):
            a = _dot(h, wg_ref[...])
            u = _dot(h, wu_ref[...])
            return _dot(((a * _sigmoid(a)) * u * w).astype(BF16), wd_ref[...])

        o = (expert(wga_ref, wua_ref, wda_ref, rec[:, D_MODEL + 1:D_MODEL + 2])
             + expert(wgb_ref, wub_ref, wdb_ref, rec[:, D_MODEL + 2:D_MODEL + 3]))
        ybuf[slot, :rows] = o.reshape(rows, D_CHUNKS, LANES)

    @pl.when(n_rows(j) > 0)
    def _():
        def gather_row(g, k):
            tok = token_of(j, g * COPY_GROUP + k)
            grp, sub = lax.shift_right_logical(tok, 3), jnp.bitwise_and(tok, SUBLANES - 1)
            hbuf[g, pl.ds(k, 1), :] = ha_ref[grp, pl.ds(sub, 1), :]
        for_groups(j, gather_row)

        @pl.when(n_rows(j) > SMALL_TILE)
        def _():
            run_experts(TM_MOE)

        @pl.when(n_rows(j) <= SMALL_TILE)
        def _():
            run_experts(SMALL_TILE)

        for_groups(j, lambda g, k: scatter_copy(j, slot, g * COPY_GROUP + k).start(priority=k % 2))

    @pl.when(j == N_TILES - 1)
    def _():
        @pl.when(n_rows(j - 1) > 0)
        def _():
            retire(j - 1, lambda r: scatter_copy(j - 1, 1 - slot, r))

        @pl.when(n_rows(j) > 0)
        def _():
            retire(j, lambda r: scatter_copy(j, slot, r))


def _moe_call(layer, tiles, pos, ha, wge, wue, wde):
    up = (None, None, D_MODEL, D_EXPERT)
    down = (None, None, D_EXPERT, D_MODEL)
    slot_a = lambda j, tl, ps: (layer, tl[TILE_SLOT_A, j], 0, 0)
    slot_b = lambda j, tl, ps: (layer, tl[TILE_SLOT_B, j], 0, 0)
    expert_w = pl.BlockSpec
    grid_spec = pltpu.PrefetchScalarGridSpec(
        num_scalar_prefetch=2,
        grid=(N_TILES,),
        in_specs=[
            pl.BlockSpec((T_ALL // SUBLANES, SUBLANES, HA_W), lambda j, tl, ps: (0, 0, 0),
                         pipeline_mode=pl.Buffered(1)),
            expert_w(up, slot_a), expert_w(up, slot_a), expert_w(down, slot_a),
            expert_w(up, slot_b), expert_w(up, slot_b), expert_w(down, slot_b),
        ],
        out_specs=pl.BlockSpec(memory_space=pl.ANY),
        scratch_shapes=[
            pltpu.VMEM((TM_MOE // SUBLANES, SUBLANES, HA_W), F32),
            pltpu.VMEM((2, TM_MOE, D_CHUNKS, LANES), F32),
            pltpu.SMEM((R_ROWS,), jnp.int32),
            pltpu.SemaphoreType.DMA((2,)),
        ],
    )
    return pl.pallas_call(
        _moe_kernel,
        out_shape=jax.ShapeDtypeStruct((T_ALL + N_PARK, D_CHUNKS, LANES), F32),
        grid_spec=grid_spec,
        compiler_params=pltpu.CompilerParams(dimension_semantics=("arbitrary",), vmem_limit_bytes=V7X_VMEM_LIMIT,
                                             has_side_effects=True),
        name="moe_pairs",
    )(tiles, pos,
      ha.reshape(T_ALL // SUBLANES, SUBLANES, HA_W), wge, wue, wde, wge, wue, wde)


def _post_kernel(x1_ref, moe_ref, mod_ref, l2g_ref, l2b_ref, oc_ref, ol_ref):
    m = mod_ref[0, 0]
    moe = moe_ref[...].reshape(TM, D_MODEL)
    y = _ln(ALPHA * x1_ref[...] + m[5:6] * moe) * l2g_ref[0] + l2b_ref[0]
    i = pl.program_id(0)

    @pl.when(i < N_CTX_TILES)
    def _():
        oc_ref[...] = y

    @pl.when(i >= N_CTX_TILES)
    def _():
        ol_ref[...] = y


def _post_call(layer, x1, moe, mod, l2g, l2b):
    tok = lambda i: (i, 0)
    per_layer = lambda i: (layer, 0, 0)
    return pl.pallas_call(
        _post_kernel,
        out_shape=(jax.ShapeDtypeStruct((T_CTX, D_MODEL), F32), jax.ShapeDtypeStruct((T_LAT, D_MODEL), F32)),
        grid=(T_ALL // TM,),
        in_specs=[
            pl.BlockSpec((TM, D_MODEL), tok),
            pl.BlockSpec((TM, D_CHUNKS, LANES), lambda i: (i, 0, 0)),
            pl.BlockSpec((1, 1, N_MOD, D_MODEL), lambda i: (layer, _cond_row(i, TM), 0, 0)),
            pl.BlockSpec((1, 1, D_MODEL), per_layer),
            pl.BlockSpec((1, 1, D_MODEL), per_layer),
        ],
        out_specs=tuple(_group_specs()),
        compiler_params=_cparams("arbitrary"),
        name="post_moe",
    )(x1, moe, mod, l2g, l2b)


def _rope_tables():
    pos = np.arange(DEC_SEQ)
    quarter = HEAD_DIM // 4
    inv = ROPE_THETA ** (-np.arange(quarter, dtype=np.float64) / quarter)
    ang_r = (pos // GRID_W)[:, None] * inv[None, :]
    ang_c = (pos % GRID_W)[:, None] * inv[None, :]
    cos = np.concatenate([np.cos(ang_r)] * 2 + [np.cos(ang_c)] * 2, axis=-1)
    sin = np.concatenate([-np.sin(ang_r), np.sin(ang_r), -np.sin(ang_c), np.sin(ang_c)], axis=-1)
    return (jnp.asarray(np.tile(cos, (1, N_Q_HEADS)).astype(np.float32)),
            jnp.asarray(np.tile(sin, (1, N_Q_HEADS)).astype(np.float32)))


def _dft_mats(n, scale):
    j = np.arange(n)
    ang = ((j[:, None] * j[None, :]) % n) * (2 * np.pi / n)
    return np.cos(ang) * scale, np.sin(ang) * scale


def _block_diag(m, reps):
    return np.kron(np.eye(reps), m)


def _const_bf16(a):
    return jnp.asarray(np.asarray(a, np.float32)).astype(BF16)


def _const_01(a):
    return jnp.asarray(np.asarray(a, np.float32).astype(BF16))


def kernel(x_prompt, x_sample, cache_k, cache_v, c, c_ctx, w_in, q_norm_g, k_norm_g, w_proj_att,
           w_proj_fourier, w_proj_chunk, w_out, chunk_ln_g, chunk_ln_b, chunk_ws, chunk_bs, w_ada, b_ada,
           ln1_g, ln1_b, ln2_g, ln2_b, w_router, router_bias, w_gate_e, w_up_e, w_down_e):
    x = (x_prompt.reshape(T_CTX, D_MODEL), x_sample.reshape(T_LAT, D_MODEL))
    cond = jnp.concatenate([c_ctx[None, :], c, jnp.zeros((N_COND - 1 - DEC_BATCH, D_MODEL), F32)], axis=0)
    mod = _mod_call(cond, w_ada, b_ada).reshape(DEPTH, N_COND, N_MOD, D_MODEL)

    cos_t, sin_t = _rope_tables()
    bd_heads = _const_01(_block_diag(np.ones((HEAD_DIM, HEAD_DIM)), N_Q_HEADS))
    c64, s64 = _dft_mats(FOURIER_GROUP_W, 1.0)
    bdc = _const_bf16(_block_diag(c64, FOURIER_W // FOURIER_GROUP_W))
    bds = _const_bf16(_block_diag(s64, FOURIER_W // FOURIER_GROUP_W))
    dft = {}
    for seq in (SEQ, DEC_SEQ):
        cs, ss = _dft_mats(seq, 1.0 / math.sqrt(seq * FOURIER_GROUP_W))
        dft[seq] = _const_bf16(np.concatenate([cs, -ss], axis=1))
    ctx_k = cache_k.reshape(DEC_BATCH, DEPTH, PAST_LEN, KV_W)
    ctx_v = cache_v.reshape(DEC_BATCH, DEPTH, PAST_LEN, KV_W)
    wr = jnp.pad(w_router, ((0, 0), (0, ROUTER_PAD - N_EXPERTS)))
    wr_hi = wr.astype(BF16)
    wr_lo = (wr - wr_hi.astype(F32)).astype(BF16)
    bias_t = jnp.broadcast_to(router_bias[:, None], (N_EXPERTS, TM))
    tri = _const_01(np.triu(np.ones((SORT_BLK, SORT_BLK)), 1))
    low = _const_01(np.tril(np.ones((CLASS_PAD, CLASS_PAD)), -1))

    w_in_b = w_in.astype(BF16)
    wpa, wpf, wpc, wo = (w.astype(BF16) for w in (w_proj_att, w_proj_fourier, w_proj_chunk, w_out))
    wge, wue, wde = (w.astype(BF16) for w in (w_gate_e, w_up_e, w_down_e))
    qg = jnp.tile(q_norm_g, (1, N_Q_HEADS))[:, None, :]
    kg = jnp.tile(k_norm_g, (1, N_KV_HEADS))[:, None, :]
    ws = chunk_ws.astype(BF16)
    bsm = jnp.repeat(jnp.swapaxes(chunk_bs, 1, 2), CHUNK_GROUP_W, axis=2)
    row3 = lambda a: a[:, None, :]

    new_k = jnp.zeros((BATCH, DEPTH, SEQ, KV_W), F32)
    new_v = jnp.zeros((BATCH, DEPTH, SEQ, KV_W), F32)
    for l in range(DEPTH):
        mix, new_k, new_v = _proj_call(l, x, mod, w_in_b, qg, kg, cos_t, sin_t, bd_heads, new_k, new_v)
        mixer_args = (bdc, bds, ws, bsm, row3(chunk_ln_g), row3(chunk_ln_b))
        mixed_ctx = _mixer_call(l, mix, None, None, dft[SEQ], *mixer_args, latent=False)
        mixed_lat = _mixer_call(l, mix, ctx_k, ctx_v, dft[DEC_SEQ], *mixer_args, latent=True)
        x1, ha, rt = _merge_call(l, x, mixed_ctx, mixed_lat, mod, w_in_b, wpa, wpf, wpc, wo,
                                 row3(ln1_g), row3(ln1_b), wr_hi, wr_lo, bias_t)
        pos2d, tiles = _sort_call(rt, tri, low)
        moe = _moe_call(l, tiles, pos2d[0], ha, wge, wue, wde)
        x = _post_call(l, x1, moe, mod, row3(ln2_g), row3(ln2_b))

    y_prompt = x[0].reshape(BATCH, SEQ, D_MODEL)
    y_sample = x[1].reshape(DEC_BATCH, DEC_SEQ, D_MODEL)
    cache_shape = (BATCH, DEPTH, SEQ, N_KV_HEADS, HEAD_DIM)
    return (y_prompt, y_sample, new_k.reshape(cache_shape), new_v.reshape(cache_shape))
```

```python
import functools
import math

import jax
import jax.numpy as jnp
import numpy as np
from jax import lax
from jax.experimental import pallas as pl
from jax.experimental.pallas import tpu as pltpu

F32 = jnp.float32
BF16 = jnp.bfloat16

D_MODEL = 1024
BATCH = 16
SEQ = 256
DEPTH = 2
DEC_BATCH = 2
DEC_SEQ = 1024
PAST_LEN = 256
GRID_W = 64
N_Q_HEADS = 8
N_KV_HEADS = 2
HEAD_DIM = 64
Q_PER_KV = N_Q_HEADS // N_KV_HEADS
ATT_W = N_Q_HEADS * HEAD_DIM
KV_W = N_KV_HEADS * HEAD_DIM
ROPE_THETA = 10000.0
FOURIER_GROUP_W = 64
FOURIER_W = 256
N_CHUNK_GROUPS = 4
CHUNK_GROUP_W = 64
CHUNK_W = 256
CHUNK = 128
N_EXPERTS = 16
N_EXPERT_GROUPS = 4
EXPERTS_PER_GROUP = 4
D_EXPERT = 512
ALPHA = (2 * DEPTH) ** 0.25
LN_EPS = 1e-6
RMS_EPS = 1e-6

LANES = 128
SUBLANES = 8
T_CTX = BATCH * SEQ
T_LAT = DEC_BATCH * DEC_SEQ
T_ALL = T_CTX + T_LAT
IN_W = ATT_W + 2 * KV_W + FOURIER_W + 2 * CHUNK_W + 3 * D_MODEL
MIX_W = ATT_W + 2 * KV_W + FOURIER_W + 2 * CHUNK_W
N_MOD = 6
N_COND = 8
ROUTER_PAD = LANES
TM = 512
TQ = 256
MERGE_SUB = 256
PROJ_SUB = 256
V7X_VMEM_LIMIT = 56 * 1024 * 1024

PAIR_A = (0, 0, 0, 1, 1, 3)
PAIR_B = (1, 2, 3, 3, 2, 2)
N_PAIRS = len(PAIR_A)
N_CLASSES = N_EXPERT_GROUPS * N_PAIRS
CLASS_PAD = 32
ROUTE_ROWS = 8
TM_MOE = 256
N_TILES = T_ALL // TM_MOE + N_CLASSES
R_ROWS = N_TILES * TM_MOE
SORT_BLK = 512
D_CHUNKS = D_MODEL // LANES
HA_W = D_MODEL + LANES
TILE_SLOT_A, TILE_SLOT_B, TILE_ROWS = 0, 1, 2
SMALL_TILE = 64
N_PARK = 2 * TM_MOE
COPY_GROUP = 8


def _cparams(*sem):
    return pltpu.CompilerParams(dimension_semantics=sem, vmem_limit_bytes=V7X_VMEM_LIMIT)


def _cond_row(i, tm):
    n_ctx = T_CTX // tm
    return jnp.where(i < n_ctx, 0, 1 + (i - n_ctx) // (DEC_SEQ // tm))


N_CTX_TILES = T_CTX // TM


def _group_specs():
    return [pl.BlockSpec((TM, D_MODEL), lambda i: (jnp.minimum(i, N_CTX_TILES - 1), 0)),
            pl.BlockSpec((TM, D_MODEL), lambda i: (jnp.maximum(i - N_CTX_TILES, 0), 0))]


def _group_tile(ctx_ref, lat_ref):
    return jnp.where(pl.program_id(0) < N_CTX_TILES, ctx_ref[...], lat_ref[...])


def _ln(x):
    mu = jnp.mean(x, axis=-1, keepdims=True)
    xc = x - mu
    var = jnp.mean(xc * xc, axis=-1, keepdims=True)
    return xc * lax.rsqrt(var + LN_EPS)


def _split_bf16(x):
    hi = x.astype(BF16)
    lo = (x - hi.astype(F32)).astype(BF16)
    return hi, lo


def _dot(a, b):
    return jnp.dot(a, b, preferred_element_type=F32)


def _sigmoid(x):
    return 1.0 / (1.0 + jnp.exp(-x))


def _gelu_tanh(x):
    c = np.float32(np.sqrt(2 / np.pi))
    return x * (0.5 * (1.0 + jnp.tanh(c * (x + 0.044715 * (x * x * x)))))


def _mod_kernel(c_ref, w_ref, b_ref, o_ref):
    c = c_ref[...]
    s = c * _sigmoid(c)
    o_ref[0] = _dot(s.astype(BF16), w_ref[0].astype(BF16)) + b_ref[0]


def _mod_call(cond, w_ada, b_ada):
    tn = 1536
    n = N_MOD * D_MODEL
    return pl.pallas_call(
        _mod_kernel,
        out_shape=jax.ShapeDtypeStruct((DEPTH, N_COND, n), F32),
        grid=(DEPTH, n // tn),
        in_specs=[
            pl.BlockSpec((N_COND, D_MODEL), lambda l, j: (0, 0)),
            pl.BlockSpec((1, D_MODEL, tn), lambda l, j: (l, 0, j)),
            pl.BlockSpec((1, 1, tn), lambda l, j: (l, 0, j)),
        ],
        out_specs=pl.BlockSpec((1, N_COND, tn), lambda l, j: (l, 0, j)),
        compiler_params=_cparams("arbitrary", "arbitrary"),
        name="adaln_mod",
    )(cond, w_ada, b_ada.reshape(DEPTH, 1, n))


def _head_rms(q, bd, g):
    hi, lo = _split_bf16(q * q)
    ssum = _dot(hi, bd) + _dot(lo, bd)
    return q * lax.rsqrt(ssum * (1.0 / HEAD_DIM) + RMS_EPS) * g


def _rope(x, cos, sin_signed):
    w = x.shape[-1]
    lane = lax.broadcasted_iota(jnp.int32, x.shape, 1)
    swapped = jnp.where((lane % 32) < 16, pltpu.roll(x, w - 16, 1), pltpu.roll(x, 16, 1))
    return x * cos + swapped * sin_signed


def _proj_kernel(xc_ref, xl_ref, mod_ref, w_ref, qg_ref, kg_ref, cos_ref, sin_ref, bd_ref, kc_in, vc_in,
                 o_ref, kc_ref, vc_ref):
    del kc_in, vc_in
    m = mod_ref[0, 0]
    in_ctx = pl.program_id(0) < N_CTX_TILES
    bd = bd_ref[...]
    sub_rows = [slice(s * PROJ_SUB, (s + 1) * PROJ_SUB) for s in range(TM // PROJ_SUB)]
    projs = []
    for rows in sub_rows:
        x = jnp.where(in_ctx, xc_ref[rows, :], xl_ref[rows, :])
        h = _ln(x) * (1.0 + m[1:2]) + m[0:1]
        proj = _dot(h.astype(BF16), w_ref[...])
        o_ref[rows, ATT_W + KV_W:] = proj[:, ATT_W + KV_W:]
        projs.append(proj[:, :ATT_W + 2 * KV_W])
    keys = []
    for rows, proj in zip(sub_rows, projs):
        qn = _head_rms(proj[:, :ATT_W], bd, qg_ref[0])
        kn = _head_rms(proj[:, ATT_W:ATT_W + KV_W], bd[:KV_W, :KV_W], kg_ref[0])
        cos = cos_ref[rows, :]
        sin = sin_ref[rows, :]
        o_ref[rows, :ATT_W] = jnp.where(in_ctx, qn, _rope(qn, cos, sin))
        o_ref[rows, ATT_W:ATT_W + KV_W] = jnp.where(in_ctx, kn, _rope(kn, cos[:, :KV_W], sin[:, :KV_W]))
        keys.append(kn)

    @pl.when(in_ctx)
    def _():
        for s in range(len(sub_rows)):
            kc_ref[s] = keys[s]
            vc_ref[s] = projs[s][:, ATT_W + KV_W:]


def _proj_call(layer, x, mod, w_in_b, qg, kg, cos_t, sin_t, bd, k_cache, v_cache):
    assert PROJ_SUB == SEQ
    n_ctx = T_CTX // TM
    per_seq = DEC_SEQ // TM

    def rope_idx(i):
        return (jnp.where(i < n_ctx, 0, (i - n_ctx) % per_seq), 0)

    cache_spec = pl.BlockSpec((TM // SEQ, None, SEQ, KV_W), lambda i: (jnp.minimum(i, n_ctx - 1), layer, 0, 0))
    cache_shape = jax.ShapeDtypeStruct((BATCH, DEPTH, SEQ, KV_W), F32)
    n_in = 11
    return pl.pallas_call(
        _proj_kernel,
        out_shape=(jax.ShapeDtypeStruct((T_ALL, MIX_W), F32), cache_shape, cache_shape),
        grid=(T_ALL // TM,),
        in_specs=_group_specs() + [
            pl.BlockSpec((1, 1, N_MOD, D_MODEL), lambda i: (layer, _cond_row(i, TM), 0, 0)),
            pl.BlockSpec((None, D_MODEL, MIX_W), lambda i: (layer, 0, 0)),
            pl.BlockSpec((1, 1, ATT_W), lambda i: (layer, 0, 0)),
            pl.BlockSpec((1, 1, KV_W), lambda i: (layer, 0, 0)),
            pl.BlockSpec((TM, ATT_W), rope_idx),
            pl.BlockSpec((TM, ATT_W), rope_idx),
            pl.BlockSpec((ATT_W, ATT_W), lambda i: (0, 0)),
            pl.BlockSpec(memory_space=pl.ANY),
            pl.BlockSpec(memory_space=pl.ANY),
        ],
        out_specs=(pl.BlockSpec((TM, MIX_W), lambda i: (i, 0)), cache_spec, cache_spec),
        input_output_aliases={n_in - 2: 1, n_in - 1: 2},
        compiler_params=_cparams("arbitrary"),
        name="in_proj",
    )(*x, mod, w_in_b, qg, kg, cos_t, sin_t, bd, k_cache, v_cache)


def _attention_tile(q, k_parts, v_parts, phase_major):
    def kv_of(h, parts):
        g = h // Q_PER_KV
        return [p[:, g * HEAD_DIM:(g + 1) * HEAD_DIM] for p in parts]

    def scores(h):
        qh = q[:, h * HEAD_DIM:(h + 1) * HEAD_DIM]
        return [lax.dot_general(qh, k, (((1,), (1,)), ((), ())), preferred_element_type=F32)
                for k in kv_of(h, k_parts)]

    if phase_major:
        all_ss = [scores(h) for h in range(N_Q_HEADS)]
        all_m = [functools.reduce(jnp.maximum, [jnp.max(s, axis=-1, keepdims=True) for s in ss]) for ss in all_ss]
        all_es = [[jnp.exp(s - m) for s in ss] for ss, m in zip(all_ss, all_m)]
        all_den = [functools.reduce(jnp.add, [jnp.sum(e, axis=-1, keepdims=True) for e in es]) for es in all_es]
        outs = [functools.reduce(jnp.add, [_dot(e.astype(BF16), v) for e, v in zip(es, kv_of(h, v_parts))])
                * (1.0 / den) for h, (es, den) in enumerate(zip(all_es, all_den))]
        return jnp.concatenate(outs, axis=-1)
    outs = []
    ss_next = scores(0)
    for h in range(N_Q_HEADS):
        ss = ss_next
        if h + 1 < N_Q_HEADS:
            ss_next = scores(h + 1)
        m = functools.reduce(jnp.maximum, [jnp.max(s, axis=-1, keepdims=True) for s in ss])
        es = [jnp.exp(s - m) for s in ss]
        denom = functools.reduce(jnp.add, [jnp.sum(e, axis=-1, keepdims=True) for e in es])
        o = functools.reduce(jnp.add, [_dot(e.astype(BF16), v) for e, v in zip(es, kv_of(h, v_parts))])
        outs.append(o * (1.0 / denom))
    return jnp.concatenate(outs, axis=-1)


def _mixer_kernel(*refs, seq, latent):
    if latent:
        (mix_ref, ck_ref, cv_ref, dft_ref, bdc_ref, bds_ref, ws_ref, bsm_ref, clg_ref, clb_ref, o_ref) = refs
    else:
        (mix_ref, dft_ref, bdc_ref, bds_ref, ws_ref, bsm_ref, clg_ref, clb_ref, o_ref) = refs

    k_new = mix_ref[:, ATT_W:ATT_W + KV_W].astype(BF16)
    v_new = mix_ref[:, ATT_W + KV_W:ATT_W + 2 * KV_W].astype(BF16)
    if latent:
        k_parts = [ck_ref[...].astype(BF16), k_new]
        v_parts = [cv_ref[...].astype(BF16), v_new]
    else:
        k_parts, v_parts = [k_new], [v_new]

    def q_tile(t, carry):
        r0 = pl.multiple_of(t * TQ, TQ)
        q = (mix_ref[pl.ds(r0, TQ), :ATT_W] * (HEAD_DIM ** -0.5)).astype(BF16)
        o_ref[pl.ds(r0, TQ), :ATT_W] = _attention_tile(q, k_parts, v_parts, phase_major=not latent).astype(BF16)
        return carry

    lax.fori_loop(0, seq // TQ, q_tile, 0)

    f_hi, f_lo = _split_bf16(mix_ref[:, ATT_W + 2 * KV_W:ATT_W + 2 * KV_W + FOURIER_W])
    bdc = bdc_ref[...]
    bds = bds_ref[...]
    y = jnp.concatenate([_dot(f_hi, bdc) + _dot(f_lo, bdc), _dot(f_hi, bds) + _dot(f_lo, bds)], axis=0)
    y_hi, y_lo = _split_bf16(y)
    dft = dft_ref[...]
    four = _dot(dft, y_hi) + _dot(dft, y_lo)
    o_ref[:, ATT_W:ATT_W + FOURIER_W] = four.astype(BF16)

    c0 = ATT_W + 2 * KV_W + FOURIER_W
    u = _gelu_tanh(mix_ref[:, c0:c0 + CHUNK_W])
    vn = (_ln(_gelu_tanh(mix_ref[:, c0 + CHUNK_W:c0 + 2 * CHUNK_W])) * clg_ref[0] + clb_ref[0]).astype(BF16)
    lane = lax.broadcasted_iota(jnp.int32, (CHUNK, CHUNK_W), 1)
    bsm = bsm_ref[...]
    for c in range(seq // CHUNK):
        vc = vn[c * CHUNK:(c + 1) * CHUNK]
        sv = bsm
        for g in range(N_CHUNK_GROUPS):
            vg = jnp.where(lane // CHUNK_GROUP_W == g, vc, jnp.zeros_like(vc))
            sv = sv + _dot(ws_ref[g], vg)
        o_ref[c * CHUNK:(c + 1) * CHUNK, ATT_W + FOURIER_W:] = (u[c * CHUNK:(c + 1) * CHUNK] * sv).astype(BF16)


def _mixer_call(layer, mix, ctx_k, ctx_v, dft, bdc, bds, ws, bsm, clg, clb, *, latent):
    seq = DEC_SEQ if latent else SEQ
    nb = DEC_BATCH if latent else BATCH
    row0 = (T_CTX // seq) if latent else 0
    const2 = lambda b: (0, 0)
    per_layer = lambda b: (layer, 0, 0)
    in_specs = [pl.BlockSpec((seq, MIX_W), lambda b: (row0 + b, 0))]
    args = [mix]
    if latent:
        in_specs += [pl.BlockSpec((None, None, PAST_LEN, KV_W), lambda b: (b, layer, 0, 0))] * 2
        args += [ctx_k, ctx_v]
    in_specs += [
        pl.BlockSpec((seq, 2 * seq), const2),
        pl.BlockSpec((FOURIER_W, FOURIER_W), const2),
        pl.BlockSpec((FOURIER_W, FOURIER_W), const2),
        pl.BlockSpec((None, N_CHUNK_GROUPS, CHUNK, CHUNK), lambda b: (layer, 0, 0, 0)),
        pl.BlockSpec((None, CHUNK, CHUNK_W), per_layer),
        pl.BlockSpec((1, 1, CHUNK_W), per_layer),
        pl.BlockSpec((1, 1, CHUNK_W), per_layer),
    ]
    args += [dft, bdc, bds, ws, bsm, clg, clb]
    return pl.pallas_call(
        functools.partial(_mixer_kernel, seq=seq, latent=latent),
        out_shape=jax.ShapeDtypeStruct((nb * seq, D_MODEL), BF16),
        grid=(nb,),
        in_specs=in_specs,
        out_specs=pl.BlockSpec((seq, D_MODEL), lambda b: (b, 0)),
        compiler_params=_cparams("arbitrary"),
        name="mixer_latent" if latent else "mixer_context",
    )(*args)


def _route(logits, bias_t):
    lt = logits.T[:N_EXPERTS]
    ex = jnp.exp(lt - jnp.max(lt, axis=0, keepdims=True))
    probs = ex / jnp.sum(ex, axis=0, keepdims=True)
    sel = probs + bias_t
    p = [probs[e:e + 1] for e in range(N_EXPERTS)]
    s = [sel[e:e + 1] for e in range(N_EXPERTS)]
    n = EXPERTS_PER_GROUP
    scores = []
    for g in range(N_EXPERT_GROUPS):
        pair = [s[g * n + a] + s[g * n + b] for a in range(n) for b in range(a + 1, n)]
        scores.append(functools.reduce(jnp.maximum, pair))
    best = jnp.zeros_like(scores[0], dtype=jnp.int32)
    best_score = scores[0]
    for g in range(1, N_EXPERT_GROUPS):
        better = scores[g] > best_score
        best = jnp.where(better, g, best)
        best_score = jnp.where(better, scores[g], best_score)
    cls = jnp.zeros_like(best_score)
    w_a = jnp.zeros_like(best_score)
    w_b = jnp.zeros_like(best_score)
    for g in range(N_EXPERT_GROUPS):
        in_g = best == g
        chosen = []
        for a in range(n):
            rank = jnp.zeros_like(best)
            for b in range(n):
                if b == a:
                    continue
                ahead = (s[g * n + b] > s[g * n + a]) if b > a else (s[g * n + b] >= s[g * n + a])
                rank = rank + ahead.astype(jnp.int32)
            chosen.append(jnp.logical_and(in_g, rank < 2))
        for k in range(N_PAIRS):
            pa, pb = p[g * n + PAIR_A[k]], p[g * n + PAIR_B[k]]
            hit = jnp.logical_and(chosen[PAIR_A[k]], chosen[PAIR_B[k]])
            wsum = pa + pb
            cls = jnp.where(hit, float(g * N_PAIRS + k), cls)
            w_a = jnp.where(hit, pa / wsum, w_a)
            w_b = jnp.where(hit, pb / wsum, w_b)
    return jnp.concatenate([cls, w_a, w_b, jnp.zeros((ROUTE_ROWS - 3, lt.shape[1]), F32)], axis=0)


def _merge_kernel(xc_ref, xl_ref, mixc_ref, mixl_ref, mod_ref, win_ref, wpa_ref, wpf_ref, wpc_ref, wo_ref,
                  l1g_ref, l1b_ref, wrh_ref, wrl_ref, bias_ref, x1_ref, ha_ref, rt_ref):
    m = mod_ref[0, 0]
    in_ctx = pl.program_id(0) < N_CTX_TILES
    branch_w = ((0, ATT_W, wpa_ref), (ATT_W, FOURIER_W, wpf_ref), (ATT_W + FOURIER_W, CHUNK_W, wpc_ref))
    sub_rows = [slice(s * MERGE_SUB, (s + 1) * MERGE_SUB) for s in range(TM // MERGE_SUB)]
    residual = []
    for rows in sub_rows:
        x = jnp.where(in_ctx, xc_ref[rows, :], xl_ref[rows, :])
        mixed = jnp.where(in_ctx, mixc_ref[rows, :], mixl_ref[rows, :])
        h = (_ln(x) * (1.0 + m[1:2]) + m[0:1]).astype(BF16)
        merged = None
        for b, (c0, width, w_ref) in enumerate(branch_w):
            gate = _sigmoid(_dot(h, win_ref[:, MIX_W + b * D_MODEL:MIX_W + (b + 1) * D_MODEL]))
            term = gate * _dot(mixed[:, c0:c0 + width], w_ref[...])
            merged = term if merged is None else merged + term
        mix = _dot(merged.astype(BF16), wo_ref[...])
        residual.append(ALPHA * x + m[2:3] * mix)
    for rows, pre in zip(sub_rows, residual):
        x1 = _ln(pre) * l1g_ref[0] + l1b_ref[0]
        x1_ref[rows, :] = x1
        h2 = _ln(x1) * (1.0 + m[4:5]) + m[3:4]
        h_hi, h_lo = _split_bf16(h2)
        wrh = wrh_ref[...]
        logits = _dot(h_hi, wrh) + _dot(h_lo, wrh) + _dot(h_hi, wrl_ref[...])
        rt = _route(logits, bias_ref[:, rows])
        rt_ref[:, rows] = rt
        ha_ref[rows, :D_MODEL] = h2
        ha_ref[rows, D_MODEL:] = jnp.concatenate([rt, jnp.zeros((LANES - ROUTE_ROWS, MERGE_SUB), F32)], axis=0).T


def _merge_call(layer, x, mixed_ctx, mixed_lat, mod, w_in_b, wpa, wpf, wpc, wo, l1g, l1b, wrh, wrl, bias_t):
    const2 = lambda i: (0, 0)
    per_layer = lambda i: (layer, 0, 0)
    tok = lambda i: (i, 0)
    return pl.pallas_call(
        _merge_kernel,
        out_shape=(jax.ShapeDtypeStruct((T_ALL, D_MODEL), F32),
                   jax.ShapeDtypeStruct((T_ALL, HA_W), F32),
                   jax.ShapeDtypeStruct((ROUTE_ROWS, T_ALL), F32)),
        grid=(T_ALL // TM,),
        in_specs=_group_specs() + _group_specs() + [
            pl.BlockSpec((1, 1, N_MOD, D_MODEL), lambda i: (layer, _cond_row(i, TM), 0, 0)),
            pl.BlockSpec((None, D_MODEL, IN_W), per_layer, pipeline_mode=pl.Buffered(1)),
            pl.BlockSpec((None, ATT_W, D_MODEL), per_layer, pipeline_mode=pl.Buffered(1)),
            pl.BlockSpec((None, FOURIER_W, D_MODEL), per_layer, pipeline_mode=pl.Buffered(1)),
            pl.BlockSpec((None, CHUNK_W, D_MODEL), per_layer, pipeline_mode=pl.Buffered(1)),
            pl.BlockSpec((None, D_MODEL, D_MODEL), per_layer, pipeline_mode=pl.Buffered(1)),
            pl.BlockSpec((1, 1, D_MODEL), per_layer),
            pl.BlockSpec((1, 1, D_MODEL), per_layer),
            pl.BlockSpec((D_MODEL, ROUTER_PAD), const2),
            pl.BlockSpec((D_MODEL, ROUTER_PAD), const2),
            pl.BlockSpec((N_EXPERTS, TM), const2),
        ],
        out_specs=(pl.BlockSpec((TM, D_MODEL), tok),
                   pl.BlockSpec((TM, HA_W), tok),
                   pl.BlockSpec((ROUTE_ROWS, TM), lambda i: (0, i))),
        compiler_params=_cparams("arbitrary"),
        name="merge_route",
    )(*x, mixed_ctx, mixed_lat, mod, w_in_b, wpa, wpf, wpc, wo, l1g, l1b, wrh, wrl, bias_t)


def _sort_kernel(rt_ref, tri_ref, low_ref, pos_ref, tile_ref):
    crow = lax.broadcasted_iota(jnp.int32, (CLASS_PAD, SORT_BLK), 0)
    tri = tri_ref[...]
    n_blk = T_ALL // SORT_BLK
    carry = jnp.zeros((CLASS_PAD, 1), F32)
    ranks = []
    for b in range(n_blk):
        hot = rt_ref[0:1, b * SORT_BLK:(b + 1) * SORT_BLK].astype(jnp.int32) == crow
        hot_f = jnp.where(hot, 1.0, 0.0)
        before = _dot(hot_f.astype(BF16), tri) + carry
        ranks.append(jnp.sum(jnp.where(hot, before, 0.0), axis=0, keepdims=True))
        carry = carry + jnp.sum(hot_f, axis=1, keepdims=True)
    padded = jnp.floor((carry + (TM_MOE - 1.0)) * (1.0 / TM_MOE)) * TM_MOE
    padded = jnp.broadcast_to(padded, (CLASS_PAD, LANES))
    offs = _dot(low_ref[...], padded.astype(BF16))
    for b in range(n_blk):
        hot = rt_ref[0:1, b * SORT_BLK:(b + 1) * SORT_BLK].astype(jnp.int32) == crow
        base = jnp.sum(jnp.where(hot, offs[:, 0:1], 0.0), axis=0, keepdims=True)
        pos_ref[:, b * SORT_BLK:(b + 1) * SORT_BLK] = (base + ranks[b]).astype(jnp.int32)

    start = lax.broadcasted_iota(jnp.int32, (1, LANES), 1).astype(F32) * TM_MOE
    is_class = lax.broadcasted_iota(jnp.int32, (CLASS_PAD, LANES), 0) < N_CLASSES
    ends = jnp.where(is_class, offs + padded, 0.0)
    total = jnp.max(ends, axis=0, keepdims=True)
    valid = start < total
    tcls = jnp.sum(jnp.where(jnp.logical_and(is_class, ends <= start), 1.0, 0.0), axis=0, keepdims=True)
    last = jnp.max(jnp.where(valid, tcls, 0.0), axis=1, keepdims=True)
    tcls = jnp.where(valid, tcls, last)
    grp = functools.reduce(jnp.add, [jnp.where(tcls >= g * N_PAIRS, 1.0, 0.0) for g in range(1, N_EXPERT_GROUPS)])
    pair = tcls - grp * N_PAIRS
    slot_a = functools.reduce(jnp.add, [jnp.where(pair == k, float(PAIR_A[k]), 0.0) for k in range(N_PAIRS)])
    slot_b = functools.reduce(jnp.add, [jnp.where(pair == k, float(PAIR_B[k]), 0.0) for k in range(N_PAIRS)])
    crow_t = lax.broadcasted_iota(jnp.int32, (CLASS_PAD, LANES), 0).astype(F32)
    real_end = jnp.sum(jnp.where(crow_t == tcls, offs + carry, 0.0), axis=0, keepdims=True)
    n_rows = jnp.where(valid, jnp.clip(real_end - start, 0.0, float(TM_MOE)), 0.0)
    rows = [grp * EXPERTS_PER_GROUP + slot_a, grp * EXPERTS_PER_GROUP + slot_b, n_rows]
    tile_ref[...] = jnp.concatenate(rows + [jnp.zeros((8 - len(rows), LANES), F32)], axis=0).astype(jnp.int32)


def _sort_call(rt, tri, low):
    return pl.pallas_call(
        _sort_kernel,
        out_shape=(jax.ShapeDtypeStruct((1, T_ALL), jnp.int32),
                   jax.ShapeDtypeStruct((8, LANES), jnp.int32)),
        name="route_sort",
    )(rt, tri, low)


def _moe_kernel(tile_ref, pos_ref, ha_ref, wg_hbm, wu_hbm, wd_hbm, out_ref, hbuf, ybuf, src_ref, ssem,
                stg_g, stg_u, stg_d, act_g, act_u, act_d, wsem, *, layer):
    j = pl.program_id(0)
    slot = j % 2

    def n_rows(tile):
        inside = jnp.logical_and(tile >= 0, tile < N_TILES)
        return jnp.where(inside, tile_ref[TILE_ROWS, jnp.clip(tile, 0, N_TILES - 1)], 0)

    def token_of(tile, r):
        return src_ref[tile * TM_MOE + jnp.minimum(r, tile_ref[TILE_ROWS, tile] - 1)]

    def scatter_copy(tile, buf_slot, r):
        dst = jnp.where(r < tile_ref[TILE_ROWS, tile], token_of(tile, r), T_ALL + buf_slot * TM_MOE + r)
        return pltpu.make_async_copy(ybuf.at[buf_slot, r], out_ref.at[dst], ssem.at[buf_slot])

    def for_groups(tile, fn):
        def body(g, c):
            for k in range(COPY_GROUP):
                fn(g, k)
            return c
        lax.fori_loop(0, (n_rows(tile) + COPY_GROUP - 1) // COPY_GROUP, body, 0)

    def retire(tile, copy_of):
        for_groups(tile, lambda g, k: copy_of(0).wait())

    def expert_of(x, tile):
        return tile_ref[TILE_SLOT_A + x, tile]

    def weight_copies(x, e):
        return [pltpu.make_async_copy(src.at[layer, e], stg.at[x], wsem.at[x])
                for src, stg in ((wg_hbm, stg_g), (wu_hbm, stg_u), (wd_hbm, stg_d))]

    @pl.when(j == 0)
    def _():
        for x in range(2):
            for c in weight_copies(x, expert_of(x, 0)):
                c.start()

        def place(t, c):
            src_ref[pos_ref[t]] = t
            return c
        lax.fori_loop(0, T_ALL, place, 0, unroll=8)
        hbuf[...] = jnp.zeros(hbuf.shape, F32)
        ybuf[...] = jnp.zeros(ybuf.shape, F32)
        for s in range(2):
            park = pltpu.make_async_copy(ybuf.at[s], out_ref.at[pl.ds(T_ALL + s * TM_MOE, TM_MOE)], ssem.at[s])
            park.start()
            park.wait()

    @pl.when(n_rows(j - 2) > 0)
    def _():
        retire(j - 2, lambda r: scatter_copy(j - 2, slot, r))

    def run_experts(rows):
        rec = hbuf[:rows // SUBLANES].reshape(rows, HA_W)
        h = rec[:, :D_MODEL].astype(BF16)

        def expert(x):
            a = _dot(h, act_g[x])
            u = _dot(h, act_u[x])
            w = rec[:, D_MODEL + 1 + x:D_MODEL + 2 + x]
            return _dot(((a * _sigmoid(a)) * u * w).astype(BF16), act_d[x])

        ybuf[slot, :rows] = (expert(0) + expert(1)).reshape(rows, D_CHUNKS, LANES)

    @pl.when(n_rows(j) > 0)
    def _():
        for x in range(2):
            prev = jnp.maximum(j - 1, 0)

            @pl.when(jnp.logical_or(j == 0, expert_of(x, j) != expert_of(x, prev)))
            def _():
                for c in weight_copies(x, expert_of(x, j)):
                    c.wait()
                act_g[x] = stg_g[x].astype(BF16)
                act_u[x] = stg_u[x].astype(BF16)
                act_d[x] = stg_d[x].astype(BF16)

        for x in range(2):
            nxt = jnp.minimum(j + 1, N_TILES - 1)

            @pl.when(jnp.logical_and(n_rows(j + 1) > 0, expert_of(x, nxt) != expert_of(x, j)))
            def _():
                for c in weight_copies(x, expert_of(x, nxt)):
                    c.start()

        def gather_row(g, k):
            tok = token_of(j, g * COPY_GROUP + k)
            grp, sub = lax.shift_right_logical(tok, 3), jnp.bitwise_and(tok, SUBLANES - 1)
            hbuf[g, pl.ds(k, 1), :] = ha_ref[grp, pl.ds(sub, 1), :]
        for_groups(j, gather_row)

        @pl.when(n_rows(j) > SMALL_TILE)
        def _():
            run_experts(TM_MOE)

        @pl.when(n_rows(j) <= SMALL_TILE)
        def _():
            run_experts(SMALL_TILE)

        for_groups(j, lambda g, k: scatter_copy(j, slot, g * COPY_GROUP + k).start(priority=k % 2))

    @pl.when(j == N_TILES - 1)
    def _():
        @pl.when(n_rows(j - 1) > 0)
        def _():
            retire(j - 1, lambda r: scatter_copy(j - 1, 1 - slot, r))

        @pl.when(n_rows(j) > 0)
        def _():
            retire(j, lambda r: scatter_copy(j, slot, r))


def _moe_call(layer, tiles, pos, ha, wge, wue, wde):
    up = (2, D_MODEL, D_EXPERT)
    down = (2, D_EXPERT, D_MODEL)
    grid_spec = pltpu.PrefetchScalarGridSpec(
        num_scalar_prefetch=2,
        grid=(N_TILES,),
        in_specs=[
            pl.BlockSpec((T_ALL // SUBLANES, SUBLANES, HA_W), lambda j, tl, ps: (0, 0, 0),
                         pipeline_mode=pl.Buffered(1)),
            pl.BlockSpec(memory_space=pl.ANY), pl.BlockSpec(memory_space=pl.ANY), pl.BlockSpec(memory_space=pl.ANY),
        ],
        out_specs=pl.BlockSpec(memory_space=pl.ANY),
        scratch_shapes=[
            pltpu.VMEM((TM_MOE // SUBLANES, SUBLANES, HA_W), F32),
            pltpu.VMEM((2, TM_MOE, D_CHUNKS, LANES), F32),
            pltpu.SMEM((R_ROWS,), jnp.int32),
            pltpu.SemaphoreType.DMA((2,)),
            pltpu.VMEM(up, F32), pltpu.VMEM(up, F32), pltpu.VMEM(down, F32),
            pltpu.VMEM(up, BF16), pltpu.VMEM(up, BF16), pltpu.VMEM(down, BF16),
            pltpu.SemaphoreType.DMA((2,)),
        ],
    )
    return pl.pallas_call(
        functools.partial(_moe_kernel, layer=layer),
        out_shape=jax.ShapeDtypeStruct((T_ALL + N_PARK, D_CHUNKS, LANES), F32),
        grid_spec=grid_spec,
        compiler_params=pltpu.CompilerParams(dimension_semantics=("arbitrary",), vmem_limit_bytes=V7X_VMEM_LIMIT,
                                             has_side_effects=True),
        name="moe_pairs",
    )(tiles, pos,
      ha.reshape(T_ALL // SUBLANES, SUBLANES, HA_W), wge, wue, wde)


def _post_kernel(x1_ref, moe_ref, mod_ref, l2g_ref, l2b_ref, oc_ref, ol_ref):
    m = mod_ref[0, 0]
    moe = moe_ref[...].reshape(TM, D_MODEL)
    y = _ln(ALPHA * x1_ref[...] + m[5:6] * moe) * l2g_ref[0] + l2b_ref[0]
    i = pl.program_id(0)

    @pl.when(i < N_CTX_TILES)
    def _():
        oc_ref[...] = y

    @pl.when(i >= N_CTX_TILES)
    def _():
        ol_ref[...] = y


def _post_call(layer, x1, moe, mod, l2g, l2b):
    tok = lambda i: (i, 0)
    per_layer = lambda i: (layer, 0, 0)
    return pl.pallas_call(
        _post_kernel,
        out_shape=(jax.ShapeDtypeStruct((T_CTX, D_MODEL), F32), jax.ShapeDtypeStruct((T_LAT, D_MODEL), F32)),
        grid=(T_ALL // TM,),
        in_specs=[
            pl.BlockSpec((TM, D_MODEL), tok),
            pl.BlockSpec((TM, D_CHUNKS, LANES), lambda i: (i, 0, 0)),
            pl.BlockSpec((1, 1, N_MOD, D_MODEL), lambda i: (layer, _cond_row(i, TM), 0, 0)),
            pl.BlockSpec((1, 1, D_MODEL), per_layer),
            pl.BlockSpec((1, 1, D_MODEL), per_layer),
        ],
        out_specs=tuple(_group_specs()),
        compiler_params=_cparams("arbitrary"),
        name="post_moe",
    )(x1, moe, mod, l2g, l2b)


def _rope_tables():
    pos = np.arange(DEC_SEQ)
    quarter = HEAD_DIM // 4
    inv = ROPE_THETA ** (-np.arange(quarter, dtype=np.float64) / quarter)
    ang_r = (pos // GRID_W)[:, None] * inv[None, :]
    ang_c = (pos % GRID_W)[:, None] * inv[None, :]
    cos = np.concatenate([np.cos(ang_r)] * 2 + [np.cos(ang_c)] * 2, axis=-1)
    sin = np.concatenate([-np.sin(ang_r), np.sin(ang_r), -np.sin(ang_c), np.sin(ang_c)], axis=-1)
    return (jnp.asarray(np.tile(cos, (1, N_Q_HEADS)).astype(np.float32)),
            jnp.asarray(np.tile(sin, (1, N_Q_HEADS)).astype(np.float32)))


def _dft_mats(n, scale):
    j = np.arange(n)
    ang = ((j[:, None] * j[None, :]) % n) * (2 * np.pi / n)
    return np.cos(ang) * scale, np.sin(ang) * scale


def _block_diag(m, reps):
    return np.kron(np.eye(reps), m)


def _const_bf16(a):
    return jnp.asarray(np.asarray(a, np.float32)).astype(BF16)


def _const_01(a):
    return jnp.asarray(np.asarray(a, np.float32).astype(BF16))


def kernel(x_prompt, x_sample, cache_k, cache_v, c, c_ctx, w_in, q_norm_g, k_norm_g, w_proj_att,
           w_proj_fourier, w_proj_chunk, w_out, chunk_ln_g, chunk_ln_b, chunk_ws, chunk_bs, w_ada, b_ada,
           ln1_g, ln1_b, ln2_g, ln2_b, w_router, router_bias, w_gate_e, w_up_e, w_down_e):
    x = (x_prompt.reshape(T_CTX, D_MODEL), x_sample.reshape(T_LAT, D_MODEL))
    cond = jnp.concatenate([c_ctx[None, :], c, jnp.zeros((N_COND - 1 - DEC_BATCH, D_MODEL), F32)], axis=0)
    mod = _mod_call(cond, w_ada, b_ada).reshape(DEPTH, N_COND, N_MOD, D_MODEL)

    cos_t, sin_t = _rope_tables()
    bd_heads = _const_01(_block_diag(np.ones((HEAD_DIM, HEAD_DIM)), N_Q_HEADS))
    c64, s64 = _dft_mats(FOURIER_GROUP_W, 1.0)
    bdc = _const_bf16(_block_diag(c64, FOURIER_W // FOURIER_GROUP_W))
    bds = _const_bf16(_block_diag(s64, FOURIER_W // FOURIER_GROUP_W))
    dft = {}
    for seq in (SEQ, DEC_SEQ):
        cs, ss = _dft_mats(seq, 1.0 / math.sqrt(seq * FOURIER_GROUP_W))
        dft[seq] = _const_bf16(np.concatenate([cs, -ss], axis=1))
    ctx_k = cache_k.reshape(DEC_BATCH, DEPTH, PAST_LEN, KV_W)
    ctx_v = cache_v.reshape(DEC_BATCH, DEPTH, PAST_LEN, KV_W)
    wr = jnp.pad(w_router, ((0, 0), (0, ROUTER_PAD - N_EXPERTS)))
    wr_hi = wr.astype(BF16)
    wr_lo = (wr - wr_hi.astype(F32)).astype(BF16)
    bias_t = jnp.broadcast_to(router_bias[:, None], (N_EXPERTS, TM))
    tri = _const_01(np.triu(np.ones((SORT_BLK, SORT_BLK)), 1))
    low = _const_01(np.tril(np.ones((CLASS_PAD, CLASS_PAD)), -1))

    w_in_b = w_in.astype(BF16)
    wpa, wpf, wpc, wo = (w.astype(BF16) for w in (w_proj_att, w_proj_fourier, w_proj_chunk, w_out))
    wge, wue, wde = w_gate_e, w_up_e, w_down_e
    qg = jnp.tile(q_norm_g, (1, N_Q_HEADS))[:, None, :]
    kg = jnp.tile(k_norm_g, (1, N_KV_HEADS))[:, None, :]
    ws = chunk_ws.astype(BF16)
    bsm = jnp.repeat(jnp.swapaxes(chunk_bs, 1, 2), CHUNK_GROUP_W, axis=2)
    row3 = lambda a: a[:, None, :]

    new_k = jnp.zeros((BATCH, DEPTH, SEQ, KV_W), F32)
    new_v = jnp.zeros((BATCH, DEPTH, SEQ, KV_W), F32)
    for l in range(DEPTH):
        mix, new_k, new_v = _proj_call(l, x, mod, w_in_b, qg, kg, cos_t, sin_t, bd_heads, new_k, new_v)
        mixer_args = (bdc, bds, ws, bsm, row3(chunk_ln_g), row3(chunk_ln_b))
        mixed_ctx = _mixer_call(l, mix, None, None, dft[SEQ], *mixer_args, latent=False)
        mixed_lat = _mixer_call(l, mix, ctx_k, ctx_v, dft[DEC_SEQ], *mixer_args, latent=True)
        x1, ha, rt = _merge_call(l, x, mixed_ctx, mixed_lat, mod, w_in_b, wpa, wpf, wpc, wo,
                                 row3(ln1_g), row3(ln1_b), wr_hi, wr_lo, bias_t)
        pos2d, tiles = _sort_call(rt, tri, low)
        moe = _moe_call(l, tiles, pos2d[0], ha, wge, wue, wde)
        x = _post_call(l, x1, moe, mod, row3(ln2_g), row3(ln2_b))

    y_prompt = x[0].reshape(BATCH, SEQ, D_MODEL)
    y_sample = x[1].reshape(DEC_BATCH, DEC_SEQ, D_MODEL)
    cache_shape = (BATCH, DEPTH, SEQ, N_KV_HEADS, HEAD_DIM)
    return (y_prompt, y_sample, new_k.reshape(cache_shape), new_v.reshape(cache_shape))
```

```python
import functools
import math

import jax
import jax.numpy as jnp
import numpy as np
from jax import lax
from jax.experimental import pallas as pl
from jax.experimental.pallas import tpu as pltpu

F32 = jnp.float32
BF16 = jnp.bfloat16

D_MODEL = 1024
BATCH = 16
SEQ = 256
DEPTH = 2
DEC_BATCH = 2
DEC_SEQ = 1024
PAST_LEN = 256
GRID_W = 64
N_Q_HEADS = 8
N_KV_HEADS = 2
HEAD_DIM = 64
Q_PER_KV = N_Q_HEADS // N_KV_HEADS
ATT_W = N_Q_HEADS * HEAD_DIM
KV_W = N_KV_HEADS * HEAD_DIM
ROPE_THETA = 10000.0
FOURIER_GROUP_W = 64
FOURIER_W = 256
N_CHUNK_GROUPS = 4
CHUNK_GROUP_W = 64
CHUNK_W = 256
CHUNK = 128
N_EXPERTS = 16
N_EXPERT_GROUPS = 4
EXPERTS_PER_GROUP = 4
D_EXPERT = 512
ALPHA = (2 * DEPTH) ** 0.25
LN_EPS = 1e-6
RMS_EPS = 1e-6

LANES = 128
SUBLANES = 8
T_CTX = BATCH * SEQ
T_LAT = DEC_BATCH * DEC_SEQ
T_ALL = T_CTX + T_LAT
IN_W = ATT_W + 2 * KV_W + FOURIER_W + 2 * CHUNK_W + 3 * D_MODEL
MIX_W = ATT_W + 2 * KV_W + FOURIER_W + 2 * CHUNK_W
N_MOD = 6
N_COND = 8
ROUTER_PAD = LANES
TM = 512
TQ = 256
MERGE_SUB = 256
PROJ_SUB = 256
V7X_VMEM_LIMIT = 56 * 1024 * 1024

PAIR_A = (0, 0, 0, 1, 1, 3)
PAIR_B = (1, 2, 3, 3, 2, 2)
N_PAIRS = len(PAIR_A)
N_CLASSES = N_EXPERT_GROUPS * N_PAIRS
CLASS_PAD = 32
ROUTE_ROWS = 8
TM_MOE = 256
N_TILES = T_ALL // TM_MOE + N_CLASSES
R_ROWS = N_TILES * TM_MOE
SORT_BLK = 512
D_CHUNKS = D_MODEL // LANES
HA_W = D_MODEL + LANES
TILE_SLOT_A, TILE_SLOT_B, TILE_ROWS = 0, 1, 2
SMALL_TILE = 64
N_PARK = 2 * TM_MOE
COPY_GROUP = 8


def _cparams(*sem):
    return pltpu.CompilerParams(dimension_semantics=sem, vmem_limit_bytes=V7X_VMEM_LIMIT)


def _cond_row(i, tm):
    n_ctx = T_CTX // tm
    return jnp.where(i < n_ctx, 0, 1 + (i - n_ctx) // (DEC_SEQ // tm))


N_CTX_TILES = T_CTX // TM


def _group_specs():
    return [pl.BlockSpec((TM, D_MODEL), lambda i: (jnp.minimum(i, N_CTX_TILES - 1), 0)),
            pl.BlockSpec((TM, D_MODEL), lambda i: (jnp.maximum(i - N_CTX_TILES, 0), 0))]


def _group_tile(ctx_ref, lat_ref):
    return jnp.where(pl.program_id(0) < N_CTX_TILES, ctx_ref[...], lat_ref[...])


def _ln(x):
    mu = jnp.mean(x, axis=-1, keepdims=True)
    xc = x - mu
    var = jnp.mean(xc * xc, axis=-1, keepdims=True)
    return xc * lax.rsqrt(var + LN_EPS)


def _split_bf16(x):
    hi = x.astype(BF16)
    lo = (x - hi.astype(F32)).astype(BF16)
    return hi, lo


def _dot(a, b):
    return jnp.dot(a, b, preferred_element_type=F32)


def _sigmoid(x):
    return 1.0 / (1.0 + jnp.exp(-x))


def _gelu_tanh(x):
    c = np.float32(np.sqrt(2 / np.pi))
    return x * (0.5 * (1.0 + jnp.tanh(c * (x + 0.044715 * (x * x * x)))))


def _mod_kernel(c_ref, w_ref, b_ref, o_ref):
    c = c_ref[...]
    s = c * _sigmoid(c)
    o_ref[0] = _dot(s.astype(BF16), w_ref[0].astype(BF16)) + b_ref[0]


def _mod_call(cond, w_ada, b_ada):
    tn = 1536
    n = N_MOD * D_MODEL
    return pl.pallas_call(
        _mod_kernel,
        out_shape=jax.ShapeDtypeStruct((DEPTH, N_COND, n), F32),
        grid=(DEPTH, n // tn),
        in_specs=[
            pl.BlockSpec((N_COND, D_MODEL), lambda l, j: (0, 0)),
            pl.BlockSpec((1, D_MODEL, tn), lambda l, j: (l, 0, j)),
            pl.BlockSpec((1, 1, tn), lambda l, j: (l, 0, j)),
        ],
        out_specs=pl.BlockSpec((1, N_COND, tn), lambda l, j: (l, 0, j)),
        compiler_params=_cparams("arbitrary", "arbitrary"),
        name="adaln_mod",
    )(cond, w_ada, b_ada.reshape(DEPTH, 1, n))


def _head_rms(q, bd, g):
    hi, lo = _split_bf16(q * q)
    ssum = _dot(hi, bd) + _dot(lo, bd)
    return q * lax.rsqrt(ssum * (1.0 / HEAD_DIM) + RMS_EPS) * g


def _rope(x, cos, sin_signed):
    w = x.shape[-1]
    lane = lax.broadcasted_iota(jnp.int32, x.shape, 1)
    swapped = jnp.where((lane % 32) < 16, pltpu.roll(x, w - 16, 1), pltpu.roll(x, 16, 1))
    return x * cos + swapped * sin_signed


def _proj_kernel(xc_ref, xl_ref, mod_ref, w_ref, qg_ref, kg_ref, cos_ref, sin_ref, bd_ref, kc_in, vc_in,
                 o_ref, kc_ref, vc_ref):
    del kc_in, vc_in
    m = mod_ref[0, 0]
    in_ctx = pl.program_id(0) < N_CTX_TILES
    bd = bd_ref[...]
    sub_rows = [slice(s * PROJ_SUB, (s + 1) * PROJ_SUB) for s in range(TM // PROJ_SUB)]
    projs = []
    for rows in sub_rows:
        x = jnp.where(in_ctx, xc_ref[rows, :], xl_ref[rows, :])
        h = _ln(x) * (1.0 + m[1:2]) + m[0:1]
        proj = _dot(h.astype(BF16), w_ref[...])
        o_ref[rows, ATT_W + KV_W:] = proj[:, ATT_W + KV_W:]
        projs.append(proj[:, :ATT_W + 2 * KV_W])
    keys = []
    for rows, proj in zip(sub_rows, projs):
        qn = _head_rms(proj[:, :ATT_W], bd, qg_ref[0])
        kn = _head_rms(proj[:, ATT_W:ATT_W + KV_W], bd[:KV_W, :KV_W], kg_ref[0])
        cos = cos_ref[rows, :]
        sin = sin_ref[rows, :]
        o_ref[rows, :ATT_W] = jnp.where(in_ctx, qn, _rope(qn, cos, sin))
        o_ref[rows, ATT_W:ATT_W + KV_W] = jnp.where(in_ctx, kn, _rope(kn, cos[:, :KV_W], sin[:, :KV_W]))
        keys.append(kn)

    @pl.when(in_ctx)
    def _():
        for s in range(len(sub_rows)):
            kc_ref[s] = keys[s]
            vc_ref[s] = projs[s][:, ATT_W + KV_W:]


def _proj_call(layer, x, mod, w_in_b, qg, kg, cos_t, sin_t, bd, k_cache, v_cache):
    assert PROJ_SUB == SEQ
    n_ctx = T_CTX // TM
    per_seq = DEC_SEQ // TM

    def rope_idx(i):
        return (jnp.where(i < n_ctx, 0, (i - n_ctx) % per_seq), 0)

    cache_spec = pl.BlockSpec((TM // SEQ, None, SEQ, KV_W), lambda i: (jnp.minimum(i, n_ctx - 1), layer, 0, 0))
    cache_shape = jax.ShapeDtypeStruct((BATCH, DEPTH, SEQ, KV_W), F32)
    n_in = 11
    return pl.pallas_call(
        _proj_kernel,
        out_shape=(jax.ShapeDtypeStruct((T_ALL, MIX_W), F32), cache_shape, cache_shape),
        grid=(T_ALL // TM,),
        in_specs=_group_specs() + [
            pl.BlockSpec((1, 1, N_MOD, D_MODEL), lambda i: (layer, _cond_row(i, TM), 0, 0)),
            pl.BlockSpec((None, D_MODEL, MIX_W), lambda i: (layer, 0, 0)),
            pl.BlockSpec((1, 1, ATT_W), lambda i: (layer, 0, 0)),
            pl.BlockSpec((1, 1, KV_W), lambda i: (layer, 0, 0)),
            pl.BlockSpec((TM, ATT_W), rope_idx),
            pl.BlockSpec((TM, ATT_W), rope_idx),
            pl.BlockSpec((ATT_W, ATT_W), lambda i: (0, 0)),
            pl.BlockSpec(memory_space=pl.ANY),
            pl.BlockSpec(memory_space=pl.ANY),
        ],
        out_specs=(pl.BlockSpec((TM, MIX_W), lambda i: (i, 0)), cache_spec, cache_spec),
        input_output_aliases={n_in - 2: 1, n_in - 1: 2},
        compiler_params=_cparams("arbitrary"),
        name="in_proj",
    )(*x, mod, w_in_b, qg, kg, cos_t, sin_t, bd, k_cache, v_cache)


def _attention_tile(q, k_parts, v_parts, phase_major):
    def kv_of(h, parts):
        g = h // Q_PER_KV
        return [p[:, g * HEAD_DIM:(g + 1) * HEAD_DIM] for p in parts]

    def scores(h):
        qh = q[:, h * HEAD_DIM:(h + 1) * HEAD_DIM]
        return [lax.dot_general(qh, k, (((1,), (1,)), ((), ())), preferred_element_type=F32)
                for k in kv_of(h, k_parts)]

    if phase_major:
        all_ss = [scores(h) for h in range(N_Q_HEADS)]
        all_m = [functools.reduce(jnp.maximum, [jnp.max(s, axis=-1, keepdims=True) for s in ss]) for ss in all_ss]
        all_es = [[jnp.exp(s - m) for s in ss] for ss, m in zip(all_ss, all_m)]
        all_den = [functools.reduce(jnp.add, [jnp.sum(e, axis=-1, keepdims=True) for e in es]) for es in all_es]
        outs = [functools.reduce(jnp.add, [_dot(e.astype(BF16), v) for e, v in zip(es, kv_of(h, v_parts))])
                * (1.0 / den) for h, (es, den) in enumerate(zip(all_es, all_den))]
        return jnp.concatenate(outs, axis=-1)
    outs = []
    ss_next = scores(0)
    for h in range(N_Q_HEADS):
        ss = ss_next
        if h + 1 < N_Q_HEADS:
            ss_next = scores(h + 1)
        m = functools.reduce(jnp.maximum, [jnp.max(s, axis=-1, keepdims=True) for s in ss])
        es = [jnp.exp(s - m) for s in ss]
        denom = functools.reduce(jnp.add, [jnp.sum(e, axis=-1, keepdims=True) for e in es])
        o = functools.reduce(jnp.add, [_dot(e.astype(BF16), v) for e, v in zip(es, kv_of(h, v_parts))])
        outs.append(o * (1.0 / denom))
    return jnp.concatenate(outs, axis=-1)


def _mixer_kernel(*refs, seq, latent):
    if latent:
        (mix_ref, ck_ref, cv_ref, dft_ref, bdc_ref, bds_ref, ws_ref, bsm_ref, clg_ref, clb_ref, o_ref) = refs
    else:
        (mix_ref, dft_ref, bdc_ref, bds_ref, ws_ref, bsm_ref, clg_ref, clb_ref, o_ref) = refs

    k_new = mix_ref[:, ATT_W:ATT_W + KV_W].astype(BF16)
    v_new = mix_ref[:, ATT_W + KV_W:ATT_W + 2 * KV_W].astype(BF16)
    if latent:
        k_parts = [ck_ref[...].astype(BF16), k_new]
        v_parts = [cv_ref[...].astype(BF16), v_new]
    else:
        k_parts, v_parts = [k_new], [v_new]

    def q_tile(t, carry):
        r0 = pl.multiple_of(t * TQ, TQ)
        q = (mix_ref[pl.ds(r0, TQ), :ATT_W] * (HEAD_DIM ** -0.5)).astype(BF16)
        o_ref[pl.ds(r0, TQ), :ATT_W] = _attention_tile(q, k_parts, v_parts, phase_major=not latent).astype(BF16)
        return carry

    lax.fori_loop(0, seq // TQ, q_tile, 0)

    f_hi, f_lo = _split_bf16(mix_ref[:, ATT_W + 2 * KV_W:ATT_W + 2 * KV_W + FOURIER_W])
    bdc = bdc_ref[...]
    bds = bds_ref[...]
    y = jnp.concatenate([_dot(f_hi, bdc) + _dot(f_lo, bdc), _dot(f_hi, bds) + _dot(f_lo, bds)], axis=0)
    y_hi, y_lo = _split_bf16(y)
    dft = dft_ref[...]
    four = _dot(dft, y_hi) + _dot(dft, y_lo)
    o_ref[:, ATT_W:ATT_W + FOURIER_W] = four.astype(BF16)

    c0 = ATT_W + 2 * KV_W + FOURIER_W
    u = _gelu_tanh(mix_ref[:, c0:c0 + CHUNK_W])
    vn = (_ln(_gelu_tanh(mix_ref[:, c0 + CHUNK_W:c0 + 2 * CHUNK_W])) * clg_ref[0] + clb_ref[0]).astype(BF16)
    lane = lax.broadcasted_iota(jnp.int32, (CHUNK, CHUNK_W), 1)
    bsm = bsm_ref[...]
    for c in range(seq // CHUNK):
        vc = vn[c * CHUNK:(c + 1) * CHUNK]
        sv = bsm
        for g in range(N_CHUNK_GROUPS):
            vg = jnp.where(lane // CHUNK_GROUP_W == g, vc, jnp.zeros_like(vc))
            sv = sv + _dot(ws_ref[g], vg)
        o_ref[c * CHUNK:(c + 1) * CHUNK, ATT_W + FOURIER_W:] = (u[c * CHUNK:(c + 1) * CHUNK] * sv).astype(BF16)


def _mixer_call(layer, mix, ctx_k, ctx_v, dft, bdc, bds, ws, bsm, clg, clb, *, latent):
    seq = DEC_SEQ if latent else SEQ
    nb = DEC_BATCH if latent else BATCH
    row0 = (T_CTX // seq) if latent else 0
    const2 = lambda b: (0, 0)
    per_layer = lambda b: (layer, 0, 0)
    in_specs = [pl.BlockSpec((seq, MIX_W), lambda b: (row0 + b, 0))]
    args = [mix]
    if latent:
        in_specs += [pl.BlockSpec((None, None, PAST_LEN, KV_W), lambda b: (b, layer, 0, 0))] * 2
        args += [ctx_k, ctx_v]
    in_specs += [
        pl.BlockSpec((seq, 2 * seq), const2),
        pl.BlockSpec((FOURIER_W, FOURIER_W), const2),
        pl.BlockSpec((FOURIER_W, FOURIER_W), const2),
        pl.BlockSpec((None, N_CHUNK_GROUPS, CHUNK, CHUNK), lambda b: (layer, 0, 0, 0)),
        pl.BlockSpec((None, CHUNK, CHUNK_W), per_layer),
        pl.BlockSpec((1, 1, CHUNK_W), per_layer),
        pl.BlockSpec((1, 1, CHUNK_W), per_layer),
    ]
    args += [dft, bdc, bds, ws, bsm, clg, clb]
    return pl.pallas_call(
        functools.partial(_mixer_kernel, seq=seq, latent=latent),
        out_shape=jax.ShapeDtypeStruct((nb * seq, D_MODEL), BF16),
        grid=(nb,),
        in_specs=in_specs,
        out_specs=pl.BlockSpec((seq, D_MODEL), lambda b: (b, 0)),
        compiler_params=_cparams("arbitrary"),
        name="mixer_latent" if latent else "mixer_context",
    )(*args)


def _route(logits, bias_t):
    lt = logits.T[:N_EXPERTS]
    ex = jnp.exp(lt - jnp.max(lt, axis=0, keepdims=True))
    probs = ex / jnp.sum(ex, axis=0, keepdims=True)
    sel = probs + bias_t
    p = [probs[e:e + 1] for e in range(N_EXPERTS)]
    s = [sel[e:e + 1] for e in range(N_EXPERTS)]
    n = EXPERTS_PER_GROUP
    scores = []
    for g in range(N_EXPERT_GROUPS):
        pair = [s[g * n + a] + s[g * n + b] for a in range(n) for b in range(a + 1, n)]
        scores.append(functools.reduce(jnp.maximum, pair))
    best = jnp.zeros_like(scores[0], dtype=jnp.int32)
    best_score = scores[0]
    for g in range(1, N_EXPERT_GROUPS):
        better = scores[g] > best_score
        best = jnp.where(better, g, best)
        best_score = jnp.where(better, scores[g], best_score)
    cls = jnp.zeros_like(best_score)
    w_a = jnp.zeros_like(best_score)
    w_b = jnp.zeros_like(best_score)
    for g in range(N_EXPERT_GROUPS):
        in_g = best == g
        chosen = []
        for a in range(n):
            rank = jnp.zeros_like(best)
            for b in range(n):
                if b == a:
                    continue
                ahead = (s[g * n + b] > s[g * n + a]) if b > a else (s[g * n + b] >= s[g * n + a])
                rank = rank + ahead.astype(jnp.int32)
            chosen.append(jnp.logical_and(in_g, rank < 2))
        for k in range(N_PAIRS):
            pa, pb = p[g * n + PAIR_A[k]], p[g * n + PAIR_B[k]]
            hit = jnp.logical_and(chosen[PAIR_A[k]], chosen[PAIR_B[k]])
            wsum = pa + pb
            cls = jnp.where(hit, float(g * N_PAIRS + k), cls)
            w_a = jnp.where(hit, pa / wsum, w_a)
            w_b = jnp.where(hit, pb / wsum, w_b)
    return jnp.concatenate([cls, w_a, w_b, jnp.zeros((ROUTE_ROWS - 3, lt.shape[1]), F32)], axis=0)


def _merge_kernel(xc_ref, xl_ref, mixc_ref, mixl_ref, mod_ref, win_ref, wpa_ref, wpf_ref, wpc_ref, wo_ref,
                  l1g_ref, l1b_ref, wrh_ref, wrl_ref, bias_ref, x1_ref, ha_ref, rt_ref):
    m = mod_ref[0, 0]
    in_ctx = pl.program_id(0) < N_CTX_TILES
    branch_w = ((0, ATT_W, wpa_ref), (ATT_W, FOURIER_W, wpf_ref), (ATT_W + FOURIER_W, CHUNK_W, wpc_ref))
    sub_rows = [slice(s * MERGE_SUB, (s + 1) * MERGE_SUB) for s in range(TM // MERGE_SUB)]
    residual = []
    for rows in sub_rows:
        x = jnp.where(in_ctx, xc_ref[rows, :], xl_ref[rows, :])
        mixed = jnp.where(in_ctx, mixc_ref[rows, :], mixl_ref[rows, :])
        h = (_ln(x) * (1.0 + m[1:2]) + m[0:1]).astype(BF16)
        merged = None
        for b, (c0, width, w_ref) in enumerate(branch_w):
            gate = _sigmoid(_dot(h, win_ref[:, MIX_W + b * D_MODEL:MIX_W + (b + 1) * D_MODEL]))
            term = gate * _dot(mixed[:, c0:c0 + width], w_ref[...])
            merged = term if merged is None else merged + term
        mix = _dot(merged.astype(BF16), wo_ref[...])
        residual.append(ALPHA * x + m[2:3] * mix)
    for rows, pre in zip(sub_rows, residual):
        x1 = _ln(pre) * l1g_ref[0] + l1b_ref[0]
        x1_ref[rows, :] = x1
        h2 = _ln(x1) * (1.0 + m[4:5]) + m[3:4]
        h_hi, h_lo = _split_bf16(h2)
        wrh = wrh_ref[...]
        logits = _dot(h_hi, wrh) + _dot(h_lo, wrh) + _dot(h_hi, wrl_ref[...])
        rt = _route(logits, bias_ref[:, rows])
        rt_ref[:, rows] = rt
        ha_ref[rows, :D_MODEL] = h2
        ha_ref[rows, D_MODEL:] = jnp.concatenate([rt, jnp.zeros((LANES - ROUTE_ROWS, MERGE_SUB), F32)], axis=0).T


def _merge_call(layer, x, mixed_ctx, mixed_lat, mod, w_in_b, wpa, wpf, wpc, wo, l1g, l1b, wrh, wrl, bias_t):
    const2 = lambda i: (0, 0)
    per_layer = lambda i: (layer, 0, 0)
    tok = lambda i: (i, 0)
    return pl.pallas_call(
        _merge_kernel,
        out_shape=(jax.ShapeDtypeStruct((T_ALL, D_MODEL), F32),
                   jax.ShapeDtypeStruct((T_ALL, HA_W), F32),
                   jax.ShapeDtypeStruct((ROUTE_ROWS, T_ALL), F32)),
        grid=(T_ALL // TM,),
        in_specs=_group_specs() + _group_specs() + [
            pl.BlockSpec((1, 1, N_MOD, D_MODEL), lambda i: (layer, _cond_row(i, TM), 0, 0)),
            pl.BlockSpec((None, D_MODEL, IN_W), per_layer, pipeline_mode=pl.Buffered(1)),
            pl.BlockSpec((None, ATT_W, D_MODEL), per_layer, pipeline_mode=pl.Buffered(1)),
            pl.BlockSpec((None, FOURIER_W, D_MODEL), per_layer, pipeline_mode=pl.Buffered(1)),
            pl.BlockSpec((None, CHUNK_W, D_MODEL), per_layer, pipeline_mode=pl.Buffered(1)),
            pl.BlockSpec((None, D_MODEL, D_MODEL), per_layer, pipeline_mode=pl.Buffered(1)),
            pl.BlockSpec((1, 1, D_MODEL), per_layer),
            pl.BlockSpec((1, 1, D_MODEL), per_layer),
            pl.BlockSpec((D_MODEL, ROUTER_PAD), const2),
            pl.BlockSpec((D_MODEL, ROUTER_PAD), const2),
            pl.BlockSpec((N_EXPERTS, TM), const2),
        ],
        out_specs=(pl.BlockSpec((TM, D_MODEL), tok),
                   pl.BlockSpec((TM, HA_W), tok),
                   pl.BlockSpec((ROUTE_ROWS, TM), lambda i: (0, i))),
        compiler_params=_cparams("arbitrary"),
        name="merge_route",
    )(*x, mixed_ctx, mixed_lat, mod, w_in_b, wpa, wpf, wpc, wo, l1g, l1b, wrh, wrl, bias_t)


def _sort_kernel(rt_ref, tri_ref, low_ref, pos_ref, tile_ref):
    crow = lax.broadcasted_iota(jnp.int32, (CLASS_PAD, SORT_BLK), 0)
    tri = tri_ref[...]
    n_blk = T_ALL // SORT_BLK
    carry = jnp.zeros((CLASS_PAD, 1), F32)
    ranks = []
    for b in range(n_blk):
        hot = rt_ref[0:1, b * SORT_BLK:(b + 1) * SORT_BLK].astype(jnp.int32) == crow
        hot_f = jnp.where(hot, 1.0, 0.0)
        before = _dot(hot_f.astype(BF16), tri) + carry
        ranks.append(jnp.sum(jnp.where(hot, before, 0.0), axis=0, keepdims=True))
        carry = carry + jnp.sum(hot_f, axis=1, keepdims=True)
    padded = jnp.floor((carry + (TM_MOE - 1.0)) * (1.0 / TM_MOE)) * TM_MOE
    padded = jnp.broadcast_to(padded, (CLASS_PAD, LANES))
    offs = _dot(low_ref[...], padded.astype(BF16))
    for b in range(n_blk):
        hot = rt_ref[0:1, b * SORT_BLK:(b + 1) * SORT_BLK].astype(jnp.int32) == crow
        base = jnp.sum(jnp.where(hot, offs[:, 0:1], 0.0), axis=0, keepdims=True)
        pos_ref[:, b * SORT_BLK:(b + 1) * SORT_BLK] = (base + ranks[b]).astype(jnp.int32)

    start = lax.broadcasted_iota(jnp.int32, (1, LANES), 1).astype(F32) * TM_MOE
    is_class = lax.broadcasted_iota(jnp.int32, (CLASS_PAD, LANES), 0) < N_CLASSES
    ends = jnp.where(is_class, offs + padded, 0.0)
    total = jnp.max(ends, axis=0, keepdims=True)
    valid = start < total
    tcls = jnp.sum(jnp.where(jnp.logical_and(is_class, ends <= start), 1.0, 0.0), axis=0, keepdims=True)
    last = jnp.max(jnp.where(valid, tcls, 0.0), axis=1, keepdims=True)
    tcls = jnp.where(valid, tcls, last)
    grp = functools.reduce(jnp.add, [jnp.where(tcls >= g * N_PAIRS, 1.0, 0.0) for g in range(1, N_EXPERT_GROUPS)])
    pair = tcls - grp * N_PAIRS
    slot_a = functools.reduce(jnp.add, [jnp.where(pair == k, float(PAIR_A[k]), 0.0) for k in range(N_PAIRS)])
    slot_b = functools.reduce(jnp.add, [jnp.where(pair == k, float(PAIR_B[k]), 0.0) for k in range(N_PAIRS)])
    crow_t = lax.broadcasted_iota(jnp.int32, (CLASS_PAD, LANES), 0).astype(F32)
    real_end = jnp.sum(jnp.where(crow_t == tcls, offs + carry, 0.0), axis=0, keepdims=True)
    n_rows = jnp.where(valid, jnp.clip(real_end - start, 0.0, float(TM_MOE)), 0.0)
    rows = [grp * EXPERTS_PER_GROUP + slot_a, grp * EXPERTS_PER_GROUP + slot_b, n_rows]
    tile_ref[...] = jnp.concatenate(rows + [jnp.zeros((8 - len(rows), LANES), F32)], axis=0).astype(jnp.int32)


def _sort_call(rt, tri, low):
    return pl.pallas_call(
        _sort_kernel,
        out_shape=(jax.ShapeDtypeStruct((1, T_ALL), jnp.int32),
                   jax.ShapeDtypeStruct((8, LANES), jnp.int32)),
        name="route_sort",
    )(rt, tri, low)


def _moe_kernel(tile_ref, pos_ref, ha_ref, wg_hbm, wu_hbm, wd_hbm, out_ref, hbuf, ybuf, src_ref, ssem,
                stg_g, stg_u, stg_d, act_g, act_u, act_d, wsem, *, layer):
    j = pl.program_id(0)
    slot = j % 2

    def n_rows(tile):
        inside = jnp.logical_and(tile >= 0, tile < N_TILES)
        return jnp.where(inside, tile_ref[TILE_ROWS, jnp.clip(tile, 0, N_TILES - 1)], 0)

    def token_of(tile, n, r, tail):
        return src_ref[tile * TM_MOE + (jnp.minimum(r, n - 1) if tail else r)]

    def scatter_copy(tile, n, buf_slot, r, tail):
        tok = token_of(tile, n, r, tail)
        dst = jnp.where(r < n, tok, T_ALL + buf_slot * TM_MOE + r) if tail else tok
        return pltpu.make_async_copy(ybuf.at[buf_slot, r], out_ref.at[dst], ssem.at[buf_slot])

    def for_groups(n, fn):
        full = lax.shift_right_logical(n, 3)

        def body(g, c):
            for k in range(COPY_GROUP):
                fn(g, k, False)
            return c
        lax.fori_loop(0, full, body, 0)

        @pl.when(jnp.bitwise_and(n, COPY_GROUP - 1) != 0)
        def _():
            for k in range(COPY_GROUP):
                fn(full, k, True)

    def retire(tile, buf_slot):
        n = n_rows(tile)
        one = scatter_copy(tile, n, buf_slot, 0, False)

        def body(g, c):
            for _ in range(COPY_GROUP):
                one.wait()
            return c
        lax.fori_loop(0, lax.shift_right_logical(n + COPY_GROUP - 1, 3), body, 0)

    def expert_of(x, tile):
        return tile_ref[TILE_SLOT_A + x, tile]

    def weight_copies(x, e):
        return [pltpu.make_async_copy(src.at[layer, e], stg.at[x], wsem.at[x])
                for src, stg in ((wg_hbm, stg_g), (wu_hbm, stg_u), (wd_hbm, stg_d))]

    @pl.when(j == 0)
    def _():
        for x in range(2):
            for c in weight_copies(x, expert_of(x, 0)):
                c.start()

        def place(t, c):
            src_ref[pos_ref[t]] = t
            return c
        lax.fori_loop(0, T_ALL, place, 0, unroll=8)
        hbuf[...] = jnp.zeros(hbuf.shape, F32)
        ybuf[...] = jnp.zeros(ybuf.shape, F32)
        for s in range(2):
            park = pltpu.make_async_copy(ybuf.at[s], out_ref.at[pl.ds(T_ALL + s * TM_MOE, TM_MOE)], ssem.at[s])
            park.start()
            park.wait()

    @pl.when(n_rows(j - 2) > 0)
    def _():
        retire(j - 2, slot)

    def run_experts(rows):
        rec = hbuf[:rows // SUBLANES].reshape(rows, HA_W)
        h = rec[:, :D_MODEL].astype(BF16)

        def expert(x):
            a = _dot(h, act_g[x])
            u = _dot(h, act_u[x])
            w = rec[:, D_MODEL + 1 + x:D_MODEL + 2 + x]
            return _dot(((a * _sigmoid(a)) * u * w).astype(BF16), act_d[x])

        ybuf[slot, :rows] = (expert(0) + expert(1)).reshape(rows, D_CHUNKS, LANES)

    @pl.when(n_rows(j) > 0)
    def _():
        for x in range(2):
            prev = jnp.maximum(j - 1, 0)

            @pl.when(jnp.logical_or(j == 0, expert_of(x, j) != expert_of(x, prev)))
            def _():
                for c in weight_copies(x, expert_of(x, j)):
                    c.wait()
                act_g[x] = stg_g[x].astype(BF16)
                act_u[x] = stg_u[x].astype(BF16)
                act_d[x] = stg_d[x].astype(BF16)

        for x in range(2):
            nxt = jnp.minimum(j + 1, N_TILES - 1)

            @pl.when(jnp.logical_and(n_rows(j + 1) > 0, expert_of(x, nxt) != expert_of(x, j)))
            def _():
                for c in weight_copies(x, expert_of(x, nxt)):
                    c.start()

        n = n_rows(j)

        def gather_row(g, k, tail):
            tok = token_of(j, n, g * COPY_GROUP + k, tail)
            grp, sub = lax.shift_right_logical(tok, 3), jnp.bitwise_and(tok, SUBLANES - 1)
            hbuf[g, pl.ds(k, 1), :] = ha_ref[grp, pl.ds(sub, 1), :]
        for_groups(n, gather_row)

        @pl.when(n > SMALL_TILE)
        def _():
            run_experts(TM_MOE)

        @pl.when(n <= SMALL_TILE)
        def _():
            run_experts(SMALL_TILE)

        for_groups(n, lambda g, k, tail: scatter_copy(j, n, slot, g * COPY_GROUP + k, tail).start(priority=k % 2))

    @pl.when(j == N_TILES - 1)
    def _():
        @pl.when(n_rows(j - 1) > 0)
        def _():
            retire(j - 1, 1 - slot)

        @pl.when(n_rows(j) > 0)
        def _():
            retire(j, slot)


def _moe_call(layer, tiles, pos, ha, wge, wue, wde):
    up = (2, D_MODEL, D_EXPERT)
    down = (2, D_EXPERT, D_MODEL)
    grid_spec = pltpu.PrefetchScalarGridSpec(
        num_scalar_prefetch=2,
        grid=(N_TILES,),
        in_specs=[
            pl.BlockSpec((T_ALL // SUBLANES, SUBLANES, HA_W), lambda j, tl, ps: (0, 0, 0),
                         pipeline_mode=pl.Buffered(1)),
            pl.BlockSpec(memory_space=pl.ANY), pl.BlockSpec(memory_space=pl.ANY), pl.BlockSpec(memory_space=pl.ANY),
        ],
        out_specs=pl.BlockSpec(memory_space=pl.ANY),
        scratch_shapes=[
            pltpu.VMEM((TM_MOE // SUBLANES, SUBLANES, HA_W), F32),
            pltpu.VMEM((2, TM_MOE, D_CHUNKS, LANES), F32),
            pltpu.SMEM((R_ROWS,), jnp.int32),
            pltpu.SemaphoreType.DMA((2,)),
            pltpu.VMEM(up, F32), pltpu.VMEM(up, F32), pltpu.VMEM(down, F32),
            pltpu.VMEM(up, BF16), pltpu.VMEM(up, BF16), pltpu.VMEM(down, BF16),
            pltpu.SemaphoreType.DMA((2,)),
        ],
    )
    return pl.pallas_call(
        functools.partial(_moe_kernel, layer=layer),
        out_shape=jax.ShapeDtypeStruct((T_ALL + N_PARK, D_CHUNKS, LANES), F32),
        grid_spec=grid_spec,
        compiler_params=pltpu.CompilerParams(dimension_semantics=("arbitrary",), vmem_limit_bytes=V7X_VMEM_LIMIT,
                                             has_side_effects=True),
        name="moe_pairs",
    )(tiles, pos,
      ha.reshape(T_ALL // SUBLANES, SUBLANES, HA_W), wge, wue, wde)


def _post_kernel(x1_ref, moe_ref, mod_ref, l2g_ref, l2b_ref, oc_ref, ol_ref):
    m = mod_ref[0, 0]
    moe = moe_ref[...].reshape(TM, D_MODEL)
    y = _ln(ALPHA * x1_ref[...] + m[5:6] * moe) * l2g_ref[0] + l2b_ref[0]
    i = pl.program_id(0)

    @pl.when(i < N_CTX_TILES)
    def _():
        oc_ref[...] = y

    @pl.when(i >= N_CTX_TILES)
    def _():
        ol_ref[...] = y


def _post_call(layer, x1, moe, mod, l2g, l2b):
    tok = lambda i: (i, 0)
    per_layer = lambda i: (layer, 0, 0)
    return pl.pallas_call(
        _post_kernel,
        out_shape=(jax.ShapeDtypeStruct((T_CTX, D_MODEL), F32), jax.ShapeDtypeStruct((T_LAT, D_MODEL), F32)),
        grid=(T_ALL // TM,),
        in_specs=[
            pl.BlockSpec((TM, D_MODEL), tok),
            pl.BlockSpec((TM, D_CHUNKS, LANES), lambda i: (i, 0, 0)),
            pl.BlockSpec((1, 1, N_MOD, D_MODEL), lambda i: (layer, _cond_row(i, TM), 0, 0)),
            pl.BlockSpec((1, 1, D_MODEL), per_layer),
            pl.BlockSpec((1, 1, D_MODEL), per_layer),
        ],
        out_specs=tuple(_group_specs()),
        compiler_params=_cparams("arbitrary"),
        name="post_moe",
    )(x1, moe, mod, l2g, l2b)


def _rope_tables():
    pos = np.arange(DEC_SEQ)
    quarter = HEAD_DIM // 4
    inv = ROPE_THETA ** (-np.arange(quarter, dtype=np.float64) / quarter)
    ang_r = (pos // GRID_W)[:, None] * inv[None, :]
    ang_c = (pos % GRID_W)[:, None] * inv[None, :]
    cos = np.concatenate([np.cos(ang_r)] * 2 + [np.cos(ang_c)] * 2, axis=-1)
    sin = np.concatenate([-np.sin(ang_r), np.sin(ang_r), -np.sin(ang_c), np.sin(ang_c)], axis=-1)
    return (jnp.asarray(np.tile(cos, (1, N_Q_HEADS)).astype(np.float32)),
            jnp.asarray(np.tile(sin, (1, N_Q_HEADS)).astype(np.float32)))


def _dft_mats(n, scale):
    j = np.arange(n)
    ang = ((j[:, None] * j[None, :]) % n) * (2 * np.pi / n)
    return np.cos(ang) * scale, np.sin(ang) * scale


def _block_diag(m, reps):
    return np.kron(np.eye(reps), m)


def _const_bf16(a):
    return jnp.asarray(np.asarray(a, np.float32)).astype(BF16)


def _const_01(a):
    return jnp.asarray(np.asarray(a, np.float32).astype(BF16))


def kernel(x_prompt, x_sample, cache_k, cache_v, c, c_ctx, w_in, q_norm_g, k_norm_g, w_proj_att,
           w_proj_fourier, w_proj_chunk, w_out, chunk_ln_g, chunk_ln_b, chunk_ws, chunk_bs, w_ada, b_ada,
           ln1_g, ln1_b, ln2_g, ln2_b, w_router, router_bias, w_gate_e, w_up_e, w_down_e):
    x = (x_prompt.reshape(T_CTX, D_MODEL), x_sample.reshape(T_LAT, D_MODEL))
    cond = jnp.concatenate([c_ctx[None, :], c, jnp.zeros((N_COND - 1 - DEC_BATCH, D_MODEL), F32)], axis=0)
    mod = _mod_call(cond, w_ada, b_ada).reshape(DEPTH, N_COND, N_MOD, D_MODEL)

    cos_t, sin_t = _rope_tables()
    bd_heads = _const_01(_block_diag(np.ones((HEAD_DIM, HEAD_DIM)), N_Q_HEADS))
    c64, s64 = _dft_mats(FOURIER_GROUP_W, 1.0)
    bdc = _const_bf16(_block_diag(c64, FOURIER_W // FOURIER_GROUP_W))
    bds = _const_bf16(_block_diag(s64, FOURIER_W // FOURIER_GROUP_W))
    dft = {}
    for seq in (SEQ, DEC_SEQ):
        cs, ss = _dft_mats(seq, 1.0 / math.sqrt(seq * FOURIER_GROUP_W))
        dft[seq] = _const_bf16(np.concatenate([cs, -ss], axis=1))
    ctx_k = cache_k.reshape(DEC_BATCH, DEPTH, PAST_LEN, KV_W)
    ctx_v = cache_v.reshape(DEC_BATCH, DEPTH, PAST_LEN, KV_W)
    wr = jnp.pad(w_router, ((0, 0), (0, ROUTER_PAD - N_EXPERTS)))
    wr_hi = wr.astype(BF16)
    wr_lo = (wr - wr_hi.astype(F32)).astype(BF16)
    bias_t = jnp.broadcast_to(router_bias[:, None], (N_EXPERTS, TM))
    tri = _const_01(np.triu(np.ones((SORT_BLK, SORT_BLK)), 1))
    low = _const_01(np.tril(np.ones((CLASS_PAD, CLASS_PAD)), -1))

    w_in_b = w_in.astype(BF16)
    wpa, wpf, wpc, wo = (w.astype(BF16) for w in (w_proj_att, w_proj_fourier, w_proj_chunk, w_out))
    wge, wue, wde = w_gate_e, w_up_e, w_down_e
    qg = jnp.tile(q_norm_g, (1, N_Q_HEADS))[:, None, :]
    kg = jnp.tile(k_norm_g, (1, N_KV_HEADS))[:, None, :]
    ws = chunk_ws.astype(BF16)
    bsm = jnp.repeat(jnp.swapaxes(chunk_bs, 1, 2), CHUNK_GROUP_W, axis=2)
    row3 = lambda a: a[:, None, :]

    new_k = jnp.zeros((BATCH, DEPTH, SEQ, KV_W), F32)
    new_v = jnp.zeros((BATCH, DEPTH, SEQ, KV_W), F32)
    for l in range(DEPTH):
        mix, new_k, new_v = _proj_call(l, x, mod, w_in_b, qg, kg, cos_t, sin_t, bd_heads, new_k, new_v)
        mixer_args = (bdc, bds, ws, bsm, row3(chunk_ln_g), row3(chunk_ln_b))
        mixed_ctx = _mixer_call(l, mix, None, None, dft[SEQ], *mixer_args, latent=False)
        mixed_lat = _mixer_call(l, mix, ctx_k, ctx_v, dft[DEC_SEQ], *mixer_args, latent=True)
        x1, ha, rt = _merge_call(l, x, mixed_ctx, mixed_lat, mod, w_in_b, wpa, wpf, wpc, wo,
                                 row3(ln1_g), row3(ln1_b), wr_hi, wr_lo, bias_t)
        pos2d, tiles = _sort_call(rt, tri, low)
        moe = _moe_call(l, tiles, pos2d[0], ha, wge, wue, wde)
        x = _post_call(l, x1, moe, mod, row3(ln2_g), row3(ln2_b))

    y_prompt = x[0].reshape(BATCH, SEQ, D_MODEL)
    y_sample = x[1].reshape(DEC_BATCH, DEC_SEQ, D_MODEL)
    cache_shape = (BATCH, DEPTH, SEQ, N_KV_HEADS, HEAD_DIM)
    return (y_prompt, y_sample, new_k.reshape(cache_shape), new_v.reshape(cache_shape))
```

```python
import functools
import math

import jax
import jax.numpy as jnp
import numpy as np
from jax import lax
from jax.experimental import pallas as pl
from jax.experimental.pallas import tpu as pltpu

F32 = jnp.float32
BF16 = jnp.bfloat16

D_MODEL = 1024
BATCH = 16
SEQ = 256
DEPTH = 2
DEC_BATCH = 2
DEC_SEQ = 1024
PAST_LEN = 256
GRID_W = 64
N_Q_HEADS = 8
N_KV_HEADS = 2
HEAD_DIM = 64
Q_PER_KV = N_Q_HEADS // N_KV_HEADS
ATT_W = N_Q_HEADS * HEAD_DIM
KV_W = N_KV_HEADS * HEAD_DIM
ROPE_THETA = 10000.0
FOURIER_GROUP_W = 64
FOURIER_W = 256
N_CHUNK_GROUPS = 4
CHUNK_GROUP_W = 64
CHUNK_W = 256
CHUNK = 128
N_EXPERTS = 16
N_EXPERT_GROUPS = 4
EXPERTS_PER_GROUP = 4
D_EXPERT = 512
ALPHA = (2 * DEPTH) ** 0.25
LN_EPS = 1e-6
RMS_EPS = 1e-6

LANES = 128
SUBLANES = 8
T_CTX = BATCH * SEQ
T_LAT = DEC_BATCH * DEC_SEQ
T_ALL = T_CTX + T_LAT
IN_W = ATT_W + 2 * KV_W + FOURIER_W + 2 * CHUNK_W + 3 * D_MODEL
MIX_W = ATT_W + 2 * KV_W + FOURIER_W + 2 * CHUNK_W
N_MOD = 6
N_COND = 8
ROUTER_PAD = LANES
TM = 512
TQ = 256
MERGE_SUB = 256
PROJ_SUB = 256
V7X_VMEM_LIMIT = 56 * 1024 * 1024

PAIR_A = (0, 0, 0, 1, 1, 3)
PAIR_B = (1, 2, 3, 3, 2, 2)
N_PAIRS = len(PAIR_A)
N_CLASSES = N_EXPERT_GROUPS * N_PAIRS
CLASS_PAD = 32
ROUTE_ROWS = 8
TM_MOE = 256
N_TILES = T_ALL // TM_MOE + N_CLASSES
R_ROWS = N_TILES * TM_MOE
SORT_BLK = 512
D_CHUNKS = D_MODEL // LANES
HA_W = D_MODEL + LANES
TILE_SLOT_A, TILE_SLOT_B, TILE_ROWS = 0, 1, 2
SMALL_TILE = 64
N_PARK = 2 * TM_MOE
COPY_GROUP = 8


def _cparams(*sem):
    return pltpu.CompilerParams(dimension_semantics=sem, vmem_limit_bytes=V7X_VMEM_LIMIT)


def _cond_row(i, tm):
    n_ctx = T_CTX // tm
    return jnp.where(i < n_ctx, 0, 1 + (i - n_ctx) // (DEC_SEQ // tm))


N_CTX_TILES = T_CTX // TM


def _group_specs():
    return [pl.BlockSpec((TM, D_MODEL), lambda i: (jnp.minimum(i, N_CTX_TILES - 1), 0)),
            pl.BlockSpec((TM, D_MODEL), lambda i: (jnp.maximum(i - N_CTX_TILES, 0), 0))]


def _group_tile(ctx_ref, lat_ref):
    return jnp.where(pl.program_id(0) < N_CTX_TILES, ctx_ref[...], lat_ref[...])


def _ln(x):
    mu = jnp.mean(x, axis=-1, keepdims=True)
    xc = x - mu
    var = jnp.mean(xc * xc, axis=-1, keepdims=True)
    return xc * lax.rsqrt(var + LN_EPS)


def _split_bf16(x):
    hi = x.astype(BF16)
    lo = (x - hi.astype(F32)).astype(BF16)
    return hi, lo


def _dot(a, b):
    return jnp.dot(a, b, preferred_element_type=F32)


def _sigmoid(x):
    return 1.0 / (1.0 + jnp.exp(-x))


def _gelu_tanh(x):
    c = np.float32(np.sqrt(2 / np.pi))
    return x * (0.5 * (1.0 + jnp.tanh(c * (x + 0.044715 * (x * x * x)))))


def _mod_kernel(c_ref, w_ref, b_ref, o_ref):
    c = c_ref[...]
    s = c * _sigmoid(c)
    o_ref[0] = _dot(s.astype(BF16), w_ref[0].astype(BF16)) + b_ref[0]


def _mod_call(cond, w_ada, b_ada):
    tn = 1536
    n = N_MOD * D_MODEL
    return pl.pallas_call(
        _mod_kernel,
        out_shape=jax.ShapeDtypeStruct((DEPTH, N_COND, n), F32),
        grid=(DEPTH, n // tn),
        in_specs=[
            pl.BlockSpec((N_COND, D_MODEL), lambda l, j: (0, 0)),
            pl.BlockSpec((1, D_MODEL, tn), lambda l, j: (l, 0, j)),
            pl.BlockSpec((1, 1, tn), lambda l, j: (l, 0, j)),
        ],
        out_specs=pl.BlockSpec((1, N_COND, tn), lambda l, j: (l, 0, j)),
        compiler_params=_cparams("arbitrary", "arbitrary"),
        name="adaln_mod",
    )(cond, w_ada, b_ada.reshape(DEPTH, 1, n))


def _head_rms(q, bd, g):
    hi, lo = _split_bf16(q * q)
    ssum = _dot(hi, bd) + _dot(lo, bd)
    return q * lax.rsqrt(ssum * (1.0 / HEAD_DIM) + RMS_EPS) * g


def _rope(x, cos, sin_signed):
    w = x.shape[-1]
    lane = lax.broadcasted_iota(jnp.int32, x.shape, 1)
    swapped = jnp.where((lane % 32) < 16, pltpu.roll(x, w - 16, 1), pltpu.roll(x, 16, 1))
    return x * cos + swapped * sin_signed


def _proj_kernel(xc_ref, xl_ref, mod_ref, w_ref, qg_ref, kg_ref, cos_ref, sin_ref, bd_ref, kc_in, vc_in,
                 o_ref, kc_ref, vc_ref):
    del kc_in, vc_in
    m = mod_ref[0, 0]
    in_ctx = pl.program_id(0) < N_CTX_TILES
    bd = bd_ref[...]
    sub_rows = [slice(s * PROJ_SUB, (s + 1) * PROJ_SUB) for s in range(TM // PROJ_SUB)]
    projs = []
    for rows in sub_rows:
        x = jnp.where(in_ctx, xc_ref[rows, :], xl_ref[rows, :])
        h = _ln(x) * (1.0 + m[1:2]) + m[0:1]
        proj = _dot(h.astype(BF16), w_ref[...])
        o_ref[rows, ATT_W + KV_W:] = proj[:, ATT_W + KV_W:]
        projs.append(proj[:, :ATT_W + 2 * KV_W])
    keys = []
    for rows, proj in zip(sub_rows, projs):
        qn = _head_rms(proj[:, :ATT_W], bd, qg_ref[0])
        kn = _head_rms(proj[:, ATT_W:ATT_W + KV_W], bd[:KV_W, :KV_W], kg_ref[0])
        cos = cos_ref[rows, :]
        sin = sin_ref[rows, :]
        o_ref[rows, :ATT_W] = jnp.where(in_ctx, qn, _rope(qn, cos, sin))
        o_ref[rows, ATT_W:ATT_W + KV_W] = jnp.where(in_ctx, kn, _rope(kn, cos[:, :KV_W], sin[:, :KV_W]))
        keys.append(kn)

    @pl.when(in_ctx)
    def _():
        for s in range(len(sub_rows)):
            kc_ref[s] = keys[s]
            vc_ref[s] = projs[s][:, ATT_W + KV_W:]


def _proj_call(layer, x, mod, w_in_b, qg, kg, cos_t, sin_t, bd, k_cache, v_cache):
    assert PROJ_SUB == SEQ
    n_ctx = T_CTX // TM
    per_seq = DEC_SEQ // TM

    def rope_idx(i):
        return (jnp.where(i < n_ctx, 0, (i - n_ctx) % per_seq), 0)

    cache_spec = pl.BlockSpec((TM // SEQ, None, SEQ, KV_W), lambda i: (jnp.minimum(i, n_ctx - 1), layer, 0, 0))
    cache_shape = jax.ShapeDtypeStruct((BATCH, DEPTH, SEQ, KV_W), F32)
    n_in = 11
    return pl.pallas_call(
        _proj_kernel,
        out_shape=(jax.ShapeDtypeStruct((T_ALL, MIX_W), F32), cache_shape, cache_shape),
        grid=(T_ALL // TM,),
        in_specs=_group_specs() + [
            pl.BlockSpec((1, 1, N_MOD, D_MODEL), lambda i: (layer, _cond_row(i, TM), 0, 0)),
            pl.BlockSpec((None, D_MODEL, MIX_W), lambda i: (layer, 0, 0)),
            pl.BlockSpec((1, 1, ATT_W), lambda i: (layer, 0, 0)),
            pl.BlockSpec((1, 1, KV_W), lambda i: (layer, 0, 0)),
            pl.BlockSpec((TM, ATT_W), rope_idx),
            pl.BlockSpec((TM, ATT_W), rope_idx),
            pl.BlockSpec((ATT_W, ATT_W), lambda i: (0, 0)),
            pl.BlockSpec(memory_space=pl.ANY),
            pl.BlockSpec(memory_space=pl.ANY),
        ],
        out_specs=(pl.BlockSpec((TM, MIX_W), lambda i: (i, 0)), cache_spec, cache_spec),
        input_output_aliases={n_in - 2: 1, n_in - 1: 2},
        compiler_params=_cparams("arbitrary"),
        name="in_proj",
    )(*x, mod, w_in_b, qg, kg, cos_t, sin_t, bd, k_cache, v_cache)


def _attention_tile(q, k_parts, v_parts, phase_major):
    def kv_of(h, parts):
        g = h // Q_PER_KV
        return [p[:, g * HEAD_DIM:(g + 1) * HEAD_DIM] for p in parts]

    def scores(h):
        qh = q[:, h * HEAD_DIM:(h + 1) * HEAD_DIM]
        return [lax.dot_general(qh, k, (((1,), (1,)), ((), ())), preferred_element_type=F32)
                for k in kv_of(h, k_parts)]

    if phase_major:
        all_ss = [scores(h) for h in range(N_Q_HEADS)]
        all_m = [functools.reduce(jnp.maximum, [jnp.max(s, axis=-1, keepdims=True) for s in ss]) for ss in all_ss]
        all_es = [[jnp.exp(s - m) for s in ss] for ss, m in zip(all_ss, all_m)]
        all_den = [functools.reduce(jnp.add, [jnp.sum(e, axis=-1, keepdims=True) for e in es]) for es in all_es]
        outs = [functools.reduce(jnp.add, [_dot(e.astype(BF16), v) for e, v in zip(es, kv_of(h, v_parts))])
                * (1.0 / den) for h, (es, den) in enumerate(zip(all_es, all_den))]
        return jnp.concatenate(outs, axis=-1)
    outs = []
    ss_next = scores(0)
    for h in range(N_Q_HEADS):
        ss = ss_next
        if h + 1 < N_Q_HEADS:
            ss_next = scores(h + 1)
        m = functools.reduce(jnp.maximum, [jnp.max(s, axis=-1, keepdims=True) for s in ss])
        es = [jnp.exp(s - m) for s in ss]
        denom = functools.reduce(jnp.add, [jnp.sum(e, axis=-1, keepdims=True) for e in es])
        o = functools.reduce(jnp.add, [_dot(e.astype(BF16), v) for e, v in zip(es, kv_of(h, v_parts))])
        outs.append(o * (1.0 / denom))
    return jnp.concatenate(outs, axis=-1)


def _mixer_kernel(*refs, seq, latent):
    if latent:
        (mix_ref, ck_ref, cv_ref, dft_ref, bdc_ref, bds_ref, ws_ref, bsm_ref, clg_ref, clb_ref, o_ref) = refs
    else:
        (mix_ref, dft_ref, bdc_ref, bds_ref, ws_ref, bsm_ref, clg_ref, clb_ref, o_ref) = refs

    k_new = mix_ref[:, ATT_W:ATT_W + KV_W].astype(BF16)
    v_new = mix_ref[:, ATT_W + KV_W:ATT_W + 2 * KV_W].astype(BF16)
    if latent:
        k_parts = [ck_ref[...].astype(BF16), k_new]
        v_parts = [cv_ref[...].astype(BF16), v_new]
    else:
        k_parts, v_parts = [k_new], [v_new]

    def q_tile(t, carry):
        r0 = pl.multiple_of(t * TQ, TQ)
        q = (mix_ref[pl.ds(r0, TQ), :ATT_W] * (HEAD_DIM ** -0.5)).astype(BF16)
        o_ref[pl.ds(r0, TQ), :ATT_W] = _attention_tile(q, k_parts, v_parts, phase_major=not latent).astype(BF16)
        return carry

    lax.fori_loop(0, seq // TQ, q_tile, 0)

    f_hi, f_lo = _split_bf16(mix_ref[:, ATT_W + 2 * KV_W:ATT_W + 2 * KV_W + FOURIER_W])
    bdc = bdc_ref[...]
    bds = bds_ref[...]
    y = jnp.concatenate([_dot(f_hi, bdc) + _dot(f_lo, bdc), _dot(f_hi, bds) + _dot(f_lo, bds)], axis=0)
    y_hi, y_lo = _split_bf16(y)
    dft = dft_ref[...]
    four = _dot(dft, y_hi) + _dot(dft, y_lo)
    o_ref[:, ATT_W:ATT_W + FOURIER_W] = four.astype(BF16)

    c0 = ATT_W + 2 * KV_W + FOURIER_W
    u = _gelu_tanh(mix_ref[:, c0:c0 + CHUNK_W])
    vn = (_ln(_gelu_tanh(mix_ref[:, c0 + CHUNK_W:c0 + 2 * CHUNK_W])) * clg_ref[0] + clb_ref[0]).astype(BF16)
    lane = lax.broadcasted_iota(jnp.int32, (CHUNK, CHUNK_W), 1)
    bsm = bsm_ref[...]
    for c in range(seq // CHUNK):
        vc = vn[c * CHUNK:(c + 1) * CHUNK]
        sv = bsm
        for g in range(N_CHUNK_GROUPS):
            vg = jnp.where(lane // CHUNK_GROUP_W == g, vc, jnp.zeros_like(vc))
            sv = sv + _dot(ws_ref[g], vg)
        o_ref[c * CHUNK:(c + 1) * CHUNK, ATT_W + FOURIER_W:] = (u[c * CHUNK:(c + 1) * CHUNK] * sv).astype(BF16)


def _mixer_call(layer, mix, ctx_k, ctx_v, dft, bdc, bds, ws, bsm, clg, clb, *, latent):
    seq = DEC_SEQ if latent else SEQ
    nb = DEC_BATCH if latent else BATCH
    row0 = (T_CTX // seq) if latent else 0
    const2 = lambda b: (0, 0)
    per_layer = lambda b: (layer, 0, 0)
    in_specs = [pl.BlockSpec((seq, MIX_W), lambda b: (row0 + b, 0))]
    args = [mix]
    if latent:
        in_specs += [pl.BlockSpec((None, None, PAST_LEN, KV_W), lambda b: (b, layer, 0, 0))] * 2
        args += [ctx_k, ctx_v]
    in_specs += [
        pl.BlockSpec((seq, 2 * seq), const2),
        pl.BlockSpec((FOURIER_W, FOURIER_W), const2),
        pl.BlockSpec((FOURIER_W, FOURIER_W), const2),
        pl.BlockSpec((None, N_CHUNK_GROUPS, CHUNK, CHUNK), lambda b: (layer, 0, 0, 0)),
        pl.BlockSpec((None, CHUNK, CHUNK_W), per_layer),
        pl.BlockSpec((1, 1, CHUNK_W), per_layer),
        pl.BlockSpec((1, 1, CHUNK_W), per_layer),
    ]
    args += [dft, bdc, bds, ws, bsm, clg, clb]
    return pl.pallas_call(
        functools.partial(_mixer_kernel, seq=seq, latent=latent),
        out_shape=jax.ShapeDtypeStruct((nb * seq, D_MODEL), BF16),
        grid=(nb,),
        in_specs=in_specs,
        out_specs=pl.BlockSpec((seq, D_MODEL), lambda b: (b, 0)),
        compiler_params=_cparams("arbitrary"),
        name="mixer_latent" if latent else "mixer_context",
    )(*args)


def _route(logits, bias_t):
    lt = logits.T[:N_EXPERTS]
    ex = jnp.exp(lt - jnp.max(lt, axis=0, keepdims=True))
    probs = ex / jnp.sum(ex, axis=0, keepdims=True)
    sel = probs + bias_t
    p = [probs[e:e + 1] for e in range(N_EXPERTS)]
    s = [sel[e:e + 1] for e in range(N_EXPERTS)]
    n = EXPERTS_PER_GROUP
    scores = []
    for g in range(N_EXPERT_GROUPS):
        pair = [s[g * n + a] + s[g * n + b] for a in range(n) for b in range(a + 1, n)]
        scores.append(functools.reduce(jnp.maximum, pair))
    best = jnp.zeros_like(scores[0], dtype=jnp.int32)
    best_score = scores[0]
    for g in range(1, N_EXPERT_GROUPS):
        better = scores[g] > best_score
        best = jnp.where(better, g, best)
        best_score = jnp.where(better, scores[g], best_score)
    cls = jnp.zeros_like(best_score)
    w_a = jnp.zeros_like(best_score)
    w_b = jnp.zeros_like(best_score)
    for g in range(N_EXPERT_GROUPS):
        in_g = best == g
        chosen = []
        for a in range(n):
            rank = jnp.zeros_like(best)
            for b in range(n):
                if b == a:
                    continue
                ahead = (s[g * n + b] > s[g * n + a]) if b > a else (s[g * n + b] >= s[g * n + a])
                rank = rank + ahead.astype(jnp.int32)
            chosen.append(jnp.logical_and(in_g, rank < 2))
        for k in range(N_PAIRS):
            pa, pb = p[g * n + PAIR_A[k]], p[g * n + PAIR_B[k]]
            hit = jnp.logical_and(chosen[PAIR_A[k]], chosen[PAIR_B[k]])
            wsum = pa + pb
            cls = jnp.where(hit, float(g * N_PAIRS + k), cls)
            w_a = jnp.where(hit, pa / wsum, w_a)
            w_b = jnp.where(hit, pb / wsum, w_b)
    return jnp.concatenate([cls, w_a, w_b, jnp.zeros((ROUTE_ROWS - 3, lt.shape[1]), F32)], axis=0)


def _merge_kernel(xc_ref, xl_ref, mixc_ref, mixl_ref, mod_ref, win_ref, wpa_ref, wpf_ref, wpc_ref, wo_ref,
                  l1g_ref, l1b_ref, wrh_ref, wrl_ref, bias_ref, x1_ref, ha_ref, rt_ref):
    m = mod_ref[0, 0]
    in_ctx = pl.program_id(0) < N_CTX_TILES
    branch_w = ((0, ATT_W, wpa_ref), (ATT_W, FOURIER_W, wpf_ref), (ATT_W + FOURIER_W, CHUNK_W, wpc_ref))
    sub_rows = [slice(s * MERGE_SUB, (s + 1) * MERGE_SUB) for s in range(TM // MERGE_SUB)]
    residual = []
    for rows in sub_rows:
        x = jnp.where(in_ctx, xc_ref[rows, :], xl_ref[rows, :])
        mixed = jnp.where(in_ctx, mixc_ref[rows, :], mixl_ref[rows, :])
        h = (_ln(x) * (1.0 + m[1:2]) + m[0:1]).astype(BF16)
        merged = None
        for b, (c0, width, w_ref) in enumerate(branch_w):
            gate = _sigmoid(_dot(h, win_ref[:, MIX_W + b * D_MODEL:MIX_W + (b + 1) * D_MODEL]))
            term = gate * _dot(mixed[:, c0:c0 + width], w_ref[...])
            merged = term if merged is None else merged + term
        mix = _dot(merged.astype(BF16), wo_ref[...])
        residual.append(ALPHA * x + m[2:3] * mix)
    for rows, pre in zip(sub_rows, residual):
        x1 = _ln(pre) * l1g_ref[0] + l1b_ref[0]
        x1_ref[rows, :] = x1
        h2 = _ln(x1) * (1.0 + m[4:5]) + m[3:4]
        h_hi, h_lo = _split_bf16(h2)
        wrh = wrh_ref[...]
        logits = _dot(h_hi, wrh) + _dot(h_lo, wrh) + _dot(h_hi, wrl_ref[...])
        rt = _route(logits, bias_ref[:, rows])
        rt_ref[:, rows] = rt
        ha_ref[rows, :D_MODEL] = h2
        ha_ref[rows, D_MODEL:] = jnp.concatenate([rt, jnp.zeros((LANES - ROUTE_ROWS, MERGE_SUB), F32)], axis=0).T


def _merge_call(layer, x, mixed_ctx, mixed_lat, mod, w_in_b, wpa, wpf, wpc, wo, l1g, l1b, wrh, wrl, bias_t):
    const2 = lambda i: (0, 0)
    per_layer = lambda i: (layer, 0, 0)
    tok = lambda i: (i, 0)
    return pl.pallas_call(
        _merge_kernel,
        out_shape=(jax.ShapeDtypeStruct((T_ALL, D_MODEL), F32),
                   jax.ShapeDtypeStruct((T_ALL, HA_W), F32),
                   jax.ShapeDtypeStruct((ROUTE_ROWS, T_ALL), F32)),
        grid=(T_ALL // TM,),
        in_specs=_group_specs() + _group_specs() + [
            pl.BlockSpec((1, 1, N_MOD, D_MODEL), lambda i: (layer, _cond_row(i, TM), 0, 0)),
            pl.BlockSpec((None, D_MODEL, IN_W), per_layer, pipeline_mode=pl.Buffered(1)),
            pl.BlockSpec((None, ATT_W, D_MODEL), per_layer, pipeline_mode=pl.Buffered(1)),
            pl.BlockSpec((None, FOURIER_W, D_MODEL), per_layer, pipeline_mode=pl.Buffered(1)),
            pl.BlockSpec((None, CHUNK_W, D_MODEL), per_layer, pipeline_mode=pl.Buffered(1)),
            pl.BlockSpec((None, D_MODEL, D_MODEL), per_layer, pipeline_mode=pl.Buffered(1)),
            pl.BlockSpec((1, 1, D_MODEL), per_layer),
            pl.BlockSpec((1, 1, D_MODEL), per_layer),
            pl.BlockSpec((D_MODEL, ROUTER_PAD), const2),
            pl.BlockSpec((D_MODEL, ROUTER_PAD), const2),
            pl.BlockSpec((N_EXPERTS, TM), const2),
        ],
        out_specs=(pl.BlockSpec((TM, D_MODEL), tok),
                   pl.BlockSpec((TM, HA_W), tok),
                   pl.BlockSpec((ROUTE_ROWS, TM), lambda i: (0, i))),
        compiler_params=_cparams("arbitrary"),
        name="merge_route",
    )(*x, mixed_ctx, mixed_lat, mod, w_in_b, wpa, wpf, wpc, wo, l1g, l1b, wrh, wrl, bias_t)


def _sort_kernel(rt_ref, tri_ref, low_ref, pos_ref, tile_ref):
    crow = lax.broadcasted_iota(jnp.int32, (CLASS_PAD, SORT_BLK), 0)
    tri = tri_ref[...]
    n_blk = T_ALL // SORT_BLK
    carry = jnp.zeros((CLASS_PAD, 1), F32)
    ranks = []
    for b in range(n_blk):
        hot = rt_ref[0:1, b * SORT_BLK:(b + 1) * SORT_BLK].astype(jnp.int32) == crow
        hot_f = jnp.where(hot, 1.0, 0.0)
        before = _dot(hot_f.astype(BF16), tri) + carry
        ranks.append(jnp.sum(jnp.where(hot, before, 0.0), axis=0, keepdims=True))
        carry = carry + jnp.sum(hot_f, axis=1, keepdims=True)
    padded = jnp.floor((carry + (TM_MOE - 1.0)) * (1.0 / TM_MOE)) * TM_MOE
    padded = jnp.broadcast_to(padded, (CLASS_PAD, LANES))
    offs = _dot(low_ref[...], padded.astype(BF16))
    for b in range(n_blk):
        hot = rt_ref[0:1, b * SORT_BLK:(b + 1) * SORT_BLK].astype(jnp.int32) == crow
        base = jnp.sum(jnp.where(hot, offs[:, 0:1], 0.0), axis=0, keepdims=True)
        pos_ref[:, b * SORT_BLK:(b + 1) * SORT_BLK] = (base + ranks[b]).astype(jnp.int32)

    start = lax.broadcasted_iota(jnp.int32, (1, LANES), 1).astype(F32) * TM_MOE
    is_class = lax.broadcasted_iota(jnp.int32, (CLASS_PAD, LANES), 0) < N_CLASSES
    ends = jnp.where(is_class, offs + padded, 0.0)
    total = jnp.max(ends, axis=0, keepdims=True)
    valid = start < total
    tcls = jnp.sum(jnp.where(jnp.logical_and(is_class, ends <= start), 1.0, 0.0), axis=0, keepdims=True)
    last = jnp.max(jnp.where(valid, tcls, 0.0), axis=1, keepdims=True)
    tcls = jnp.where(valid, tcls, last)
    grp = functools.reduce(jnp.add, [jnp.where(tcls >= g * N_PAIRS, 1.0, 0.0) for g in range(1, N_EXPERT_GROUPS)])
    pair = tcls - grp * N_PAIRS
    slot_a = functools.reduce(jnp.add, [jnp.where(pair == k, float(PAIR_A[k]), 0.0) for k in range(N_PAIRS)])
    slot_b = functools.reduce(jnp.add, [jnp.where(pair == k, float(PAIR_B[k]), 0.0) for k in range(N_PAIRS)])
    crow_t = lax.broadcasted_iota(jnp.int32, (CLASS_PAD, LANES), 0).astype(F32)
    real_end = jnp.sum(jnp.where(crow_t == tcls, offs + carry, 0.0), axis=0, keepdims=True)
    n_rows = jnp.where(valid, jnp.clip(real_end - start, 0.0, float(TM_MOE)), 0.0)
    rows = [grp * EXPERTS_PER_GROUP + slot_a, grp * EXPERTS_PER_GROUP + slot_b, n_rows]
    tile_ref[...] = jnp.concatenate(rows + [jnp.zeros((8 - len(rows), LANES), F32)], axis=0).astype(jnp.int32)


def _sort_call(rt, tri, low):
    return pl.pallas_call(
        _sort_kernel,
        out_shape=(jax.ShapeDtypeStruct((1, T_ALL), jnp.int32),
                   jax.ShapeDtypeStruct((8, LANES), jnp.int32)),
        name="route_sort",
    )(rt, tri, low)


def _moe_kernel(tile_ref, pos_ref, ha_hbm, wg_hbm, wu_hbm, wd_hbm, out_ref, hbuf, ybuf, src_ref, ssem,
                stg_g, stg_u, stg_d, act_g, act_u, act_d, wsem, ha_ref, hsem, *, layer):
    j = pl.program_id(0)
    slot = j % 2

    def n_rows(tile):
        inside = jnp.logical_and(tile >= 0, tile < N_TILES)
        return jnp.where(inside, tile_ref[TILE_ROWS, jnp.clip(tile, 0, N_TILES - 1)], 0)

    def token_of(tile, n, r, tail):
        return src_ref[tile * TM_MOE + (jnp.minimum(r, n - 1) if tail else r)]

    def scatter_copy(tile, n, buf_slot, r, tail):
        tok = token_of(tile, n, r, tail)
        dst = jnp.where(r < n, tok, T_ALL + buf_slot * TM_MOE + r) if tail else tok
        return pltpu.make_async_copy(ybuf.at[buf_slot, r], out_ref.at[dst], ssem.at[buf_slot])

    def for_groups(n, fn):
        full = lax.shift_right_logical(n, 3)

        def body(g, c):
            for k in range(COPY_GROUP):
                fn(g, k, False)
            return c
        lax.fori_loop(0, full, body, 0)

        @pl.when(jnp.bitwise_and(n, COPY_GROUP - 1) != 0)
        def _():
            for k in range(COPY_GROUP):
                fn(full, k, True)

    def retire(tile, buf_slot):
        n = n_rows(tile)
        one = scatter_copy(tile, n, buf_slot, 0, False)

        def body(g, c):
            for _ in range(COPY_GROUP):
                one.wait()
            return c
        lax.fori_loop(0, lax.shift_right_logical(n + COPY_GROUP - 1, 3), body, 0)

    def expert_of(x, tile):
        return tile_ref[TILE_SLOT_A + x, tile]

    def weight_copies(x, e):
        return [pltpu.make_async_copy(src.at[layer, e], stg.at[x], wsem.at[x])
                for src, stg in ((wg_hbm, stg_g), (wu_hbm, stg_u), (wd_hbm, stg_d))]

    @pl.when(j == 0)
    def _():
        load_inputs = pltpu.make_async_copy(ha_hbm, ha_ref, hsem)
        load_inputs.start()
        for x in range(2):
            for c in weight_copies(x, expert_of(x, 0)):
                c.start()

        def place(t, c):
            src_ref[pos_ref[t]] = t
            return c
        lax.fori_loop(0, T_ALL, place, 0, unroll=8)
        load_inputs.wait()
        hbuf[...] = jnp.zeros(hbuf.shape, F32)
        ybuf[...] = jnp.zeros(ybuf.shape, F32)
        for s in range(2):
            park = pltpu.make_async_copy(ybuf.at[s], out_ref.at[pl.ds(T_ALL + s * TM_MOE, TM_MOE)], ssem.at[s])
            park.start()
            park.wait()

    @pl.when(n_rows(j - 2) > 0)
    def _():
        retire(j - 2, slot)

    def run_experts(rows):
        rec = hbuf[:rows // SUBLANES].reshape(rows, HA_W)
        h = rec[:, :D_MODEL].astype(BF16)

        def expert(x):
            a = _dot(h, act_g[x])
            u = _dot(h, act_u[x])
            w = rec[:, D_MODEL + 1 + x:D_MODEL + 2 + x]
            return _dot(((a * _sigmoid(a)) * u * w).astype(BF16), act_d[x])

        ybuf[slot, :rows] = (expert(0) + expert(1)).reshape(rows, D_CHUNKS, LANES)

    @pl.when(n_rows(j) > 0)
    def _():
        for x in range(2):
            prev = jnp.maximum(j - 1, 0)

            @pl.when(jnp.logical_or(j == 0, expert_of(x, j) != expert_of(x, prev)))
            def _():
                for c in weight_copies(x, expert_of(x, j)):
                    c.wait()
                act_g[x] = stg_g[x].astype(BF16)
                act_u[x] = stg_u[x].astype(BF16)
                act_d[x] = stg_d[x].astype(BF16)

        for x in range(2):
            nxt = jnp.minimum(j + 1, N_TILES - 1)

            @pl.when(jnp.logical_and(n_rows(j + 1) > 0, expert_of(x, nxt) != expert_of(x, j)))
            def _():
                for c in weight_copies(x, expert_of(x, nxt)):
                    c.start()

        n = n_rows(j)

        def gather_row(g, k, tail):
            tok = token_of(j, n, g * COPY_GROUP + k, tail)
            grp, sub = lax.shift_right_logical(tok, 3), jnp.bitwise_and(tok, SUBLANES - 1)
            hbuf[g, pl.ds(k, 1), :] = ha_ref[grp, pl.ds(sub, 1), :]
        for_groups(n, gather_row)

        @pl.when(n > SMALL_TILE)
        def _():
            run_experts(TM_MOE)

        @pl.when(n <= SMALL_TILE)
        def _():
            run_experts(SMALL_TILE)

        for_groups(n, lambda g, k, tail: scatter_copy(j, n, slot, g * COPY_GROUP + k, tail).start(priority=k % 2))

    @pl.when(j == N_TILES - 1)
    def _():
        @pl.when(n_rows(j - 1) > 0)
        def _():
            retire(j - 1, 1 - slot)

        @pl.when(n_rows(j) > 0)
        def _():
            retire(j, slot)


def _moe_call(layer, tiles, pos, ha, wge, wue, wde):
    up = (2, D_MODEL, D_EXPERT)
    down = (2, D_EXPERT, D_MODEL)
    grid_spec = pltpu.PrefetchScalarGridSpec(
        num_scalar_prefetch=2,
        grid=(N_TILES,),
        in_specs=[pl.BlockSpec(memory_space=pl.ANY)] * 4,
        out_specs=pl.BlockSpec(memory_space=pl.ANY),
        scratch_shapes=[
            pltpu.VMEM((TM_MOE // SUBLANES, SUBLANES, HA_W), F32),
            pltpu.VMEM((2, TM_MOE, D_CHUNKS, LANES), F32),
            pltpu.SMEM((R_ROWS,), jnp.int32),
            pltpu.SemaphoreType.DMA((2,)),
            pltpu.VMEM(up, F32), pltpu.VMEM(up, F32), pltpu.VMEM(down, F32),
            pltpu.VMEM(up, BF16), pltpu.VMEM(up, BF16), pltpu.VMEM(down, BF16),
            pltpu.SemaphoreType.DMA((2,)),
            pltpu.VMEM((T_ALL // SUBLANES, SUBLANES, HA_W), F32),
            pltpu.SemaphoreType.DMA(()),
        ],
    )
    return pl.pallas_call(
        functools.partial(_moe_kernel, layer=layer),
        out_shape=jax.ShapeDtypeStruct((T_ALL + N_PARK, D_CHUNKS, LANES), F32),
        grid_spec=grid_spec,
        compiler_params=pltpu.CompilerParams(dimension_semantics=("arbitrary",), vmem_limit_bytes=V7X_VMEM_LIMIT,
                                             has_side_effects=True),
        name="moe_pairs",
    )(tiles, pos,
      ha.reshape(T_ALL // SUBLANES, SUBLANES, HA_W), wge, wue, wde)


def _post_kernel(x1_ref, moe_ref, mod_ref, l2g_ref, l2b_ref, oc_ref, ol_ref):
    m = mod_ref[0, 0]
    moe = moe_ref[...].reshape(TM, D_MODEL)
    y = _ln(ALPHA * x1_ref[...] + m[5:6] * moe) * l2g_ref[0] + l2b_ref[0]
    i = pl.program_id(0)

    @pl.when(i < N_CTX_TILES)
    def _():
        oc_ref[...] = y

    @pl.when(i >= N_CTX_TILES)
    def _():
        ol_ref[...] = y


def _post_call(layer, x1, moe, mod, l2g, l2b):
    tok = lambda i: (i, 0)
    per_layer = lambda i: (layer, 0, 0)
    return pl.pallas_call(
        _post_kernel,
        out_shape=(jax.ShapeDtypeStruct((T_CTX, D_MODEL), F32), jax.ShapeDtypeStruct((T_LAT, D_MODEL), F32)),
        grid=(T_ALL // TM,),
        in_specs=[
            pl.BlockSpec((TM, D_MODEL), tok),
            pl.BlockSpec((TM, D_CHUNKS, LANES), lambda i: (i, 0, 0)),
            pl.BlockSpec((1, 1, N_MOD, D_MODEL), lambda i: (layer, _cond_row(i, TM), 0, 0)),
            pl.BlockSpec((1, 1, D_MODEL), per_layer),
            pl.BlockSpec((1, 1, D_MODEL), per_layer),
        ],
        out_specs=tuple(_group_specs()),
        compiler_params=_cparams("arbitrary"),
        name="post_moe",
    )(x1, moe, mod, l2g, l2b)


def _rope_tables():
    pos = np.arange(DEC_SEQ)
    quarter = HEAD_DIM // 4
    inv = ROPE_THETA ** (-np.arange(quarter, dtype=np.float64) / quarter)
    ang_r = (pos // GRID_W)[:, None] * inv[None, :]
    ang_c = (pos % GRID_W)[:, None] * inv[None, :]
    cos = np.concatenate([np.cos(ang_r)] * 2 + [np.cos(ang_c)] * 2, axis=-1)
    sin = np.concatenate([-np.sin(ang_r), np.sin(ang_r), -np.sin(ang_c), np.sin(ang_c)], axis=-1)
    return (jnp.asarray(np.tile(cos, (1, N_Q_HEADS)).astype(np.float32)),
            jnp.asarray(np.tile(sin, (1, N_Q_HEADS)).astype(np.float32)))


def _dft_mats(n, scale):
    j = np.arange(n)
    ang = ((j[:, None] * j[None, :]) % n) * (2 * np.pi / n)
    return np.cos(ang) * scale, np.sin(ang) * scale


def _block_diag(m, reps):
    return np.kron(np.eye(reps), m)


def _const_bf16(a):
    return jnp.asarray(np.asarray(a, np.float32)).astype(BF16)


def _const_01(a):
    return jnp.asarray(np.asarray(a, np.float32).astype(BF16))


def kernel(x_prompt, x_sample, cache_k, cache_v, c, c_ctx, w_in, q_norm_g, k_norm_g, w_proj_att,
           w_proj_fourier, w_proj_chunk, w_out, chunk_ln_g, chunk_ln_b, chunk_ws, chunk_bs, w_ada, b_ada,
           ln1_g, ln1_b, ln2_g, ln2_b, w_router, router_bias, w_gate_e, w_up_e, w_down_e):
    x = (x_prompt.reshape(T_CTX, D_MODEL), x_sample.reshape(T_LAT, D_MODEL))
    cond = jnp.concatenate([c_ctx[None, :], c, jnp.zeros((N_COND - 1 - DEC_BATCH, D_MODEL), F32)], axis=0)
    mod = _mod_call(cond, w_ada, b_ada).reshape(DEPTH, N_COND, N_MOD, D_MODEL)

    cos_t, sin_t = _rope_tables()
    bd_heads = _const_01(_block_diag(np.ones((HEAD_DIM, HEAD_DIM)), N_Q_HEADS))
    c64, s64 = _dft_mats(FOURIER_GROUP_W, 1.0)
    bdc = _const_bf16(_block_diag(c64, FOURIER_W // FOURIER_GROUP_W))
    bds = _const_bf16(_block_diag(s64, FOURIER_W // FOURIER_GROUP_W))
    dft = {}
    for seq in (SEQ, DEC_SEQ):
        cs, ss = _dft_mats(seq, 1.0 / math.sqrt(seq * FOURIER_GROUP_W))
        dft[seq] = _const_bf16(np.concatenate([cs, -ss], axis=1))
    ctx_k = cache_k.reshape(DEC_BATCH, DEPTH, PAST_LEN, KV_W)
    ctx_v = cache_v.reshape(DEC_BATCH, DEPTH, PAST_LEN, KV_W)
    wr = jnp.pad(w_router, ((0, 0), (0, ROUTER_PAD - N_EXPERTS)))
    wr_hi = wr.astype(BF16)
    wr_lo = (wr - wr_hi.astype(F32)).astype(BF16)
    bias_t = jnp.broadcast_to(router_bias[:, None], (N_EXPERTS, TM))
    tri = _const_01(np.triu(np.ones((SORT_BLK, SORT_BLK)), 1))
    low = _const_01(np.tril(np.ones((CLASS_PAD, CLASS_PAD)), -1))

    w_in_b = w_in.astype(BF16)
    wpa, wpf, wpc, wo = (w.astype(BF16) for w in (w_proj_att, w_proj_fourier, w_proj_chunk, w_out))
    wge, wue, wde = w_gate_e, w_up_e, w_down_e
    qg = jnp.tile(q_norm_g, (1, N_Q_HEADS))[:, None, :]
    kg = jnp.tile(k_norm_g, (1, N_KV_HEADS))[:, None, :]
    ws = chunk_ws.astype(BF16)
    bsm = jnp.repeat(jnp.swapaxes(chunk_bs, 1, 2), CHUNK_GROUP_W, axis=2)
    row3 = lambda a: a[:, None, :]

    new_k = jnp.zeros((BATCH, DEPTH, SEQ, KV_W), F32)
    new_v = jnp.zeros((BATCH, DEPTH, SEQ, KV_W), F32)
    for l in range(DEPTH):
        mix, new_k, new_v = _proj_call(l, x, mod, w_in_b, qg, kg, cos_t, sin_t, bd_heads, new_k, new_v)
        mixer_args = (bdc, bds, ws, bsm, row3(chunk_ln_g), row3(chunk_ln_b))
        mixed_ctx = _mixer_call(l, mix, None, None, dft[SEQ], *mixer_args, latent=False)
        mixed_lat = _mixer_call(l, mix, ctx_k, ctx_v, dft[DEC_SEQ], *mixer_args, latent=True)
        x1, ha, rt = _merge_call(l, x, mixed_ctx, mixed_lat, mod, w_in_b, wpa, wpf, wpc, wo,
                                 row3(ln1_g), row3(ln1_b), wr_hi, wr_lo, bias_t)
        pos2d, tiles = _sort_call(rt, tri, low)
        moe = _moe_call(l, tiles, pos2d[0], ha, wge, wue, wde)
        x = _post_call(l, x1, moe, mod, row3(ln2_g), row3(ln2_b))

    y_prompt = x[0].reshape(BATCH, SEQ, D_MODEL)
    y_sample = x[1].reshape(DEC_BATCH, DEC_SEQ, D_MODEL)
    cache_shape = (BATCH, DEPTH, SEQ, N_KV_HEADS, HEAD_DIM)
    return (y_prompt, y_sample, new_k.reshape(cache_shape), new_v.reshape(cache_shape))
```

```python
import functools
import math

import jax
import jax.numpy as jnp
import numpy as np
from jax import lax
from jax.experimental import pallas as pl
from jax.experimental.pallas import tpu as pltpu

F32 = jnp.float32
BF16 = jnp.bfloat16

D_MODEL = 1024
BATCH = 16
SEQ = 256
DEPTH = 2
DEC_BATCH = 2
DEC_SEQ = 1024
PAST_LEN = 256
GRID_W = 64
N_Q_HEADS = 8
N_KV_HEADS = 2
HEAD_DIM = 64
Q_PER_KV = N_Q_HEADS // N_KV_HEADS
ATT_W = N_Q_HEADS * HEAD_DIM
KV_W = N_KV_HEADS * HEAD_DIM
ROPE_THETA = 10000.0
FOURIER_GROUP_W = 64
FOURIER_W = 256
N_CHUNK_GROUPS = 4
CHUNK_GROUP_W = 64
CHUNK_W = 256
CHUNK = 128
N_EXPERTS = 16
N_EXPERT_GROUPS = 4
EXPERTS_PER_GROUP = 4
D_EXPERT = 512
ALPHA = (2 * DEPTH) ** 0.25
LN_EPS = 1e-6
RMS_EPS = 1e-6

LANES = 128
SUBLANES = 8
T_CTX = BATCH * SEQ
T_LAT = DEC_BATCH * DEC_SEQ
T_ALL = T_CTX + T_LAT
IN_W = ATT_W + 2 * KV_W + FOURIER_W + 2 * CHUNK_W + 3 * D_MODEL
MIX_W = ATT_W + 2 * KV_W + FOURIER_W + 2 * CHUNK_W
N_MOD = 6
N_COND = 8
ROUTER_PAD = LANES
TM = 512
TQ = 256
MERGE_SUB = 256
PROJ_SUB = 256
V7X_VMEM_LIMIT = 56 * 1024 * 1024

PAIR_A = (0, 0, 0, 1, 1, 3)
PAIR_B = (1, 2, 3, 3, 2, 2)
N_PAIRS = len(PAIR_A)
N_CLASSES = N_EXPERT_GROUPS * N_PAIRS
CLASS_PAD = 32
ROUTE_ROWS = 8
TM_MOE = 256
N_TILES = T_ALL // TM_MOE + N_CLASSES
R_ROWS = N_TILES * TM_MOE
SORT_BLK = 512
D_CHUNKS = D_MODEL // LANES
HA_W = D_MODEL + LANES
TILE_SLOT_A, TILE_SLOT_B, TILE_ROWS = 0, 1, 2
SMALL_TILE = 64
N_PARK = 2 * TM_MOE
COPY_GROUP = 8


def _cparams(*sem):
    return pltpu.CompilerParams(dimension_semantics=sem, vmem_limit_bytes=V7X_VMEM_LIMIT)


def _cond_row(i, tm):
    n_ctx = T_CTX // tm
    return jnp.where(i < n_ctx, 0, 1 + (i - n_ctx) // (DEC_SEQ // tm))


N_CTX_TILES = T_CTX // TM


def _group_specs():
    return [pl.BlockSpec((TM, D_MODEL), lambda i: (jnp.minimum(i, N_CTX_TILES - 1), 0)),
            pl.BlockSpec((TM, D_MODEL), lambda i: (jnp.maximum(i - N_CTX_TILES, 0), 0))]


def _group_tile(ctx_ref, lat_ref):
    return jnp.where(pl.program_id(0) < N_CTX_TILES, ctx_ref[...], lat_ref[...])


def _ln(x):
    mu = jnp.mean(x, axis=-1, keepdims=True)
    xc = x - mu
    var = jnp.mean(xc * xc, axis=-1, keepdims=True)
    return xc * lax.rsqrt(var + LN_EPS)


def _split_bf16(x):
    hi = x.astype(BF16)
    lo = (x - hi.astype(F32)).astype(BF16)
    return hi, lo


def _dot(a, b):
    return jnp.dot(a, b, preferred_element_type=F32)


def _sigmoid(x):
    return 1.0 / (1.0 + jnp.exp(-x))


def _gelu_tanh(x):
    c = np.float32(np.sqrt(2 / np.pi))
    return x * (0.5 * (1.0 + jnp.tanh(c * (x + 0.044715 * (x * x * x)))))


def _mod_kernel(c_ref, w_ref, b_ref, o_ref):
    c = c_ref[...]
    s = c * _sigmoid(c)
    o_ref[0] = _dot(s.astype(BF16), w_ref[0].astype(BF16)) + b_ref[0]


def _mod_call(cond, w_ada, b_ada):
    tn = 1536
    n = N_MOD * D_MODEL
    return pl.pallas_call(
        _mod_kernel,
        out_shape=jax.ShapeDtypeStruct((DEPTH, N_COND, n), F32),
        grid=(DEPTH, n // tn),
        in_specs=[
            pl.BlockSpec((N_COND, D_MODEL), lambda l, j: (0, 0)),
            pl.BlockSpec((1, D_MODEL, tn), lambda l, j: (l, 0, j)),
            pl.BlockSpec((1, 1, tn), lambda l, j: (l, 0, j)),
        ],
        out_specs=pl.BlockSpec((1, N_COND, tn), lambda l, j: (l, 0, j)),
        compiler_params=_cparams("arbitrary", "arbitrary"),
        name="adaln_mod",
    )(cond, w_ada, b_ada.reshape(DEPTH, 1, n))


def _head_rms(q, bd, g):
    hi, lo = _split_bf16(q * q)
    ssum = _dot(hi, bd) + _dot(lo, bd)
    return q * lax.rsqrt(ssum * (1.0 / HEAD_DIM) + RMS_EPS) * g


def _rope(x, cos, sin_signed):
    w = x.shape[-1]
    lane = lax.broadcasted_iota(jnp.int32, x.shape, 1)
    swapped = jnp.where((lane % 32) < 16, pltpu.roll(x, w - 16, 1), pltpu.roll(x, 16, 1))
    return x * cos + swapped * sin_signed


def _proj_kernel(xc_ref, xl_ref, mod_ref, w_ref, qg_ref, kg_ref, cos_ref, sin_ref, bd_ref, kc_in, vc_in,
                 o_ref, kc_ref, vc_ref):
    del kc_in, vc_in
    m = mod_ref[0, 0]
    in_ctx = pl.program_id(0) < N_CTX_TILES
    bd = bd_ref[...]
    sub_rows = [slice(s * PROJ_SUB, (s + 1) * PROJ_SUB) for s in range(TM // PROJ_SUB)]
    projs = []
    for rows in sub_rows:
        x = jnp.where(in_ctx, xc_ref[rows, :], xl_ref[rows, :])
        h = _ln(x) * (1.0 + m[1:2]) + m[0:1]
        proj = _dot(h.astype(BF16), w_ref[...])
        o_ref[rows, ATT_W + KV_W:] = proj[:, ATT_W + KV_W:]
        projs.append(proj[:, :ATT_W + 2 * KV_W])
    keys = []
    for rows, proj in zip(sub_rows, projs):
        qn = _head_rms(proj[:, :ATT_W], bd, qg_ref[0])
        kn = _head_rms(proj[:, ATT_W:ATT_W + KV_W], bd[:KV_W, :KV_W], kg_ref[0])
        cos = cos_ref[rows, :]
        sin = sin_ref[rows, :]
        o_ref[rows, :ATT_W] = jnp.where(in_ctx, qn, _rope(qn, cos, sin))
        o_ref[rows, ATT_W:ATT_W + KV_W] = jnp.where(in_ctx, kn, _rope(kn, cos[:, :KV_W], sin[:, :KV_W]))
        keys.append(kn)

    @pl.when(in_ctx)
    def _():
        for s in range(len(sub_rows)):
            kc_ref[s] = keys[s]
            vc_ref[s] = projs[s][:, ATT_W + KV_W:]


def _proj_call(layer, x, mod, w_in_b, qg, kg, cos_t, sin_t, bd, k_cache, v_cache):
    assert PROJ_SUB == SEQ
    n_ctx = T_CTX // TM
    per_seq = DEC_SEQ // TM

    def rope_idx(i):
        return (jnp.where(i < n_ctx, 0, (i - n_ctx) % per_seq), 0)

    cache_spec = pl.BlockSpec((TM // SEQ, None, SEQ, KV_W), lambda i: (jnp.minimum(i, n_ctx - 1), layer, 0, 0))
    cache_shape = jax.ShapeDtypeStruct((BATCH, DEPTH, SEQ, KV_W), F32)
    n_in = 11
    return pl.pallas_call(
        _proj_kernel,
        out_shape=(jax.ShapeDtypeStruct((T_ALL, MIX_W), F32), cache_shape, cache_shape),
        grid=(T_ALL // TM,),
        in_specs=_group_specs() + [
            pl.BlockSpec((1, 1, N_MOD, D_MODEL), lambda i: (layer, _cond_row(i, TM), 0, 0)),
            pl.BlockSpec((None, D_MODEL, MIX_W), lambda i: (layer, 0, 0)),
            pl.BlockSpec((1, 1, ATT_W), lambda i: (layer, 0, 0)),
            pl.BlockSpec((1, 1, KV_W), lambda i: (layer, 0, 0)),
            pl.BlockSpec((TM, ATT_W), rope_idx),
            pl.BlockSpec((TM, ATT_W), rope_idx),
            pl.BlockSpec((ATT_W, ATT_W), lambda i: (0, 0)),
            pl.BlockSpec(memory_space=pl.ANY),
            pl.BlockSpec(memory_space=pl.ANY),
        ],
        out_specs=(pl.BlockSpec((TM, MIX_W), lambda i: (i, 0)), cache_spec, cache_spec),
        input_output_aliases={n_in - 2: 1, n_in - 1: 2},
        compiler_params=_cparams("arbitrary"),
        name="in_proj",
    )(*x, mod, w_in_b, qg, kg, cos_t, sin_t, bd, k_cache, v_cache)


def _attention_tile(q, k_parts, v_parts, phase_major):
    def kv_of(h, parts):
        g = h // Q_PER_KV
        return [p[:, g * HEAD_DIM:(g + 1) * HEAD_DIM] for p in parts]

    def scores(h):
        qh = q[:, h * HEAD_DIM:(h + 1) * HEAD_DIM]
        return [lax.dot_general(qh, k, (((1,), (1,)), ((), ())), preferred_element_type=F32)
                for k in kv_of(h, k_parts)]

    if phase_major:
        all_ss = [scores(h) for h in range(N_Q_HEADS)]
        all_m = [functools.reduce(jnp.maximum, [jnp.max(s, axis=-1, keepdims=True) for s in ss]) for ss in all_ss]
        all_es = [[jnp.exp(s - m) for s in ss] for ss, m in zip(all_ss, all_m)]
        all_den = [functools.reduce(jnp.add, [jnp.sum(e, axis=-1, keepdims=True) for e in es]) for es in all_es]
        outs = [functools.reduce(jnp.add, [_dot(e.astype(BF16), v) for e, v in zip(es, kv_of(h, v_parts))])
                * (1.0 / den) for h, (es, den) in enumerate(zip(all_es, all_den))]
        return jnp.concatenate(outs, axis=-1)
    outs = []
    ss_next = scores(0)
    for h in range(N_Q_HEADS):
        ss = ss_next
        if h + 1 < N_Q_HEADS:
            ss_next = scores(h + 1)
        m = functools.reduce(jnp.maximum, [jnp.max(s, axis=-1, keepdims=True) for s in ss])
        es = [jnp.exp(s - m) for s in ss]
        denom = functools.reduce(jnp.add, [jnp.sum(e, axis=-1, keepdims=True) for e in es])
        o = functools.reduce(jnp.add, [_dot(e.astype(BF16), v) for e, v in zip(es, kv_of(h, v_parts))])
        outs.append(o * (1.0 / denom))
    return jnp.concatenate(outs, axis=-1)


def _mixer_kernel(*refs, seq, latent):
    if latent:
        (mix_ref, ck_ref, cv_ref, dft_ref, bdc_ref, bds_ref, ws_ref, bsm_ref, clg_ref, clb_ref, o_ref) = refs
    else:
        (mix_ref, dft_ref, bdc_ref, bds_ref, ws_ref, bsm_ref, clg_ref, clb_ref, o_ref) = refs

    k_new = mix_ref[:, ATT_W:ATT_W + KV_W].astype(BF16)
    v_new = mix_ref[:, ATT_W + KV_W:ATT_W + 2 * KV_W].astype(BF16)
    if latent:
        k_parts = [ck_ref[...].astype(BF16), k_new]
        v_parts = [cv_ref[...].astype(BF16), v_new]
    else:
        k_parts, v_parts = [k_new], [v_new]

    def q_tile(t, carry):
        r0 = pl.multiple_of(t * TQ, TQ)
        q = (mix_ref[pl.ds(r0, TQ), :ATT_W] * (HEAD_DIM ** -0.5)).astype(BF16)
        o_ref[pl.ds(r0, TQ), :ATT_W] = _attention_tile(q, k_parts, v_parts, phase_major=not latent).astype(BF16)
        return carry

    lax.fori_loop(0, seq // TQ, q_tile, 0)

    f_hi, f_lo = _split_bf16(mix_ref[:, ATT_W + 2 * KV_W:ATT_W + 2 * KV_W + FOURIER_W])
    bdc = bdc_ref[...]
    bds = bds_ref[...]
    y = jnp.concatenate([_dot(f_hi, bdc) + _dot(f_lo, bdc), _dot(f_hi, bds) + _dot(f_lo, bds)], axis=0)
    y_hi, y_lo = _split_bf16(y)
    dft = dft_ref[...]
    four = _dot(dft, y_hi) + _dot(dft, y_lo)
    o_ref[:, ATT_W:ATT_W + FOURIER_W] = four.astype(BF16)

    c0 = ATT_W + 2 * KV_W + FOURIER_W
    u = _gelu_tanh(mix_ref[:, c0:c0 + CHUNK_W])
    vn = (_ln(_gelu_tanh(mix_ref[:, c0 + CHUNK_W:c0 + 2 * CHUNK_W])) * clg_ref[0] + clb_ref[0]).astype(BF16)
    lane = lax.broadcasted_iota(jnp.int32, (CHUNK, CHUNK_W), 1)
    bsm = bsm_ref[...]
    for c in range(seq // CHUNK):
        vc = vn[c * CHUNK:(c + 1) * CHUNK]
        sv = bsm
        for g in range(N_CHUNK_GROUPS):
            vg = jnp.where(lane // CHUNK_GROUP_W == g, vc, jnp.zeros_like(vc))
            sv = sv + _dot(ws_ref[g], vg)
        o_ref[c * CHUNK:(c + 1) * CHUNK, ATT_W + FOURIER_W:] = (u[c * CHUNK:(c + 1) * CHUNK] * sv).astype(BF16)


def _mixer_call(layer, mix, ctx_k, ctx_v, dft, bdc, bds, ws, bsm, clg, clb, *, latent):
    seq = DEC_SEQ if latent else SEQ
    nb = DEC_BATCH if latent else BATCH
    row0 = (T_CTX // seq) if latent else 0
    const2 = lambda b: (0, 0)
    per_layer = lambda b: (layer, 0, 0)
    in_specs = [pl.BlockSpec((seq, MIX_W), lambda b: (row0 + b, 0))]
    args = [mix]
    if latent:
        in_specs += [pl.BlockSpec((None, None, PAST_LEN, KV_W), lambda b: (b, layer, 0, 0))] * 2
        args += [ctx_k, ctx_v]
    in_specs += [
        pl.BlockSpec((seq, 2 * seq), const2),
        pl.BlockSpec((FOURIER_W, FOURIER_W), const2),
        pl.BlockSpec((FOURIER_W, FOURIER_W), const2),
        pl.BlockSpec((None, N_CHUNK_GROUPS, CHUNK, CHUNK), lambda b: (layer, 0, 0, 0)),
        pl.BlockSpec((None, CHUNK, CHUNK_W), per_layer),
        pl.BlockSpec((1, 1, CHUNK_W), per_layer),
        pl.BlockSpec((1, 1, CHUNK_W), per_layer),
    ]
    args += [dft, bdc, bds, ws, bsm, clg, clb]
    return pl.pallas_call(
        functools.partial(_mixer_kernel, seq=seq, latent=latent),
        out_shape=jax.ShapeDtypeStruct((nb * seq, D_MODEL), BF16),
        grid=(nb,),
        in_specs=in_specs,
        out_specs=pl.BlockSpec((seq, D_MODEL), lambda b: (b, 0)),
        compiler_params=_cparams("arbitrary"),
        name="mixer_latent" if latent else "mixer_context",
    )(*args)


def _route(logits, bias_t):
    lt = logits.T[:N_EXPERTS]
    ex = jnp.exp(lt - jnp.max(lt, axis=0, keepdims=True))
    probs = ex / jnp.sum(ex, axis=0, keepdims=True)
    sel = probs + bias_t
    p = [probs[e:e + 1] for e in range(N_EXPERTS)]
    s = [sel[e:e + 1] for e in range(N_EXPERTS)]
    n = EXPERTS_PER_GROUP
    scores = []
    for g in range(N_EXPERT_GROUPS):
        pair = [s[g * n + a] + s[g * n + b] for a in range(n) for b in range(a + 1, n)]
        scores.append(functools.reduce(jnp.maximum, pair))
    best = jnp.zeros_like(scores[0], dtype=jnp.int32)
    best_score = scores[0]
    for g in range(1, N_EXPERT_GROUPS):
        better = scores[g] > best_score
        best = jnp.where(better, g, best)
        best_score = jnp.where(better, scores[g], best_score)
    cls = jnp.zeros_like(best_score)
    w_a = jnp.zeros_like(best_score)
    w_b = jnp.zeros_like(best_score)
    for g in range(N_EXPERT_GROUPS):
        in_g = best == g
        chosen = []
        for a in range(n):
            rank = jnp.zeros_like(best)
            for b in range(n):
                if b == a:
                    continue
                ahead = (s[g * n + b] > s[g * n + a]) if b > a else (s[g * n + b] >= s[g * n + a])
                rank = rank + ahead.astype(jnp.int32)
            chosen.append(jnp.logical_and(in_g, rank < 2))
        for k in range(N_PAIRS):
            pa, pb = p[g * n + PAIR_A[k]], p[g * n + PAIR_B[k]]
            hit = jnp.logical_and(chosen[PAIR_A[k]], chosen[PAIR_B[k]])
            wsum = pa + pb
            cls = jnp.where(hit, float(g * N_PAIRS + k), cls)
            w_a = jnp.where(hit, pa / wsum, w_a)
            w_b = jnp.where(hit, pb / wsum, w_b)
    return jnp.concatenate([cls, w_a, w_b, jnp.zeros((ROUTE_ROWS - 3, lt.shape[1]), F32)], axis=0)


def _merge_kernel(xc_ref, xl_ref, mixc_ref, mixl_ref, mod_ref, win_ref, wpa_ref, wpf_ref, wpc_ref, wo_ref,
                  l1g_ref, l1b_ref, wrh_ref, wrl_ref, bias_ref, x1_ref, ha_ref, rt_ref):
    m = mod_ref[0, 0]
    in_ctx = pl.program_id(0) < N_CTX_TILES
    branch_w = ((0, ATT_W, wpa_ref), (ATT_W, FOURIER_W, wpf_ref), (ATT_W + FOURIER_W, CHUNK_W, wpc_ref))
    sub_rows = [slice(s * MERGE_SUB, (s + 1) * MERGE_SUB) for s in range(TM // MERGE_SUB)]
    residual = []
    for rows in sub_rows:
        x = jnp.where(in_ctx, xc_ref[rows, :], xl_ref[rows, :])
        mixed = jnp.where(in_ctx, mixc_ref[rows, :], mixl_ref[rows, :])
        h = (_ln(x) * (1.0 + m[1:2]) + m[0:1]).astype(BF16)
        merged = None
        for b, (c0, width, w_ref) in enumerate(branch_w):
            gate = _sigmoid(_dot(h, win_ref[:, MIX_W + b * D_MODEL:MIX_W + (b + 1) * D_MODEL]))
            term = gate * _dot(mixed[:, c0:c0 + width], w_ref[...])
            merged = term if merged is None else merged + term
        mix = _dot(merged.astype(BF16), wo_ref[...])
        residual.append(ALPHA * x + m[2:3] * mix)
    for rows, pre in zip(sub_rows, residual):
        x1 = _ln(pre) * l1g_ref[0] + l1b_ref[0]
        x1_ref[rows, :] = x1
        h2 = _ln(x1) * (1.0 + m[4:5]) + m[3:4]
        h_hi, h_lo = _split_bf16(h2)
        wrh = wrh_ref[...]
        logits = _dot(h_hi, wrh) + _dot(h_lo, wrh) + _dot(h_hi, wrl_ref[...])
        rt = _route(logits, bias_ref[:, rows])
        rt_ref[:, rows] = rt
        ha_ref[rows, :D_MODEL] = h2
        ha_ref[rows, D_MODEL:] = jnp.concatenate([rt, jnp.zeros((LANES - ROUTE_ROWS, MERGE_SUB), F32)], axis=0).T


def _merge_call(layer, x, mixed_ctx, mixed_lat, mod, w_in_b, wpa, wpf, wpc, wo, l1g, l1b, wrh, wrl, bias_t):
    const2 = lambda i: (0, 0)
    per_layer = lambda i: (layer, 0, 0)
    tok = lambda i: (i, 0)
    return pl.pallas_call(
        _merge_kernel,
        out_shape=(jax.ShapeDtypeStruct((T_ALL, D_MODEL), F32),
                   jax.ShapeDtypeStruct((T_ALL, HA_W), F32),
                   jax.ShapeDtypeStruct((ROUTE_ROWS, T_ALL), F32)),
        grid=(T_ALL // TM,),
        in_specs=_group_specs() + _group_specs() + [
            pl.BlockSpec((1, 1, N_MOD, D_MODEL), lambda i: (layer, _cond_row(i, TM), 0, 0)),
            pl.BlockSpec((None, D_MODEL, IN_W), per_layer, pipeline_mode=pl.Buffered(1)),
            pl.BlockSpec((None, ATT_W, D_MODEL), per_layer, pipeline_mode=pl.Buffered(1)),
            pl.BlockSpec((None, FOURIER_W, D_MODEL), per_layer, pipeline_mode=pl.Buffered(1)),
            pl.BlockSpec((None, CHUNK_W, D_MODEL), per_layer, pipeline_mode=pl.Buffered(1)),
            pl.BlockSpec((None, D_MODEL, D_MODEL), per_layer, pipeline_mode=pl.Buffered(1)),
            pl.BlockSpec((1, 1, D_MODEL), per_layer),
            pl.BlockSpec((1, 1, D_MODEL), per_layer),
            pl.BlockSpec((D_MODEL, ROUTER_PAD), const2),
            pl.BlockSpec((D_MODEL, ROUTER_PAD), const2),
            pl.BlockSpec((N_EXPERTS, TM), const2),
        ],
        out_specs=(pl.BlockSpec((TM, D_MODEL), tok),
                   pl.BlockSpec((TM, HA_W), tok),
                   pl.BlockSpec((ROUTE_ROWS, TM), lambda i: (0, i))),
        compiler_params=_cparams("arbitrary"),
        name="merge_route",
    )(*x, mixed_ctx, mixed_lat, mod, w_in_b, wpa, wpf, wpc, wo, l1g, l1b, wrh, wrl, bias_t)


def _sort_kernel(rt_ref, tri_ref, low_ref, pos_ref, tile_ref):
    crow = lax.broadcasted_iota(jnp.int32, (CLASS_PAD, SORT_BLK), 0)
    tri = tri_ref[...]
    n_blk = T_ALL // SORT_BLK
    carry = jnp.zeros((CLASS_PAD, 1), F32)
    ranks = []
    for b in range(n_blk):
        hot = rt_ref[0:1, b * SORT_BLK:(b + 1) * SORT_BLK].astype(jnp.int32) == crow
        hot_f = jnp.where(hot, 1.0, 0.0)
        before = _dot(hot_f.astype(BF16), tri) + carry
        ranks.append(jnp.sum(jnp.where(hot, before, 0.0), axis=0, keepdims=True))
        carry = carry + jnp.sum(hot_f, axis=1, keepdims=True)
    padded = jnp.floor((carry + (TM_MOE - 1.0)) * (1.0 / TM_MOE)) * TM_MOE
    padded = jnp.broadcast_to(padded, (CLASS_PAD, LANES))
    offs = _dot(low_ref[...], padded.astype(BF16))
    for b in range(n_blk):
        hot = rt_ref[0:1, b * SORT_BLK:(b + 1) * SORT_BLK].astype(jnp.int32) == crow
        base = jnp.sum(jnp.where(hot, offs[:, 0:1], 0.0), axis=0, keepdims=True)
        pos_ref[:, b * SORT_BLK:(b + 1) * SORT_BLK] = (base + ranks[b]).astype(jnp.int32)

    start = lax.broadcasted_iota(jnp.int32, (1, LANES), 1).astype(F32) * TM_MOE
    is_class = lax.broadcasted_iota(jnp.int32, (CLASS_PAD, LANES), 0) < N_CLASSES
    ends = jnp.where(is_class, offs + padded, 0.0)
    total = jnp.max(ends, axis=0, keepdims=True)
    valid = start < total
    tcls = jnp.sum(jnp.where(jnp.logical_and(is_class, ends <= start), 1.0, 0.0), axis=0, keepdims=True)
    last = jnp.max(jnp.where(valid, tcls, 0.0), axis=1, keepdims=True)
    tcls = jnp.where(valid, tcls, last)
    grp = functools.reduce(jnp.add, [jnp.where(tcls >= g * N_PAIRS, 1.0, 0.0) for g in range(1, N_EXPERT_GROUPS)])
    pair = tcls - grp * N_PAIRS
    slot_a = functools.reduce(jnp.add, [jnp.where(pair == k, float(PAIR_A[k]), 0.0) for k in range(N_PAIRS)])
    slot_b = functools.reduce(jnp.add, [jnp.where(pair == k, float(PAIR_B[k]), 0.0) for k in range(N_PAIRS)])
    crow_t = lax.broadcasted_iota(jnp.int32, (CLASS_PAD, LANES), 0).astype(F32)
    real_end = jnp.sum(jnp.where(crow_t == tcls, offs + carry, 0.0), axis=0, keepdims=True)
    n_rows = jnp.where(valid, jnp.clip(real_end - start, 0.0, float(TM_MOE)), 0.0)
    rows = [grp * EXPERTS_PER_GROUP + slot_a, grp * EXPERTS_PER_GROUP + slot_b, n_rows]
    tile_ref[...] = jnp.concatenate(rows + [jnp.zeros((8 - len(rows), LANES), F32)], axis=0).astype(jnp.int32)


def _sort_call(rt, tri, low):
    return pl.pallas_call(
        _sort_kernel,
        out_shape=(jax.ShapeDtypeStruct((1, T_ALL), jnp.int32),
                   jax.ShapeDtypeStruct((8, LANES), jnp.int32)),
        name="route_sort",
    )(rt, tri, low)


def _moe_kernel(tile_ref, pos_ref, ha_hbm, wg_hbm, wu_hbm, wd_hbm, out_ref, hbuf, ybuf, src_ref, ssem,
                stg_g, stg_u, stg_d, act_g, act_u, act_d, wsem, ha_ref, hsem, staged_ref, *, layer):
    j = pl.program_id(0)
    slot = j % 2

    def n_rows(tile):
        inside = jnp.logical_and(tile >= 0, tile < N_TILES)
        return jnp.where(inside, tile_ref[TILE_ROWS, jnp.clip(tile, 0, N_TILES - 1)], 0)

    def token_of(tile, n, r, tail):
        return src_ref[tile * TM_MOE + (jnp.minimum(r, n - 1) if tail else r)]

    def scatter_copy(tile, n, buf_slot, r, tail):
        tok = token_of(tile, n, r, tail)
        dst = jnp.where(r < n, tok, T_ALL + buf_slot * TM_MOE + r) if tail else tok
        return pltpu.make_async_copy(ybuf.at[buf_slot, r], out_ref.at[dst], ssem.at[buf_slot])

    def for_groups(n, fn):
        full = lax.shift_right_logical(n, 3)

        def body(g, c):
            for k in range(COPY_GROUP):
                fn(g, k, False)
            return c
        lax.fori_loop(0, full, body, 0)

        @pl.when(jnp.bitwise_and(n, COPY_GROUP - 1) != 0)
        def _():
            for k in range(COPY_GROUP):
                fn(full, k, True)

    def retire(tile, buf_slot):
        n = n_rows(tile)
        one = scatter_copy(tile, n, buf_slot, 0, False)

        def body(g, c):
            for _ in range(COPY_GROUP):
                one.wait()
            return c
        lax.fori_loop(0, lax.shift_right_logical(n + COPY_GROUP - 1, 3), body, 0)

    def expert_of(x, tile):
        return tile_ref[TILE_SLOT_A + x, tile]

    def weight_copies(x, e):
        return [pltpu.make_async_copy(src.at[layer, e], stg.at[x], wsem.at[x])
                for src, stg in ((wg_hbm, stg_g), (wu_hbm, stg_u), (wd_hbm, stg_d))]

    @pl.when(j == 0)
    def _():
        load_inputs = pltpu.make_async_copy(ha_hbm, ha_ref, hsem)
        load_inputs.start()
        for x in range(2):
            for c in weight_copies(x, expert_of(x, 0)):
                c.start()
            staged_ref[x] = expert_of(x, 0)

        def place(t, c):
            src_ref[pos_ref[t]] = t
            return c
        lax.fori_loop(0, T_ALL, place, 0, unroll=8)
        load_inputs.wait()
        hbuf[...] = jnp.zeros(hbuf.shape, F32)
        ybuf[...] = jnp.zeros(ybuf.shape, F32)
        for s in range(2):
            park = pltpu.make_async_copy(ybuf.at[s], out_ref.at[pl.ds(T_ALL + s * TM_MOE, TM_MOE)], ssem.at[s])
            park.start()
            park.wait()

    @pl.when(n_rows(j - 2) > 0)
    def _():
        retire(j - 2, slot)

    def run_experts(rows):
        rec = hbuf[:rows // SUBLANES].reshape(rows, HA_W)
        h = rec[:, :D_MODEL].astype(BF16)

        def expert(x):
            a = _dot(h, act_g[x])
            u = _dot(h, act_u[x])
            w = rec[:, D_MODEL + 1 + x:D_MODEL + 2 + x]
            return _dot(((a * _sigmoid(a)) * u * w).astype(BF16), act_d[x])

        ybuf[slot, :rows] = (expert(0) + expert(1)).reshape(rows, D_CHUNKS, LANES)

    @pl.when(n_rows(j) > 0)
    def _():
        for x in range(2):
            prev = jnp.maximum(j - 1, 0)

            @pl.when(jnp.logical_or(j == 0, expert_of(x, j) != expert_of(x, prev)))
            def _():
                for c in weight_copies(x, expert_of(x, j)):
                    c.wait()
                act_g[x] = stg_g[x].astype(BF16)
                act_u[x] = stg_u[x].astype(BF16)
                act_d[x] = stg_d[x].astype(BF16)

        for x in range(2):
            cur = expert_of(x, j)

            def expert_or_cur(tile):
                return jnp.where(n_rows(tile) > 0, expert_of(x, jnp.minimum(tile, N_TILES - 1)), cur)
            e1, e2 = expert_or_cur(j + 1), expert_or_cur(j + 2)
            target = jnp.where(e1 != cur, e1, e2)

            @pl.when(jnp.logical_and(target != cur, target != staged_ref[x]))
            def _():
                for c in weight_copies(x, target):
                    c.start()
                staged_ref[x] = target

        n = n_rows(j)

        def gather_row(g, k, tail):
            tok = token_of(j, n, g * COPY_GROUP + k, tail)
            grp, sub = lax.shift_right_logical(tok, 3), jnp.bitwise_and(tok, SUBLANES - 1)
            hbuf[g, pl.ds(k, 1), :] = ha_ref[grp, pl.ds(sub, 1), :]
        for_groups(n, gather_row)

        @pl.when(n > SMALL_TILE)
        def _():
            run_experts(TM_MOE)

        @pl.when(n <= SMALL_TILE)
        def _():
            run_experts(SMALL_TILE)

        for_groups(n, lambda g, k, tail: scatter_copy(j, n, slot, g * COPY_GROUP + k, tail).start(priority=k % 2))

    @pl.when(j == N_TILES - 1)
    def _():
        @pl.when(n_rows(j - 1) > 0)
        def _():
            retire(j - 1, 1 - slot)

        @pl.when(n_rows(j) > 0)
        def _():
            retire(j, slot)


def _moe_call(layer, tiles, pos, ha, wge, wue, wde):
    up = (2, D_MODEL, D_EXPERT)
    down = (2, D_EXPERT, D_MODEL)
    grid_spec = pltpu.PrefetchScalarGridSpec(
        num_scalar_prefetch=2,
        grid=(N_TILES,),
        in_specs=[pl.BlockSpec(memory_space=pl.ANY)] * 4,
        out_specs=pl.BlockSpec(memory_space=pl.ANY),
        scratch_shapes=[
            pltpu.VMEM((TM_MOE // SUBLANES, SUBLANES, HA_W), F32),
            pltpu.VMEM((2, TM_MOE, D_CHUNKS, LANES), F32),
            pltpu.SMEM((R_ROWS,), jnp.int32),
            pltpu.SemaphoreType.DMA((2,)),
            pltpu.VMEM(up, F32), pltpu.VMEM(up, F32), pltpu.VMEM(down, F32),
            pltpu.VMEM(up, BF16), pltpu.VMEM(up, BF16), pltpu.VMEM(down, BF16),
            pltpu.SemaphoreType.DMA((2,)),
            pltpu.VMEM((T_ALL // SUBLANES, SUBLANES, HA_W), F32),
            pltpu.SemaphoreType.DMA(()),
            pltpu.SMEM((2,), jnp.int32),
        ],
    )
    return pl.pallas_call(
        functools.partial(_moe_kernel, layer=layer),
        out_shape=jax.ShapeDtypeStruct((T_ALL + N_PARK, D_CHUNKS, LANES), F32),
        grid_spec=grid_spec,
        compiler_params=pltpu.CompilerParams(dimension_semantics=("arbitrary",), vmem_limit_bytes=V7X_VMEM_LIMIT,
                                             has_side_effects=True),
        name="moe_pairs",
    )(tiles, pos,
      ha.reshape(T_ALL // SUBLANES, SUBLANES, HA_W), wge, wue, wde)


def _post_kernel(x1_ref, moe_ref, mod_ref, l2g_ref, l2b_ref, oc_ref, ol_ref):
    m = mod_ref[0, 0]
    moe = moe_ref[...].reshape(TM, D_MODEL)
    y = _ln(ALPHA * x1_ref[...] + m[5:6] * moe) * l2g_ref[0] + l2b_ref[0]
    i = pl.program_id(0)

    @pl.when(i < N_CTX_TILES)
    def _():
        oc_ref[...] = y

    @pl.when(i >= N_CTX_TILES)
    def _():
        ol_ref[...] = y


def _post_call(layer, x1, moe, mod, l2g, l2b):
    tok = lambda i: (i, 0)
    per_layer = lambda i: (layer, 0, 0)
    return pl.pallas_call(
        _post_kernel,
        out_shape=(jax.ShapeDtypeStruct((T_CTX, D_MODEL), F32), jax.ShapeDtypeStruct((T_LAT, D_MODEL), F32)),
        grid=(T_ALL // TM,),
        in_specs=[
            pl.BlockSpec((TM, D_MODEL), tok),
            pl.BlockSpec((TM, D_CHUNKS, LANES), lambda i: (i, 0, 0)),
            pl.BlockSpec((1, 1, N_MOD, D_MODEL), lambda i: (layer, _cond_row(i, TM), 0, 0)),
            pl.BlockSpec((1, 1, D_MODEL), per_layer),
            pl.BlockSpec((1, 1, D_MODEL), per_layer),
        ],
        out_specs=tuple(_group_specs()),
        compiler_params=_cparams("arbitrary"),
        name="post_moe",
    )(x1, moe, mod, l2g, l2b)


def _rope_tables():
    pos = np.arange(DEC_SEQ)
    quarter = HEAD_DIM // 4
    inv = ROPE_THETA ** (-np.arange(quarter, dtype=np.float64) / quarter)
    ang_r = (pos // GRID_W)[:, None] * inv[None, :]
    ang_c = (pos % GRID_W)[:, None] * inv[None, :]
    cos = np.concatenate([np.cos(ang_r)] * 2 + [np.cos(ang_c)] * 2, axis=-1)
    sin = np.concatenate([-np.sin(ang_r), np.sin(ang_r), -np.sin(ang_c), np.sin(ang_c)], axis=-1)
    return (jnp.asarray(np.tile(cos, (1, N_Q_HEADS)).astype(np.float32)),
            jnp.asarray(np.tile(sin, (1, N_Q_HEADS)).astype(np.float32)))


def _dft_mats(n, scale):
    j = np.arange(n)
    ang = ((j[:, None] * j[None, :]) % n) * (2 * np.pi / n)
    return np.cos(ang) * scale, np.sin(ang) * scale


def _block_diag(m, reps):
    return np.kron(np.eye(reps), m)


def _const_bf16(a):
    return jnp.asarray(np.asarray(a, np.float32)).astype(BF16)


def _const_01(a):
    return jnp.asarray(np.asarray(a, np.float32).astype(BF16))


def kernel(x_prompt, x_sample, cache_k, cache_v, c, c_ctx, w_in, q_norm_g, k_norm_g, w_proj_att,
           w_proj_fourier, w_proj_chunk, w_out, chunk_ln_g, chunk_ln_b, chunk_ws, chunk_bs, w_ada, b_ada,
           ln1_g, ln1_b, ln2_g, ln2_b, w_router, router_bias, w_gate_e, w_up_e, w_down_e):
    x = (x_prompt.reshape(T_CTX, D_MODEL), x_sample.reshape(T_LAT, D_MODEL))
    cond = jnp.concatenate([c_ctx[None, :], c, jnp.zeros((N_COND - 1 - DEC_BATCH, D_MODEL), F32)], axis=0)
    mod = _mod_call(cond, w_ada, b_ada).reshape(DEPTH, N_COND, N_MOD, D_MODEL)

    cos_t, sin_t = _rope_tables()
    bd_heads = _const_01(_block_diag(np.ones((HEAD_DIM, HEAD_DIM)), N_Q_HEADS))
    c64, s64 = _dft_mats(FOURIER_GROUP_W, 1.0)
    bdc = _const_bf16(_block_diag(c64, FOURIER_W // FOURIER_GROUP_W))
    bds = _const_bf16(_block_diag(s64, FOURIER_W // FOURIER_GROUP_W))
    dft = {}
    for seq in (SEQ, DEC_SEQ):
        cs, ss = _dft_mats(seq, 1.0 / math.sqrt(seq * FOURIER_GROUP_W))
        dft[seq] = _const_bf16(np.concatenate([cs, -ss], axis=1))
    ctx_k = cache_k.reshape(DEC_BATCH, DEPTH, PAST_LEN, KV_W)
    ctx_v = cache_v.reshape(DEC_BATCH, DEPTH, PAST_LEN, KV_W)
    wr = jnp.pad(w_router, ((0, 0), (0, ROUTER_PAD - N_EXPERTS)))
    wr_hi = wr.astype(BF16)
    wr_lo = (wr - wr_hi.astype(F32)).astype(BF16)
    bias_t = jnp.broadcast_to(router_bias[:, None], (N_EXPERTS, TM))
    tri = _const_01(np.triu(np.ones((SORT_BLK, SORT_BLK)), 1))
    low = _const_01(np.tril(np.ones((CLASS_PAD, CLASS_PAD)), -1))

    w_in_b = w_in.astype(BF16)
    wpa, wpf, wpc, wo = (w.astype(BF16) for w in (w_proj_att, w_proj_fourier, w_proj_chunk, w_out))
    wge, wue, wde = w_gate_e, w_up_e, w_down_e
    qg = jnp.tile(q_norm_g, (1, N_Q_HEADS))[:, None, :]
    kg = jnp.tile(k_norm_g, (1, N_KV_HEADS))[:, None, :]
    ws = chunk_ws.astype(BF16)
    bsm = jnp.repeat(jnp.swapaxes(chunk_bs, 1, 2), CHUNK_GROUP_W, axis=2)
    row3 = lambda a: a[:, None, :]

    new_k = jnp.zeros((BATCH, DEPTH, SEQ, KV_W), F32)
    new_v = jnp.zeros((BATCH, DEPTH, SEQ, KV_W), F32)
    for l in range(DEPTH):
        mix, new_k, new_v = _proj_call(l, x, mod, w_in_b, qg, kg, cos_t, sin_t, bd_heads, new_k, new_v)
        mixer_args = (bdc, bds, ws, bsm, row3(chunk_ln_g), row3(chunk_ln_b))
        mixed_ctx = _mixer_call(l, mix, None, None, dft[SEQ], *mixer_args, latent=False)
        mixed_lat = _mixer_call(l, mix, ctx_k, ctx_v, dft[DEC_SEQ], *mixer_args, latent=True)
        x1, ha, rt = _merge_call(l, x, mixed_ctx, mixed_lat, mod, w_in_b, wpa, wpf, wpc, wo,
                                 row3(ln1_g), row3(ln1_b), wr_hi, wr_lo, bias_t)
        pos2d, tiles = _sort_call(rt, tri, low)
        moe = _moe_call(l, tiles, pos2d[0], ha, wge, wue, wde)
        x = _post_call(l, x1, moe, mod, row3(ln2_g), row3(ln2_b))

    y_prompt = x[0].reshape(BATCH, SEQ, D_MODEL)
    y_sample = x[1].reshape(DEC_BATCH, DEC_SEQ, D_MODEL)
    cache_shape = (BATCH, DEPTH, SEQ, N_KV_HEADS, HEAD_DIM)
    return (y_prompt, y_sample, new_k.reshape(cache_shape), new_v.reshape(cache_shape))
```

```python
import functools
import math

import jax
import jax.numpy as jnp
import numpy as np
from jax import lax
from jax.experimental import pallas as pl
from jax.experimental.pallas import tpu as pltpu

F32 = jnp.float32
BF16 = jnp.bfloat16

D_MODEL = 1024
BATCH = 16
SEQ = 256
DEPTH = 2
DEC_BATCH = 2
DEC_SEQ = 1024
PAST_LEN = 256
GRID_W = 64
N_Q_HEADS = 8
N_KV_HEADS = 2
HEAD_DIM = 64
Q_PER_KV = N_Q_HEADS // N_KV_HEADS
ATT_W = N_Q_HEADS * HEAD_DIM
KV_W = N_KV_HEADS * HEAD_DIM
ROPE_THETA = 10000.0
FOURIER_GROUP_W = 64
FOURIER_W = 256
N_CHUNK_GROUPS = 4
CHUNK_GROUP_W = 64
CHUNK_W = 256
CHUNK = 128
N_EXPERTS = 16
N_EXPERT_GROUPS = 4
EXPERTS_PER_GROUP = 4
D_EXPERT = 512
ALPHA = (2 * DEPTH) ** 0.25
LN_EPS = 1e-6
RMS_EPS = 1e-6

LANES = 128
SUBLANES = 8
T_CTX = BATCH * SEQ
T_LAT = DEC_BATCH * DEC_SEQ
T_ALL = T_CTX + T_LAT
IN_W = ATT_W + 2 * KV_W + FOURIER_W + 2 * CHUNK_W + 3 * D_MODEL
MIX_W = ATT_W + 2 * KV_W + FOURIER_W + 2 * CHUNK_W
N_MOD = 6
N_COND = 8
ROUTER_PAD = LANES
TM = 512
TQ = 256
MERGE_SUB = 256
PROJ_SUB = 256
V7X_VMEM_LIMIT = 56 * 1024 * 1024

PAIR_A = (0, 0, 0, 1, 1, 3)
PAIR_B = (1, 2, 3, 3, 2, 2)
N_PAIRS = len(PAIR_A)
N_CLASSES = N_EXPERT_GROUPS * N_PAIRS
CLASS_PAD = 32
ROUTE_ROWS = 8
TM_MOE = 256
N_TILES = T_ALL // TM_MOE + N_CLASSES
R_ROWS = N_TILES * TM_MOE
SORT_BLK = 512
D_CHUNKS = D_MODEL // LANES
HA_W = D_MODEL + LANES
TILE_SLOT_A, TILE_SLOT_B, TILE_ROWS = 0, 1, 2
SMALL_TILE = 64
N_PARK = 2 * TM_MOE
COPY_GROUP = 8


def _cparams(*sem):
    return pltpu.CompilerParams(dimension_semantics=sem, vmem_limit_bytes=V7X_VMEM_LIMIT)


def _cond_row(i, tm):
    n_ctx = T_CTX // tm
    return jnp.where(i < n_ctx, 0, 1 + (i - n_ctx) // (DEC_SEQ // tm))


N_CTX_TILES = T_CTX // TM


def _group_specs():
    return [pl.BlockSpec((TM, D_MODEL), lambda i: (jnp.minimum(i, N_CTX_TILES - 1), 0)),
            pl.BlockSpec((TM, D_MODEL), lambda i: (jnp.maximum(i - N_CTX_TILES, 0), 0))]


def _group_tile(ctx_ref, lat_ref):
    return jnp.where(pl.program_id(0) < N_CTX_TILES, ctx_ref[...], lat_ref[...])


def _ln(x):
    mu = jnp.mean(x, axis=-1, keepdims=True)
    xc = x - mu
    var = jnp.mean(xc * xc, axis=-1, keepdims=True)
    return xc * lax.rsqrt(var + LN_EPS)


def _split_bf16(x):
    hi = x.astype(BF16)
    lo = (x - hi.astype(F32)).astype(BF16)
    return hi, lo


def _dot(a, b):
    return jnp.dot(a, b, preferred_element_type=F32)


def _sigmoid(x):
    return 1.0 / (1.0 + jnp.exp(-x))


def _gelu_tanh(x):
    c = np.float32(np.sqrt(2 / np.pi))
    return x * (0.5 * (1.0 + jnp.tanh(c * (x + 0.044715 * (x * x * x)))))


def _mod_kernel(c_ref, w_ref, b_ref, o_ref):
    c = c_ref[...]
    s = c * _sigmoid(c)
    o_ref[0] = _dot(s.astype(BF16), w_ref[0].astype(BF16)) + b_ref[0]


def _mod_call(cond, w_ada, b_ada):
    tn = 1536
    n = N_MOD * D_MODEL
    return pl.pallas_call(
        _mod_kernel,
        out_shape=jax.ShapeDtypeStruct((DEPTH, N_COND, n), F32),
        grid=(DEPTH, n // tn),
        in_specs=[
            pl.BlockSpec((N_COND, D_MODEL), lambda l, j: (0, 0)),
            pl.BlockSpec((1, D_MODEL, tn), lambda l, j: (l, 0, j)),
            pl.BlockSpec((1, 1, tn), lambda l, j: (l, 0, j)),
        ],
        out_specs=pl.BlockSpec((1, N_COND, tn), lambda l, j: (l, 0, j)),
        compiler_params=_cparams("arbitrary", "arbitrary"),
        name="adaln_mod",
    )(cond, w_ada, b_ada.reshape(DEPTH, 1, n))


def _head_rms(q, bd, g):
    hi, lo = _split_bf16(q * q)
    ssum = _dot(hi, bd) + _dot(lo, bd)
    return q * lax.rsqrt(ssum * (1.0 / HEAD_DIM) + RMS_EPS) * g


def _rope(x, cos, sin_signed):
    w = x.shape[-1]
    lane = lax.broadcasted_iota(jnp.int32, x.shape, 1)
    swapped = jnp.where((lane % 32) < 16, pltpu.roll(x, w - 16, 1), pltpu.roll(x, 16, 1))
    return x * cos + swapped * sin_signed


def _proj_kernel(*refs, after_moe):
    if after_moe:
        (x1_ref, moe_ref, modp_ref, l2g_ref, l2b_ref, mod_ref, w_ref, qg_ref, kg_ref, cos_ref, sin_ref, bd_ref,
         kc_in, vc_in, o_ref, kc_ref, vc_ref, xoc_ref, xol_ref) = refs
    else:
        (xc_ref, xl_ref, mod_ref, w_ref, qg_ref, kg_ref, cos_ref, sin_ref, bd_ref,
         kc_in, vc_in, o_ref, kc_ref, vc_ref) = refs
    del kc_in, vc_in
    m = mod_ref[0, 0]
    in_ctx = pl.program_id(0) < N_CTX_TILES
    bd = bd_ref[...]
    sub_rows = [slice(s * PROJ_SUB, (s + 1) * PROJ_SUB) for s in range(TM // PROJ_SUB)]
    projs = []
    xs = []
    for rows in sub_rows:
        if after_moe:
            moe = moe_ref[rows].reshape(PROJ_SUB, D_MODEL)
            x = _ln(ALPHA * x1_ref[rows, :] + modp_ref[0, 0][5:6] * moe) * l2g_ref[0] + l2b_ref[0]
            xs.append(x)
        else:
            x = jnp.where(in_ctx, xc_ref[rows, :], xl_ref[rows, :])
        h = _ln(x) * (1.0 + m[1:2]) + m[0:1]
        proj = _dot(h.astype(BF16), w_ref[...])
        o_ref[rows, ATT_W + KV_W:] = proj[:, ATT_W + KV_W:]
        projs.append(proj[:, :ATT_W + 2 * KV_W])
    keys = []
    for rows, proj in zip(sub_rows, projs):
        qn = _head_rms(proj[:, :ATT_W], bd, qg_ref[0])
        kn = _head_rms(proj[:, ATT_W:ATT_W + KV_W], bd[:KV_W, :KV_W], kg_ref[0])
        cos = cos_ref[rows, :]
        sin = sin_ref[rows, :]
        o_ref[rows, :ATT_W] = jnp.where(in_ctx, qn, _rope(qn, cos, sin))
        o_ref[rows, ATT_W:ATT_W + KV_W] = jnp.where(in_ctx, kn, _rope(kn, cos[:, :KV_W], sin[:, :KV_W]))
        keys.append(kn)

    @pl.when(in_ctx)
    def _():
        for s in range(len(sub_rows)):
            kc_ref[s] = keys[s]
            vc_ref[s] = projs[s][:, ATT_W + KV_W:]
        if after_moe:
            for rows, x in zip(sub_rows, xs):
                xoc_ref[rows, :] = x

    if after_moe:
        @pl.when(jnp.logical_not(in_ctx))
        def _():
            for rows, x in zip(sub_rows, xs):
                xol_ref[rows, :] = x


def _proj_call(layer, x, mod, w_in_b, qg, kg, cos_t, sin_t, bd, k_cache, v_cache, prev=None):
    assert PROJ_SUB == SEQ
    n_ctx = T_CTX // TM
    per_seq = DEC_SEQ // TM
    after_moe = prev is not None

    def rope_idx(i):
        return (jnp.where(i < n_ctx, 0, (i - n_ctx) % per_seq), 0)

    cache_spec = pl.BlockSpec((TM // SEQ, None, SEQ, KV_W), lambda i: (jnp.minimum(i, n_ctx - 1), layer, 0, 0))
    cache_shape = jax.ShapeDtypeStruct((BATCH, DEPTH, SEQ, KV_W), F32)
    out_shape = [jax.ShapeDtypeStruct((T_ALL, MIX_W), F32), cache_shape, cache_shape]
    out_specs = [pl.BlockSpec((TM, MIX_W), lambda i: (i, 0)), cache_spec, cache_spec]
    if after_moe:
        x1, moe, l2g, l2b = prev
        prev_layer = lambda i: (layer - 1, 0, 0)
        x_specs = [
            pl.BlockSpec((TM, D_MODEL), lambda i: (i, 0)),
            pl.BlockSpec((TM, D_CHUNKS, LANES), lambda i: (i, 0, 0)),
            pl.BlockSpec((1, 1, N_MOD, D_MODEL), lambda i: (layer - 1, _cond_row(i, TM), 0, 0)),
            pl.BlockSpec((1, 1, D_MODEL), prev_layer),
            pl.BlockSpec((1, 1, D_MODEL), prev_layer),
        ]
        x_args = [x1, moe, mod, l2g, l2b]
        out_shape += [jax.ShapeDtypeStruct((T_CTX, D_MODEL), F32), jax.ShapeDtypeStruct((T_LAT, D_MODEL), F32)]
        out_specs += _group_specs()
    else:
        x_specs, x_args = _group_specs(), list(x)
    n_in = len(x_args) + 9
    return pl.pallas_call(
        functools.partial(_proj_kernel, after_moe=after_moe),
        out_shape=tuple(out_shape),
        grid=(T_ALL // TM,),
        in_specs=x_specs + [
            pl.BlockSpec((1, 1, N_MOD, D_MODEL), lambda i: (layer, _cond_row(i, TM), 0, 0)),
            pl.BlockSpec((None, D_MODEL, MIX_W), lambda i: (layer, 0, 0)),
            pl.BlockSpec((1, 1, ATT_W), lambda i: (layer, 0, 0)),
            pl.BlockSpec((1, 1, KV_W), lambda i: (layer, 0, 0)),
            pl.BlockSpec((TM, ATT_W), rope_idx),
            pl.BlockSpec((TM, ATT_W), rope_idx),
            pl.BlockSpec((ATT_W, ATT_W), lambda i: (0, 0)),
            pl.BlockSpec(memory_space=pl.ANY),
            pl.BlockSpec(memory_space=pl.ANY),
        ],
        out_specs=tuple(out_specs),
        input_output_aliases={n_in - 2: 1, n_in - 1: 2},
        compiler_params=_cparams("arbitrary"),
        name="in_proj",
    )(*x_args, mod, w_in_b, qg, kg, cos_t, sin_t, bd, k_cache, v_cache)


def _attention_tile(q, k_parts, v_parts, phase_major):
    def kv_of(h, parts):
        g = h // Q_PER_KV
        return [p[:, g * HEAD_DIM:(g + 1) * HEAD_DIM] for p in parts]

    def scores(h):
        qh = q[:, h * HEAD_DIM:(h + 1) * HEAD_DIM]
        return [lax.dot_general(qh, k, (((1,), (1,)), ((), ())), preferred_element_type=F32)
                for k in kv_of(h, k_parts)]

    if phase_major:
        all_ss = [scores(h) for h in range(N_Q_HEADS)]
        all_m = [functools.reduce(jnp.maximum, [jnp.max(s, axis=-1, keepdims=True) for s in ss]) for ss in all_ss]
        all_es = [[jnp.exp(s - m) for s in ss] for ss, m in zip(all_ss, all_m)]
        all_den = [functools.reduce(jnp.add, [jnp.sum(e, axis=-1, keepdims=True) for e in es]) for es in all_es]
        outs = [functools.reduce(jnp.add, [_dot(e.astype(BF16), v) for e, v in zip(es, kv_of(h, v_parts))])
                * (1.0 / den) for h, (es, den) in enumerate(zip(all_es, all_den))]
        return jnp.concatenate(outs, axis=-1)
    outs = []
    ss_next = scores(0)
    for h in range(N_Q_HEADS):
        ss = ss_next
        if h + 1 < N_Q_HEADS:
            ss_next = scores(h + 1)
        m = functools.reduce(jnp.maximum, [jnp.max(s, axis=-1, keepdims=True) for s in ss])
        es = [jnp.exp(s - m) for s in ss]
        denom = functools.reduce(jnp.add, [jnp.sum(e, axis=-1, keepdims=True) for e in es])
        o = functools.reduce(jnp.add, [_dot(e.astype(BF16), v) for e, v in zip(es, kv_of(h, v_parts))])
        outs.append(o * (1.0 / denom))
    return jnp.concatenate(outs, axis=-1)


def _mixer_kernel(*refs, seq, latent):
    if latent:
        (mix_ref, ck_ref, cv_ref, dft_ref, bdc_ref, bds_ref, ws_ref, bsm_ref, clg_ref, clb_ref, o_ref) = refs
    else:
        (mix_ref, dft_ref, bdc_ref, bds_ref, ws_ref, bsm_ref, clg_ref, clb_ref, o_ref) = refs

    k_new = mix_ref[:, ATT_W:ATT_W + KV_W].astype(BF16)
    v_new = mix_ref[:, ATT_W + KV_W:ATT_W + 2 * KV_W].astype(BF16)
    if latent:
        k_parts = [ck_ref[...].astype(BF16), k_new]
        v_parts = [cv_ref[...].astype(BF16), v_new]
    else:
        k_parts, v_parts = [k_new], [v_new]

    def q_tile(t, carry):
        r0 = pl.multiple_of(t * TQ, TQ)
        q = (mix_ref[pl.ds(r0, TQ), :ATT_W] * (HEAD_DIM ** -0.5)).astype(BF16)
        o_ref[pl.ds(r0, TQ), :ATT_W] = _attention_tile(q, k_parts, v_parts, phase_major=not latent).astype(BF16)
        return carry

    lax.fori_loop(0, seq // TQ, q_tile, 0)

    f_hi, f_lo = _split_bf16(mix_ref[:, ATT_W + 2 * KV_W:ATT_W + 2 * KV_W + FOURIER_W])
    bdc = bdc_ref[...]
    bds = bds_ref[...]
    y = jnp.concatenate([_dot(f_hi, bdc) + _dot(f_lo, bdc), _dot(f_hi, bds) + _dot(f_lo, bds)], axis=0)
    y_hi, y_lo = _split_bf16(y)
    dft = dft_ref[...]
    four = _dot(dft, y_hi) + _dot(dft, y_lo)
    o_ref[:, ATT_W:ATT_W + FOURIER_W] = four.astype(BF16)

    c0 = ATT_W + 2 * KV_W + FOURIER_W
    u = _gelu_tanh(mix_ref[:, c0:c0 + CHUNK_W])
    vn = (_ln(_gelu_tanh(mix_ref[:, c0 + CHUNK_W:c0 + 2 * CHUNK_W])) * clg_ref[0] + clb_ref[0]).astype(BF16)
    lane = lax.broadcasted_iota(jnp.int32, (CHUNK, CHUNK_W), 1)
    bsm = bsm_ref[...]
    for c in range(seq // CHUNK):
        vc = vn[c * CHUNK:(c + 1) * CHUNK]
        sv = bsm
        for g in range(N_CHUNK_GROUPS):
            vg = jnp.where(lane // CHUNK_GROUP_W == g, vc, jnp.zeros_like(vc))
            sv = sv + _dot(ws_ref[g], vg)
        o_ref[c * CHUNK:(c + 1) * CHUNK, ATT_W + FOURIER_W:] = (u[c * CHUNK:(c + 1) * CHUNK] * sv).astype(BF16)


def _mixer_call(layer, mix, ctx_k, ctx_v, dft, bdc, bds, ws, bsm, clg, clb, *, latent):
    seq = DEC_SEQ if latent else SEQ
    nb = DEC_BATCH if latent else BATCH
    row0 = (T_CTX // seq) if latent else 0
    const2 = lambda b: (0, 0)
    per_layer = lambda b: (layer, 0, 0)
    in_specs = [pl.BlockSpec((seq, MIX_W), lambda b: (row0 + b, 0))]
    args = [mix]
    if latent:
        in_specs += [pl.BlockSpec((None, None, PAST_LEN, KV_W), lambda b: (b, layer, 0, 0))] * 2
        args += [ctx_k, ctx_v]
    in_specs += [
        pl.BlockSpec((seq, 2 * seq), const2),
        pl.BlockSpec((FOURIER_W, FOURIER_W), const2),
        pl.BlockSpec((FOURIER_W, FOURIER_W), const2),
        pl.BlockSpec((None, N_CHUNK_GROUPS, CHUNK, CHUNK), lambda b: (layer, 0, 0, 0)),
        pl.BlockSpec((None, CHUNK, CHUNK_W), per_layer),
        pl.BlockSpec((1, 1, CHUNK_W), per_layer),
        pl.BlockSpec((1, 1, CHUNK_W), per_layer),
    ]
    args += [dft, bdc, bds, ws, bsm, clg, clb]
    return pl.pallas_call(
        functools.partial(_mixer_kernel, seq=seq, latent=latent),
        out_shape=jax.ShapeDtypeStruct((nb * seq, D_MODEL), BF16),
        grid=(nb,),
        in_specs=in_specs,
        out_specs=pl.BlockSpec((seq, D_MODEL), lambda b: (b, 0)),
        compiler_params=_cparams("arbitrary"),
        name="mixer_latent" if latent else "mixer_context",
    )(*args)


def _route(logits, bias_t):
    lt = logits.T[:N_EXPERTS]
    ex = jnp.exp(lt - jnp.max(lt, axis=0, keepdims=True))
    probs = ex / jnp.sum(ex, axis=0, keepdims=True)
    sel = probs + bias_t
    p = [probs[e:e + 1] for e in range(N_EXPERTS)]
    s = [sel[e:e + 1] for e in range(N_EXPERTS)]
    n = EXPERTS_PER_GROUP
    scores = []
    for g in range(N_EXPERT_GROUPS):
        pair = [s[g * n + a] + s[g * n + b] for a in range(n) for b in range(a + 1, n)]
        scores.append(functools.reduce(jnp.maximum, pair))
    best = jnp.zeros_like(scores[0], dtype=jnp.int32)
    best_score = scores[0]
    for g in range(1, N_EXPERT_GROUPS):
        better = scores[g] > best_score
        best = jnp.where(better, g, best)
        best_score = jnp.where(better, scores[g], best_score)
    cls = jnp.zeros_like(best_score)
    w_a = jnp.zeros_like(best_score)
    w_b = jnp.zeros_like(best_score)
    for g in range(N_EXPERT_GROUPS):
        in_g = best == g
        chosen = []
        for a in range(n):
            rank = jnp.zeros_like(best)
            for b in range(n):
                if b == a:
                    continue
                ahead = (s[g * n + b] > s[g * n + a]) if b > a else (s[g * n + b] >= s[g * n + a])
                rank = rank + ahead.astype(jnp.int32)
            chosen.append(jnp.logical_and(in_g, rank < 2))
        for k in range(N_PAIRS):
            pa, pb = p[g * n + PAIR_A[k]], p[g * n + PAIR_B[k]]
            hit = jnp.logical_and(chosen[PAIR_A[k]], chosen[PAIR_B[k]])
            wsum = pa + pb
            cls = jnp.where(hit, float(g * N_PAIRS + k), cls)
            w_a = jnp.where(hit, pa / wsum, w_a)
            w_b = jnp.where(hit, pb / wsum, w_b)
    return jnp.concatenate([cls, w_a, w_b, jnp.zeros((ROUTE_ROWS - 3, lt.shape[1]), F32)], axis=0)


def _merge_kernel(xc_ref, xl_ref, mixc_ref, mixl_ref, mod_ref, win_ref, wpa_ref, wpf_ref, wpc_ref, wo_ref,
                  l1g_ref, l1b_ref, wrh_ref, wrl_ref, bias_ref, x1_ref, ha_ref, rt_ref):
    m = mod_ref[0, 0]
    in_ctx = pl.program_id(0) < N_CTX_TILES
    branch_w = ((0, ATT_W, wpa_ref), (ATT_W, FOURIER_W, wpf_ref), (ATT_W + FOURIER_W, CHUNK_W, wpc_ref))
    sub_rows = [slice(s * MERGE_SUB, (s + 1) * MERGE_SUB) for s in range(TM // MERGE_SUB)]
    residual = []
    for rows in sub_rows:
        x = jnp.where(in_ctx, xc_ref[rows, :], xl_ref[rows, :])
        mixed = jnp.where(in_ctx, mixc_ref[rows, :], mixl_ref[rows, :])
        h = (_ln(x) * (1.0 + m[1:2]) + m[0:1]).astype(BF16)
        merged = None
        for b, (c0, width, w_ref) in enumerate(branch_w):
            gate = _sigmoid(_dot(h, win_ref[:, MIX_W + b * D_MODEL:MIX_W + (b + 1) * D_MODEL]))
            term = gate * _dot(mixed[:, c0:c0 + width], w_ref[...])
            merged = term if merged is None else merged + term
        mix = _dot(merged.astype(BF16), wo_ref[...])
        residual.append(ALPHA * x + m[2:3] * mix)
    for rows, pre in zip(sub_rows, residual):
        x1 = _ln(pre) * l1g_ref[0] + l1b_ref[0]
        x1_ref[rows, :] = x1
        h2 = _ln(x1) * (1.0 + m[4:5]) + m[3:4]
        h_hi, h_lo = _split_bf16(h2)
        wrh = wrh_ref[...]
        logits = _dot(h_hi, wrh) + _dot(h_lo, wrh) + _dot(h_hi, wrl_ref[...])
        rt = _route(logits, bias_ref[:, rows])
        rt_ref[:, rows] = rt
        ha_ref[rows, :D_MODEL] = h2
        ha_ref[rows, D_MODEL:] = jnp.concatenate([rt, jnp.zeros((LANES - ROUTE_ROWS, MERGE_SUB), F32)], axis=0).T


def _merge_call(layer, x, mixed_ctx, mixed_lat, mod, w_in_b, wpa, wpf, wpc, wo, l1g, l1b, wrh, wrl, bias_t):
    const2 = lambda i: (0, 0)
    per_layer = lambda i: (layer, 0, 0)
    tok = lambda i: (i, 0)
    return pl.pallas_call(
        _merge_kernel,
        out_shape=(jax.ShapeDtypeStruct((T_ALL, D_MODEL), F32),
                   jax.ShapeDtypeStruct((T_ALL, HA_W), F32),
                   jax.ShapeDtypeStruct((ROUTE_ROWS, T_ALL), F32)),
        grid=(T_ALL // TM,),
        in_specs=_group_specs() + _group_specs() + [
            pl.BlockSpec((1, 1, N_MOD, D_MODEL), lambda i: (layer, _cond_row(i, TM), 0, 0)),
            pl.BlockSpec((None, D_MODEL, IN_W), per_layer, pipeline_mode=pl.Buffered(1)),
            pl.BlockSpec((None, ATT_W, D_MODEL), per_layer, pipeline_mode=pl.Buffered(1)),
            pl.BlockSpec((None, FOURIER_W, D_MODEL), per_layer, pipeline_mode=pl.Buffered(1)),
            pl.BlockSpec((None, CHUNK_W, D_MODEL), per_layer, pipeline_mode=pl.Buffered(1)),
            pl.BlockSpec((None, D_MODEL, D_MODEL), per_layer, pipeline_mode=pl.Buffered(1)),
            pl.BlockSpec((1, 1, D_MODEL), per_layer),
            pl.BlockSpec((1, 1, D_MODEL), per_layer),
            pl.BlockSpec((D_MODEL, ROUTER_PAD), const2),
            pl.BlockSpec((D_MODEL, ROUTER_PAD), const2),
            pl.BlockSpec((N_EXPERTS, TM), const2),
        ],
        out_specs=(pl.BlockSpec((TM, D_MODEL), tok),
                   pl.BlockSpec((TM, HA_W), tok),
                   pl.BlockSpec((ROUTE_ROWS, TM), lambda i: (0, i))),
        compiler_params=_cparams("arbitrary"),
        name="merge_route",
    )(*x, mixed_ctx, mixed_lat, mod, w_in_b, wpa, wpf, wpc, wo, l1g, l1b, wrh, wrl, bias_t)


def _sort_kernel(rt_ref, tri_ref, low_ref, pos_ref, tile_ref):
    crow = lax.broadcasted_iota(jnp.int32, (CLASS_PAD, SORT_BLK), 0)
    tri = tri_ref[...]
    n_blk = T_ALL // SORT_BLK
    carry = jnp.zeros((CLASS_PAD, 1), F32)
    ranks = []
    for b in range(n_blk):
        hot = rt_ref[0:1, b * SORT_BLK:(b + 1) * SORT_BLK].astype(jnp.int32) == crow
        hot_f = jnp.where(hot, 1.0, 0.0)
        before = _dot(hot_f.astype(BF16), tri) + carry
        ranks.append(jnp.sum(jnp.where(hot, before, 0.0), axis=0, keepdims=True))
        carry = carry + jnp.sum(hot_f, axis=1, keepdims=True)
    padded = jnp.floor((carry + (TM_MOE - 1.0)) * (1.0 / TM_MOE)) * TM_MOE
    padded = jnp.broadcast_to(padded, (CLASS_PAD, LANES))
    offs = _dot(low_ref[...], padded.astype(BF16))
    for b in range(n_blk):
        hot = rt_ref[0:1, b * SORT_BLK:(b + 1) * SORT_BLK].astype(jnp.int32) == crow
        base = jnp.sum(jnp.where(hot, offs[:, 0:1], 0.0), axis=0, keepdims=True)
        pos_ref[:, b * SORT_BLK:(b + 1) * SORT_BLK] = (base + ranks[b]).astype(jnp.int32)

    start = lax.broadcasted_iota(jnp.int32, (1, LANES), 1).astype(F32) * TM_MOE
    is_class = lax.broadcasted_iota(jnp.int32, (CLASS_PAD, LANES), 0) < N_CLASSES
    ends = jnp.where(is_class, offs + padded, 0.0)
    total = jnp.max(ends, axis=0, keepdims=True)
    valid = start < total
    tcls = jnp.sum(jnp.where(jnp.logical_and(is_class, ends <= start), 1.0, 0.0), axis=0, keepdims=True)
    last = jnp.max(jnp.where(valid, tcls, 0.0), axis=1, keepdims=True)
    tcls = jnp.where(valid, tcls, last)
    grp = functools.reduce(jnp.add, [jnp.where(tcls >= g * N_PAIRS, 1.0, 0.0) for g in range(1, N_EXPERT_GROUPS)])
    pair = tcls - grp * N_PAIRS
    slot_a = functools.reduce(jnp.add, [jnp.where(pair == k, float(PAIR_A[k]), 0.0) for k in range(N_PAIRS)])
    slot_b = functools.reduce(jnp.add, [jnp.where(pair == k, float(PAIR_B[k]), 0.0) for k in range(N_PAIRS)])
    crow_t = lax.broadcasted_iota(jnp.int32, (CLASS_PAD, LANES), 0).astype(F32)
    real_end = jnp.sum(jnp.where(crow_t == tcls, offs + carry, 0.0), axis=0, keepdims=True)
    n_rows = jnp.where(valid, jnp.clip(real_end - start, 0.0, float(TM_MOE)), 0.0)
    rows = [grp * EXPERTS_PER_GROUP + slot_a, grp * EXPERTS_PER_GROUP + slot_b, n_rows]
    tile_ref[...] = jnp.concatenate(rows + [jnp.zeros((8 - len(rows), LANES), F32)], axis=0).astype(jnp.int32)


def _sort_call(rt, tri, low):
    return pl.pallas_call(
        _sort_kernel,
        out_shape=(jax.ShapeDtypeStruct((1, T_ALL), jnp.int32),
                   jax.ShapeDtypeStruct((8, LANES), jnp.int32)),
        name="route_sort",
    )(rt, tri, low)


def _moe_kernel(tile_ref, pos_ref, ha_hbm, wg_hbm, wu_hbm, wd_hbm, out_ref, hbuf, ybuf, src_ref, ssem,
                stg_g, stg_u, stg_d, act_g, act_u, act_d, wsem, ha_ref, hsem, staged_ref, *, layer):
    j = pl.program_id(0)
    slot = j % 2

    def n_rows(tile):
        inside = jnp.logical_and(tile >= 0, tile < N_TILES)
        return jnp.where(inside, tile_ref[TILE_ROWS, jnp.clip(tile, 0, N_TILES - 1)], 0)

    def token_of(tile, n, r, tail):
        return src_ref[tile * TM_MOE + (jnp.minimum(r, n - 1) if tail else r)]

    def scatter_copy(tile, n, buf_slot, r, tail):
        tok = token_of(tile, n, r, tail)
        dst = jnp.where(r < n, tok, T_ALL + buf_slot * TM_MOE + r) if tail else tok
        return pltpu.make_async_copy(ybuf.at[buf_slot, r], out_ref.at[dst], ssem.at[buf_slot])

    def for_groups(n, fn):
        full = lax.shift_right_logical(n, 3)

        def body(g, c):
            for k in range(COPY_GROUP):
                fn(g, k, False)
            return c
        lax.fori_loop(0, full, body, 0)

        @pl.when(jnp.bitwise_and(n, COPY_GROUP - 1) != 0)
        def _():
            for k in range(COPY_GROUP):
                fn(full, k, True)

    def retire(tile, buf_slot):
        n = n_rows(tile)
        one = scatter_copy(tile, n, buf_slot, 0, False)

        def body(g, c):
            for _ in range(COPY_GROUP):
                one.wait()
            return c
        lax.fori_loop(0, lax.shift_right_logical(n + COPY_GROUP - 1, 3), body, 0)

    def expert_of(x, tile):
        return tile_ref[TILE_SLOT_A + x, tile]

    def weight_copies(x, e):
        return [pltpu.make_async_copy(src.at[layer, e], stg.at[x], wsem.at[x])
                for src, stg in ((wg_hbm, stg_g), (wu_hbm, stg_u), (wd_hbm, stg_d))]

    @pl.when(j == 0)
    def _():
        load_inputs = pltpu.make_async_copy(ha_hbm, ha_ref, hsem)
        load_inputs.start()
        for x in range(2):
            for c in weight_copies(x, expert_of(x, 0)):
                c.start()
            staged_ref[x] = expert_of(x, 0)

        def place(t, c):
            src_ref[pos_ref[t]] = t
            return c
        lax.fori_loop(0, T_ALL, place, 0, unroll=8)
        load_inputs.wait()
        hbuf[...] = jnp.zeros(hbuf.shape, F32)
        ybuf[...] = jnp.zeros(ybuf.shape, F32)
        for s in range(2):
            park = pltpu.make_async_copy(ybuf.at[s], out_ref.at[pl.ds(T_ALL + s * TM_MOE, TM_MOE)], ssem.at[s])
            park.start()
            park.wait()

    @pl.when(n_rows(j - 2) > 0)
    def _():
        retire(j - 2, slot)

    def run_experts(rows):
        rec = hbuf[:rows // SUBLANES].reshape(rows, HA_W)
        h = rec[:, :D_MODEL].astype(BF16)

        def expert(x):
            a = _dot(h, act_g[x])
            u = _dot(h, act_u[x])
            w = rec[:, D_MODEL + 1 + x:D_MODEL + 2 + x]
            return _dot(((a * _sigmoid(a)) * u * w).astype(BF16), act_d[x])

        ybuf[slot, :rows] = (expert(0) + expert(1)).reshape(rows, D_CHUNKS, LANES)

    @pl.when(n_rows(j) > 0)
    def _():
        for x in range(2):
            prev = jnp.maximum(j - 1, 0)

            @pl.when(jnp.logical_or(j == 0, expert_of(x, j) != expert_of(x, prev)))
            def _():
                for c in weight_copies(x, expert_of(x, j)):
                    c.wait()
                act_g[x] = stg_g[x].astype(BF16)
                act_u[x] = stg_u[x].astype(BF16)
                act_d[x] = stg_d[x].astype(BF16)

        for x in range(2):
            cur = expert_of(x, j)

            def expert_or_cur(tile):
                return jnp.where(n_rows(tile) > 0, expert_of(x, jnp.minimum(tile, N_TILES - 1)), cur)
            e1, e2 = expert_or_cur(j + 1), expert_or_cur(j + 2)
            target = jnp.where(e1 != cur, e1, e2)

            @pl.when(jnp.logical_and(target != cur, target != staged_ref[x]))
            def _():
                for c in weight_copies(x, target):
                    c.start()
                staged_ref[x] = target

        n = n_rows(j)

        def gather_row(g, k, tail):
            tok = token_of(j, n, g * COPY_GROUP + k, tail)
            grp, sub = lax.shift_right_logical(tok, 3), jnp.bitwise_and(tok, SUBLANES - 1)
            hbuf[g, pl.ds(k, 1), :] = ha_ref[grp, pl.ds(sub, 1), :]
        for_groups(n, gather_row)

        @pl.when(n > SMALL_TILE)
        def _():
            run_experts(TM_MOE)

        @pl.when(n <= SMALL_TILE)
        def _():
            run_experts(SMALL_TILE)

        for_groups(n, lambda g, k, tail: scatter_copy(j, n, slot, g * COPY_GROUP + k, tail).start(priority=k % 2))

    @pl.when(j == N_TILES - 1)
    def _():
        @pl.when(n_rows(j - 1) > 0)
        def _():
            retire(j - 1, 1 - slot)

        @pl.when(n_rows(j) > 0)
        def _():
            retire(j, slot)


def _moe_call(layer, tiles, pos, ha, wge, wue, wde):
    up = (2, D_MODEL, D_EXPERT)
    down = (2, D_EXPERT, D_MODEL)
    grid_spec = pltpu.PrefetchScalarGridSpec(
        num_scalar_prefetch=2,
        grid=(N_TILES,),
        in_specs=[pl.BlockSpec(memory_space=pl.ANY)] * 4,
        out_specs=pl.BlockSpec(memory_space=pl.ANY),
        scratch_shapes=[
            pltpu.VMEM((TM_MOE // SUBLANES, SUBLANES, HA_W), F32),
            pltpu.VMEM((2, TM_MOE, D_CHUNKS, LANES), F32),
            pltpu.SMEM((R_ROWS,), jnp.int32),
            pltpu.SemaphoreType.DMA((2,)),
            pltpu.VMEM(up, F32), pltpu.VMEM(up, F32), pltpu.VMEM(down, F32),
            pltpu.VMEM(up, BF16), pltpu.VMEM(up, BF16), pltpu.VMEM(down, BF16),
            pltpu.SemaphoreType.DMA((2,)),
            pltpu.VMEM((T_ALL // SUBLANES, SUBLANES, HA_W), F32),
            pltpu.SemaphoreType.DMA(()),
            pltpu.SMEM((2,), jnp.int32),
        ],
    )
    return pl.pallas_call(
        functools.partial(_moe_kernel, layer=layer),
        out_shape=jax.ShapeDtypeStruct((T_ALL + N_PARK, D_CHUNKS, LANES), F32),
        grid_spec=grid_spec,
        compiler_params=pltpu.CompilerParams(dimension_semantics=("arbitrary",), vmem_limit_bytes=V7X_VMEM_LIMIT,
                                             has_side_effects=True),
        name="moe_pairs",
    )(tiles, pos,
      ha.reshape(T_ALL // SUBLANES, SUBLANES, HA_W), wge, wue, wde)


def _post_kernel(x1_ref, moe_ref, mod_ref, l2g_ref, l2b_ref, oc_ref, ol_ref):
    m = mod_ref[0, 0]
    moe = moe_ref[...].reshape(TM, D_MODEL)
    y = _ln(ALPHA * x1_ref[...] + m[5:6] * moe) * l2g_ref[0] + l2b_ref[0]
    i = pl.program_id(0)

    @pl.when(i < N_CTX_TILES)
    def _():
        oc_ref[...] = y

    @pl.when(i >= N_CTX_TILES)
    def _():
        ol_ref[...] = y


def _post_call(layer, x1, moe, mod, l2g, l2b):
    tok = lambda i: (i, 0)
    per_layer = lambda i: (layer, 0, 0)
    return pl.pallas_call(
        _post_kernel,
        out_shape=(jax.ShapeDtypeStruct((T_CTX, D_MODEL), F32), jax.ShapeDtypeStruct((T_LAT, D_MODEL), F32)),
        grid=(T_ALL // TM,),
        in_specs=[
            pl.BlockSpec((TM, D_MODEL), tok),
            pl.BlockSpec((TM, D_CHUNKS, LANES), lambda i: (i, 0, 0)),
            pl.BlockSpec((1, 1, N_MOD, D_MODEL), lambda i: (layer, _cond_row(i, TM), 0, 0)),
            pl.BlockSpec((1, 1, D_MODEL), per_layer),
            pl.BlockSpec((1, 1, D_MODEL), per_layer),
        ],
        out_specs=tuple(_group_specs()),
        compiler_params=_cparams("arbitrary"),
        name="post_moe",
    )(x1, moe, mod, l2g, l2b)


def _rope_tables():
    pos = np.arange(DEC_SEQ)
    quarter = HEAD_DIM // 4
    inv = ROPE_THETA ** (-np.arange(quarter, dtype=np.float64) / quarter)
    ang_r = (pos // GRID_W)[:, None] * inv[None, :]
    ang_c = (pos % GRID_W)[:, None] * inv[None, :]
    cos = np.concatenate([np.cos(ang_r)] * 2 + [np.cos(ang_c)] * 2, axis=-1)
    sin = np.concatenate([-np.sin(ang_r), np.sin(ang_r), -np.sin(ang_c), np.sin(ang_c)], axis=-1)
    return (jnp.asarray(np.tile(cos, (1, N_Q_HEADS)).astype(np.float32)),
            jnp.asarray(np.tile(sin, (1, N_Q_HEADS)).astype(np.float32)))


def _dft_mats(n, scale):
    j = np.arange(n)
    ang = ((j[:, None] * j[None, :]) % n) * (2 * np.pi / n)
    return np.cos(ang) * scale, np.sin(ang) * scale


def _block_diag(m, reps):
    return np.kron(np.eye(reps), m)


def _const_bf16(a):
    return jnp.asarray(np.asarray(a, np.float32)).astype(BF16)


def _const_01(a):
    return jnp.asarray(np.asarray(a, np.float32).astype(BF16))


def kernel(x_prompt, x_sample, cache_k, cache_v, c, c_ctx, w_in, q_norm_g, k_norm_g, w_proj_att,
           w_proj_fourier, w_proj_chunk, w_out, chunk_ln_g, chunk_ln_b, chunk_ws, chunk_bs, w_ada, b_ada,
           ln1_g, ln1_b, ln2_g, ln2_b, w_router, router_bias, w_gate_e, w_up_e, w_down_e):
    x = (x_prompt.reshape(T_CTX, D_MODEL), x_sample.reshape(T_LAT, D_MODEL))
    cond = jnp.concatenate([c_ctx[None, :], c, jnp.zeros((N_COND - 1 - DEC_BATCH, D_MODEL), F32)], axis=0)
    mod = _mod_call(cond, w_ada, b_ada).reshape(DEPTH, N_COND, N_MOD, D_MODEL)

    cos_t, sin_t = _rope_tables()
    bd_heads = _const_01(_block_diag(np.ones((HEAD_DIM, HEAD_DIM)), N_Q_HEADS))
    c64, s64 = _dft_mats(FOURIER_GROUP_W, 1.0)
    bdc = _const_bf16(_block_diag(c64, FOURIER_W // FOURIER_GROUP_W))
    bds = _const_bf16(_block_diag(s64, FOURIER_W // FOURIER_GROUP_W))
    dft = {}
    for seq in (SEQ, DEC_SEQ):
        cs, ss = _dft_mats(seq, 1.0 / math.sqrt(seq * FOURIER_GROUP_W))
        dft[seq] = _const_bf16(np.concatenate([cs, -ss], axis=1))
    ctx_k = cache_k.reshape(DEC_BATCH, DEPTH, PAST_LEN, KV_W)
    ctx_v = cache_v.reshape(DEC_BATCH, DEPTH, PAST_LEN, KV_W)
    wr = jnp.pad(w_router, ((0, 0), (0, ROUTER_PAD - N_EXPERTS)))
    wr_hi = wr.astype(BF16)
    wr_lo = (wr - wr_hi.astype(F32)).astype(BF16)
    bias_t = jnp.broadcast_to(router_bias[:, None], (N_EXPERTS, TM))
    tri = _const_01(np.triu(np.ones((SORT_BLK, SORT_BLK)), 1))
    low = _const_01(np.tril(np.ones((CLASS_PAD, CLASS_PAD)), -1))

    w_in_b = w_in.astype(BF16)
    wpa, wpf, wpc, wo = (w.astype(BF16) for w in (w_proj_att, w_proj_fourier, w_proj_chunk, w_out))
    wge, wue, wde = w_gate_e, w_up_e, w_down_e
    qg = jnp.tile(q_norm_g, (1, N_Q_HEADS))[:, None, :]
    kg = jnp.tile(k_norm_g, (1, N_KV_HEADS))[:, None, :]
    ws = chunk_ws.astype(BF16)
    bsm = jnp.repeat(jnp.swapaxes(chunk_bs, 1, 2), CHUNK_GROUP_W, axis=2)
    row3 = lambda a: a[:, None, :]

    new_k = jnp.zeros((BATCH, DEPTH, SEQ, KV_W), F32)
    new_v = jnp.zeros((BATCH, DEPTH, SEQ, KV_W), F32)
    proj_args = (mod, w_in_b, qg, kg, cos_t, sin_t, bd_heads)
    for l in range(DEPTH):
        if l == 0:
            mix, new_k, new_v = _proj_call(l, x, *proj_args, new_k, new_v)
        else:
            mix, new_k, new_v, *x = _proj_call(l, None, *proj_args, new_k, new_v,
                                               prev=(x1, moe, row3(ln2_g), row3(ln2_b)))
        mixer_args = (bdc, bds, ws, bsm, row3(chunk_ln_g), row3(chunk_ln_b))
        mixed_ctx = _mixer_call(l, mix, None, None, dft[SEQ], *mixer_args, latent=False)
        mixed_lat = _mixer_call(l, mix, ctx_k, ctx_v, dft[DEC_SEQ], *mixer_args, latent=True)
        x1, ha, rt = _merge_call(l, x, mixed_ctx, mixed_lat, mod, w_in_b, wpa, wpf, wpc, wo,
                                 row3(ln1_g), row3(ln1_b), wr_hi, wr_lo, bias_t)
        pos2d, tiles = _sort_call(rt, tri, low)
        moe = _moe_call(l, tiles, pos2d[0], ha, wge, wue, wde)
    x = _post_call(DEPTH - 1, x1, moe, mod, row3(ln2_g), row3(ln2_b))

    y_prompt = x[0].reshape(BATCH, SEQ, D_MODEL)
    y_sample = x[1].reshape(DEC_BATCH, DEC_SEQ, D_MODEL)
    cache_shape = (BATCH, DEPTH, SEQ, N_KV_HEADS, HEAD_DIM)
    return (y_prompt, y_sample, new_k.reshape(cache_shape), new_v.reshape(cache_shape))
```

```python
import functools
import math

import jax
import jax.numpy as jnp
import numpy as np
from jax import lax
from jax.experimental import pallas as pl
from jax.experimental.pallas import tpu as pltpu

F32 = jnp.float32
BF16 = jnp.bfloat16

D_MODEL = 1024
BATCH = 16
SEQ = 256
DEPTH = 2
DEC_BATCH = 2
DEC_SEQ = 1024
PAST_LEN = 256
GRID_W = 64
N_Q_HEADS = 8
N_KV_HEADS = 2
HEAD_DIM = 64
Q_PER_KV = N_Q_HEADS // N_KV_HEADS
ATT_W = N_Q_HEADS * HEAD_DIM
KV_W = N_KV_HEADS * HEAD_DIM
ROPE_THETA = 10000.0
FOURIER_GROUP_W = 64
FOURIER_W = 256
N_CHUNK_GROUPS = 4
CHUNK_GROUP_W = 64
CHUNK_W = 256
CHUNK = 128
N_EXPERTS = 16
N_EXPERT_GROUPS = 4
EXPERTS_PER_GROUP = 4
D_EXPERT = 512
ALPHA = (2 * DEPTH) ** 0.25
LN_EPS = 1e-6
RMS_EPS = 1e-6

LANES = 128
SUBLANES = 8
T_CTX = BATCH * SEQ
T_LAT = DEC_BATCH * DEC_SEQ
T_ALL = T_CTX + T_LAT
IN_W = ATT_W + 2 * KV_W + FOURIER_W + 2 * CHUNK_W + 3 * D_MODEL
MIX_W = ATT_W + 2 * KV_W + FOURIER_W + 2 * CHUNK_W
N_MOD = 6
N_COND = 8
ROUTER_PAD = LANES
TM = 512
TQ = 512
MERGE_SUBS = (256, 256)
PROJ_SUB = 256
V7X_VMEM_LIMIT = 56 * 1024 * 1024

PAIR_A = (0, 0, 0, 1, 1, 3)
PAIR_B = (1, 2, 3, 3, 2, 2)
N_PAIRS = len(PAIR_A)
N_CLASSES = N_EXPERT_GROUPS * N_PAIRS
CLASS_PAD = 32
ROUTE_ROWS = 8
TM_MOE = 256
N_TILES = T_ALL // TM_MOE + N_CLASSES
R_ROWS = N_TILES * TM_MOE
SORT_BLK = 512
D_CHUNKS = D_MODEL // LANES
HA_W = D_MODEL + LANES
TILE_SLOT_A, TILE_SLOT_B, TILE_ROWS = 0, 1, 2
SMALL_TILE = 64
N_PARK = 2 * TM_MOE
COPY_GROUP = 8


def _cparams(*sem):
    return pltpu.CompilerParams(dimension_semantics=sem, vmem_limit_bytes=V7X_VMEM_LIMIT)


def _cond_row(i, tm):
    n_ctx = T_CTX // tm
    return jnp.where(i < n_ctx, 0, 1 + (i - n_ctx) // (DEC_SEQ // tm))


N_CTX_TILES = T_CTX // TM


def _group_specs():
    return [pl.BlockSpec((TM, D_MODEL), lambda i: (jnp.minimum(i, N_CTX_TILES - 1), 0)),
            pl.BlockSpec((TM, D_MODEL), lambda i: (jnp.maximum(i - N_CTX_TILES, 0), 0))]


def _group_tile(ctx_ref, lat_ref):
    return jnp.where(pl.program_id(0) < N_CTX_TILES, ctx_ref[...], lat_ref[...])


def _ln(x):
    mu = jnp.mean(x, axis=-1, keepdims=True)
    xc = x - mu
    var = jnp.mean(xc * xc, axis=-1, keepdims=True)
    return xc * lax.rsqrt(var + LN_EPS)


def _split_bf16(x):
    hi = x.astype(BF16)
    lo = (x - hi.astype(F32)).astype(BF16)
    return hi, lo


def _dot(a, b):
    return jnp.dot(a, b, preferred_element_type=F32)


def _sigmoid(x):
    return 1.0 / (1.0 + jnp.exp(-x))


def _gelu_tanh(x):
    c = np.float32(np.sqrt(2 / np.pi))
    return x * (0.5 * (1.0 + jnp.tanh(c * (x + 0.044715 * (x * x * x)))))


def _mod_kernel(c_ref, w_ref, b_ref, o_ref):
    c = c_ref[...]
    s = c * _sigmoid(c)
    o_ref[0] = _dot(s.astype(BF16), w_ref[0].astype(BF16)) + b_ref[0]


def _mod_call(cond, w_ada, b_ada):
    tn = 1536
    n = N_MOD * D_MODEL
    return pl.pallas_call(
        _mod_kernel,
        out_shape=jax.ShapeDtypeStruct((DEPTH, N_COND, n), F32),
        grid=(DEPTH, n // tn),
        in_specs=[
            pl.BlockSpec((N_COND, D_MODEL), lambda l, j: (0, 0)),
            pl.BlockSpec((1, D_MODEL, tn), lambda l, j: (l, 0, j)),
            pl.BlockSpec((1, 1, tn), lambda l, j: (l, 0, j)),
        ],
        out_specs=pl.BlockSpec((1, N_COND, tn), lambda l, j: (l, 0, j)),
        compiler_params=_cparams("arbitrary", "arbitrary"),
        name="adaln_mod",
    )(cond, w_ada, b_ada.reshape(DEPTH, 1, n))


def _head_rms(q, bd, g):
    hi, lo = _split_bf16(q * q)
    ssum = _dot(hi, bd) + _dot(lo, bd)
    return q * lax.rsqrt(ssum * (1.0 / HEAD_DIM) + RMS_EPS) * g


def _rope(x, cos, sin_signed):
    w = x.shape[-1]
    lane = lax.broadcasted_iota(jnp.int32, x.shape, 1)
    swapped = jnp.where((lane % 32) < 16, pltpu.roll(x, w - 16, 1), pltpu.roll(x, 16, 1))
    return x * cos + swapped * sin_signed


def _proj_kernel(*refs, after_moe):
    if after_moe:
        (x1_ref, moe_ref, modp_ref, l2g_ref, l2b_ref, mod_ref, w_ref, qg_ref, kg_ref, cos_ref, sin_ref, bd_ref,
         kc_in, vc_in, o_ref, kc_ref, vc_ref, xoc_ref, xol_ref) = refs
    else:
        (xc_ref, xl_ref, mod_ref, w_ref, qg_ref, kg_ref, cos_ref, sin_ref, bd_ref,
         kc_in, vc_in, o_ref, kc_ref, vc_ref) = refs
    del kc_in, vc_in
    m = mod_ref[0, 0]
    in_ctx = pl.program_id(0) < N_CTX_TILES
    bd = bd_ref[...]
    sub_rows = [slice(s * PROJ_SUB, (s + 1) * PROJ_SUB) for s in range(TM // PROJ_SUB)]
    projs = []
    xs = []
    for rows in sub_rows:
        if after_moe:
            moe = moe_ref[rows].reshape(PROJ_SUB, D_MODEL)
            x = _ln(ALPHA * x1_ref[rows, :] + modp_ref[0, 0][5:6] * moe) * l2g_ref[0] + l2b_ref[0]
            xs.append(x)
        else:
            x = jnp.where(in_ctx, xc_ref[rows, :], xl_ref[rows, :])
        h = _ln(x) * (1.0 + m[1:2]) + m[0:1]
        proj = _dot(h.astype(BF16), w_ref[...])
        o_ref[rows, ATT_W + KV_W:] = proj[:, ATT_W + KV_W:]
        projs.append(proj[:, :ATT_W + 2 * KV_W])
    keys = []
    for rows, proj in zip(sub_rows, projs):
        qn = _head_rms(proj[:, :ATT_W], bd, qg_ref[0])
        kn = _head_rms(proj[:, ATT_W:ATT_W + KV_W], bd[:KV_W, :KV_W], kg_ref[0])
        cos = cos_ref[rows, :]
        sin = sin_ref[rows, :]
        o_ref[rows, :ATT_W] = jnp.where(in_ctx, qn, _rope(qn, cos, sin))
        o_ref[rows, ATT_W:ATT_W + KV_W] = jnp.where(in_ctx, kn, _rope(kn, cos[:, :KV_W], sin[:, :KV_W]))
        keys.append(kn)

    @pl.when(in_ctx)
    def _():
        for s in range(len(sub_rows)):
            kc_ref[s] = keys[s]
            vc_ref[s] = projs[s][:, ATT_W + KV_W:]
        if after_moe:
            for rows, x in zip(sub_rows, xs):
                xoc_ref[rows, :] = x

    if after_moe:
        @pl.when(jnp.logical_not(in_ctx))
        def _():
            for rows, x in zip(sub_rows, xs):
                xol_ref[rows, :] = x


def _proj_call(layer, x, mod, w_in_b, qg, kg, cos_t, sin_t, bd, k_cache, v_cache, prev=None):
    assert PROJ_SUB == SEQ
    n_ctx = T_CTX // TM
    per_seq = DEC_SEQ // TM
    after_moe = prev is not None

    def rope_idx(i):
        return (jnp.where(i < n_ctx, 0, (i - n_ctx) % per_seq), 0)

    cache_spec = pl.BlockSpec((TM // SEQ, None, SEQ, KV_W), lambda i: (jnp.minimum(i, n_ctx - 1), layer, 0, 0))
    cache_shape = jax.ShapeDtypeStruct((BATCH, DEPTH, SEQ, KV_W), F32)
    out_shape = [jax.ShapeDtypeStruct((T_ALL, MIX_W), F32), cache_shape, cache_shape]
    out_specs = [pl.BlockSpec((TM, MIX_W), lambda i: (i, 0)), cache_spec, cache_spec]
    if after_moe:
        x1, moe, l2g, l2b = prev
        prev_layer = lambda i: (layer - 1, 0, 0)
        x_specs = [
            pl.BlockSpec((TM, D_MODEL), lambda i: (i, 0)),
            pl.BlockSpec((TM, D_CHUNKS, LANES), lambda i: (i, 0, 0)),
            pl.BlockSpec((1, 1, N_MOD, D_MODEL), lambda i: (layer - 1, _cond_row(i, TM), 0, 0)),
            pl.BlockSpec((1, 1, D_MODEL), prev_layer),
            pl.BlockSpec((1, 1, D_MODEL), prev_layer),
        ]
        x_args = [x1, moe, mod, l2g, l2b]
        out_shape += [jax.ShapeDtypeStruct((T_CTX, D_MODEL), F32), jax.ShapeDtypeStruct((T_LAT, D_MODEL), F32)]
        out_specs += _group_specs()
    else:
        x_specs, x_args = _group_specs(), list(x)
    n_in = len(x_args) + 9
    return pl.pallas_call(
        functools.partial(_proj_kernel, after_moe=after_moe),
        out_shape=tuple(out_shape),
        grid=(T_ALL // TM,),
        in_specs=x_specs + [
            pl.BlockSpec((1, 1, N_MOD, D_MODEL), lambda i: (layer, _cond_row(i, TM), 0, 0)),
            pl.BlockSpec((None, D_MODEL, MIX_W), lambda i: (layer, 0, 0)),
            pl.BlockSpec((1, 1, ATT_W), lambda i: (layer, 0, 0)),
            pl.BlockSpec((1, 1, KV_W), lambda i: (layer, 0, 0)),
            pl.BlockSpec((TM, ATT_W), rope_idx),
            pl.BlockSpec((TM, ATT_W), rope_idx),
            pl.BlockSpec((ATT_W, ATT_W), lambda i: (0, 0)),
            pl.BlockSpec(memory_space=pl.ANY),
            pl.BlockSpec(memory_space=pl.ANY),
        ],
        out_specs=tuple(out_specs),
        input_output_aliases={n_in - 2: 1, n_in - 1: 2},
        compiler_params=_cparams("arbitrary"),
        name="in_proj",
    )(*x_args, mod, w_in_b, qg, kg, cos_t, sin_t, bd, k_cache, v_cache)


def _attention_tile(q, k_parts, v_parts, phase_major):
    def kv_of(h, parts):
        g = h // Q_PER_KV
        return [p[:, g * HEAD_DIM:(g + 1) * HEAD_DIM] for p in parts]

    def scores(h):
        qh = q[:, h * HEAD_DIM:(h + 1) * HEAD_DIM]
        return [lax.dot_general(qh, k, (((1,), (1,)), ((), ())), preferred_element_type=F32)
                for k in kv_of(h, k_parts)]

    if phase_major:
        all_ss = [scores(h) for h in range(N_Q_HEADS)]
        all_m = [functools.reduce(jnp.maximum, [jnp.max(s, axis=-1, keepdims=True) for s in ss]) for ss in all_ss]
        all_es = [[jnp.exp(s - m) for s in ss] for ss, m in zip(all_ss, all_m)]
        all_den = [functools.reduce(jnp.add, [jnp.sum(e, axis=-1, keepdims=True) for e in es]) for es in all_es]
        outs = [functools.reduce(jnp.add, [_dot(e.astype(BF16), v) for e, v in zip(es, kv_of(h, v_parts))])
                * (1.0 / den) for h, (es, den) in enumerate(zip(all_es, all_den))]
        return jnp.concatenate(outs, axis=-1)
    outs = []
    ss_next = scores(0)
    for h in range(N_Q_HEADS):
        ss = ss_next
        if h + 1 < N_Q_HEADS:
            ss_next = scores(h + 1)
        m = functools.reduce(jnp.maximum, [jnp.max(s, axis=-1, keepdims=True) for s in ss])
        es = [jnp.exp(s - m) for s in ss]
        denom = functools.reduce(jnp.add, [jnp.sum(e, axis=-1, keepdims=True) for e in es])
        o = functools.reduce(jnp.add, [_dot(e.astype(BF16), v) for e, v in zip(es, kv_of(h, v_parts))])
        outs.append(o * (1.0 / denom))
    return jnp.concatenate(outs, axis=-1)


def _mixer_kernel(*refs, seq, latent):
    if latent:
        (mix_ref, ck_ref, cv_ref, dft_ref, bdc_ref, bds_ref, ws_ref, bsm_ref, clg_ref, clb_ref, o_ref) = refs
    else:
        (mix_ref, dft_ref, bdc_ref, bds_ref, ws_ref, bsm_ref, clg_ref, clb_ref, o_ref) = refs

    k_new = mix_ref[:, ATT_W:ATT_W + KV_W].astype(BF16)
    v_new = mix_ref[:, ATT_W + KV_W:ATT_W + 2 * KV_W].astype(BF16)
    if latent:
        k_parts = [ck_ref[...].astype(BF16), k_new]
        v_parts = [cv_ref[...].astype(BF16), v_new]
    else:
        k_parts, v_parts = [k_new], [v_new]

    tq = min(seq, TQ)

    def q_tile(t, carry):
        r0 = pl.multiple_of(t * tq, tq)
        q = (mix_ref[pl.ds(r0, tq), :ATT_W] * (HEAD_DIM ** -0.5)).astype(BF16)
        o_ref[pl.ds(r0, tq), :ATT_W] = _attention_tile(q, k_parts, v_parts, phase_major=not latent).astype(BF16)
        return carry

    lax.fori_loop(0, seq // tq, q_tile, 0)

    f_hi, f_lo = _split_bf16(mix_ref[:, ATT_W + 2 * KV_W:ATT_W + 2 * KV_W + FOURIER_W])
    bdc = bdc_ref[...]
    bds = bds_ref[...]
    y = jnp.concatenate([_dot(f_hi, bdc) + _dot(f_lo, bdc), _dot(f_hi, bds) + _dot(f_lo, bds)], axis=0)
    y_hi, y_lo = _split_bf16(y)
    dft = dft_ref[...]
    four = _dot(dft, y_hi) + _dot(dft, y_lo)
    o_ref[:, ATT_W:ATT_W + FOURIER_W] = four.astype(BF16)

    c0 = ATT_W + 2 * KV_W + FOURIER_W
    u = _gelu_tanh(mix_ref[:, c0:c0 + CHUNK_W])
    vn = (_ln(_gelu_tanh(mix_ref[:, c0 + CHUNK_W:c0 + 2 * CHUNK_W])) * clg_ref[0] + clb_ref[0]).astype(BF16)
    lane = lax.broadcasted_iota(jnp.int32, (CHUNK, CHUNK_W), 1)
    bsm = bsm_ref[...]
    for c in range(seq // CHUNK):
        vc = vn[c * CHUNK:(c + 1) * CHUNK]
        sv = bsm
        for g in range(N_CHUNK_GROUPS):
            vg = jnp.where(lane // CHUNK_GROUP_W == g, vc, jnp.zeros_like(vc))
            sv = sv + _dot(ws_ref[g], vg)
        o_ref[c * CHUNK:(c + 1) * CHUNK, ATT_W + FOURIER_W:] = (u[c * CHUNK:(c + 1) * CHUNK] * sv).astype(BF16)


def _mixer_call(layer, mix, ctx_k, ctx_v, dft, bdc, bds, ws, bsm, clg, clb, *, latent):
    seq = DEC_SEQ if latent else SEQ
    nb = DEC_BATCH if latent else BATCH
    row0 = (T_CTX // seq) if latent else 0
    const2 = lambda b: (0, 0)
    per_layer = lambda b: (layer, 0, 0)
    in_specs = [pl.BlockSpec((seq, MIX_W), lambda b: (row0 + b, 0))]
    args = [mix]
    if latent:
        in_specs += [pl.BlockSpec((None, None, PAST_LEN, KV_W), lambda b: (b, layer, 0, 0))] * 2
        args += [ctx_k, ctx_v]
    in_specs += [
        pl.BlockSpec((seq, 2 * seq), const2),
        pl.BlockSpec((FOURIER_W, FOURIER_W), const2),
        pl.BlockSpec((FOURIER_W, FOURIER_W), const2),
        pl.BlockSpec((None, N_CHUNK_GROUPS, CHUNK, CHUNK), lambda b: (layer, 0, 0, 0)),
        pl.BlockSpec((None, CHUNK, CHUNK_W), per_layer),
        pl.BlockSpec((1, 1, CHUNK_W), per_layer),
        pl.BlockSpec((1, 1, CHUNK_W), per_layer),
    ]
    args += [dft, bdc, bds, ws, bsm, clg, clb]
    return pl.pallas_call(
        functools.partial(_mixer_kernel, seq=seq, latent=latent),
        out_shape=jax.ShapeDtypeStruct((nb * seq, D_MODEL), BF16),
        grid=(nb,),
        in_specs=in_specs,
        out_specs=pl.BlockSpec((seq, D_MODEL), lambda b: (b, 0)),
        compiler_params=_cparams("arbitrary"),
        name="mixer_latent" if latent else "mixer_context",
    )(*args)


def _route(logits, bias_t):
    lt = logits.T[:N_EXPERTS]
    ex = jnp.exp(lt - jnp.max(lt, axis=0, keepdims=True))
    probs = ex / jnp.sum(ex, axis=0, keepdims=True)
    sel = probs + bias_t
    p = [probs[e:e + 1] for e in range(N_EXPERTS)]
    s = [sel[e:e + 1] for e in range(N_EXPERTS)]
    n = EXPERTS_PER_GROUP
    scores = []
    for g in range(N_EXPERT_GROUPS):
        pair = [s[g * n + a] + s[g * n + b] for a in range(n) for b in range(a + 1, n)]
        scores.append(functools.reduce(jnp.maximum, pair))
    best = jnp.zeros_like(scores[0], dtype=jnp.int32)
    best_score = scores[0]
    for g in range(1, N_EXPERT_GROUPS):
        better = scores[g] > best_score
        best = jnp.where(better, g, best)
        best_score = jnp.where(better, scores[g], best_score)
    cls = jnp.zeros_like(best_score)
    w_a = jnp.zeros_like(best_score)
    w_b = jnp.zeros_like(best_score)
    for g in range(N_EXPERT_GROUPS):
        in_g = best == g
        chosen = []
        for a in range(n):
            rank = jnp.zeros_like(best)
            for b in range(n):
                if b == a:
                    continue
                ahead = (s[g * n + b] > s[g * n + a]) if b > a else (s[g * n + b] >= s[g * n + a])
                rank = rank + ahead.astype(jnp.int32)
            chosen.append(jnp.logical_and(in_g, rank < 2))
        for k in range(N_PAIRS):
            pa, pb = p[g * n + PAIR_A[k]], p[g * n + PAIR_B[k]]
            hit = jnp.logical_and(chosen[PAIR_A[k]], chosen[PAIR_B[k]])
            wsum = pa + pb
            cls = jnp.where(hit, float(g * N_PAIRS + k), cls)
            w_a = jnp.where(hit, pa / wsum, w_a)
            w_b = jnp.where(hit, pb / wsum, w_b)
    return jnp.concatenate([cls, w_a, w_b, jnp.zeros((ROUTE_ROWS - 3, lt.shape[1]), F32)], axis=0)


def _merge_kernel(xc_ref, xl_ref, mixc_ref, mixl_ref, mod_ref, win_ref, wpa_ref, wpf_ref, wpc_ref, wo_ref,
                  l1g_ref, l1b_ref, wrh_ref, wrl_ref, bias_ref, x1_ref, ha_ref, rt_ref):
    m = mod_ref[0, 0]
    in_ctx = pl.program_id(0) < N_CTX_TILES
    branch_w = ((0, ATT_W, wpa_ref), (ATT_W, FOURIER_W, wpf_ref), (ATT_W + FOURIER_W, CHUNK_W, wpc_ref))
    bounds = np.cumsum((0,) + MERGE_SUBS)
    sub_rows = [slice(int(a), int(b)) for a, b in zip(bounds[:-1], bounds[1:])]
    residual = []
    for rows in sub_rows:
        x = jnp.where(in_ctx, xc_ref[rows, :], xl_ref[rows, :])
        mixed = jnp.where(in_ctx, mixc_ref[rows, :], mixl_ref[rows, :])
        h = (_ln(x) * (1.0 + m[1:2]) + m[0:1]).astype(BF16)
        merged = None
        for b, (c0, width, w_ref) in enumerate(branch_w):
            gate = _sigmoid(_dot(h, win_ref[:, MIX_W + b * D_MODEL:MIX_W + (b + 1) * D_MODEL]))
            term = gate * _dot(mixed[:, c0:c0 + width], w_ref[...])
            merged = term if merged is None else merged + term
        mix = _dot(merged.astype(BF16), wo_ref[...])
        residual.append(ALPHA * x + m[2:3] * mix)
    for rows, pre in zip(sub_rows, residual):
        x1 = _ln(pre) * l1g_ref[0] + l1b_ref[0]
        x1_ref[rows, :] = x1
        h2 = _ln(x1) * (1.0 + m[4:5]) + m[3:4]
        h_hi, h_lo = _split_bf16(h2)
        wrh = wrh_ref[...]
        logits = _dot(h_hi, wrh) + _dot(h_lo, wrh) + _dot(h_hi, wrl_ref[...])
        rt = _route(logits, bias_ref[:, rows])
        rt_ref[:, rows] = rt
        ha_ref[rows, :D_MODEL] = h2
        ha_ref[rows, D_MODEL:] = jnp.concatenate(
            [rt, jnp.zeros((LANES - ROUTE_ROWS, rows.stop - rows.start), F32)], axis=0).T


def _merge_call(layer, x, mixed_ctx, mixed_lat, mod, w_in_b, wpa, wpf, wpc, wo, l1g, l1b, wrh, wrl, bias_t):
    const2 = lambda i: (0, 0)
    per_layer = lambda i: (layer, 0, 0)
    tok = lambda i: (i, 0)
    return pl.pallas_call(
        _merge_kernel,
        out_shape=(jax.ShapeDtypeStruct((T_ALL, D_MODEL), F32),
                   jax.ShapeDtypeStruct((T_ALL, HA_W), F32),
                   jax.ShapeDtypeStruct((ROUTE_ROWS, T_ALL), F32)),
        grid=(T_ALL // TM,),
        in_specs=_group_specs() + _group_specs() + [
            pl.BlockSpec((1, 1, N_MOD, D_MODEL), lambda i: (layer, _cond_row(i, TM), 0, 0)),
            pl.BlockSpec((None, D_MODEL, IN_W), per_layer, pipeline_mode=pl.Buffered(1)),
            pl.BlockSpec((None, ATT_W, D_MODEL), per_layer, pipeline_mode=pl.Buffered(1)),
            pl.BlockSpec((None, FOURIER_W, D_MODEL), per_layer, pipeline_mode=pl.Buffered(1)),
            pl.BlockSpec((None, CHUNK_W, D_MODEL), per_layer, pipeline_mode=pl.Buffered(1)),
            pl.BlockSpec((None, D_MODEL, D_MODEL), per_layer, pipeline_mode=pl.Buffered(1)),
            pl.BlockSpec((1, 1, D_MODEL), per_layer),
            pl.BlockSpec((1, 1, D_MODEL), per_layer),
            pl.BlockSpec((D_MODEL, ROUTER_PAD), const2),
            pl.BlockSpec((D_MODEL, ROUTER_PAD), const2),
            pl.BlockSpec((N_EXPERTS, TM), const2),
        ],
        out_specs=(pl.BlockSpec((TM, D_MODEL), tok),
                   pl.BlockSpec((TM, HA_W), tok),
                   pl.BlockSpec((ROUTE_ROWS, TM), lambda i: (0, i))),
        compiler_params=_cparams("arbitrary"),
        name="merge_route",
    )(*x, mixed_ctx, mixed_lat, mod, w_in_b, wpa, wpf, wpc, wo, l1g, l1b, wrh, wrl, bias_t)


def _sort_kernel(rt_ref, tri_ref, low_ref, pos_ref, tile_ref):
    crow = lax.broadcasted_iota(jnp.int32, (CLASS_PAD, SORT_BLK), 0)
    tri = tri_ref[...]
    n_blk = T_ALL // SORT_BLK
    carry = jnp.zeros((CLASS_PAD, 1), F32)
    ranks = []
    for b in range(n_blk):
        hot = rt_ref[0:1, b * SORT_BLK:(b + 1) * SORT_BLK].astype(jnp.int32) == crow
        hot_f = jnp.where(hot, 1.0, 0.0)
        before = _dot(hot_f.astype(BF16), tri) + carry
        ranks.append(jnp.sum(jnp.where(hot, before, 0.0), axis=0, keepdims=True))
        carry = carry + jnp.sum(hot_f, axis=1, keepdims=True)
    padded = jnp.floor((carry + (TM_MOE - 1.0)) * (1.0 / TM_MOE)) * TM_MOE
    padded = jnp.broadcast_to(padded, (CLASS_PAD, LANES))
    offs = _dot(low_ref[...], padded.astype(BF16))
    for b in range(n_blk):
        hot = rt_ref[0:1, b * SORT_BLK:(b + 1) * SORT_BLK].astype(jnp.int32) == crow
        base = jnp.sum(jnp.where(hot, offs[:, 0:1], 0.0), axis=0, keepdims=True)
        pos_ref[:, b * SORT_BLK:(b + 1) * SORT_BLK] = (base + ranks[b]).astype(jnp.int32)

    start = lax.broadcasted_iota(jnp.int32, (1, LANES), 1).astype(F32) * TM_MOE
    is_class = lax.broadcasted_iota(jnp.int32, (CLASS_PAD, LANES), 0) < N_CLASSES
    ends = jnp.where(is_class, offs + padded, 0.0)
    total = jnp.max(ends, axis=0, keepdims=True)
    valid = start < total
    tcls = jnp.sum(jnp.where(jnp.logical_and(is_class, ends <= start), 1.0, 0.0), axis=0, keepdims=True)
    last = jnp.max(jnp.where(valid, tcls, 0.0), axis=1, keepdims=True)
    tcls = jnp.where(valid, tcls, last)
    grp = functools.reduce(jnp.add, [jnp.where(tcls >= g * N_PAIRS, 1.0, 0.0) for g in range(1, N_EXPERT_GROUPS)])
    pair = tcls - grp * N_PAIRS
    slot_a = functools.reduce(jnp.add, [jnp.where(pair == k, float(PAIR_A[k]), 0.0) for k in range(N_PAIRS)])
    slot_b = functools.reduce(jnp.add, [jnp.where(pair == k, float(PAIR_B[k]), 0.0) for k in range(N_PAIRS)])
    crow_t = lax.broadcasted_iota(jnp.int32, (CLASS_PAD, LANES), 0).astype(F32)
    real_end = jnp.sum(jnp.where(crow_t == tcls, offs + carry, 0.0), axis=0, keepdims=True)
    n_rows = jnp.where(valid, jnp.clip(real_end - start, 0.0, float(TM_MOE)), 0.0)
    rows = [grp * EXPERTS_PER_GROUP + slot_a, grp * EXPERTS_PER_GROUP + slot_b, n_rows]
    tile_ref[...] = jnp.concatenate(rows + [jnp.zeros((8 - len(rows), LANES), F32)], axis=0).astype(jnp.int32)


def _sort_call(rt, tri, low):
    return pl.pallas_call(
        _sort_kernel,
        out_shape=(jax.ShapeDtypeStruct((1, T_ALL), jnp.int32),
                   jax.ShapeDtypeStruct((8, LANES), jnp.int32)),
        name="route_sort",
    )(rt, tri, low)


def _moe_kernel(tile_ref, pos_ref, ha_hbm, wg_hbm, wu_hbm, wd_hbm, out_ref, hbuf, ybuf, src_ref, ssem,
                stg_g, stg_u, stg_d, act_g, act_u, act_d, wsem, ha_ref, hsem, staged_ref, *, layer):
    j = pl.program_id(0)
    slot = j % 2

    def n_rows(tile):
        inside = jnp.logical_and(tile >= 0, tile < N_TILES)
        return jnp.where(inside, tile_ref[TILE_ROWS, jnp.clip(tile, 0, N_TILES - 1)], 0)

    def token_of(tile, n, r, tail):
        return src_ref[tile * TM_MOE + (jnp.minimum(r, n - 1) if tail else r)]

    def scatter_copy(tile, n, buf_slot, r, tail):
        tok = token_of(tile, n, r, tail)
        dst = jnp.where(r < n, tok, T_ALL + buf_slot * TM_MOE + r) if tail else tok
        return pltpu.make_async_copy(ybuf.at[buf_slot, r], out_ref.at[dst], ssem.at[buf_slot])

    def for_groups(n, fn):
        full = lax.shift_right_logical(n, 3)

        def body(g, c):
            for k in range(COPY_GROUP):
                fn(g, k, False)
            return c
        lax.fori_loop(0, full, body, 0)

        @pl.when(jnp.bitwise_and(n, COPY_GROUP - 1) != 0)
        def _():
            for k in range(COPY_GROUP):
                fn(full, k, True)

    def retire(tile, buf_slot):
        n = n_rows(tile)
        one = scatter_copy(tile, n, buf_slot, 0, False)

        def body(g, c):
            for _ in range(COPY_GROUP):
                one.wait()
            return c
        lax.fori_loop(0, lax.shift_right_logical(n + COPY_GROUP - 1, 3), body, 0)

    def expert_of(x, tile):
        return tile_ref[TILE_SLOT_A + x, tile]

    def weight_copies(x, e):
        return [pltpu.make_async_copy(src.at[layer, e], stg.at[x], wsem.at[x])
                for src, stg in ((wg_hbm, stg_g), (wu_hbm, stg_u), (wd_hbm, stg_d))]

    @pl.when(j == 0)
    def _():
        load_inputs = pltpu.make_async_copy(ha_hbm, ha_ref, hsem)
        load_inputs.start()
        for x in range(2):
            for c in weight_copies(x, expert_of(x, 0)):
                c.start()
            staged_ref[x] = expert_of(x, 0)

        def place(t, c):
            src_ref[pos_ref[t]] = t
            return c
        lax.fori_loop(0, T_ALL, place, 0, unroll=8)
        load_inputs.wait()
        hbuf[...] = jnp.zeros(hbuf.shape, F32)
        ybuf[...] = jnp.zeros(ybuf.shape, F32)
        for s in range(2):
            park = pltpu.make_async_copy(ybuf.at[s], out_ref.at[pl.ds(T_ALL + s * TM_MOE, TM_MOE)], ssem.at[s])
            park.start()
            park.wait()

    @pl.when(n_rows(j - 2) > 0)
    def _():
        retire(j - 2, slot)

    def run_experts(rows):
        rec = hbuf[:rows // SUBLANES].reshape(rows, HA_W)
        h = rec[:, :D_MODEL].astype(BF16)

        def expert(x):
            a = _dot(h, act_g[x])
            u = _dot(h, act_u[x])
            w = rec[:, D_MODEL + 1 + x:D_MODEL + 2 + x]
            return _dot(((a * _sigmoid(a)) * u * w).astype(BF16), act_d[x])

        ybuf[slot, :rows] = (expert(0) + expert(1)).reshape(rows, D_CHUNKS, LANES)

    @pl.when(n_rows(j) > 0)
    def _():
        for x in range(2):
            prev = jnp.maximum(j - 1, 0)

            @pl.when(jnp.logical_or(j == 0, expert_of(x, j) != expert_of(x, prev)))
            def _():
                for c in weight_copies(x, expert_of(x, j)):
                    c.wait()
                act_g[x] = stg_g[x].astype(BF16)
                act_u[x] = stg_u[x].astype(BF16)
                act_d[x] = stg_d[x].astype(BF16)

        for x in range(2):
            cur = expert_of(x, j)

            def expert_or_cur(tile):
                return jnp.where(n_rows(tile) > 0, expert_of(x, jnp.minimum(tile, N_TILES - 1)), cur)
            e1, e2 = expert_or_cur(j + 1), expert_or_cur(j + 2)
            target = jnp.where(e1 != cur, e1, e2)

            @pl.when(jnp.logical_and(target != cur, target != staged_ref[x]))
            def _():
                for c in weight_copies(x, target):
                    c.start()
                staged_ref[x] = target

        n = n_rows(j)

        def gather_row(g, k, tail):
            tok = token_of(j, n, g * COPY_GROUP + k, tail)
            grp, sub = lax.shift_right_logical(tok, 3), jnp.bitwise_and(tok, SUBLANES - 1)
            hbuf[g, pl.ds(k, 1), :] = ha_ref[grp, pl.ds(sub, 1), :]
        for_groups(n, gather_row)

        @pl.when(n > SMALL_TILE)
        def _():
            run_experts(TM_MOE)

        @pl.when(n <= SMALL_TILE)
        def _():
            run_experts(SMALL_TILE)

        for_groups(n, lambda g, k, tail: scatter_copy(j, n, slot, g * COPY_GROUP + k, tail).start(priority=k % 2))

    @pl.when(j == N_TILES - 1)
    def _():
        @pl.when(n_rows(j - 1) > 0)
        def _():
            retire(j - 1, 1 - slot)

        @pl.when(n_rows(j) > 0)
        def _():
            retire(j, slot)


def _moe_call(layer, tiles, pos, ha, wge, wue, wde):
    up = (2, D_MODEL, D_EXPERT)
    down = (2, D_EXPERT, D_MODEL)
    grid_spec = pltpu.PrefetchScalarGridSpec(
        num_scalar_prefetch=2,
        grid=(N_TILES,),
        in_specs=[pl.BlockSpec(memory_space=pl.ANY)] * 4,
        out_specs=pl.BlockSpec(memory_space=pl.ANY),
        scratch_shapes=[
            pltpu.VMEM((TM_MOE // SUBLANES, SUBLANES, HA_W), F32),
            pltpu.VMEM((2, TM_MOE, D_CHUNKS, LANES), F32),
            pltpu.SMEM((R_ROWS,), jnp.int32),
            pltpu.SemaphoreType.DMA((2,)),
            pltpu.VMEM(up, F32), pltpu.VMEM(up, F32), pltpu.VMEM(down, F32),
            pltpu.VMEM(up, BF16), pltpu.VMEM(up, BF16), pltpu.VMEM(down, BF16),
            pltpu.SemaphoreType.DMA((2,)),
            pltpu.VMEM((T_ALL // SUBLANES, SUBLANES, HA_W), F32),
            pltpu.SemaphoreType.DMA(()),
            pltpu.SMEM((2,), jnp.int32),
        ],
    )
    return pl.pallas_call(
        functools.partial(_moe_kernel, layer=layer),
        out_shape=jax.ShapeDtypeStruct((T_ALL + N_PARK, D_CHUNKS, LANES), F32),
        grid_spec=grid_spec,
        compiler_params=pltpu.CompilerParams(dimension_semantics=("arbitrary",), vmem_limit_bytes=V7X_VMEM_LIMIT,
                                             has_side_effects=True),
        name="moe_pairs",
    )(tiles, pos,
      ha.reshape(T_ALL // SUBLANES, SUBLANES, HA_W), wge, wue, wde)


def _post_kernel(x1_ref, moe_ref, mod_ref, l2g_ref, l2b_ref, oc_ref, ol_ref):
    m = mod_ref[0, 0]
    moe = moe_ref[...].reshape(TM, D_MODEL)
    y = _ln(ALPHA * x1_ref[...] + m[5:6] * moe) * l2g_ref[0] + l2b_ref[0]
    i = pl.program_id(0)

    @pl.when(i < N_CTX_TILES)
    def _():
        oc_ref[...] = y

    @pl.when(i >= N_CTX_TILES)
    def _():
        ol_ref[...] = y


def _post_call(layer, x1, moe, mod, l2g, l2b):
    tok = lambda i: (i, 0)
    per_layer = lambda i: (layer, 0, 0)
    return pl.pallas_call(
        _post_kernel,
        out_shape=(jax.ShapeDtypeStruct((T_CTX, D_MODEL), F32), jax.ShapeDtypeStruct((T_LAT, D_MODEL), F32)),
        grid=(T_ALL // TM,),
        in_specs=[
            pl.BlockSpec((TM, D_MODEL), tok),
            pl.BlockSpec((TM, D_CHUNKS, LANES), lambda i: (i, 0, 0)),
            pl.BlockSpec((1, 1, N_MOD, D_MODEL), lambda i: (layer, _cond_row(i, TM), 0, 0)),
            pl.BlockSpec((1, 1, D_MODEL), per_layer),
            pl.BlockSpec((1, 1, D_MODEL), per_layer),
        ],
        out_specs=tuple(_group_specs()),
        compiler_params=_cparams("arbitrary"),
        name="post_moe",
    )(x1, moe, mod, l2g, l2b)


def _rope_tables():
    pos = np.arange(DEC_SEQ)
    quarter = HEAD_DIM // 4
    inv = ROPE_THETA ** (-np.arange(quarter, dtype=np.float64) / quarter)
    ang_r = (pos // GRID_W)[:, None] * inv[None, :]
    ang_c = (pos % GRID_W)[:, None] * inv[None, :]
    cos = np.concatenate([np.cos(ang_r)] * 2 + [np.cos(ang_c)] * 2, axis=-1)
    sin = np.concatenate([-np.sin(ang_r), np.sin(ang_r), -np.sin(ang_c), np.sin(ang_c)], axis=-1)
    return (jnp.asarray(np.tile(cos, (1, N_Q_HEADS)).astype(np.float32)),
            jnp.asarray(np.tile(sin, (1, N_Q_HEADS)).astype(np.float32)))


def _dft_mats(n, scale):
    j = np.arange(n)
    ang = ((j[:, None] * j[None, :]) % n) * (2 * np.pi / n)
    return np.cos(ang) * scale, np.sin(ang) * scale


def _block_diag(m, reps):
    return np.kron(np.eye(reps), m)


def _const_bf16(a):
    return jnp.asarray(np.asarray(a, np.float32)).astype(BF16)


def _const_01(a):
    return jnp.asarray(np.asarray(a, np.float32).astype(BF16))


def kernel(x_prompt, x_sample, cache_k, cache_v, c, c_ctx, w_in, q_norm_g, k_norm_g, w_proj_att,
           w_proj_fourier, w_proj_chunk, w_out, chunk_ln_g, chunk_ln_b, chunk_ws, chunk_bs, w_ada, b_ada,
           ln1_g, ln1_b, ln2_g, ln2_b, w_router, router_bias, w_gate_e, w_up_e, w_down_e):
    x = (x_prompt.reshape(T_CTX, D_MODEL), x_sample.reshape(T_LAT, D_MODEL))
    cond = jnp.concatenate([c_ctx[None, :], c, jnp.zeros((N_COND - 1 - DEC_BATCH, D_MODEL), F32)], axis=0)
    mod = _mod_call(cond, w_ada, b_ada).reshape(DEPTH, N_COND, N_MOD, D_MODEL)

    cos_t, sin_t = _rope_tables()
    bd_heads = _const_01(_block_diag(np.ones((HEAD_DIM, HEAD_DIM)), N_Q_HEADS))
    c64, s64 = _dft_mats(FOURIER_GROUP_W, 1.0)
    bdc = _const_bf16(_block_diag(c64, FOURIER_W // FOURIER_GROUP_W))
    bds = _const_bf16(_block_diag(s64, FOURIER_W // FOURIER_GROUP_W))
    dft = {}
    for seq in (SEQ, DEC_SEQ):
        cs, ss = _dft_mats(seq, 1.0 / math.sqrt(seq * FOURIER_GROUP_W))
        dft[seq] = _const_bf16(np.concatenate([cs, -ss], axis=1))
    ctx_k = cache_k.reshape(DEC_BATCH, DEPTH, PAST_LEN, KV_W)
    ctx_v = cache_v.reshape(DEC_BATCH, DEPTH, PAST_LEN, KV_W)
    wr = jnp.pad(w_router, ((0, 0), (0, ROUTER_PAD - N_EXPERTS)))
    wr_hi = wr.astype(BF16)
    wr_lo = (wr - wr_hi.astype(F32)).astype(BF16)
    bias_t = jnp.broadcast_to(router_bias[:, None], (N_EXPERTS, TM))
    tri = _const_01(np.triu(np.ones((SORT_BLK, SORT_BLK)), 1))
    low = _const_01(np.tril(np.ones((CLASS_PAD, CLASS_PAD)), -1))

    w_in_b = w_in.astype(BF16)
    wpa, wpf, wpc, wo = (w.astype(BF16) for w in (w_proj_att, w_proj_fourier, w_proj_chunk, w_out))
    wge, wue, wde = w_gate_e, w_up_e, w_down_e
    qg = jnp.tile(q_norm_g, (1, N_Q_HEADS))[:, None, :]
    kg = jnp.tile(k_norm_g, (1, N_KV_HEADS))[:, None, :]
    ws = chunk_ws.astype(BF16)
    bsm = jnp.repeat(jnp.swapaxes(chunk_bs, 1, 2), CHUNK_GROUP_W, axis=2)
    row3 = lambda a: a[:, None, :]

    new_k = jnp.zeros((BATCH, DEPTH, SEQ, KV_W), F32)
    new_v = jnp.zeros((BATCH, DEPTH, SEQ, KV_W), F32)
    proj_args = (mod, w_in_b, qg, kg, cos_t, sin_t, bd_heads)
    for l in range(DEPTH):
        if l == 0:
            mix, new_k, new_v = _proj_call(l, x, *proj_args, new_k, new_v)
        else:
            mix, new_k, new_v, *x = _proj_call(l, None, *proj_args, new_k, new_v,
                                               prev=(x1, moe, row3(ln2_g), row3(ln2_b)))
        mixer_args = (bdc, bds, ws, bsm, row3(chunk_ln_g), row3(chunk_ln_b))
        mixed_ctx = _mixer_call(l, mix, None, None, dft[SEQ], *mixer_args, latent=False)
        mixed_lat = _mixer_call(l, mix, ctx_k, ctx_v, dft[DEC_SEQ], *mixer_args, latent=True)
        x1, ha, rt = _merge_call(l, x, mixed_ctx, mixed_lat, mod, w_in_b, wpa, wpf, wpc, wo,
                                 row3(ln1_g), row3(ln1_b), wr_hi, wr_lo, bias_t)
        pos2d, tiles = _sort_call(rt, tri, low)
        moe = _moe_call(l, tiles, pos2d[0], ha, wge, wue, wde)
    x = _post_call(DEPTH - 1, x1, moe, mod, row3(ln2_g), row3(ln2_b))

    y_prompt = x[0].reshape(BATCH, SEQ, D_MODEL)
    y_sample = x[1].reshape(DEC_BATCH, DEC_SEQ, D_MODEL)
    cache_shape = (BATCH, DEPTH, SEQ, N_KV_HEADS, HEAD_DIM)
    return (y_prompt, y_sample, new_k.reshape(cache_shape), new_v.reshape(cache_shape))
```

```python
import functools
import math

import jax
import jax.numpy as jnp
import numpy as np
from jax import lax
from jax.experimental import pallas as pl
from jax.experimental.pallas import tpu as pltpu

F32 = jnp.float32
BF16 = jnp.bfloat16

D_MODEL = 1024
BATCH = 16
SEQ = 256
DEPTH = 2
DEC_BATCH = 2
DEC_SEQ = 1024
PAST_LEN = 256
GRID_W = 64
N_Q_HEADS = 8
N_KV_HEADS = 2
HEAD_DIM = 64
Q_PER_KV = N_Q_HEADS // N_KV_HEADS
ATT_W = N_Q_HEADS * HEAD_DIM
KV_W = N_KV_HEADS * HEAD_DIM
ROPE_THETA = 10000.0
FOURIER_GROUP_W = 64
FOURIER_W = 256
N_CHUNK_GROUPS = 4
CHUNK_GROUP_W = 64
CHUNK_W = 256
CHUNK = 128
N_EXPERTS = 16
N_EXPERT_GROUPS = 4
EXPERTS_PER_GROUP = 4
D_EXPERT = 512
ALPHA = (2 * DEPTH) ** 0.25
LN_EPS = 1e-6
RMS_EPS = 1e-6

LANES = 128
SUBLANES = 8
T_CTX = BATCH * SEQ
T_LAT = DEC_BATCH * DEC_SEQ
T_ALL = T_CTX + T_LAT
IN_W = ATT_W + 2 * KV_W + FOURIER_W + 2 * CHUNK_W + 3 * D_MODEL
MIX_W = ATT_W + 2 * KV_W + FOURIER_W + 2 * CHUNK_W
N_MOD = 6
N_COND = 8
ROUTER_PAD = LANES
TM = 512
TQ = 512
MERGE_SUBS = (256, 256)
PROJ_SUB = 256
CTX_SEQS_PER_STEP = 2
V7X_VMEM_LIMIT = 56 * 1024 * 1024

PAIR_A = (0, 0, 0, 1, 1, 3)
PAIR_B = (1, 2, 3, 3, 2, 2)
N_PAIRS = len(PAIR_A)
N_CLASSES = N_EXPERT_GROUPS * N_PAIRS
CLASS_PAD = 32
ROUTE_ROWS = 8
TM_MOE = 256
N_TILES = T_ALL // TM_MOE + N_CLASSES
R_ROWS = N_TILES * TM_MOE
SORT_BLK = 512
D_CHUNKS = D_MODEL // LANES
HA_W = D_MODEL + LANES
TILE_SLOT_A, TILE_SLOT_B, TILE_ROWS = 0, 1, 2
SMALL_TILE = 64
N_PARK = 2 * TM_MOE
COPY_GROUP = 8


def _cparams(*sem):
    return pltpu.CompilerParams(dimension_semantics=sem, vmem_limit_bytes=V7X_VMEM_LIMIT)


def _cond_row(i, tm):
    n_ctx = T_CTX // tm
    return jnp.where(i < n_ctx, 0, 1 + (i - n_ctx) // (DEC_SEQ // tm))


N_CTX_TILES = T_CTX // TM


def _group_specs():
    return [pl.BlockSpec((TM, D_MODEL), lambda i: (jnp.minimum(i, N_CTX_TILES - 1), 0)),
            pl.BlockSpec((TM, D_MODEL), lambda i: (jnp.maximum(i - N_CTX_TILES, 0), 0))]


def _ln(x):
    mu = jnp.mean(x, axis=-1, keepdims=True)
    xc = x - mu
    var = jnp.mean(xc * xc, axis=-1, keepdims=True)
    return xc * lax.rsqrt(var + LN_EPS)


def _split_bf16(x):
    hi = x.astype(BF16)
    lo = (x - hi.astype(F32)).astype(BF16)
    return hi, lo


def _dot(a, b):
    return jnp.dot(a, b, preferred_element_type=F32)


def _sigmoid(x):
    return 1.0 / (1.0 + jnp.exp(-x))


def _gelu_tanh(x):
    c = np.float32(np.sqrt(2 / np.pi))
    return x * (0.5 * (1.0 + jnp.tanh(c * (x + 0.044715 * (x * x * x)))))


def _mod_kernel(c_ref, w_ref, b_ref, o_ref):
    c = c_ref[...]
    s = c * _sigmoid(c)
    o_ref[0] = _dot(s.astype(BF16), w_ref[0].astype(BF16)) + b_ref[0]


def _mod_call(cond, w_ada, b_ada):
    tn = 1536
    n = N_MOD * D_MODEL
    return pl.pallas_call(
        _mod_kernel,
        out_shape=jax.ShapeDtypeStruct((DEPTH, N_COND, n), F32),
        grid=(DEPTH, n // tn),
        in_specs=[
            pl.BlockSpec((N_COND, D_MODEL), lambda l, j: (0, 0)),
            pl.BlockSpec((1, D_MODEL, tn), lambda l, j: (l, 0, j)),
            pl.BlockSpec((1, 1, tn), lambda l, j: (l, 0, j)),
        ],
        out_specs=pl.BlockSpec((1, N_COND, tn), lambda l, j: (l, 0, j)),
        compiler_params=_cparams("arbitrary", "arbitrary"),
        name="adaln_mod",
    )(cond, w_ada, b_ada.reshape(DEPTH, 1, n))


def _head_rms(q, bd, g):
    hi, lo = _split_bf16(q * q)
    ssum = _dot(hi, bd) + _dot(lo, bd)
    return q * lax.rsqrt(ssum * (1.0 / HEAD_DIM) + RMS_EPS) * g


def _rope(x, cos, sin_signed):
    w = x.shape[-1]
    lane = lax.broadcasted_iota(jnp.int32, x.shape, 1)
    swapped = jnp.where((lane % 32) < 16, pltpu.roll(x, w - 16, 1), pltpu.roll(x, 16, 1))
    return x * cos + swapped * sin_signed


def _proj_kernel(*refs, after_moe):
    if after_moe:
        (x1_ref, moe_ref, modp_ref, l2g_ref, l2b_ref, mod_ref, w_ref, qg_ref, kg_ref, cos_ref, sin_ref, bd_ref,
         kc_in, vc_in, o_ref, kc_ref, vc_ref, xoc_ref, xol_ref) = refs
    else:
        (xc_ref, xl_ref, mod_ref, w_ref, qg_ref, kg_ref, cos_ref, sin_ref, bd_ref,
         kc_in, vc_in, o_ref, kc_ref, vc_ref) = refs
    del kc_in, vc_in
    m = mod_ref[0, 0]
    in_ctx = pl.program_id(0) < N_CTX_TILES
    bd = bd_ref[...]
    sub_rows = [slice(s * PROJ_SUB, (s + 1) * PROJ_SUB) for s in range(TM // PROJ_SUB)]
    projs = []
    xs = []
    for rows in sub_rows:
        if after_moe:
            moe = moe_ref[rows].reshape(PROJ_SUB, D_MODEL)
            x = _ln(ALPHA * x1_ref[rows, :] + modp_ref[0, 0][5:6] * moe) * l2g_ref[0] + l2b_ref[0]
            xs.append(x)
        else:
            x = jnp.where(in_ctx, xc_ref[rows, :], xl_ref[rows, :])
        h = _ln(x) * (1.0 + m[1:2]) + m[0:1]
        proj = _dot(h.astype(BF16), w_ref[...])
        o_ref[rows, ATT_W + KV_W:] = proj[:, ATT_W + KV_W:]
        projs.append(proj[:, :ATT_W + 2 * KV_W])
    keys = []
    for rows, proj in zip(sub_rows, projs):
        qn = _head_rms(proj[:, :ATT_W], bd, qg_ref[0])
        kn = _head_rms(proj[:, ATT_W:ATT_W + KV_W], bd[:KV_W, :KV_W], kg_ref[0])
        cos = cos_ref[rows, :]
        sin = sin_ref[rows, :]
        o_ref[rows, :ATT_W] = jnp.where(in_ctx, qn, _rope(qn, cos, sin))
        o_ref[rows, ATT_W:ATT_W + KV_W] = jnp.where(in_ctx, kn, _rope(kn, cos[:, :KV_W], sin[:, :KV_W]))
        keys.append(kn)

    @pl.when(in_ctx)
    def _():
        for s in range(len(sub_rows)):
            kc_ref[s] = keys[s]
            vc_ref[s] = projs[s][:, ATT_W + KV_W:]
        if after_moe:
            for rows, x in zip(sub_rows, xs):
                xoc_ref[rows, :] = x

    if after_moe:
        @pl.when(jnp.logical_not(in_ctx))
        def _():
            for rows, x in zip(sub_rows, xs):
                xol_ref[rows, :] = x


def _proj_call(layer, x, mod, w_in_b, qg, kg, cos_t, sin_t, bd, k_cache, v_cache, prev=None):
    assert PROJ_SUB == SEQ
    n_ctx = T_CTX // TM
    per_seq = DEC_SEQ // TM
    after_moe = prev is not None

    def rope_idx(i):
        return (jnp.where(i < n_ctx, 0, (i - n_ctx) % per_seq), 0)

    cache_spec = pl.BlockSpec((TM // SEQ, None, SEQ, KV_W), lambda i: (jnp.minimum(i, n_ctx - 1), layer, 0, 0))
    cache_shape = jax.ShapeDtypeStruct((BATCH, DEPTH, SEQ, KV_W), F32)
    out_shape = [jax.ShapeDtypeStruct((T_ALL, MIX_W), F32), cache_shape, cache_shape]
    out_specs = [pl.BlockSpec((TM, MIX_W), lambda i: (i, 0)), cache_spec, cache_spec]
    if after_moe:
        x1, moe, l2g, l2b = prev
        prev_layer = lambda i: (layer - 1, 0, 0)
        x_specs = [
            pl.BlockSpec((TM, D_MODEL), lambda i: (i, 0)),
            pl.BlockSpec((TM, D_CHUNKS, LANES), lambda i: (i, 0, 0)),
            pl.BlockSpec((1, 1, N_MOD, D_MODEL), lambda i: (layer - 1, _cond_row(i, TM), 0, 0)),
            pl.BlockSpec((1, 1, D_MODEL), prev_layer),
            pl.BlockSpec((1, 1, D_MODEL), prev_layer),
        ]
        x_args = [x1, moe, mod, l2g, l2b]
        out_shape += [jax.ShapeDtypeStruct((T_CTX, D_MODEL), F32), jax.ShapeDtypeStruct((T_LAT, D_MODEL), F32)]
        out_specs += _group_specs()
    else:
        x_specs, x_args = _group_specs(), list(x)
    n_in = len(x_args) + 9
    return pl.pallas_call(
        functools.partial(_proj_kernel, after_moe=after_moe),
        out_shape=tuple(out_shape),
        grid=(T_ALL // TM,),
        in_specs=x_specs + [
            pl.BlockSpec((1, 1, N_MOD, D_MODEL), lambda i: (layer, _cond_row(i, TM), 0, 0)),
            pl.BlockSpec((None, D_MODEL, MIX_W), lambda i: (layer, 0, 0)),
            pl.BlockSpec((1, 1, ATT_W), lambda i: (layer, 0, 0)),
            pl.BlockSpec((1, 1, KV_W), lambda i: (layer, 0, 0)),
            pl.BlockSpec((TM, ATT_W), rope_idx),
            pl.BlockSpec((TM, ATT_W), rope_idx),
            pl.BlockSpec((ATT_W, ATT_W), lambda i: (0, 0)),
            pl.BlockSpec(memory_space=pl.ANY),
            pl.BlockSpec(memory_space=pl.ANY),
        ],
        out_specs=tuple(out_specs),
        input_output_aliases={n_in - 2: 1, n_in - 1: 2},
        compiler_params=_cparams("arbitrary"),
        name="in_proj",
    )(*x_args, mod, w_in_b, qg, kg, cos_t, sin_t, bd, k_cache, v_cache)


def _attention_tile(q, k_parts, v_parts, phase_major):
    def kv_of(h, parts):
        g = h // Q_PER_KV
        return [p[:, g * HEAD_DIM:(g + 1) * HEAD_DIM] for p in parts]

    def scores(h):
        qh = q[:, h * HEAD_DIM:(h + 1) * HEAD_DIM]
        return [lax.dot_general(qh, k, (((1,), (1,)), ((), ())), preferred_element_type=F32)
                for k in kv_of(h, k_parts)]

    if phase_major:
        all_ss = [scores(h) for h in range(N_Q_HEADS)]
        all_m = [functools.reduce(jnp.maximum, [jnp.max(s, axis=-1, keepdims=True) for s in ss]) for ss in all_ss]
        all_es = [[jnp.exp(s - m) for s in ss] for ss, m in zip(all_ss, all_m)]
        all_den = [functools.reduce(jnp.add, [jnp.sum(e, axis=-1, keepdims=True) for e in es]) for es in all_es]
        outs = [functools.reduce(jnp.add, [_dot(e.astype(BF16), v) for e, v in zip(es, kv_of(h, v_parts))])
                * (1.0 / den) for h, (es, den) in enumerate(zip(all_es, all_den))]
        return jnp.concatenate(outs, axis=-1)
    outs = []
    ss_next = scores(0)
    for h in range(N_Q_HEADS):
        ss = ss_next
        if h + 1 < N_Q_HEADS:
            ss_next = scores(h + 1)
        m = functools.reduce(jnp.maximum, [jnp.max(s, axis=-1, keepdims=True) for s in ss])
        es = [jnp.exp(s - m) for s in ss]
        denom = functools.reduce(jnp.add, [jnp.sum(e, axis=-1, keepdims=True) for e in es])
        o = functools.reduce(jnp.add, [_dot(e.astype(BF16), v) for e, v in zip(es, kv_of(h, v_parts))])
        outs.append(o * (1.0 / denom))
    return jnp.concatenate(outs, axis=-1)


def _mixer_kernel(*refs, seq, seqs, latent):
    if latent:
        (mix_ref, ck_ref, cv_ref, dft_ref, bdc_ref, bds_ref, ws_ref, bsm_ref, clg_ref, clb_ref, o_ref) = refs
    else:
        (mix_ref, dft_ref, bdc_ref, bds_ref, ws_ref, bsm_ref, clg_ref, clb_ref, o_ref) = refs
    tq = min(seq, TQ)
    c0 = ATT_W + 2 * KV_W + FOURIER_W
    lane = lax.broadcasted_iota(jnp.int32, (CHUNK, CHUNK_W), 1)

    for b in range(seqs):
        base = b * seq
        rows = slice(base, base + seq)

        k_new = mix_ref[rows, ATT_W:ATT_W + KV_W].astype(BF16)
        v_new = mix_ref[rows, ATT_W + KV_W:ATT_W + 2 * KV_W].astype(BF16)
        if latent:
            k_parts = [ck_ref[...].astype(BF16), k_new]
            v_parts = [cv_ref[...].astype(BF16), v_new]
        else:
            k_parts, v_parts = [k_new], [v_new]

        def q_tile(t, carry, base=base, k_parts=k_parts, v_parts=v_parts):
            r0 = base + (pl.multiple_of(t * tq, tq) if seq > tq else 0)
            q = (mix_ref[pl.ds(r0, tq), :ATT_W] * (HEAD_DIM ** -0.5)).astype(BF16)
            o_ref[pl.ds(r0, tq), :ATT_W] = _attention_tile(q, k_parts, v_parts, phase_major=not latent).astype(BF16)
            return carry

        if seq > tq:
            lax.fori_loop(0, seq // tq, q_tile, 0)
        else:
            q_tile(0, 0)

        f_hi, f_lo = _split_bf16(mix_ref[rows, ATT_W + 2 * KV_W:ATT_W + 2 * KV_W + FOURIER_W])
        bdc = bdc_ref[...]
        bds = bds_ref[...]
        y = jnp.concatenate([_dot(f_hi, bdc) + _dot(f_lo, bdc), _dot(f_hi, bds) + _dot(f_lo, bds)], axis=0)
        y_hi, y_lo = _split_bf16(y)
        dft = dft_ref[...]
        four = _dot(dft, y_hi) + _dot(dft, y_lo)
        o_ref[rows, ATT_W:ATT_W + FOURIER_W] = four.astype(BF16)

        u = _gelu_tanh(mix_ref[rows, c0:c0 + CHUNK_W])
        vn = (_ln(_gelu_tanh(mix_ref[rows, c0 + CHUNK_W:c0 + 2 * CHUNK_W])) * clg_ref[0] + clb_ref[0]).astype(BF16)
        bsm = bsm_ref[...]
        for c in range(seq // CHUNK):
            vc = vn[c * CHUNK:(c + 1) * CHUNK]
            sv = bsm
            for g in range(N_CHUNK_GROUPS):
                vg = jnp.where(lane // CHUNK_GROUP_W == g, vc, jnp.zeros_like(vc))
                sv = sv + _dot(ws_ref[g], vg)
            o_ref[base + c * CHUNK:base + (c + 1) * CHUNK, ATT_W + FOURIER_W:] = (
                u[c * CHUNK:(c + 1) * CHUNK] * sv).astype(BF16)


def _mixer_call(layer, mix, ctx_k, ctx_v, dft, bdc, bds, ws, bsm, clg, clb, *, latent):
    seq = DEC_SEQ if latent else SEQ
    seqs = 1 if latent else CTX_SEQS_PER_STEP
    nb = (DEC_BATCH if latent else BATCH) // seqs
    row0 = (T_CTX // seq) if latent else 0
    const2 = lambda b: (0, 0)
    per_layer = lambda b: (layer, 0, 0)
    in_specs = [pl.BlockSpec((seqs * seq, MIX_W), lambda b: (row0 + b, 0))]
    args = [mix]
    if latent:
        in_specs += [pl.BlockSpec((None, None, PAST_LEN, KV_W), lambda b: (b, layer, 0, 0))] * 2
        args += [ctx_k, ctx_v]
    in_specs += [
        pl.BlockSpec((seq, 2 * seq), const2),
        pl.BlockSpec((FOURIER_W, FOURIER_W), const2),
        pl.BlockSpec((FOURIER_W, FOURIER_W), const2),
        pl.BlockSpec((None, N_CHUNK_GROUPS, CHUNK, CHUNK), lambda b: (layer, 0, 0, 0)),
        pl.BlockSpec((None, CHUNK, CHUNK_W), per_layer),
        pl.BlockSpec((1, 1, CHUNK_W), per_layer),
        pl.BlockSpec((1, 1, CHUNK_W), per_layer),
    ]
    args += [dft, bdc, bds, ws, bsm, clg, clb]
    return pl.pallas_call(
        functools.partial(_mixer_kernel, seq=seq, seqs=seqs, latent=latent),
        out_shape=jax.ShapeDtypeStruct((nb * seqs * seq, D_MODEL), BF16),
        grid=(nb,),
        in_specs=in_specs,
        out_specs=pl.BlockSpec((seqs * seq, D_MODEL), lambda b: (b, 0)),
        compiler_params=_cparams("arbitrary"),
        name="mixer_latent" if latent else "mixer_context",
    )(*args)


def _route(logits, bias_t):
    lt = logits.T[:N_EXPERTS]
    ex = jnp.exp(lt - jnp.max(lt, axis=0, keepdims=True))
    probs = ex / jnp.sum(ex, axis=0, keepdims=True)
    sel = probs + bias_t
    p = [probs[e:e + 1] for e in range(N_EXPERTS)]
    s = [sel[e:e + 1] for e in range(N_EXPERTS)]
    n = EXPERTS_PER_GROUP
    scores = []
    for g in range(N_EXPERT_GROUPS):
        pair = [s[g * n + a] + s[g * n + b] for a in range(n) for b in range(a + 1, n)]
        scores.append(functools.reduce(jnp.maximum, pair))
    best = jnp.zeros_like(scores[0], dtype=jnp.int32)
    best_score = scores[0]
    for g in range(1, N_EXPERT_GROUPS):
        better = scores[g] > best_score
        best = jnp.where(better, g, best)
        best_score = jnp.where(better, scores[g], best_score)
    cls = jnp.zeros_like(best_score)
    w_a = jnp.zeros_like(best_score)
    w_b = jnp.zeros_like(best_score)
    for g in range(N_EXPERT_GROUPS):
        in_g = best == g
        chosen = []
        for a in range(n):
            rank = jnp.zeros_like(best)
            for b in range(n):
                if b == a:
                    continue
                ahead = (s[g * n + b] > s[g * n + a]) if b > a else (s[g * n + b] >= s[g * n + a])
                rank = rank + ahead.astype(jnp.int32)
            chosen.append(jnp.logical_and(in_g, rank < 2))
        for k in range(N_PAIRS):
            pa, pb = p[g * n + PAIR_A[k]], p[g * n + PAIR_B[k]]
            hit = jnp.logical_and(chosen[PAIR_A[k]], chosen[PAIR_B[k]])
            wsum = pa + pb
            cls = jnp.where(hit, float(g * N_PAIRS + k), cls)
            w_a = jnp.where(hit, pa / wsum, w_a)
            w_b = jnp.where(hit, pb / wsum, w_b)
    return jnp.concatenate([cls, w_a, w_b, jnp.zeros((ROUTE_ROWS - 3, lt.shape[1]), F32)], axis=0)


def _merge_kernel(xc_ref, xl_ref, mixc_ref, mixl_ref, mod_ref, win_ref, wpa_ref, wpf_ref, wpc_ref, wo_ref,
                  l1g_ref, l1b_ref, wrh_ref, wrl_ref, bias_ref, x1_ref, ha_ref, rt_ref):
    m = mod_ref[0, 0]
    in_ctx = pl.program_id(0) < N_CTX_TILES
    branch_w = ((0, ATT_W, wpa_ref), (ATT_W, FOURIER_W, wpf_ref), (ATT_W + FOURIER_W, CHUNK_W, wpc_ref))
    bounds = np.cumsum((0,) + MERGE_SUBS)
    sub_rows = [slice(int(a), int(b)) for a, b in zip(bounds[:-1], bounds[1:])]
    residual = []
    for rows in sub_rows:
        x = jnp.where(in_ctx, xc_ref[rows, :], xl_ref[rows, :])
        mixed = jnp.where(in_ctx, mixc_ref[rows, :], mixl_ref[rows, :])
        h = (_ln(x) * (1.0 + m[1:2]) + m[0:1]).astype(BF16)
        merged = None
        for b, (c0, width, w_ref) in enumerate(branch_w):
            gate = _sigmoid(_dot(h, win_ref[:, MIX_W + b * D_MODEL:MIX_W + (b + 1) * D_MODEL]))
            term = gate * _dot(mixed[:, c0:c0 + width], w_ref[...])
            merged = term if merged is None else merged + term
        mix = _dot(merged.astype(BF16), wo_ref[...])
        residual.append(ALPHA * x + m[2:3] * mix)
    for rows, pre in zip(sub_rows, residual):
        x1 = _ln(pre) * l1g_ref[0] + l1b_ref[0]
        x1_ref[rows, :] = x1
        h2 = _ln(x1) * (1.0 + m[4:5]) + m[3:4]
        h_hi, h_lo = _split_bf16(h2)
        wrh = wrh_ref[...]
        logits = _dot(h_hi, wrh) + _dot(h_lo, wrh) + _dot(h_hi, wrl_ref[...])
        rt = _route(logits, bias_ref[:, rows])
        rt_ref[:, rows] = rt
        ha_ref[rows, :D_MODEL] = h2
        ha_ref[rows, D_MODEL:] = jnp.concatenate(
            [rt, jnp.zeros((LANES - ROUTE_ROWS, rows.stop - rows.start), F32)], axis=0).T


def _merge_call(layer, x, mixed_ctx, mixed_lat, mod, w_in_b, wpa, wpf, wpc, wo, l1g, l1b, wrh, wrl, bias_t):
    const2 = lambda i: (0, 0)
    per_layer = lambda i: (layer, 0, 0)
    tok = lambda i: (i, 0)
    return pl.pallas_call(
        _merge_kernel,
        out_shape=(jax.ShapeDtypeStruct((T_ALL, D_MODEL), F32),
                   jax.ShapeDtypeStruct((T_ALL, HA_W), F32),
                   jax.ShapeDtypeStruct((ROUTE_ROWS, T_ALL), F32)),
        grid=(T_ALL // TM,),
        in_specs=_group_specs() + _group_specs() + [
            pl.BlockSpec((1, 1, N_MOD, D_MODEL), lambda i: (layer, _cond_row(i, TM), 0, 0)),
            pl.BlockSpec((None, D_MODEL, IN_W), per_layer, pipeline_mode=pl.Buffered(1)),
            pl.BlockSpec((None, ATT_W, D_MODEL), per_layer, pipeline_mode=pl.Buffered(1)),
            pl.BlockSpec((None, FOURIER_W, D_MODEL), per_layer, pipeline_mode=pl.Buffered(1)),
            pl.BlockSpec((None, CHUNK_W, D_MODEL), per_layer, pipeline_mode=pl.Buffered(1)),
            pl.BlockSpec((None, D_MODEL, D_MODEL), per_layer, pipeline_mode=pl.Buffered(1)),
            pl.BlockSpec((1, 1, D_MODEL), per_layer),
            pl.BlockSpec((1, 1, D_MODEL), per_layer),
            pl.BlockSpec((D_MODEL, ROUTER_PAD), const2),
            pl.BlockSpec((D_MODEL, ROUTER_PAD), const2),
            pl.BlockSpec((N_EXPERTS, TM), const2),
        ],
        out_specs=(pl.BlockSpec((TM, D_MODEL), tok),
                   pl.BlockSpec((TM, HA_W), tok),
                   pl.BlockSpec((ROUTE_ROWS, TM), lambda i: (0, i))),
        compiler_params=_cparams("arbitrary"),
        name="merge_route",
    )(*x, mixed_ctx, mixed_lat, mod, w_in_b, wpa, wpf, wpc, wo, l1g, l1b, wrh, wrl, bias_t)


def _sort_kernel(rt_ref, tri_ref, low_ref, pos_ref, tile_ref):
    crow = lax.broadcasted_iota(jnp.int32, (CLASS_PAD, SORT_BLK), 0)
    tri = tri_ref[...]
    n_blk = T_ALL // SORT_BLK
    carry = jnp.zeros((CLASS_PAD, 1), F32)
    ranks = []
    for b in range(n_blk):
        hot = rt_ref[0:1, b * SORT_BLK:(b + 1) * SORT_BLK].astype(jnp.int32) == crow
        hot_f = jnp.where(hot, 1.0, 0.0)
        before = _dot(hot_f.astype(BF16), tri) + carry
        ranks.append(jnp.sum(jnp.where(hot, before, 0.0), axis=0, keepdims=True))
        carry = carry + jnp.sum(hot_f, axis=1, keepdims=True)
    padded = jnp.floor((carry + (TM_MOE - 1.0)) * (1.0 / TM_MOE)) * TM_MOE
    padded = jnp.broadcast_to(padded, (CLASS_PAD, LANES))
    offs = _dot(low_ref[...], padded.astype(BF16))
    for b in range(n_blk):
        hot = rt_ref[0:1, b * SORT_BLK:(b + 1) * SORT_BLK].astype(jnp.int32) == crow
        base = jnp.sum(jnp.where(hot, offs[:, 0:1], 0.0), axis=0, keepdims=True)
        pos_ref[:, b * SORT_BLK:(b + 1) * SORT_BLK] = (base + ranks[b]).astype(jnp.int32)

    start = lax.broadcasted_iota(jnp.int32, (1, LANES), 1).astype(F32) * TM_MOE
    is_class = lax.broadcasted_iota(jnp.int32, (CLASS_PAD, LANES), 0) < N_CLASSES
    ends = jnp.where(is_class, offs + padded, 0.0)
    total = jnp.max(ends, axis=0, keepdims=True)
    valid = start < total
    tcls = jnp.sum(jnp.where(jnp.logical_and(is_class, ends <= start), 1.0, 0.0), axis=0, keepdims=True)
    last = jnp.max(jnp.where(valid, tcls, 0.0), axis=1, keepdims=True)
    tcls = jnp.where(valid, tcls, last)
    grp = functools.reduce(jnp.add, [jnp.where(tcls >= g * N_PAIRS, 1.0, 0.0) for g in range(1, N_EXPERT_GROUPS)])
    pair = tcls - grp * N_PAIRS
    slot_a = functools.reduce(jnp.add, [jnp.where(pair == k, float(PAIR_A[k]), 0.0) for k in range(N_PAIRS)])
    slot_b = functools.reduce(jnp.add, [jnp.where(pair == k, float(PAIR_B[k]), 0.0) for k in range(N_PAIRS)])
    crow_t = lax.broadcasted_iota(jnp.int32, (CLASS_PAD, LANES), 0).astype(F32)
    real_end = jnp.sum(jnp.where(crow_t == tcls, offs + carry, 0.0), axis=0, keepdims=True)
    n_rows = jnp.where(valid, jnp.clip(real_end - start, 0.0, float(TM_MOE)), 0.0)
    rows = [grp * EXPERTS_PER_GROUP + slot_a, grp * EXPERTS_PER_GROUP + slot_b, n_rows]
    tile_ref[...] = jnp.concatenate(rows + [jnp.zeros((8 - len(rows), LANES), F32)], axis=0).astype(jnp.int32)


def _sort_call(rt, tri, low):
    return pl.pallas_call(
        _sort_kernel,
        out_shape=(jax.ShapeDtypeStruct((1, T_ALL), jnp.int32),
                   jax.ShapeDtypeStruct((8, LANES), jnp.int32)),
        name="route_sort",
    )(rt, tri, low)


def _moe_kernel(tile_ref, pos_ref, ha_hbm, wg_hbm, wu_hbm, wd_hbm, out_ref, hbuf, ybuf, src_ref, ssem,
                stg_g, stg_u, stg_d, act_g, act_u, act_d, wsem, ha_ref, hsem, staged_ref, *, layer):
    j = pl.program_id(0)
    slot = j % 2

    def n_rows(tile):
        inside = jnp.logical_and(tile >= 0, tile < N_TILES)
        return jnp.where(inside, tile_ref[TILE_ROWS, jnp.clip(tile, 0, N_TILES - 1)], 0)

    def token_of(tile, n, r, tail):
        return src_ref[tile * TM_MOE + (jnp.minimum(r, n - 1) if tail else r)]

    def scatter_copy(tile, n, buf_slot, r, tail):
        tok = token_of(tile, n, r, tail)
        dst = jnp.where(r < n, tok, T_ALL + buf_slot * TM_MOE + r) if tail else tok
        return pltpu.make_async_copy(ybuf.at[buf_slot, r], out_ref.at[dst], ssem.at[buf_slot])

    def for_groups(n, fn):
        full = lax.shift_right_logical(n, 3)

        def body(g, c):
            for k in range(COPY_GROUP):
                fn(g, k, False)
            return c
        lax.fori_loop(0, full, body, 0)

        @pl.when(jnp.bitwise_and(n, COPY_GROUP - 1) != 0)
        def _():
            for k in range(COPY_GROUP):
                fn(full, k, True)

    def retire(tile, buf_slot):
        n = n_rows(tile)
        one = scatter_copy(tile, n, buf_slot, 0, False)

        def body(g, c):
            for _ in range(COPY_GROUP):
                one.wait()
            return c
        lax.fori_loop(0, lax.shift_right_logical(n + COPY_GROUP - 1, 3), body, 0)

    def expert_of(x, tile):
        return tile_ref[TILE_SLOT_A + x, tile]

    def weight_copies(x, e):
        return [pltpu.make_async_copy(src.at[layer, e], stg.at[x], wsem.at[x])
                for src, stg in ((wg_hbm, stg_g), (wu_hbm, stg_u), (wd_hbm, stg_d))]

    @pl.when(j == 0)
    def _():
        load_inputs = pltpu.make_async_copy(ha_hbm, ha_ref, hsem)
        load_inputs.start()
        for x in range(2):
            for c in weight_copies(x, expert_of(x, 0)):
                c.start()
            staged_ref[x] = expert_of(x, 0)

        def place(t, c):
            src_ref[pos_ref[t]] = t
            return c
        lax.fori_loop(0, T_ALL, place, 0, unroll=8)
        load_inputs.wait()
        hbuf[...] = jnp.zeros(hbuf.shape, F32)
        ybuf[...] = jnp.zeros(ybuf.shape, F32)
        for s in range(2):
            park = pltpu.make_async_copy(ybuf.at[s], out_ref.at[pl.ds(T_ALL + s * TM_MOE, TM_MOE)], ssem.at[s])
            park.start()
            park.wait()

    @pl.when(n_rows(j - 2) > 0)
    def _():
        retire(j - 2, slot)

    def run_experts(rows):
        rec = hbuf[:rows // SUBLANES].reshape(rows, HA_W)
        h = rec[:, :D_MODEL].astype(BF16)

        def expert(x):
            a = _dot(h, act_g[x])
            u = _dot(h, act_u[x])
            w = rec[:, D_MODEL + 1 + x:D_MODEL + 2 + x]
            return _dot(((a * _sigmoid(a)) * u * w).astype(BF16), act_d[x])

        ybuf[slot, :rows] = (expert(0) + expert(1)).reshape(rows, D_CHUNKS, LANES)

    @pl.when(n_rows(j) > 0)
    def _():
        for x in range(2):
            prev = jnp.maximum(j - 1, 0)

            @pl.when(jnp.logical_or(j == 0, expert_of(x, j) != expert_of(x, prev)))
            def _():
                for c in weight_copies(x, expert_of(x, j)):
                    c.wait()
                act_g[x] = stg_g[x].astype(BF16)
                act_u[x] = stg_u[x].astype(BF16)
                act_d[x] = stg_d[x].astype(BF16)

        for x in range(2):
            cur = expert_of(x, j)

            def expert_or_cur(tile):
                return jnp.where(n_rows(tile) > 0, expert_of(x, jnp.minimum(tile, N_TILES - 1)), cur)
            e1, e2 = expert_or_cur(j + 1), expert_or_cur(j + 2)
            target = jnp.where(e1 != cur, e1, e2)

            @pl.when(jnp.logical_and(target != cur, target != staged_ref[x]))
            def _():
                for c in weight_copies(x, target):
                    c.start()
                staged_ref[x] = target

        n = n_rows(j)

        def gather_row(g, k, tail):
            tok = token_of(j, n, g * COPY_GROUP + k, tail)
            grp, sub = lax.shift_right_logical(tok, 3), jnp.bitwise_and(tok, SUBLANES - 1)
            hbuf[g, pl.ds(k, 1), :] = ha_ref[grp, pl.ds(sub, 1), :]
        for_groups(n, gather_row)

        @pl.when(n > SMALL_TILE)
        def _():
            run_experts(TM_MOE)

        @pl.when(n <= SMALL_TILE)
        def _():
            run_experts(SMALL_TILE)

        for_groups(n, lambda g, k, tail: scatter_copy(j, n, slot, g * COPY_GROUP + k, tail).start(priority=k % 2))

    @pl.when(j == N_TILES - 1)
    def _():
        @pl.when(n_rows(j - 1) > 0)
        def _():
            retire(j - 1, 1 - slot)

        @pl.when(n_rows(j) > 0)
        def _():
            retire(j, slot)


def _moe_call(layer, tiles, pos, ha, wge, wue, wde):
    up = (2, D_MODEL, D_EXPERT)
    down = (2, D_EXPERT, D_MODEL)
    grid_spec = pltpu.PrefetchScalarGridSpec(
        num_scalar_prefetch=2,
        grid=(N_TILES,),
        in_specs=[pl.BlockSpec(memory_space=pl.ANY)] * 4,
        out_specs=pl.BlockSpec(memory_space=pl.ANY),
        scratch_shapes=[
            pltpu.VMEM((TM_MOE // SUBLANES, SUBLANES, HA_W), F32),
            pltpu.VMEM((2, TM_MOE, D_CHUNKS, LANES), F32),
            pltpu.SMEM((R_ROWS,), jnp.int32),
            pltpu.SemaphoreType.DMA((2,)),
            pltpu.VMEM(up, F32), pltpu.VMEM(up, F32), pltpu.VMEM(down, F32),
            pltpu.VMEM(up, BF16), pltpu.VMEM(up, BF16), pltpu.VMEM(down, BF16),
            pltpu.SemaphoreType.DMA((2,)),
            pltpu.VMEM((T_ALL // SUBLANES, SUBLANES, HA_W), F32),
            pltpu.SemaphoreType.DMA(()),
            pltpu.SMEM((2,), jnp.int32),
        ],
    )
    return pl.pallas_call(
        functools.partial(_moe_kernel, layer=layer),
        out_shape=jax.ShapeDtypeStruct((T_ALL + N_PARK, D_CHUNKS, LANES), F32),
        grid_spec=grid_spec,
        compiler_params=pltpu.CompilerParams(dimension_semantics=("arbitrary",), vmem_limit_bytes=V7X_VMEM_LIMIT,
                                             has_side_effects=True),
        name="moe_pairs",
    )(tiles, pos,
      ha.reshape(T_ALL // SUBLANES, SUBLANES, HA_W), wge, wue, wde)


def _post_kernel(x1_ref, moe_ref, mod_ref, l2g_ref, l2b_ref, oc_ref, ol_ref):
    m = mod_ref[0, 0]
    moe = moe_ref[...].reshape(TM, D_MODEL)
    y = _ln(ALPHA * x1_ref[...] + m[5:6] * moe) * l2g_ref[0] + l2b_ref[0]
    i = pl.program_id(0)

    @pl.when(i < N_CTX_TILES)
    def _():
        oc_ref[...] = y

    @pl.when(i >= N_CTX_TILES)
    def _():
        ol_ref[...] = y


def _post_call(layer, x1, moe, mod, l2g, l2b):
    tok = lambda i: (i, 0)
    per_layer = lambda i: (layer, 0, 0)
    return pl.pallas_call(
        _post_kernel,
        out_shape=(jax.ShapeDtypeStruct((T_CTX, D_MODEL), F32), jax.ShapeDtypeStruct((T_LAT, D_MODEL), F32)),
        grid=(T_ALL // TM,),
        in_specs=[
            pl.BlockSpec((TM, D_MODEL), tok),
            pl.BlockSpec((TM, D_CHUNKS, LANES), lambda i: (i, 0, 0)),
            pl.BlockSpec((1, 1, N_MOD, D_MODEL), lambda i: (layer, _cond_row(i, TM), 0, 0)),
            pl.BlockSpec((1, 1, D_MODEL), per_layer),
            pl.BlockSpec((1, 1, D_MODEL), per_layer),
        ],
        out_specs=tuple(_group_specs()),
        compiler_params=_cparams("arbitrary"),
        name="post_moe",
    )(x1, moe, mod, l2g, l2b)


def _rope_tables():
    pos = np.arange(DEC_SEQ)
    quarter = HEAD_DIM // 4
    inv = ROPE_THETA ** (-np.arange(quarter, dtype=np.float64) / quarter)
    ang_r = (pos // GRID_W)[:, None] * inv[None, :]
    ang_c = (pos % GRID_W)[:, None] * inv[None, :]
    cos = np.concatenate([np.cos(ang_r)] * 2 + [np.cos(ang_c)] * 2, axis=-1)
    sin = np.concatenate([-np.sin(ang_r), np.sin(ang_r), -np.sin(ang_c), np.sin(ang_c)], axis=-1)
    return (jnp.asarray(np.tile(cos, (1, N_Q_HEADS)).astype(np.float32)),
            jnp.asarray(np.tile(sin, (1, N_Q_HEADS)).astype(np.float32)))


def _dft_mats(n, scale):
    j = np.arange(n)
    ang = ((j[:, None] * j[None, :]) % n) * (2 * np.pi / n)
    return np.cos(ang) * scale, np.sin(ang) * scale


def _block_diag(m, reps):
    return np.kron(np.eye(reps), m)


def _const_bf16(a):
    return jnp.asarray(np.asarray(a, np.float32)).astype(BF16)


def _const_01(a):
    return jnp.asarray(np.asarray(a, np.float32).astype(BF16))


def kernel(x_prompt, x_sample, cache_k, cache_v, c, c_ctx, w_in, q_norm_g, k_norm_g, w_proj_att,
           w_proj_fourier, w_proj_chunk, w_out, chunk_ln_g, chunk_ln_b, chunk_ws, chunk_bs, w_ada, b_ada,
           ln1_g, ln1_b, ln2_g, ln2_b, w_router, router_bias, w_gate_e, w_up_e, w_down_e):
    x = (x_prompt.reshape(T_CTX, D_MODEL), x_sample.reshape(T_LAT, D_MODEL))
    cond = jnp.concatenate([c_ctx[None, :], c, jnp.zeros((N_COND - 1 - DEC_BATCH, D_MODEL), F32)], axis=0)
    mod = _mod_call(cond, w_ada, b_ada).reshape(DEPTH, N_COND, N_MOD, D_MODEL)

    cos_t, sin_t = _rope_tables()
    bd_heads = _const_01(_block_diag(np.ones((HEAD_DIM, HEAD_DIM)), N_Q_HEADS))
    c64, s64 = _dft_mats(FOURIER_GROUP_W, 1.0)
    bdc = _const_bf16(_block_diag(c64, FOURIER_W // FOURIER_GROUP_W))
    bds = _const_bf16(_block_diag(s64, FOURIER_W // FOURIER_GROUP_W))
    dft = {}
    for seq in (SEQ, DEC_SEQ):
        cs, ss = _dft_mats(seq, 1.0 / math.sqrt(seq * FOURIER_GROUP_W))
        dft[seq] = _const_bf16(np.concatenate([cs, -ss], axis=1))
    ctx_k = cache_k.reshape(DEC_BATCH, DEPTH, PAST_LEN, KV_W)
    ctx_v = cache_v.reshape(DEC_BATCH, DEPTH, PAST_LEN, KV_W)
    wr = jnp.pad(w_router, ((0, 0), (0, ROUTER_PAD - N_EXPERTS)))
    wr_hi = wr.astype(BF16)
    wr_lo = (wr - wr_hi.astype(F32)).astype(BF16)
    bias_t = jnp.broadcast_to(router_bias[:, None], (N_EXPERTS, TM))
    tri = _const_01(np.triu(np.ones((SORT_BLK, SORT_BLK)), 1))
    low = _const_01(np.tril(np.ones((CLASS_PAD, CLASS_PAD)), -1))

    w_in_b = w_in.astype(BF16)
    wpa, wpf, wpc, wo = (w.astype(BF16) for w in (w_proj_att, w_proj_fourier, w_proj_chunk, w_out))
    wge, wue, wde = w_gate_e, w_up_e, w_down_e
    qg = jnp.tile(q_norm_g, (1, N_Q_HEADS))[:, None, :]
    kg = jnp.tile(k_norm_g, (1, N_KV_HEADS))[:, None, :]
    ws = chunk_ws.astype(BF16)
    bsm = jnp.repeat(jnp.swapaxes(chunk_bs, 1, 2), CHUNK_GROUP_W, axis=2)
    row3 = lambda a: a[:, None, :]

    new_k = jnp.zeros((BATCH, DEPTH, SEQ, KV_W), F32)
    new_v = jnp.zeros((BATCH, DEPTH, SEQ, KV_W), F32)
    proj_args = (mod, w_in_b, qg, kg, cos_t, sin_t, bd_heads)
    for l in range(DEPTH):
        if l == 0:
            mix, new_k, new_v = _proj_call(l, x, *proj_args, new_k, new_v)
        else:
            mix, new_k, new_v, *x = _proj_call(l, None, *proj_args, new_k, new_v,
                                               prev=(x1, moe, row3(ln2_g), row3(ln2_b)))
        mixer_args = (bdc, bds, ws, bsm, row3(chunk_ln_g), row3(chunk_ln_b))
        mixed_ctx = _mixer_call(l, mix, None, None, dft[SEQ], *mixer_args, latent=False)
        mixed_lat = _mixer_call(l, mix, ctx_k, ctx_v, dft[DEC_SEQ], *mixer_args, latent=True)
        x1, ha, rt = _merge_call(l, x, mixed_ctx, mixed_lat, mod, w_in_b, wpa, wpf, wpc, wo,
                                 row3(ln1_g), row3(ln1_b), wr_hi, wr_lo, bias_t)
        pos2d, tiles = _sort_call(rt, tri, low)
        moe = _moe_call(l, tiles, pos2d[0], ha, wge, wue, wde)
    x = _post_call(DEPTH - 1, x1, moe, mod, row3(ln2_g), row3(ln2_b))

    y_prompt = x[0].reshape(BATCH, SEQ, D_MODEL)
    y_sample = x[1].reshape(DEC_BATCH, DEC_SEQ, D_MODEL)
    cache_shape = (BATCH, DEPTH, SEQ, N_KV_HEADS, HEAD_DIM)
    return (y_prompt, y_sample, new_k.reshape(cache_shape), new_v.reshape(cache_shape))
```

```python
import functools
import math

import jax
import jax.numpy as jnp
import numpy as np
from jax import lax
from jax.experimental import pallas as pl
from jax.experimental.pallas import tpu as pltpu

F32 = jnp.float32
BF16 = jnp.bfloat16

D_MODEL = 1024
BATCH = 16
SEQ = 256
DEPTH = 2
DEC_BATCH = 2
DEC_SEQ = 1024
PAST_LEN = 256
GRID_W = 64
N_Q_HEADS = 8
N_KV_HEADS = 2
HEAD_DIM = 64
Q_PER_KV = N_Q_HEADS // N_KV_HEADS
ATT_W = N_Q_HEADS * HEAD_DIM
KV_W = N_KV_HEADS * HEAD_DIM
ROPE_THETA = 10000.0
FOURIER_GROUP_W = 64
FOURIER_W = 256
N_CHUNK_GROUPS = 4
CHUNK_GROUP_W = 64
CHUNK_W = 256
CHUNK = 128
N_EXPERTS = 16
N_EXPERT_GROUPS = 4
EXPERTS_PER_GROUP = 4
D_EXPERT = 512
ALPHA = (2 * DEPTH) ** 0.25
LN_EPS = 1e-6
RMS_EPS = 1e-6

LANES = 128
SUBLANES = 8
T_CTX = BATCH * SEQ
T_LAT = DEC_BATCH * DEC_SEQ
T_ALL = T_CTX + T_LAT
IN_W = ATT_W + 2 * KV_W + FOURIER_W + 2 * CHUNK_W + 3 * D_MODEL
MIX_W = ATT_W + 2 * KV_W + FOURIER_W + 2 * CHUNK_W
N_MOD = 6
N_COND = 8
ROUTER_PAD = LANES
TM = 512
TQ = 512
MERGE_SUBS = (256, 256)
PROJ_SUB = 256
CTX_SEQS_PER_STEP = 2
V7X_VMEM_LIMIT = 56 * 1024 * 1024

PAIR_A = (0, 0, 0, 1, 1, 3)
PAIR_B = (1, 2, 3, 3, 2, 2)
N_PAIRS = len(PAIR_A)
N_CLASSES = N_EXPERT_GROUPS * N_PAIRS
CLASS_PAD = 32
ROUTE_ROWS = 8
TM_MOE = 256
N_TILES = T_ALL // TM_MOE + N_CLASSES
R_ROWS = N_TILES * TM_MOE
SORT_BLK = 512
D_CHUNKS = D_MODEL // LANES
HA_W = D_MODEL + LANES
TILE_SLOT_A, TILE_SLOT_B, TILE_ROWS = 0, 1, 2
SMALL_TILE = 64
N_PARK = 2 * TM_MOE
COPY_GROUP = 8


def _cparams(*sem):
    return pltpu.CompilerParams(dimension_semantics=sem, vmem_limit_bytes=V7X_VMEM_LIMIT)


def _cond_row(i, tm):
    n_ctx = T_CTX // tm
    return jnp.where(i < n_ctx, 0, 1 + (i - n_ctx) // (DEC_SEQ // tm))


N_CTX_TILES = T_CTX // TM


def _group_specs():
    return [pl.BlockSpec((TM, D_MODEL), lambda i: (jnp.minimum(i, N_CTX_TILES - 1), 0)),
            pl.BlockSpec((TM, D_MODEL), lambda i: (jnp.maximum(i - N_CTX_TILES, 0), 0))]


def _ln(x):
    mu = jnp.mean(x, axis=-1, keepdims=True)
    xc = x - mu
    var = jnp.mean(xc * xc, axis=-1, keepdims=True)
    return xc * lax.rsqrt(var + LN_EPS)


def _split_bf16(x):
    hi = x.astype(BF16)
    lo = (x - hi.astype(F32)).astype(BF16)
    return hi, lo


def _dot(a, b):
    return jnp.dot(a, b, preferred_element_type=F32)


def _sigmoid(x):
    return 1.0 / (1.0 + jnp.exp(-x))


def _gelu_tanh(x):
    c = np.float32(np.sqrt(2 / np.pi))
    return x * (0.5 * (1.0 + jnp.tanh(c * (x + 0.044715 * (x * x * x)))))


def _mod_kernel(c_ref, w_ref, b_ref, o_ref):
    c = c_ref[...]
    s = c * _sigmoid(c)
    o_ref[0] = _dot(s.astype(BF16), w_ref[0].astype(BF16)) + b_ref[0]


def _mod_call(cond, w_ada, b_ada):
    tn = 1536
    n = N_MOD * D_MODEL
    return pl.pallas_call(
        _mod_kernel,
        out_shape=jax.ShapeDtypeStruct((DEPTH, N_COND, n), F32),
        grid=(DEPTH, n // tn),
        in_specs=[
            pl.BlockSpec((N_COND, D_MODEL), lambda l, j: (0, 0)),
            pl.BlockSpec((1, D_MODEL, tn), lambda l, j: (l, 0, j)),
            pl.BlockSpec((1, 1, tn), lambda l, j: (l, 0, j)),
        ],
        out_specs=pl.BlockSpec((1, N_COND, tn), lambda l, j: (l, 0, j)),
        compiler_params=_cparams("arbitrary", "arbitrary"),
        name="adaln_mod",
    )(cond, w_ada, b_ada.reshape(DEPTH, 1, n))


def _head_rms(q, bd, g):
    hi, lo = _split_bf16(q * q)
    ssum = _dot(hi, bd) + _dot(lo, bd)
    return q * lax.rsqrt(ssum * (1.0 / HEAD_DIM) + RMS_EPS) * g


def _rope(x, cos, sin_signed):
    w = x.shape[-1]
    lane = lax.broadcasted_iota(jnp.int32, x.shape, 1)
    swapped = jnp.where((lane % 32) < 16, pltpu.roll(x, w - 16, 1), pltpu.roll(x, 16, 1))
    return x * cos + swapped * sin_signed


def _layer_row(ref, layer):
    return ref[layer:layer + 1, :]


def _proj_kernel(*refs, after_moe, layer):
    if after_moe:
        (x1_ref, moe_ref, modp_ref, l2g_ref, l2b_ref, mod_ref, w_ref, qg_ref, kg_ref, cos_ref, sin_ref, bd_ref,
         kc_in, vc_in, o_ref, kc_ref, vc_ref, xoc_ref, xol_ref) = refs
    else:
        (xc_ref, xl_ref, mod_ref, w_ref, qg_ref, kg_ref, cos_ref, sin_ref, bd_ref,
         kc_in, vc_in, o_ref, kc_ref, vc_ref) = refs
    del kc_in, vc_in
    m = mod_ref[0, 0]
    in_ctx = pl.program_id(0) < N_CTX_TILES
    bd = bd_ref[...]
    sub_rows = [slice(s * PROJ_SUB, (s + 1) * PROJ_SUB) for s in range(TM // PROJ_SUB)]
    projs = []
    xs = []
    for rows in sub_rows:
        if after_moe:
            moe = moe_ref[rows].reshape(PROJ_SUB, D_MODEL)
            x = (_ln(ALPHA * x1_ref[rows, :] + modp_ref[0, 0][5:6] * moe) * _layer_row(l2g_ref, layer - 1)
                 + _layer_row(l2b_ref, layer - 1))
            xs.append(x)
        else:
            x = jnp.where(in_ctx, xc_ref[rows, :], xl_ref[rows, :])
        h = _ln(x) * (1.0 + m[1:2]) + m[0:1]
        proj = _dot(h.astype(BF16), w_ref[...])
        o_ref[rows, ATT_W + KV_W:] = proj[:, ATT_W + KV_W:]
        projs.append(proj[:, :ATT_W + 2 * KV_W])
    keys = []
    for rows, proj in zip(sub_rows, projs):
        qn = _head_rms(proj[:, :ATT_W], bd, _layer_row(qg_ref, layer))
        kn = _head_rms(proj[:, ATT_W:ATT_W + KV_W], bd[:KV_W, :KV_W], _layer_row(kg_ref, layer))
        cos = cos_ref[rows, :]
        sin = sin_ref[rows, :]
        o_ref[rows, :ATT_W] = jnp.where(in_ctx, qn, _rope(qn, cos, sin))
        o_ref[rows, ATT_W:ATT_W + KV_W] = jnp.where(in_ctx, kn, _rope(kn, cos[:, :KV_W], sin[:, :KV_W]))
        keys.append(kn)

    @pl.when(in_ctx)
    def _():
        for s in range(len(sub_rows)):
            kc_ref[s] = keys[s]
            vc_ref[s] = projs[s][:, ATT_W + KV_W:]
        if after_moe:
            for rows, x in zip(sub_rows, xs):
                xoc_ref[rows, :] = x

    if after_moe:
        @pl.when(jnp.logical_not(in_ctx))
        def _():
            for rows, x in zip(sub_rows, xs):
                xol_ref[rows, :] = x


def _proj_call(layer, x, mod, w_in_b, qg, kg, cos_t, sin_t, bd, k_cache, v_cache, prev=None):
    assert PROJ_SUB == SEQ
    n_ctx = T_CTX // TM
    per_seq = DEC_SEQ // TM
    after_moe = prev is not None

    def rope_idx(i):
        return (jnp.where(i < n_ctx, 0, (i - n_ctx) % per_seq), 0)

    cache_spec = pl.BlockSpec((TM // SEQ, None, SEQ, KV_W), lambda i: (jnp.minimum(i, n_ctx - 1), layer, 0, 0))
    cache_shape = jax.ShapeDtypeStruct((BATCH, DEPTH, SEQ, KV_W), F32)
    out_shape = [jax.ShapeDtypeStruct((T_ALL, MIX_W), F32), cache_shape, cache_shape]
    out_specs = [pl.BlockSpec((TM, MIX_W), lambda i: (i, 0)), cache_spec, cache_spec]
    if after_moe:
        x1, moe, l2g, l2b = prev
        x_specs = [
            pl.BlockSpec((TM, D_MODEL), lambda i: (i, 0)),
            pl.BlockSpec((TM, D_CHUNKS, LANES), lambda i: (i, 0, 0)),
            pl.BlockSpec((1, 1, N_MOD, D_MODEL), lambda i: (layer - 1, _cond_row(i, TM), 0, 0)),
            pl.BlockSpec((DEPTH, D_MODEL), lambda i: (0, 0)),
            pl.BlockSpec((DEPTH, D_MODEL), lambda i: (0, 0)),
        ]
        x_args = [x1, moe, mod, l2g, l2b]
        out_shape += [jax.ShapeDtypeStruct((T_CTX, D_MODEL), F32), jax.ShapeDtypeStruct((T_LAT, D_MODEL), F32)]
        out_specs += _group_specs()
    else:
        x_specs, x_args = _group_specs(), list(x)
    n_in = len(x_args) + 9
    return pl.pallas_call(
        functools.partial(_proj_kernel, after_moe=after_moe, layer=layer),
        out_shape=tuple(out_shape),
        grid=(T_ALL // TM,),
        in_specs=x_specs + [
            pl.BlockSpec((1, 1, N_MOD, D_MODEL), lambda i: (layer, _cond_row(i, TM), 0, 0)),
            pl.BlockSpec((None, D_MODEL, MIX_W), lambda i: (layer, 0, 0)),
            pl.BlockSpec((DEPTH, ATT_W), lambda i: (0, 0)),
            pl.BlockSpec((DEPTH, KV_W), lambda i: (0, 0)),
            pl.BlockSpec((TM, ATT_W), rope_idx),
            pl.BlockSpec((TM, ATT_W), rope_idx),
            pl.BlockSpec((ATT_W, ATT_W), lambda i: (0, 0)),
            pl.BlockSpec(memory_space=pl.ANY),
            pl.BlockSpec(memory_space=pl.ANY),
        ],
        out_specs=tuple(out_specs),
        input_output_aliases={n_in - 2: 1, n_in - 1: 2},
        compiler_params=_cparams("arbitrary"),
        name="in_proj",
    )(*x_args, mod, w_in_b, qg, kg, cos_t, sin_t, bd, k_cache, v_cache)


def _attention_tile(q, k_parts, v_parts, phase_major):
    def kv_of(h, parts):
        g = h // Q_PER_KV
        return [p[:, g * HEAD_DIM:(g + 1) * HEAD_DIM] for p in parts]

    def scores(h):
        qh = q[:, h * HEAD_DIM:(h + 1) * HEAD_DIM]
        return [lax.dot_general(qh, k, (((1,), (1,)), ((), ())), preferred_element_type=F32)
                for k in kv_of(h, k_parts)]

    if phase_major:
        all_ss = [scores(h) for h in range(N_Q_HEADS)]
        all_m = [functools.reduce(jnp.maximum, [jnp.max(s, axis=-1, keepdims=True) for s in ss]) for ss in all_ss]
        all_es = [[jnp.exp(s - m) for s in ss] for ss, m in zip(all_ss, all_m)]
        all_den = [functools.reduce(jnp.add, [jnp.sum(e, axis=-1, keepdims=True) for e in es]) for es in all_es]
        outs = [functools.reduce(jnp.add, [_dot(e.astype(BF16), v) for e, v in zip(es, kv_of(h, v_parts))])
                * (1.0 / den) for h, (es, den) in enumerate(zip(all_es, all_den))]
        return jnp.concatenate(outs, axis=-1)
    outs = []
    ss_next = scores(0)
    for h in range(N_Q_HEADS):
        ss = ss_next
        if h + 1 < N_Q_HEADS:
            ss_next = scores(h + 1)
        m = functools.reduce(jnp.maximum, [jnp.max(s, axis=-1, keepdims=True) for s in ss])
        es = [jnp.exp(s - m) for s in ss]
        denom = functools.reduce(jnp.add, [jnp.sum(e, axis=-1, keepdims=True) for e in es])
        o = functools.reduce(jnp.add, [_dot(e.astype(BF16), v) for e, v in zip(es, kv_of(h, v_parts))])
        outs.append(o * (1.0 / denom))
    return jnp.concatenate(outs, axis=-1)


def _mixer_kernel(*refs, seq, seqs, latent, layer):
    if latent:
        (mix_ref, ck_ref, cv_ref, dft_ref, bdc_ref, bds_ref, ws_ref, bsm_ref, clg_ref, clb_ref, o_ref) = refs
    else:
        (mix_ref, dft_ref, bdc_ref, bds_ref, ws_ref, bsm_ref, clg_ref, clb_ref, o_ref) = refs
    tq = min(seq, TQ)
    c0 = ATT_W + 2 * KV_W + FOURIER_W
    lane = lax.broadcasted_iota(jnp.int32, (CHUNK, CHUNK_W), 1)

    for b in range(seqs):
        base = b * seq
        rows = slice(base, base + seq)

        k_new = mix_ref[rows, ATT_W:ATT_W + KV_W].astype(BF16)
        v_new = mix_ref[rows, ATT_W + KV_W:ATT_W + 2 * KV_W].astype(BF16)
        if latent:
            k_parts = [ck_ref[...].astype(BF16), k_new]
            v_parts = [cv_ref[...].astype(BF16), v_new]
        else:
            k_parts, v_parts = [k_new], [v_new]

        def q_tile(t, carry, base=base, k_parts=k_parts, v_parts=v_parts):
            r0 = base + (pl.multiple_of(t * tq, tq) if seq > tq else 0)
            q = (mix_ref[pl.ds(r0, tq), :ATT_W] * (HEAD_DIM ** -0.5)).astype(BF16)
            o_ref[pl.ds(r0, tq), :ATT_W] = _attention_tile(q, k_parts, v_parts, phase_major=not latent).astype(BF16)
            return carry

        if seq > tq:
            lax.fori_loop(0, seq // tq, q_tile, 0)
        else:
            q_tile(0, 0)

        f_hi, f_lo = _split_bf16(mix_ref[rows, ATT_W + 2 * KV_W:ATT_W + 2 * KV_W + FOURIER_W])
        bdc = bdc_ref[...]
        bds = bds_ref[...]
        y = jnp.concatenate([_dot(f_hi, bdc) + _dot(f_lo, bdc), _dot(f_hi, bds) + _dot(f_lo, bds)], axis=0)
        y_hi, y_lo = _split_bf16(y)
        dft = dft_ref[...]
        four = _dot(dft, y_hi) + _dot(dft, y_lo)
        o_ref[rows, ATT_W:ATT_W + FOURIER_W] = four.astype(BF16)

        u = _gelu_tanh(mix_ref[rows, c0:c0 + CHUNK_W])
        vn = (_ln(_gelu_tanh(mix_ref[rows, c0 + CHUNK_W:c0 + 2 * CHUNK_W])) * _layer_row(clg_ref, layer)
              + _layer_row(clb_ref, layer)).astype(BF16)
        bsm = bsm_ref[...]
        for c in range(seq // CHUNK):
            vc = vn[c * CHUNK:(c + 1) * CHUNK]
            sv = bsm
            for g in range(N_CHUNK_GROUPS):
                vg = jnp.where(lane // CHUNK_GROUP_W == g, vc, jnp.zeros_like(vc))
                sv = sv + _dot(ws_ref[g], vg)
            o_ref[base + c * CHUNK:base + (c + 1) * CHUNK, ATT_W + FOURIER_W:] = (
                u[c * CHUNK:(c + 1) * CHUNK] * sv).astype(BF16)


def _mixer_call(layer, mix, ctx_k, ctx_v, dft, bdc, bds, ws, bsm, clg, clb, *, latent):
    seq = DEC_SEQ if latent else SEQ
    seqs = 1 if latent else CTX_SEQS_PER_STEP
    nb = (DEC_BATCH if latent else BATCH) // seqs
    row0 = (T_CTX // seq) if latent else 0
    const2 = lambda b: (0, 0)
    per_layer = lambda b: (layer, 0, 0)
    in_specs = [pl.BlockSpec((seqs * seq, MIX_W), lambda b: (row0 + b, 0))]
    args = [mix]
    if latent:
        in_specs += [pl.BlockSpec((None, None, PAST_LEN, KV_W), lambda b: (b, layer, 0, 0))] * 2
        args += [ctx_k, ctx_v]
    in_specs += [
        pl.BlockSpec((seq, 2 * seq), const2),
        pl.BlockSpec((FOURIER_W, FOURIER_W), const2),
        pl.BlockSpec((FOURIER_W, FOURIER_W), const2),
        pl.BlockSpec((None, N_CHUNK_GROUPS, CHUNK, CHUNK), lambda b: (layer, 0, 0, 0)),
        pl.BlockSpec((None, CHUNK, CHUNK_W), per_layer),
        pl.BlockSpec((DEPTH, CHUNK_W), const2),
        pl.BlockSpec((DEPTH, CHUNK_W), const2),
    ]
    args += [dft, bdc, bds, ws, bsm, clg, clb]
    return pl.pallas_call(
        functools.partial(_mixer_kernel, seq=seq, seqs=seqs, latent=latent, layer=layer),
        out_shape=jax.ShapeDtypeStruct((nb * seqs * seq, D_MODEL), BF16),
        grid=(nb,),
        in_specs=in_specs,
        out_specs=pl.BlockSpec((seqs * seq, D_MODEL), lambda b: (b, 0)),
        compiler_params=_cparams("arbitrary"),
        name="mixer_latent" if latent else "mixer_context",
    )(*args)


def _route(logits, bias_t):
    lt = logits.T[:N_EXPERTS]
    ex = jnp.exp(lt - jnp.max(lt, axis=0, keepdims=True))
    probs = ex / jnp.sum(ex, axis=0, keepdims=True)
    sel = probs + bias_t
    p = [probs[e:e + 1] for e in range(N_EXPERTS)]
    s = [sel[e:e + 1] for e in range(N_EXPERTS)]
    n = EXPERTS_PER_GROUP
    scores = []
    for g in range(N_EXPERT_GROUPS):
        pair = [s[g * n + a] + s[g * n + b] for a in range(n) for b in range(a + 1, n)]
        scores.append(functools.reduce(jnp.maximum, pair))
    best = jnp.zeros_like(scores[0], dtype=jnp.int32)
    best_score = scores[0]
    for g in range(1, N_EXPERT_GROUPS):
        better = scores[g] > best_score
        best = jnp.where(better, g, best)
        best_score = jnp.where(better, scores[g], best_score)
    cls = jnp.zeros_like(best_score)
    w_a = jnp.zeros_like(best_score)
    w_b = jnp.zeros_like(best_score)
    for g in range(N_EXPERT_GROUPS):
        in_g = best == g
        chosen = []
        for a in range(n):
            rank = jnp.zeros_like(best)
            for b in range(n):
                if b == a:
                    continue
                ahead = (s[g * n + b] > s[g * n + a]) if b > a else (s[g * n + b] >= s[g * n + a])
                rank = rank + ahead.astype(jnp.int32)
            chosen.append(jnp.logical_and(in_g, rank < 2))
        for k in range(N_PAIRS):
            pa, pb = p[g * n + PAIR_A[k]], p[g * n + PAIR_B[k]]
            hit = jnp.logical_and(chosen[PAIR_A[k]], chosen[PAIR_B[k]])
            wsum = pa + pb
            cls = jnp.where(hit, float(g * N_PAIRS + k), cls)
            w_a = jnp.where(hit, pa / wsum, w_a)
            w_b = jnp.where(hit, pb / wsum, w_b)
    return jnp.concatenate([cls, w_a, w_b, jnp.zeros((ROUTE_ROWS - 3, lt.shape[1]), F32)], axis=0)


def _merge_kernel(xc_ref, xl_ref, mixc_ref, mixl_ref, mod_ref, win_ref, wpa_ref, wpf_ref, wpc_ref, wo_ref,
                  l1g_ref, l1b_ref, wrh_ref, wrl_ref, bias_ref, x1_ref, ha_ref, rt_ref, *, layer):
    m = mod_ref[0, 0]
    in_ctx = pl.program_id(0) < N_CTX_TILES
    branch_w = ((0, ATT_W, wpa_ref), (ATT_W, FOURIER_W, wpf_ref), (ATT_W + FOURIER_W, CHUNK_W, wpc_ref))
    bounds = np.cumsum((0,) + MERGE_SUBS)
    sub_rows = [slice(int(a), int(b)) for a, b in zip(bounds[:-1], bounds[1:])]
    residual = []
    for rows in sub_rows:
        x = jnp.where(in_ctx, xc_ref[rows, :], xl_ref[rows, :])
        mixed = jnp.where(in_ctx, mixc_ref[rows, :], mixl_ref[rows, :])
        h = (_ln(x) * (1.0 + m[1:2]) + m[0:1]).astype(BF16)
        merged = None
        for b, (c0, width, w_ref) in enumerate(branch_w):
            gate = _sigmoid(_dot(h, win_ref[:, MIX_W + b * D_MODEL:MIX_W + (b + 1) * D_MODEL]))
            term = gate * _dot(mixed[:, c0:c0 + width], w_ref[...])
            merged = term if merged is None else merged + term
        mix = _dot(merged.astype(BF16), wo_ref[...])
        residual.append(ALPHA * x + m[2:3] * mix)
    for rows, pre in zip(sub_rows, residual):
        x1 = _ln(pre) * _layer_row(l1g_ref, layer) + _layer_row(l1b_ref, layer)
        x1_ref[rows, :] = x1
        h2 = _ln(x1) * (1.0 + m[4:5]) + m[3:4]
        h_hi, h_lo = _split_bf16(h2)
        wrh = wrh_ref[...]
        logits = _dot(h_hi, wrh) + _dot(h_lo, wrh) + _dot(h_hi, wrl_ref[...])
        rt = _route(logits, bias_ref[:, rows])
        rt_ref[:, rows] = rt
        ha_ref[rows, :D_MODEL] = h2
        ha_ref[rows, D_MODEL:] = jnp.concatenate(
            [rt, jnp.zeros((LANES - ROUTE_ROWS, rows.stop - rows.start), F32)], axis=0).T


def _merge_call(layer, x, mixed_ctx, mixed_lat, mod, w_in_b, wpa, wpf, wpc, wo, l1g, l1b, wrh, wrl, bias_t):
    const2 = lambda i: (0, 0)
    per_layer = lambda i: (layer, 0, 0)
    tok = lambda i: (i, 0)
    return pl.pallas_call(
        functools.partial(_merge_kernel, layer=layer),
        out_shape=(jax.ShapeDtypeStruct((T_ALL, D_MODEL), F32),
                   jax.ShapeDtypeStruct((T_ALL, HA_W), F32),
                   jax.ShapeDtypeStruct((ROUTE_ROWS, T_ALL), F32)),
        grid=(T_ALL // TM,),
        in_specs=_group_specs() + _group_specs() + [
            pl.BlockSpec((1, 1, N_MOD, D_MODEL), lambda i: (layer, _cond_row(i, TM), 0, 0)),
            pl.BlockSpec((None, D_MODEL, IN_W), per_layer, pipeline_mode=pl.Buffered(1)),
            pl.BlockSpec((None, ATT_W, D_MODEL), per_layer, pipeline_mode=pl.Buffered(1)),
            pl.BlockSpec((None, FOURIER_W, D_MODEL), per_layer, pipeline_mode=pl.Buffered(1)),
            pl.BlockSpec((None, CHUNK_W, D_MODEL), per_layer, pipeline_mode=pl.Buffered(1)),
            pl.BlockSpec((None, D_MODEL, D_MODEL), per_layer, pipeline_mode=pl.Buffered(1)),
            pl.BlockSpec((DEPTH, D_MODEL), const2),
            pl.BlockSpec((DEPTH, D_MODEL), const2),
            pl.BlockSpec((D_MODEL, ROUTER_PAD), const2),
            pl.BlockSpec((D_MODEL, ROUTER_PAD), const2),
            pl.BlockSpec((N_EXPERTS, TM), const2),
        ],
        out_specs=(pl.BlockSpec((TM, D_MODEL), tok),
                   pl.BlockSpec((TM, HA_W), tok),
                   pl.BlockSpec((ROUTE_ROWS, TM), lambda i: (0, i))),
        compiler_params=_cparams("arbitrary"),
        name="merge_route",
    )(*x, mixed_ctx, mixed_lat, mod, w_in_b, wpa, wpf, wpc, wo, l1g, l1b, wrh, wrl, bias_t)


def _sort_kernel(rt_ref, tri_ref, low_ref, pos_ref, tile_ref):
    crow = lax.broadcasted_iota(jnp.int32, (CLASS_PAD, SORT_BLK), 0)
    tri = tri_ref[...]
    n_blk = T_ALL // SORT_BLK
    carry = jnp.zeros((CLASS_PAD, 1), F32)
    ranks = []
    for b in range(n_blk):
        hot = rt_ref[0:1, b * SORT_BLK:(b + 1) * SORT_BLK].astype(jnp.int32) == crow
        hot_f = jnp.where(hot, 1.0, 0.0)
        before = _dot(hot_f.astype(BF16), tri) + carry
        ranks.append(jnp.sum(jnp.where(hot, before, 0.0), axis=0, keepdims=True))
        carry = carry + jnp.sum(hot_f, axis=1, keepdims=True)
    padded = jnp.floor((carry + (TM_MOE - 1.0)) * (1.0 / TM_MOE)) * TM_MOE
    padded = jnp.broadcast_to(padded, (CLASS_PAD, LANES))
    offs = _dot(low_ref[...], padded.astype(BF16))
    for b in range(n_blk):
        hot = rt_ref[0:1, b * SORT_BLK:(b + 1) * SORT_BLK].astype(jnp.int32) == crow
        base = jnp.sum(jnp.where(hot, offs[:, 0:1], 0.0), axis=0, keepdims=True)
        pos_ref[:, b * SORT_BLK:(b + 1) * SORT_BLK] = (base + ranks[b]).astype(jnp.int32)

    start = lax.broadcasted_iota(jnp.int32, (1, LANES), 1).astype(F32) * TM_MOE
    is_class = lax.broadcasted_iota(jnp.int32, (CLASS_PAD, LANES), 0) < N_CLASSES
    ends = jnp.where(is_class, offs + padded, 0.0)
    total = jnp.max(ends, axis=0, keepdims=True)
    valid = start < total
    tcls = jnp.sum(jnp.where(jnp.logical_and(is_class, ends <= start), 1.0, 0.0), axis=0, keepdims=True)
    last = jnp.max(jnp.where(valid, tcls, 0.0), axis=1, keepdims=True)
    tcls = jnp.where(valid, tcls, last)
    grp = functools.reduce(jnp.add, [jnp.where(tcls >= g * N_PAIRS, 1.0, 0.0) for g in range(1, N_EXPERT_GROUPS)])
    pair = tcls - grp * N_PAIRS
    slot_a = functools.reduce(jnp.add, [jnp.where(pair == k, float(PAIR_A[k]), 0.0) for k in range(N_PAIRS)])
    slot_b = functools.reduce(jnp.add, [jnp.where(pair == k, float(PAIR_B[k]), 0.0) for k in range(N_PAIRS)])
    crow_t = lax.broadcasted_iota(jnp.int32, (CLASS_PAD, LANES), 0).astype(F32)
    real_end = jnp.sum(jnp.where(crow_t == tcls, offs + carry, 0.0), axis=0, keepdims=True)
    n_rows = jnp.where(valid, jnp.clip(real_end - start, 0.0, float(TM_MOE)), 0.0)
    rows = [grp * EXPERTS_PER_GROUP + slot_a, grp * EXPERTS_PER_GROUP + slot_b, n_rows]
    tile_ref[...] = jnp.concatenate(rows + [jnp.zeros((8 - len(rows), LANES), F32)], axis=0).astype(jnp.int32)


def _sort_call(rt, tri, low):
    return pl.pallas_call(
        _sort_kernel,
        out_shape=(jax.ShapeDtypeStruct((1, T_ALL), jnp.int32),
                   jax.ShapeDtypeStruct((8, LANES), jnp.int32)),
        name="route_sort",
    )(rt, tri, low)


def _moe_kernel(tile_ref, pos_ref, ha_hbm, wg_hbm, wu_hbm, wd_hbm, out_ref, hbuf, ybuf, src_ref, ssem,
                stg_g, stg_u, stg_d, act_g, act_u, act_d, wsem, ha_ref, hsem, staged_ref, *, layer):
    j = pl.program_id(0)
    slot = j % 2

    def n_rows(tile):
        inside = jnp.logical_and(tile >= 0, tile < N_TILES)
        return jnp.where(inside, tile_ref[TILE_ROWS, jnp.clip(tile, 0, N_TILES - 1)], 0)

    def token_of(tile, n, r, tail):
        return src_ref[tile * TM_MOE + (jnp.minimum(r, n - 1) if tail else r)]

    def scatter_copy(tile, n, buf_slot, r, tail):
        tok = token_of(tile, n, r, tail)
        dst = jnp.where(r < n, tok, T_ALL + buf_slot * TM_MOE + r) if tail else tok
        return pltpu.make_async_copy(ybuf.at[buf_slot, r], out_ref.at[dst], ssem.at[buf_slot])

    def for_groups(n, fn):
        full = lax.shift_right_logical(n, 3)

        def body(g, c):
            for k in range(COPY_GROUP):
                fn(g, k, False)
            return c
        lax.fori_loop(0, full, body, 0)

        @pl.when(jnp.bitwise_and(n, COPY_GROUP - 1) != 0)
        def _():
            for k in range(COPY_GROUP):
                fn(full, k, True)

    def retire(tile, buf_slot):
        n = n_rows(tile)
        one = scatter_copy(tile, n, buf_slot, 0, False)

        def body(g, c):
            for _ in range(COPY_GROUP):
                one.wait()
            return c
        lax.fori_loop(0, lax.shift_right_logical(n + COPY_GROUP - 1, 3), body, 0)

    def expert_of(x, tile):
        return tile_ref[TILE_SLOT_A + x, tile]

    def weight_copies(x, e):
        return [pltpu.make_async_copy(src.at[layer, e], stg.at[x], wsem.at[x])
                for src, stg in ((wg_hbm, stg_g), (wu_hbm, stg_u), (wd_hbm, stg_d))]

    @pl.when(j == 0)
    def _():
        load_inputs = pltpu.make_async_copy(ha_hbm, ha_ref, hsem)
        load_inputs.start()
        for x in range(2):
            for c in weight_copies(x, expert_of(x, 0)):
                c.start()
            staged_ref[x] = expert_of(x, 0)

        def place(t, c):
            src_ref[pos_ref[t]] = t
            return c
        lax.fori_loop(0, T_ALL, place, 0, unroll=8)
        load_inputs.wait()
        hbuf[...] = jnp.zeros(hbuf.shape, F32)
        ybuf[...] = jnp.zeros(ybuf.shape, F32)
        for s in range(2):
            park = pltpu.make_async_copy(ybuf.at[s], out_ref.at[pl.ds(T_ALL + s * TM_MOE, TM_MOE)], ssem.at[s])
            park.start()
            park.wait()

    @pl.when(n_rows(j - 2) > 0)
    def _():
        retire(j - 2, slot)

    def run_experts(rows):
        rec = hbuf[:rows // SUBLANES].reshape(rows, HA_W)
        h = rec[:, :D_MODEL].astype(BF16)

        def expert(x):
            a = _dot(h, act_g[x])
            u = _dot(h, act_u[x])
            w = rec[:, D_MODEL + 1 + x:D_MODEL + 2 + x]
            return _dot(((a * _sigmoid(a)) * u * w).astype(BF16), act_d[x])

        ybuf[slot, :rows] = (expert(0) + expert(1)).reshape(rows, D_CHUNKS, LANES)

    @pl.when(n_rows(j) > 0)
    def _():
        for x in range(2):
            prev = jnp.maximum(j - 1, 0)

            @pl.when(jnp.logical_or(j == 0, expert_of(x, j) != expert_of(x, prev)))
            def _():
                for c in weight_copies(x, expert_of(x, j)):
                    c.wait()
                act_g[x] = stg_g[x].astype(BF16)
                act_u[x] = stg_u[x].astype(BF16)
                act_d[x] = stg_d[x].astype(BF16)

        for x in range(2):
            cur = expert_of(x, j)

            def expert_or_cur(tile):
                return jnp.where(n_rows(tile) > 0, expert_of(x, jnp.minimum(tile, N_TILES - 1)), cur)
            e1, e2 = expert_or_cur(j + 1), expert_or_cur(j + 2)
            target = jnp.where(e1 != cur, e1, e2)

            @pl.when(jnp.logical_and(target != cur, target != staged_ref[x]))
            def _():
                for c in weight_copies(x, target):
                    c.start()
                staged_ref[x] = target

        n = n_rows(j)

        def gather_row(g, k, tail):
            tok = token_of(j, n, g * COPY_GROUP + k, tail)
            grp, sub = lax.shift_right_logical(tok, 3), jnp.bitwise_and(tok, SUBLANES - 1)
            hbuf[g, pl.ds(k, 1), :] = ha_ref[grp, pl.ds(sub, 1), :]
        for_groups(n, gather_row)

        @pl.when(n > SMALL_TILE)
        def _():
            run_experts(TM_MOE)

        @pl.when(n <= SMALL_TILE)
        def _():
            run_experts(SMALL_TILE)

        for_groups(n, lambda g, k, tail: scatter_copy(j, n, slot, g * COPY_GROUP + k, tail).start(priority=k % 2))

    @pl.when(j == N_TILES - 1)
    def _():
        @pl.when(n_rows(j - 1) > 0)
        def _():
            retire(j - 1, 1 - slot)

        @pl.when(n_rows(j) > 0)
        def _():
            retire(j, slot)


def _moe_call(layer, tiles, pos, ha, wge, wue, wde):
    up = (2, D_MODEL, D_EXPERT)
    down = (2, D_EXPERT, D_MODEL)
    grid_spec = pltpu.PrefetchScalarGridSpec(
        num_scalar_prefetch=2,
        grid=(N_TILES,),
        in_specs=[pl.BlockSpec(memory_space=pl.ANY)] * 4,
        out_specs=pl.BlockSpec(memory_space=pl.ANY),
        scratch_shapes=[
            pltpu.VMEM((TM_MOE // SUBLANES, SUBLANES, HA_W), F32),
            pltpu.VMEM((2, TM_MOE, D_CHUNKS, LANES), F32),
            pltpu.SMEM((R_ROWS,), jnp.int32),
            pltpu.SemaphoreType.DMA((2,)),
            pltpu.VMEM(up, F32), pltpu.VMEM(up, F32), pltpu.VMEM(down, F32),
            pltpu.VMEM(up, BF16), pltpu.VMEM(up, BF16), pltpu.VMEM(down, BF16),
            pltpu.SemaphoreType.DMA((2,)),
            pltpu.VMEM((T_ALL // SUBLANES, SUBLANES, HA_W), F32),
            pltpu.SemaphoreType.DMA(()),
            pltpu.SMEM((2,), jnp.int32),
        ],
    )
    return pl.pallas_call(
        functools.partial(_moe_kernel, layer=layer),
        out_shape=jax.ShapeDtypeStruct((T_ALL + N_PARK, D_CHUNKS, LANES), F32),
        grid_spec=grid_spec,
        compiler_params=pltpu.CompilerParams(dimension_semantics=("arbitrary",), vmem_limit_bytes=V7X_VMEM_LIMIT,
                                             has_side_effects=True),
        name="moe_pairs",
    )(tiles, pos,
      ha.reshape(T_ALL // SUBLANES, SUBLANES, HA_W), wge, wue, wde)


def _post_kernel(x1_ref, moe_ref, mod_ref, l2g_ref, l2b_ref, oc_ref, ol_ref, *, layer):
    m = mod_ref[0, 0]
    moe = moe_ref[...].reshape(TM, D_MODEL)
    y = _ln(ALPHA * x1_ref[...] + m[5:6] * moe) * _layer_row(l2g_ref, layer) + _layer_row(l2b_ref, layer)
    i = pl.program_id(0)

    @pl.when(i < N_CTX_TILES)
    def _():
        oc_ref[...] = y

    @pl.when(i >= N_CTX_TILES)
    def _():
        ol_ref[...] = y


def _post_call(layer, x1, moe, mod, l2g, l2b):
    tok = lambda i: (i, 0)
    return pl.pallas_call(
        functools.partial(_post_kernel, layer=layer),
        out_shape=(jax.ShapeDtypeStruct((T_CTX, D_MODEL), F32), jax.ShapeDtypeStruct((T_LAT, D_MODEL), F32)),
        grid=(T_ALL // TM,),
        in_specs=[
            pl.BlockSpec((TM, D_MODEL), tok),
            pl.BlockSpec((TM, D_CHUNKS, LANES), lambda i: (i, 0, 0)),
            pl.BlockSpec((1, 1, N_MOD, D_MODEL), lambda i: (layer, _cond_row(i, TM), 0, 0)),
            pl.BlockSpec((DEPTH, D_MODEL), lambda i: (0, 0)),
            pl.BlockSpec((DEPTH, D_MODEL), lambda i: (0, 0)),
        ],
        out_specs=tuple(_group_specs()),
        compiler_params=_cparams("arbitrary"),
        name="post_moe",
    )(x1, moe, mod, l2g, l2b)


def _rope_tables():
    pos = np.arange(DEC_SEQ)
    quarter = HEAD_DIM // 4
    inv = ROPE_THETA ** (-np.arange(quarter, dtype=np.float64) / quarter)
    ang_r = (pos // GRID_W)[:, None] * inv[None, :]
    ang_c = (pos % GRID_W)[:, None] * inv[None, :]
    cos = np.concatenate([np.cos(ang_r)] * 2 + [np.cos(ang_c)] * 2, axis=-1)
    sin = np.concatenate([-np.sin(ang_r), np.sin(ang_r), -np.sin(ang_c), np.sin(ang_c)], axis=-1)
    return (jnp.asarray(np.tile(cos, (1, N_Q_HEADS)).astype(np.float32)),
            jnp.asarray(np.tile(sin, (1, N_Q_HEADS)).astype(np.float32)))


def _dft_mats(n, scale):
    j = np.arange(n)
    ang = ((j[:, None] * j[None, :]) % n) * (2 * np.pi / n)
    return np.cos(ang) * scale, np.sin(ang) * scale


def _block_diag(m, reps):
    return np.kron(np.eye(reps), m)


def _const_bf16(a):
    return jnp.asarray(np.asarray(a, np.float32)).astype(BF16)


def _const_01(a):
    return jnp.asarray(np.asarray(a, np.float32).astype(BF16))


def kernel(x_prompt, x_sample, cache_k, cache_v, c, c_ctx, w_in, q_norm_g, k_norm_g, w_proj_att,
           w_proj_fourier, w_proj_chunk, w_out, chunk_ln_g, chunk_ln_b, chunk_ws, chunk_bs, w_ada, b_ada,
           ln1_g, ln1_b, ln2_g, ln2_b, w_router, router_bias, w_gate_e, w_up_e, w_down_e):
    x = (x_prompt.reshape(T_CTX, D_MODEL), x_sample.reshape(T_LAT, D_MODEL))
    cond = jnp.concatenate([c_ctx[None, :], c, jnp.zeros((N_COND - 1 - DEC_BATCH, D_MODEL), F32)], axis=0)
    mod = _mod_call(cond, w_ada, b_ada).reshape(DEPTH, N_COND, N_MOD, D_MODEL)

    cos_t, sin_t = _rope_tables()
    bd_heads = _const_01(_block_diag(np.ones((HEAD_DIM, HEAD_DIM)), N_Q_HEADS))
    c64, s64 = _dft_mats(FOURIER_GROUP_W, 1.0)
    bdc = _const_bf16(_block_diag(c64, FOURIER_W // FOURIER_GROUP_W))
    bds = _const_bf16(_block_diag(s64, FOURIER_W // FOURIER_GROUP_W))
    dft = {}
    for seq in (SEQ, DEC_SEQ):
        cs, ss = _dft_mats(seq, 1.0 / math.sqrt(seq * FOURIER_GROUP_W))
        dft[seq] = _const_bf16(np.concatenate([cs, -ss], axis=1))
    ctx_k = cache_k.reshape(DEC_BATCH, DEPTH, PAST_LEN, KV_W)
    ctx_v = cache_v.reshape(DEC_BATCH, DEPTH, PAST_LEN, KV_W)
    wr = jnp.pad(w_router, ((0, 0), (0, ROUTER_PAD - N_EXPERTS)))
    wr_hi = wr.astype(BF16)
    wr_lo = (wr - wr_hi.astype(F32)).astype(BF16)
    bias_t = jnp.broadcast_to(router_bias[:, None], (N_EXPERTS, TM))
    tri = _const_01(np.triu(np.ones((SORT_BLK, SORT_BLK)), 1))
    low = _const_01(np.tril(np.ones((CLASS_PAD, CLASS_PAD)), -1))

    w_in_b = w_in.astype(BF16)
    wpa, wpf, wpc, wo = (w.astype(BF16) for w in (w_proj_att, w_proj_fourier, w_proj_chunk, w_out))
    wge, wue, wde = w_gate_e, w_up_e, w_down_e
    qg = jnp.tile(q_norm_g, (1, N_Q_HEADS))
    kg = jnp.tile(k_norm_g, (1, N_KV_HEADS))
    ws = chunk_ws.astype(BF16)
    bsm = jnp.repeat(jnp.swapaxes(chunk_bs, 1, 2), CHUNK_GROUP_W, axis=2)

    new_k = jnp.zeros((BATCH, DEPTH, SEQ, KV_W), F32)
    new_v = jnp.zeros((BATCH, DEPTH, SEQ, KV_W), F32)
    proj_args = (mod, w_in_b, qg, kg, cos_t, sin_t, bd_heads)
    for l in range(DEPTH):
        if l == 0:
            mix, new_k, new_v = _proj_call(l, x, *proj_args, new_k, new_v)
        else:
            mix, new_k, new_v, *x = _proj_call(l, None, *proj_args, new_k, new_v,
                                               prev=(x1, moe, ln2_g, ln2_b))
        mixer_args = (bdc, bds, ws, bsm, chunk_ln_g, chunk_ln_b)
        mixed_ctx = _mixer_call(l, mix, None, None, dft[SEQ], *mixer_args, latent=False)
        mixed_lat = _mixer_call(l, mix, ctx_k, ctx_v, dft[DEC_SEQ], *mixer_args, latent=True)
        x1, ha, rt = _merge_call(l, x, mixed_ctx, mixed_lat, mod, w_in_b, wpa, wpf, wpc, wo,
                                 ln1_g, ln1_b, wr_hi, wr_lo, bias_t)
        pos2d, tiles = _sort_call(rt, tri, low)
        moe = _moe_call(l, tiles, pos2d[0], ha, wge, wue, wde)
    x = _post_call(DEPTH - 1, x1, moe, mod, ln2_g, ln2_b)

    y_prompt = x[0].reshape(BATCH, SEQ, D_MODEL)
    y_sample = x[1].reshape(DEC_BATCH, DEC_SEQ, D_MODEL)
    cache_shape = (BATCH, DEPTH, SEQ, N_KV_HEADS, HEAD_DIM)
    return (y_prompt, y_sample, new_k.reshape(cache_shape), new_v.reshape(cache_shape))
```

```python
import functools
import math

import jax
import jax.numpy as jnp
import numpy as np
from jax import lax
from jax.experimental import pallas as pl
from jax.experimental.pallas import tpu as pltpu

F32 = jnp.float32
BF16 = jnp.bfloat16

D_MODEL = 1024
BATCH = 16
SEQ = 256
DEPTH = 2
DEC_BATCH = 2
DEC_SEQ = 1024
PAST_LEN = 256
GRID_W = 64
N_Q_HEADS = 8
N_KV_HEADS = 2
HEAD_DIM = 64
Q_PER_KV = N_Q_HEADS // N_KV_HEADS
ATT_W = N_Q_HEADS * HEAD_DIM
KV_W = N_KV_HEADS * HEAD_DIM
ROPE_THETA = 10000.0
FOURIER_GROUP_W = 64
FOURIER_W = 256
N_CHUNK_GROUPS = 4
CHUNK_GROUP_W = 64
CHUNK_W = 256
CHUNK = 128
N_EXPERTS = 16
N_EXPERT_GROUPS = 4
EXPERTS_PER_GROUP = 4
D_EXPERT = 512
ALPHA = (2 * DEPTH) ** 0.25
LN_EPS = 1e-6
RMS_EPS = 1e-6

LANES = 128
SUBLANES = 8
T_CTX = BATCH * SEQ
T_LAT = DEC_BATCH * DEC_SEQ
T_ALL = T_CTX + T_LAT
IN_W = ATT_W + 2 * KV_W + FOURIER_W + 2 * CHUNK_W + 3 * D_MODEL
MIX_W = ATT_W + 2 * KV_W + FOURIER_W + 2 * CHUNK_W
N_MOD = 6
N_COND = 8
ROUTER_PAD = LANES
TM = 512
TQ = 512
MERGE_SUBS = (256, 256)
PROJ_SUB = 256
CTX_SEQS_PER_STEP = 2
V7X_VMEM_LIMIT = 56 * 1024 * 1024

PAIR_A = (0, 0, 0, 1, 1, 3)
PAIR_B = (1, 2, 3, 3, 2, 2)
N_PAIRS = len(PAIR_A)
N_CLASSES = N_EXPERT_GROUPS * N_PAIRS
CLASS_PAD = 32
ROUTE_ROWS = 8
TM_MOE = 256
N_TILES = T_ALL // TM_MOE + N_CLASSES
R_ROWS = N_TILES * TM_MOE
SORT_BLK = 512
D_CHUNKS = D_MODEL // LANES
HA_W = D_MODEL + LANES
TILE_SLOT_A, TILE_SLOT_B, TILE_ROWS = 0, 1, 2
SMALL_TILE = 64
N_PARK = 2 * TM_MOE
COPY_GROUP = SUBLANES
GROUP_SHIFT = COPY_GROUP.bit_length() - 1
assert 1 << GROUP_SHIFT == COPY_GROUP


def _cparams(*sem):
    return pltpu.CompilerParams(dimension_semantics=sem, vmem_limit_bytes=V7X_VMEM_LIMIT)


def _cond_row(i, tm):
    n_ctx = T_CTX // tm
    return jnp.where(i < n_ctx, 0, 1 + (i - n_ctx) // (DEC_SEQ // tm))


N_CTX_TILES = T_CTX // TM


def _group_specs():
    return [pl.BlockSpec((TM, D_MODEL), lambda i: (jnp.minimum(i, N_CTX_TILES - 1), 0)),
            pl.BlockSpec((TM, D_MODEL), lambda i: (jnp.maximum(i - N_CTX_TILES, 0), 0))]


def _ln(x):
    mu = jnp.mean(x, axis=-1, keepdims=True)
    xc = x - mu
    var = jnp.mean(xc * xc, axis=-1, keepdims=True)
    return xc * lax.rsqrt(var + LN_EPS)


def _split_bf16(x):
    hi = x.astype(BF16)
    lo = (x - hi.astype(F32)).astype(BF16)
    return hi, lo


def _dot(a, b):
    return jnp.dot(a, b, preferred_element_type=F32)


def _sigmoid(x):
    return 1.0 / (1.0 + jnp.exp(-x))


def _gelu_tanh(x):
    c = np.float32(np.sqrt(2 / np.pi))
    return x * (0.5 * (1.0 + jnp.tanh(c * (x + 0.044715 * (x * x * x)))))


def _mod_kernel(c_ref, w_ref, b_ref, o_ref):
    c = c_ref[...]
    s = c * _sigmoid(c)
    o_ref[0] = _dot(s.astype(BF16), w_ref[0].astype(BF16)) + b_ref[0]


def _mod_call(cond, w_ada, b_ada):
    tn = 1536
    n = N_MOD * D_MODEL
    return pl.pallas_call(
        _mod_kernel,
        out_shape=jax.ShapeDtypeStruct((DEPTH, N_COND, n), F32),
        grid=(DEPTH, n // tn),
        in_specs=[
            pl.BlockSpec((N_COND, D_MODEL), lambda l, j: (0, 0)),
            pl.BlockSpec((1, D_MODEL, tn), lambda l, j: (l, 0, j)),
            pl.BlockSpec((1, 1, tn), lambda l, j: (l, 0, j)),
        ],
        out_specs=pl.BlockSpec((1, N_COND, tn), lambda l, j: (l, 0, j)),
        compiler_params=_cparams("arbitrary", "arbitrary"),
        name="adaln_mod",
    )(cond, w_ada, b_ada.reshape(DEPTH, 1, n))


def _head_rms(q, bd, g):
    hi, lo = _split_bf16(q * q)
    ssum = _dot(hi, bd) + _dot(lo, bd)
    return q * lax.rsqrt(ssum * (1.0 / HEAD_DIM) + RMS_EPS) * g


def _rope(x, cos, sin_signed):
    w = x.shape[-1]
    lane = lax.broadcasted_iota(jnp.int32, x.shape, 1)
    swapped = jnp.where((lane % 32) < 16, pltpu.roll(x, w - 16, 1), pltpu.roll(x, 16, 1))
    return x * cos + swapped * sin_signed


def _layer_row(ref, layer):
    return ref[layer:layer + 1, :]


def _proj_kernel(*refs, after_moe, layer):
    if after_moe:
        (x1_ref, moe_ref, modp_ref, l2g_ref, l2b_ref, mod_ref, w_ref, qg_ref, kg_ref, cos_ref, sin_ref, bd_ref,
         kc_in, vc_in, o_ref, kc_ref, vc_ref, xoc_ref, xol_ref) = refs
    else:
        (xc_ref, xl_ref, mod_ref, w_ref, qg_ref, kg_ref, cos_ref, sin_ref, bd_ref,
         kc_in, vc_in, o_ref, kc_ref, vc_ref) = refs
    del kc_in, vc_in
    m = mod_ref[0, 0]
    in_ctx = pl.program_id(0) < N_CTX_TILES
    bd = bd_ref[...]
    sub_rows = [slice(s * PROJ_SUB, (s + 1) * PROJ_SUB) for s in range(TM // PROJ_SUB)]
    projs = []
    xs = []
    for rows in sub_rows:
        if after_moe:
            moe = moe_ref[rows].reshape(PROJ_SUB, D_MODEL)
            x = (_ln(ALPHA * x1_ref[rows, :] + modp_ref[0, 0][5:6] * moe) * _layer_row(l2g_ref, layer - 1)
                 + _layer_row(l2b_ref, layer - 1))
            xs.append(x)
        else:
            x = jnp.where(in_ctx, xc_ref[rows, :], xl_ref[rows, :])
        h = _ln(x) * (1.0 + m[1:2]) + m[0:1]
        proj = _dot(h.astype(BF16), w_ref[...])
        o_ref[rows, ATT_W + KV_W:] = proj[:, ATT_W + KV_W:]
        projs.append(proj[:, :ATT_W + 2 * KV_W])
    keys = []
    for rows, proj in zip(sub_rows, projs):
        qn = _head_rms(proj[:, :ATT_W], bd, _layer_row(qg_ref, layer))
        kn = _head_rms(proj[:, ATT_W:ATT_W + KV_W], bd[:KV_W, :KV_W], _layer_row(kg_ref, layer))
        cos = cos_ref[rows, :]
        sin = sin_ref[rows, :]
        o_ref[rows, :ATT_W] = jnp.where(in_ctx, qn, _rope(qn, cos, sin))
        o_ref[rows, ATT_W:ATT_W + KV_W] = jnp.where(in_ctx, kn, _rope(kn, cos[:, :KV_W], sin[:, :KV_W]))
        keys.append(kn)

    @pl.when(in_ctx)
    def _():
        for s in range(len(sub_rows)):
            kc_ref[s] = keys[s]
            vc_ref[s] = projs[s][:, ATT_W + KV_W:]
        if after_moe:
            for rows, x in zip(sub_rows, xs):
                xoc_ref[rows, :] = x

    if after_moe:
        @pl.when(jnp.logical_not(in_ctx))
        def _():
            for rows, x in zip(sub_rows, xs):
                xol_ref[rows, :] = x


def _proj_call(layer, x, mod, w_in_b, qg, kg, cos_t, sin_t, bd, k_cache, v_cache, prev=None):
    assert PROJ_SUB == SEQ
    n_ctx = T_CTX // TM
    per_seq = DEC_SEQ // TM
    after_moe = prev is not None

    def rope_idx(i):
        return (jnp.where(i < n_ctx, 0, (i - n_ctx) % per_seq), 0)

    cache_spec = pl.BlockSpec((TM // SEQ, None, SEQ, KV_W), lambda i: (jnp.minimum(i, n_ctx - 1), layer, 0, 0))
    cache_shape = jax.ShapeDtypeStruct((BATCH, DEPTH, SEQ, KV_W), F32)
    out_shape = [jax.ShapeDtypeStruct((T_ALL, MIX_W), F32), cache_shape, cache_shape]
    out_specs = [pl.BlockSpec((TM, MIX_W), lambda i: (i, 0)), cache_spec, cache_spec]
    if after_moe:
        x1, moe, l2g, l2b = prev
        x_specs = [
            pl.BlockSpec((TM, D_MODEL), lambda i: (i, 0)),
            pl.BlockSpec((TM, D_CHUNKS, LANES), lambda i: (i, 0, 0)),
            pl.BlockSpec((1, 1, N_MOD, D_MODEL), lambda i: (layer - 1, _cond_row(i, TM), 0, 0)),
            pl.BlockSpec((DEPTH, D_MODEL), lambda i: (0, 0)),
            pl.BlockSpec((DEPTH, D_MODEL), lambda i: (0, 0)),
        ]
        x_args = [x1, moe, mod, l2g, l2b]
        out_shape += [jax.ShapeDtypeStruct((T_CTX, D_MODEL), F32), jax.ShapeDtypeStruct((T_LAT, D_MODEL), F32)]
        out_specs += _group_specs()
    else:
        x_specs, x_args = _group_specs(), list(x)
    n_in = len(x_args) + 9
    return pl.pallas_call(
        functools.partial(_proj_kernel, after_moe=after_moe, layer=layer),
        out_shape=tuple(out_shape),
        grid=(T_ALL // TM,),
        in_specs=x_specs + [
            pl.BlockSpec((1, 1, N_MOD, D_MODEL), lambda i: (layer, _cond_row(i, TM), 0, 0)),
            pl.BlockSpec((None, D_MODEL, MIX_W), lambda i: (layer, 0, 0)),
            pl.BlockSpec((DEPTH, ATT_W), lambda i: (0, 0)),
            pl.BlockSpec((DEPTH, KV_W), lambda i: (0, 0)),
            pl.BlockSpec((TM, ATT_W), rope_idx),
            pl.BlockSpec((TM, ATT_W), rope_idx),
            pl.BlockSpec((ATT_W, ATT_W), lambda i: (0, 0)),
            pl.BlockSpec(memory_space=pl.ANY),
            pl.BlockSpec(memory_space=pl.ANY),
        ],
        out_specs=tuple(out_specs),
        input_output_aliases={n_in - 2: 1, n_in - 1: 2},
        compiler_params=_cparams("arbitrary"),
        name="in_proj",
    )(*x_args, mod, w_in_b, qg, kg, cos_t, sin_t, bd, k_cache, v_cache)


def _attention_tile(q, k_parts, v_parts, phase_major):
    def kv_of(h, parts):
        g = h // Q_PER_KV
        return [p[:, g * HEAD_DIM:(g + 1) * HEAD_DIM] for p in parts]

    def scores(h):
        qh = q[:, h * HEAD_DIM:(h + 1) * HEAD_DIM]
        return [lax.dot_general(qh, k, (((1,), (1,)), ((), ())), preferred_element_type=F32)
                for k in kv_of(h, k_parts)]

    if phase_major:
        all_ss = [scores(h) for h in range(N_Q_HEADS)]
        all_m = [functools.reduce(jnp.maximum, [jnp.max(s, axis=-1, keepdims=True) for s in ss]) for ss in all_ss]
        all_es = [[jnp.exp(s - m) for s in ss] for ss, m in zip(all_ss, all_m)]
        all_den = [functools.reduce(jnp.add, [jnp.sum(e, axis=-1, keepdims=True) for e in es]) for es in all_es]
        outs = [functools.reduce(jnp.add, [_dot(e.astype(BF16), v) for e, v in zip(es, kv_of(h, v_parts))])
                * (1.0 / den) for h, (es, den) in enumerate(zip(all_es, all_den))]
        return jnp.concatenate(outs, axis=-1)
    outs = []
    ss_next = scores(0)
    for h in range(N_Q_HEADS):
        ss = ss_next
        if h + 1 < N_Q_HEADS:
            ss_next = scores(h + 1)
        m = functools.reduce(jnp.maximum, [jnp.max(s, axis=-1, keepdims=True) for s in ss])
        es = [jnp.exp(s - m) for s in ss]
        denom = functools.reduce(jnp.add, [jnp.sum(e, axis=-1, keepdims=True) for e in es])
        o = functools.reduce(jnp.add, [_dot(e.astype(BF16), v) for e, v in zip(es, kv_of(h, v_parts))])
        outs.append(o * (1.0 / denom))
    return jnp.concatenate(outs, axis=-1)


def _mixer_kernel(*refs, seq, seqs, latent, layer):
    if latent:
        (mix_ref, ck_ref, cv_ref, dft_ref, bdc_ref, bds_ref, ws_ref, bsm_ref, clg_ref, clb_ref, o_ref) = refs
    else:
        (mix_ref, dft_ref, bdc_ref, bds_ref, ws_ref, bsm_ref, clg_ref, clb_ref, o_ref) = refs
    tq = min(seq, TQ)
    c0 = ATT_W + 2 * KV_W + FOURIER_W
    lane = lax.broadcasted_iota(jnp.int32, (CHUNK, CHUNK_W), 1)

    for b in range(seqs):
        base = b * seq
        rows = slice(base, base + seq)

        k_new = mix_ref[rows, ATT_W:ATT_W + KV_W].astype(BF16)
        v_new = mix_ref[rows, ATT_W + KV_W:ATT_W + 2 * KV_W].astype(BF16)
        if latent:
            k_parts = [ck_ref[...].astype(BF16), k_new]
            v_parts = [cv_ref[...].astype(BF16), v_new]
        else:
            k_parts, v_parts = [k_new], [v_new]

        def q_tile(t, carry, base=base, k_parts=k_parts, v_parts=v_parts):
            r0 = base + (pl.multiple_of(t * tq, tq) if seq > tq else 0)
            q = (mix_ref[pl.ds(r0, tq), :ATT_W] * (HEAD_DIM ** -0.5)).astype(BF16)
            o_ref[pl.ds(r0, tq), :ATT_W] = _attention_tile(q, k_parts, v_parts, phase_major=not latent).astype(BF16)
            return carry

        if seq > tq:
            lax.fori_loop(0, seq // tq, q_tile, 0)
        else:
            q_tile(0, 0)

        f_hi, f_lo = _split_bf16(mix_ref[rows, ATT_W + 2 * KV_W:ATT_W + 2 * KV_W + FOURIER_W])
        bdc = bdc_ref[...]
        bds = bds_ref[...]
        y = jnp.concatenate([_dot(f_hi, bdc) + _dot(f_lo, bdc), _dot(f_hi, bds) + _dot(f_lo, bds)], axis=0)
        y_hi, y_lo = _split_bf16(y)
        dft = dft_ref[...]
        four = _dot(dft, y_hi) + _dot(dft, y_lo)
        o_ref[rows, ATT_W:ATT_W + FOURIER_W] = four.astype(BF16)

        u = _gelu_tanh(mix_ref[rows, c0:c0 + CHUNK_W])
        vn = (_ln(_gelu_tanh(mix_ref[rows, c0 + CHUNK_W:c0 + 2 * CHUNK_W])) * _layer_row(clg_ref, layer)
              + _layer_row(clb_ref, layer)).astype(BF16)
        bsm = bsm_ref[...]
        for c in range(seq // CHUNK):
            vc = vn[c * CHUNK:(c + 1) * CHUNK]
            sv = bsm
            for g in range(N_CHUNK_GROUPS):
                vg = jnp.where(lane // CHUNK_GROUP_W == g, vc, jnp.zeros_like(vc))
                sv = sv + _dot(ws_ref[g], vg)
            o_ref[base + c * CHUNK:base + (c + 1) * CHUNK, ATT_W + FOURIER_W:] = (
                u[c * CHUNK:(c + 1) * CHUNK] * sv).astype(BF16)


def _mixer_call(layer, mix, ctx_k, ctx_v, dft, bdc, bds, ws, bsm, clg, clb, *, latent):
    seq = DEC_SEQ if latent else SEQ
    seqs = 1 if latent else CTX_SEQS_PER_STEP
    nb = (DEC_BATCH if latent else BATCH) // seqs
    row0 = (T_CTX // seq) if latent else 0
    const2 = lambda b: (0, 0)
    per_layer = lambda b: (layer, 0, 0)
    in_specs = [pl.BlockSpec((seqs * seq, MIX_W), lambda b: (row0 + b, 0))]
    args = [mix]
    if latent:
        in_specs += [pl.BlockSpec((None, None, PAST_LEN, KV_W), lambda b: (b, layer, 0, 0))] * 2
        args += [ctx_k, ctx_v]
    in_specs += [
        pl.BlockSpec((seq, 2 * seq), const2),
        pl.BlockSpec((FOURIER_W, FOURIER_W), const2),
        pl.BlockSpec((FOURIER_W, FOURIER_W), const2),
        pl.BlockSpec((None, N_CHUNK_GROUPS, CHUNK, CHUNK), lambda b: (layer, 0, 0, 0)),
        pl.BlockSpec((None, CHUNK, CHUNK_W), per_layer),
        pl.BlockSpec((DEPTH, CHUNK_W), const2),
        pl.BlockSpec((DEPTH, CHUNK_W), const2),
    ]
    args += [dft, bdc, bds, ws, bsm, clg, clb]
    return pl.pallas_call(
        functools.partial(_mixer_kernel, seq=seq, seqs=seqs, latent=latent, layer=layer),
        out_shape=jax.ShapeDtypeStruct((nb * seqs * seq, D_MODEL), BF16),
        grid=(nb,),
        in_specs=in_specs,
        out_specs=pl.BlockSpec((seqs * seq, D_MODEL), lambda b: (b, 0)),
        compiler_params=_cparams("arbitrary"),
        name="mixer_latent" if latent else "mixer_context",
    )(*args)


def _route(logits, bias_t):
    lt = logits.T[:N_EXPERTS]
    ex = jnp.exp(lt - jnp.max(lt, axis=0, keepdims=True))
    probs = ex / jnp.sum(ex, axis=0, keepdims=True)
    sel = probs + bias_t
    p = [probs[e:e + 1] for e in range(N_EXPERTS)]
    s = [sel[e:e + 1] for e in range(N_EXPERTS)]
    n = EXPERTS_PER_GROUP
    scores = []
    for g in range(N_EXPERT_GROUPS):
        pair = [s[g * n + a] + s[g * n + b] for a in range(n) for b in range(a + 1, n)]
        scores.append(functools.reduce(jnp.maximum, pair))
    best = jnp.zeros_like(scores[0], dtype=jnp.int32)
    best_score = scores[0]
    for g in range(1, N_EXPERT_GROUPS):
        better = scores[g] > best_score
        best = jnp.where(better, g, best)
        best_score = jnp.where(better, scores[g], best_score)
    cls = jnp.zeros_like(best_score)
    w_a = jnp.zeros_like(best_score)
    w_b = jnp.zeros_like(best_score)
    for g in range(N_EXPERT_GROUPS):
        in_g = best == g
        chosen = []
        for a in range(n):
            rank = jnp.zeros_like(best)
            for b in range(n):
                if b == a:
                    continue
                ahead = (s[g * n + b] > s[g * n + a]) if b > a else (s[g * n + b] >= s[g * n + a])
                rank = rank + ahead.astype(jnp.int32)
            chosen.append(jnp.logical_and(in_g, rank < 2))
        for k in range(N_PAIRS):
            pa, pb = p[g * n + PAIR_A[k]], p[g * n + PAIR_B[k]]
            hit = jnp.logical_and(chosen[PAIR_A[k]], chosen[PAIR_B[k]])
            wsum = pa + pb
            cls = jnp.where(hit, float(g * N_PAIRS + k), cls)
            w_a = jnp.where(hit, pa / wsum, w_a)
            w_b = jnp.where(hit, pb / wsum, w_b)
    return jnp.concatenate([cls, w_a, w_b, jnp.zeros((ROUTE_ROWS - 3, lt.shape[1]), F32)], axis=0)


def _merge_kernel(xc_ref, xl_ref, mixc_ref, mixl_ref, mod_ref, win_ref, wpa_ref, wpf_ref, wpc_ref, wo_ref,
                  l1g_ref, l1b_ref, wrh_ref, wrl_ref, bias_ref, x1_ref, ha_ref, rt_ref, *, layer):
    m = mod_ref[0, 0]
    in_ctx = pl.program_id(0) < N_CTX_TILES
    branch_w = ((0, ATT_W, wpa_ref), (ATT_W, FOURIER_W, wpf_ref), (ATT_W + FOURIER_W, CHUNK_W, wpc_ref))
    bounds = np.cumsum((0,) + MERGE_SUBS)
    sub_rows = [slice(int(a), int(b)) for a, b in zip(bounds[:-1], bounds[1:])]
    residual = []
    for rows in sub_rows:
        x = jnp.where(in_ctx, xc_ref[rows, :], xl_ref[rows, :])
        mixed = jnp.where(in_ctx, mixc_ref[rows, :], mixl_ref[rows, :])
        h = (_ln(x) * (1.0 + m[1:2]) + m[0:1]).astype(BF16)
        merged = None
        for b, (c0, width, w_ref) in enumerate(branch_w):
            gate = _sigmoid(_dot(h, win_ref[:, MIX_W + b * D_MODEL:MIX_W + (b + 1) * D_MODEL]))
            term = gate * _dot(mixed[:, c0:c0 + width], w_ref[...])
            merged = term if merged is None else merged + term
        mix = _dot(merged.astype(BF16), wo_ref[...])
        residual.append(ALPHA * x + m[2:3] * mix)
    for rows, pre in zip(sub_rows, residual):
        x1 = _ln(pre) * _layer_row(l1g_ref, layer) + _layer_row(l1b_ref, layer)
        x1_ref[rows, :] = x1
        h2 = _ln(x1) * (1.0 + m[4:5]) + m[3:4]
        h_hi, h_lo = _split_bf16(h2)
        wrh = wrh_ref[...]
        logits = _dot(h_hi, wrh) + _dot(h_lo, wrh) + _dot(h_hi, wrl_ref[...])
        rt = _route(logits, bias_ref[:, rows])
        rt_ref[:, rows] = rt
        ha_ref[rows, :D_MODEL] = h2
        ha_ref[rows, D_MODEL:] = jnp.concatenate(
            [rt, jnp.zeros((LANES - ROUTE_ROWS, rows.stop - rows.start), F32)], axis=0).T


def _merge_call(layer, x, mixed_ctx, mixed_lat, mod, w_in_b, wpa, wpf, wpc, wo, l1g, l1b, wrh, wrl, bias_t):
    const2 = lambda i: (0, 0)
    per_layer = lambda i: (layer, 0, 0)
    tok = lambda i: (i, 0)
    return pl.pallas_call(
        functools.partial(_merge_kernel, layer=layer),
        out_shape=(jax.ShapeDtypeStruct((T_ALL, D_MODEL), F32),
                   jax.ShapeDtypeStruct((T_ALL, HA_W), F32),
                   jax.ShapeDtypeStruct((ROUTE_ROWS, T_ALL), F32)),
        grid=(T_ALL // TM,),
        in_specs=_group_specs() + _group_specs() + [
            pl.BlockSpec((1, 1, N_MOD, D_MODEL), lambda i: (layer, _cond_row(i, TM), 0, 0)),
            pl.BlockSpec((None, D_MODEL, IN_W), per_layer, pipeline_mode=pl.Buffered(1)),
            pl.BlockSpec((None, ATT_W, D_MODEL), per_layer, pipeline_mode=pl.Buffered(1)),
            pl.BlockSpec((None, FOURIER_W, D_MODEL), per_layer, pipeline_mode=pl.Buffered(1)),
            pl.BlockSpec((None, CHUNK_W, D_MODEL), per_layer, pipeline_mode=pl.Buffered(1)),
            pl.BlockSpec((None, D_MODEL, D_MODEL), per_layer, pipeline_mode=pl.Buffered(1)),
            pl.BlockSpec((DEPTH, D_MODEL), const2),
            pl.BlockSpec((DEPTH, D_MODEL), const2),
            pl.BlockSpec((D_MODEL, ROUTER_PAD), const2),
            pl.BlockSpec((D_MODEL, ROUTER_PAD), const2),
            pl.BlockSpec((N_EXPERTS, TM), const2),
        ],
        out_specs=(pl.BlockSpec((TM, D_MODEL), tok),
                   pl.BlockSpec((TM, HA_W), tok),
                   pl.BlockSpec((ROUTE_ROWS, TM), lambda i: (0, i))),
        compiler_params=_cparams("arbitrary"),
        name="merge_route",
    )(*x, mixed_ctx, mixed_lat, mod, w_in_b, wpa, wpf, wpc, wo, l1g, l1b, wrh, wrl, bias_t)


def _sort_kernel(rt_ref, tri_ref, low_ref, pos_ref, tile_ref):
    crow = lax.broadcasted_iota(jnp.int32, (CLASS_PAD, SORT_BLK), 0)
    tri = tri_ref[...]
    n_blk = T_ALL // SORT_BLK
    carry = jnp.zeros((CLASS_PAD, 1), F32)
    ranks = []
    for b in range(n_blk):
        hot = rt_ref[0:1, b * SORT_BLK:(b + 1) * SORT_BLK].astype(jnp.int32) == crow
        hot_f = jnp.where(hot, 1.0, 0.0)
        before = _dot(hot_f.astype(BF16), tri) + carry
        ranks.append(jnp.sum(jnp.where(hot, before, 0.0), axis=0, keepdims=True))
        carry = carry + jnp.sum(hot_f, axis=1, keepdims=True)
    padded = jnp.floor((carry + (TM_MOE - 1.0)) * (1.0 / TM_MOE)) * TM_MOE
    padded = jnp.broadcast_to(padded, (CLASS_PAD, LANES))
    offs = _dot(low_ref[...], padded.astype(BF16))
    for b in range(n_blk):
        hot = rt_ref[0:1, b * SORT_BLK:(b + 1) * SORT_BLK].astype(jnp.int32) == crow
        base = jnp.sum(jnp.where(hot, offs[:, 0:1], 0.0), axis=0, keepdims=True)
        pos_ref[:, b * SORT_BLK:(b + 1) * SORT_BLK] = (base + ranks[b]).astype(jnp.int32)

    start = lax.broadcasted_iota(jnp.int32, (1, LANES), 1).astype(F32) * TM_MOE
    is_class = lax.broadcasted_iota(jnp.int32, (CLASS_PAD, LANES), 0) < N_CLASSES
    ends = jnp.where(is_class, offs + padded, 0.0)
    total = jnp.max(ends, axis=0, keepdims=True)
    valid = start < total
    tcls = jnp.sum(jnp.where(jnp.logical_and(is_class, ends <= start), 1.0, 0.0), axis=0, keepdims=True)
    last = jnp.max(jnp.where(valid, tcls, 0.0), axis=1, keepdims=True)
    tcls = jnp.where(valid, tcls, last)
    grp = functools.reduce(jnp.add, [jnp.where(tcls >= g * N_PAIRS, 1.0, 0.0) for g in range(1, N_EXPERT_GROUPS)])
    pair = tcls - grp * N_PAIRS
    slot_a = functools.reduce(jnp.add, [jnp.where(pair == k, float(PAIR_A[k]), 0.0) for k in range(N_PAIRS)])
    slot_b = functools.reduce(jnp.add, [jnp.where(pair == k, float(PAIR_B[k]), 0.0) for k in range(N_PAIRS)])
    crow_t = lax.broadcasted_iota(jnp.int32, (CLASS_PAD, LANES), 0).astype(F32)
    real_end = jnp.sum(jnp.where(crow_t == tcls, offs + carry, 0.0), axis=0, keepdims=True)
    n_rows = jnp.where(valid, jnp.clip(real_end - start, 0.0, float(TM_MOE)), 0.0)
    rows = [grp * EXPERTS_PER_GROUP + slot_a, grp * EXPERTS_PER_GROUP + slot_b, n_rows]
    tile_ref[...] = jnp.concatenate(rows + [jnp.zeros((8 - len(rows), LANES), F32)], axis=0).astype(jnp.int32)


def _sort_call(rt, tri, low):
    return pl.pallas_call(
        _sort_kernel,
        out_shape=(jax.ShapeDtypeStruct((1, T_ALL), jnp.int32),
                   jax.ShapeDtypeStruct((8, LANES), jnp.int32)),
        name="route_sort",
    )(rt, tri, low)


def _moe_kernel(tile_ref, pos_ref, ha_hbm, wg_hbm, wu_hbm, wd_hbm, out_ref, hbuf, ybuf, src_ref, ssem,
                stg_g, stg_u, stg_d, act_g, act_u, act_d, wsem, ha_ref, hsem, staged_ref, *, layer):
    j = pl.program_id(0)
    slot = j % 2

    def n_rows(tile):
        inside = jnp.logical_and(tile >= 0, tile < N_TILES)
        return jnp.where(inside, tile_ref[TILE_ROWS, jnp.clip(tile, 0, N_TILES - 1)], 0)

    def token_of(tile, n, r, tail):
        return src_ref[tile * TM_MOE + (jnp.minimum(r, n - 1) if tail else r)]

    def scatter_copy(tile, n, buf_slot, r, tail):
        tok = token_of(tile, n, r, tail)
        dst = jnp.where(r < n, tok, T_ALL + buf_slot * TM_MOE + r) if tail else tok
        return pltpu.make_async_copy(ybuf.at[buf_slot, r], out_ref.at[dst], ssem.at[buf_slot])

    def for_groups(n, fn):
        full = lax.shift_right_logical(n, GROUP_SHIFT)

        def body(g, c):
            for k in range(COPY_GROUP):
                fn(g, k, False)
            return c
        lax.fori_loop(0, full, body, 0)

        @pl.when(jnp.bitwise_and(n, COPY_GROUP - 1) != 0)
        def _():
            for k in range(COPY_GROUP):
                fn(full, k, True)

    def retire(tile, buf_slot):
        n = n_rows(tile)
        one = scatter_copy(tile, n, buf_slot, 0, False)

        def body(g, c):
            for _ in range(COPY_GROUP):
                one.wait()
            return c
        lax.fori_loop(0, lax.shift_right_logical(n + COPY_GROUP - 1, GROUP_SHIFT), body, 0)

    def expert_of(x, tile):
        return tile_ref[TILE_SLOT_A + x, tile]

    def weight_copies(x, e):
        return [pltpu.make_async_copy(src.at[layer, e], stg.at[x], wsem.at[x])
                for src, stg in ((wg_hbm, stg_g), (wu_hbm, stg_u), (wd_hbm, stg_d))]

    @pl.when(j == 0)
    def _():
        load_inputs = pltpu.make_async_copy(ha_hbm, ha_ref, hsem)
        load_inputs.start()
        for x in range(2):
            for c in weight_copies(x, expert_of(x, 0)):
                c.start()
            staged_ref[x] = expert_of(x, 0)

        def place(t, c):
            src_ref[pos_ref[t]] = t
            return c
        lax.fori_loop(0, T_ALL, place, 0, unroll=8)
        load_inputs.wait()
        hbuf[...] = jnp.zeros(hbuf.shape, F32)
        ybuf[...] = jnp.zeros(ybuf.shape, F32)
        for s in range(2):
            park = pltpu.make_async_copy(ybuf.at[s], out_ref.at[pl.ds(T_ALL + s * TM_MOE, TM_MOE)], ssem.at[s])
            park.start()
            park.wait()

    @pl.when(n_rows(j - 2) > 0)
    def _():
        retire(j - 2, slot)

    def run_experts(rows):
        rec = hbuf[:rows // SUBLANES].reshape(rows, HA_W)
        h = rec[:, :D_MODEL].astype(BF16)

        def expert(x):
            a = _dot(h, act_g[x])
            u = _dot(h, act_u[x])
            w = rec[:, D_MODEL + 1 + x:D_MODEL + 2 + x]
            return _dot(((a * _sigmoid(a)) * u * w).astype(BF16), act_d[x])

        ybuf[slot, :rows] = (expert(0) + expert(1)).reshape(rows, D_CHUNKS, LANES)

    @pl.when(n_rows(j) > 0)
    def _():
        for x in range(2):
            prev = jnp.maximum(j - 1, 0)

            @pl.when(jnp.logical_or(j == 0, expert_of(x, j) != expert_of(x, prev)))
            def _():
                for c in weight_copies(x, expert_of(x, j)):
                    c.wait()
                act_g[x] = stg_g[x].astype(BF16)
                act_u[x] = stg_u[x].astype(BF16)
                act_d[x] = stg_d[x].astype(BF16)

        for x in range(2):
            cur = expert_of(x, j)

            def expert_or_cur(tile):
                return jnp.where(n_rows(tile) > 0, expert_of(x, jnp.minimum(tile, N_TILES - 1)), cur)
            e1, e2 = expert_or_cur(j + 1), expert_or_cur(j + 2)
            target = jnp.where(e1 != cur, e1, e2)

            @pl.when(jnp.logical_and(target != cur, target != staged_ref[x]))
            def _():
                for c in weight_copies(x, target):
                    c.start()
                staged_ref[x] = target

        n = n_rows(j)

        def gather_row(g, k, tail):
            tok = token_of(j, n, g * COPY_GROUP + k, tail)
            grp, sub = lax.shift_right_logical(tok, GROUP_SHIFT), jnp.bitwise_and(tok, SUBLANES - 1)
            hbuf[g, pl.ds(k, 1), :] = ha_ref[grp, pl.ds(sub, 1), :]
        for_groups(n, gather_row)

        @pl.when(n > SMALL_TILE)
        def _():
            run_experts(TM_MOE)

        @pl.when(n <= SMALL_TILE)
        def _():
            run_experts(SMALL_TILE)

        for_groups(n, lambda g, k, tail: scatter_copy(j, n, slot, g * COPY_GROUP + k, tail).start(priority=k % 2))

    @pl.when(j == N_TILES - 1)
    def _():
        @pl.when(n_rows(j - 1) > 0)
        def _():
            retire(j - 1, 1 - slot)

        @pl.when(n_rows(j) > 0)
        def _():
            retire(j, slot)


def _moe_call(layer, tiles, pos, ha, wge, wue, wde):
    up = (2, D_MODEL, D_EXPERT)
    down = (2, D_EXPERT, D_MODEL)
    grid_spec = pltpu.PrefetchScalarGridSpec(
        num_scalar_prefetch=2,
        grid=(N_TILES,),
        in_specs=[pl.BlockSpec(memory_space=pl.ANY)] * 4,
        out_specs=pl.BlockSpec(memory_space=pl.ANY),
        scratch_shapes=[
            pltpu.VMEM((TM_MOE // SUBLANES, SUBLANES, HA_W), F32),
            pltpu.VMEM((2, TM_MOE, D_CHUNKS, LANES), F32),
            pltpu.SMEM((R_ROWS,), jnp.int32),
            pltpu.SemaphoreType.DMA((2,)),
            pltpu.VMEM(up, F32), pltpu.VMEM(up, F32), pltpu.VMEM(down, F32),
            pltpu.VMEM(up, BF16), pltpu.VMEM(up, BF16), pltpu.VMEM(down, BF16),
            pltpu.SemaphoreType.DMA((2,)),
            pltpu.VMEM((T_ALL // SUBLANES, SUBLANES, HA_W), F32),
            pltpu.SemaphoreType.DMA(()),
            pltpu.SMEM((2,), jnp.int32),
        ],
    )
    return pl.pallas_call(
        functools.partial(_moe_kernel, layer=layer),
        out_shape=jax.ShapeDtypeStruct((T_ALL + N_PARK, D_CHUNKS, LANES), F32),
        grid_spec=grid_spec,
        compiler_params=pltpu.CompilerParams(dimension_semantics=("arbitrary",), vmem_limit_bytes=V7X_VMEM_LIMIT,
                                             has_side_effects=True),
        name="moe_pairs",
    )(tiles, pos,
      ha.reshape(T_ALL // SUBLANES, SUBLANES, HA_W), wge, wue, wde)


def _post_kernel(x1_ref, moe_ref, mod_ref, l2g_ref, l2b_ref, oc_ref, ol_ref, *, layer):
    m = mod_ref[0, 0]
    moe = moe_ref[...].reshape(TM, D_MODEL)
    y = _ln(ALPHA * x1_ref[...] + m[5:6] * moe) * _layer_row(l2g_ref, layer) + _layer_row(l2b_ref, layer)
    i = pl.program_id(0)

    @pl.when(i < N_CTX_TILES)
    def _():
        oc_ref[...] = y

    @pl.when(i >= N_CTX_TILES)
    def _():
        ol_ref[...] = y


def _post_call(layer, x1, moe, mod, l2g, l2b):
    tok = lambda i: (i, 0)
    return pl.pallas_call(
        functools.partial(_post_kernel, layer=layer),
        out_shape=(jax.ShapeDtypeStruct((T_CTX, D_MODEL), F32), jax.ShapeDtypeStruct((T_LAT, D_MODEL), F32)),
        grid=(T_ALL // TM,),
        in_specs=[
            pl.BlockSpec((TM, D_MODEL), tok),
            pl.BlockSpec((TM, D_CHUNKS, LANES), lambda i: (i, 0, 0)),
            pl.BlockSpec((1, 1, N_MOD, D_MODEL), lambda i: (layer, _cond_row(i, TM), 0, 0)),
            pl.BlockSpec((DEPTH, D_MODEL), lambda i: (0, 0)),
            pl.BlockSpec((DEPTH, D_MODEL), lambda i: (0, 0)),
        ],
        out_specs=tuple(_group_specs()),
        compiler_params=_cparams("arbitrary"),
        name="post_moe",
    )(x1, moe, mod, l2g, l2b)


def _rope_tables():
    pos = np.arange(DEC_SEQ)
    quarter = HEAD_DIM // 4
    inv = ROPE_THETA ** (-np.arange(quarter, dtype=np.float64) / quarter)
    ang_r = (pos // GRID_W)[:, None] * inv[None, :]
    ang_c = (pos % GRID_W)[:, None] * inv[None, :]
    cos = np.concatenate([np.cos(ang_r)] * 2 + [np.cos(ang_c)] * 2, axis=-1)
    sin = np.concatenate([-np.sin(ang_r), np.sin(ang_r), -np.sin(ang_c), np.sin(ang_c)], axis=-1)
    return (jnp.asarray(np.tile(cos, (1, N_Q_HEADS)).astype(np.float32)),
            jnp.asarray(np.tile(sin, (1, N_Q_HEADS)).astype(np.float32)))


def _dft_mats(n, scale):
    j = np.arange(n)
    ang = ((j[:, None] * j[None, :]) % n) * (2 * np.pi / n)
    return np.cos(ang) * scale, np.sin(ang) * scale


def _block_diag(m, reps):
    return np.kron(np.eye(reps), m)


def _const_bf16(a):
    return jnp.asarray(np.asarray(a, np.float32)).astype(BF16)


def _const_01(a):
    return jnp.asarray(np.asarray(a, np.float32).astype(BF16))


def kernel(x_prompt, x_sample, cache_k, cache_v, c, c_ctx, w_in, q_norm_g, k_norm_g, w_proj_att,
           w_proj_fourier, w_proj_chunk, w_out, chunk_ln_g, chunk_ln_b, chunk_ws, chunk_bs, w_ada, b_ada,
           ln1_g, ln1_b, ln2_g, ln2_b, w_router, router_bias, w_gate_e, w_up_e, w_down_e):
    x = (x_prompt.reshape(T_CTX, D_MODEL), x_sample.reshape(T_LAT, D_MODEL))
    cond = jnp.concatenate([c_ctx[None, :], c, jnp.zeros((N_COND - 1 - DEC_BATCH, D_MODEL), F32)], axis=0)
    mod = _mod_call(cond, w_ada, b_ada).reshape(DEPTH, N_COND, N_MOD, D_MODEL)

    cos_t, sin_t = _rope_tables()
    bd_heads = _const_01(_block_diag(np.ones((HEAD_DIM, HEAD_DIM)), N_Q_HEADS))
    c64, s64 = _dft_mats(FOURIER_GROUP_W, 1.0)
    bdc = _const_bf16(_block_diag(c64, FOURIER_W // FOURIER_GROUP_W))
    bds = _const_bf16(_block_diag(s64, FOURIER_W // FOURIER_GROUP_W))
    dft = {}
    for seq in (SEQ, DEC_SEQ):
        cs, ss = _dft_mats(seq, 1.0 / math.sqrt(seq * FOURIER_GROUP_W))
        dft[seq] = _const_bf16(np.concatenate([cs, -ss], axis=1))
    ctx_k = cache_k.reshape(DEC_BATCH, DEPTH, PAST_LEN, KV_W)
    ctx_v = cache_v.reshape(DEC_BATCH, DEPTH, PAST_LEN, KV_W)
    wr = jnp.pad(w_router, ((0, 0), (0, ROUTER_PAD - N_EXPERTS)))
    wr_hi = wr.astype(BF16)
    wr_lo = (wr - wr_hi.astype(F32)).astype(BF16)
    bias_t = jnp.broadcast_to(router_bias[:, None], (N_EXPERTS, TM))
    tri = _const_01(np.triu(np.ones((SORT_BLK, SORT_BLK)), 1))
    low = _const_01(np.tril(np.ones((CLASS_PAD, CLASS_PAD)), -1))

    w_in_b = w_in.astype(BF16)
    wpa, wpf, wpc, wo = (w.astype(BF16) for w in (w_proj_att, w_proj_fourier, w_proj_chunk, w_out))
    wge, wue, wde = w_gate_e, w_up_e, w_down_e
    qg = jnp.tile(q_norm_g, (1, N_Q_HEADS))
    kg = jnp.tile(k_norm_g, (1, N_KV_HEADS))
    ws = chunk_ws.astype(BF16)
    bsm = jnp.repeat(jnp.swapaxes(chunk_bs, 1, 2), CHUNK_GROUP_W, axis=2)

    new_k = jnp.zeros((BATCH, DEPTH, SEQ, KV_W), F32)
    new_v = jnp.zeros((BATCH, DEPTH, SEQ, KV_W), F32)
    proj_args = (mod, w_in_b, qg, kg, cos_t, sin_t, bd_heads)
    for l in range(DEPTH):
        if l == 0:
            mix, new_k, new_v = _proj_call(l, x, *proj_args, new_k, new_v)
        else:
            mix, new_k, new_v, *x = _proj_call(l, None, *proj_args, new_k, new_v,
                                               prev=(x1, moe, ln2_g, ln2_b))
        mixer_args = (bdc, bds, ws, bsm, chunk_ln_g, chunk_ln_b)
        mixed_ctx = _mixer_call(l, mix, None, None, dft[SEQ], *mixer_args, latent=False)
        mixed_lat = _mixer_call(l, mix, ctx_k, ctx_v, dft[DEC_SEQ], *mixer_args, latent=True)
        x1, ha, rt = _merge_call(l, x, mixed_ctx, mixed_lat, mod, w_in_b, wpa, wpf, wpc, wo,
                                 ln1_g, ln1_b, wr_hi, wr_lo, bias_t)
        pos2d, tiles = _sort_call(rt, tri, low)
        moe = _moe_call(l, tiles, pos2d[0], ha, wge, wue, wde)
    x = _post_call(DEPTH - 1, x1, moe, mod, ln2_g, ln2_b)

    y_prompt = x[0].reshape(BATCH, SEQ, D_MODEL)
    y_sample = x[1].reshape(DEC_BATCH, DEC_SEQ, D_MODEL)
    cache_shape = (BATCH, DEPTH, SEQ, N_KV_HEADS, HEAD_DIM)
    return (y_prompt, y_sample, new_k.reshape(cache_shape), new_v.reshape(cache_shape))
```

```python
import functools
import math

import jax
import jax.numpy as jnp
import numpy as np
from jax import lax
from jax.experimental import pallas as pl
from jax.experimental.pallas import tpu as pltpu

F32 = jnp.float32
BF16 = jnp.bfloat16

D_MODEL = 1024
BATCH = 16
SEQ = 256
DEPTH = 2
DEC_BATCH = 2
DEC_SEQ = 1024
PAST_LEN = 256
GRID_W = 64
N_Q_HEADS = 8
N_KV_HEADS = 2
HEAD_DIM = 64
Q_PER_KV = N_Q_HEADS // N_KV_HEADS
ATT_W = N_Q_HEADS * HEAD_DIM
KV_W = N_KV_HEADS * HEAD_DIM
ROPE_THETA = 10000.0
FOURIER_GROUP_W = 64
FOURIER_W = 256
N_CHUNK_GROUPS = 4
CHUNK_GROUP_W = 64
CHUNK_W = 256
CHUNK = 128
N_EXPERTS = 16
N_EXPERT_GROUPS = 4
EXPERTS_PER_GROUP = 4
D_EXPERT = 512
ALPHA = (2 * DEPTH) ** 0.25
LN_EPS = 1e-6
RMS_EPS = 1e-6

LANES = 128
SUBLANES = 8
T_CTX = BATCH * SEQ
T_LAT = DEC_BATCH * DEC_SEQ
T_ALL = T_CTX + T_LAT
IN_W = ATT_W + 2 * KV_W + FOURIER_W + 2 * CHUNK_W + 3 * D_MODEL
MIX_W = ATT_W + 2 * KV_W + FOURIER_W + 2 * CHUNK_W
N_MOD = 6
N_COND = 8
ROUTER_PAD = LANES
TM = 512
TQ = 512
MERGE_SUBS = (256, 256)
PROJ_SUB = 256
CTX_SEQS_PER_STEP = 2
V7X_VMEM_LIMIT = 56 * 1024 * 1024

PAIR_A = (0, 0, 0, 1, 1, 3)
PAIR_B = (1, 2, 3, 3, 2, 2)
N_PAIRS = len(PAIR_A)
N_CLASSES = N_EXPERT_GROUPS * N_PAIRS
CLASS_PAD = 32
ROUTE_ROWS = 8
TM_MOE = 256
N_TILES = T_ALL // TM_MOE + N_CLASSES
R_ROWS = N_TILES * TM_MOE
SORT_BLK = 512
D_CHUNKS = D_MODEL // LANES
HA_W = D_MODEL + LANES
TILE_SLOT_A, TILE_SLOT_B, TILE_ROWS = 0, 1, 2
SMALL_TILE = 64
N_PARK = 2 * TM_MOE
COPY_GROUP = SUBLANES
GROUP_SHIFT = COPY_GROUP.bit_length() - 1
assert 1 << GROUP_SHIFT == COPY_GROUP


def _cparams(*sem):
    return pltpu.CompilerParams(dimension_semantics=sem, vmem_limit_bytes=V7X_VMEM_LIMIT)


def _cond_row(i, tm):
    n_ctx = T_CTX // tm
    return jnp.where(i < n_ctx, 0, 1 + (i - n_ctx) // (DEC_SEQ // tm))


N_CTX_TILES = T_CTX // TM


def _group_specs():
    return [pl.BlockSpec((TM, D_MODEL), lambda i: (jnp.minimum(i, N_CTX_TILES - 1), 0)),
            pl.BlockSpec((TM, D_MODEL), lambda i: (jnp.maximum(i - N_CTX_TILES, 0), 0))]


def _ln(x):
    mu = jnp.mean(x, axis=-1, keepdims=True)
    xc = x - mu
    var = jnp.mean(xc * xc, axis=-1, keepdims=True)
    return xc * lax.rsqrt(var + LN_EPS)


def _split_bf16(x):
    hi = x.astype(BF16)
    lo = (x - hi.astype(F32)).astype(BF16)
    return hi, lo


def _dot(a, b):
    return jnp.dot(a, b, preferred_element_type=F32)


def _sigmoid(x):
    return 1.0 / (1.0 + jnp.exp(-x))


def _gelu_tanh(x):
    c = np.float32(np.sqrt(2 / np.pi))
    return x * (0.5 * (1.0 + jnp.tanh(c * (x + 0.044715 * (x * x * x)))))


def _mod_kernel(c_ref, w_ref, b_ref, o_ref):
    c = c_ref[...]
    s = c * _sigmoid(c)
    o_ref[0] = _dot(s.astype(BF16), w_ref[0].astype(BF16)) + b_ref[0]


def _mod_call(cond, w_ada, b_ada):
    tn = 1536
    n = N_MOD * D_MODEL
    return pl.pallas_call(
        _mod_kernel,
        out_shape=jax.ShapeDtypeStruct((DEPTH, N_COND, n), F32),
        grid=(DEPTH, n // tn),
        in_specs=[
            pl.BlockSpec((N_COND, D_MODEL), lambda l, j: (0, 0)),
            pl.BlockSpec((1, D_MODEL, tn), lambda l, j: (l, 0, j)),
            pl.BlockSpec((1, 1, tn), lambda l, j: (l, 0, j)),
        ],
        out_specs=pl.BlockSpec((1, N_COND, tn), lambda l, j: (l, 0, j)),
        compiler_params=_cparams("arbitrary", "arbitrary"),
        name="adaln_mod",
    )(cond, w_ada, b_ada.reshape(DEPTH, 1, n))


def _head_rms(q, bd, g):
    hi, lo = _split_bf16(q * q)
    ssum = _dot(hi, bd) + _dot(lo, bd)
    return q * lax.rsqrt(ssum * (1.0 / HEAD_DIM) + RMS_EPS) * g


def _rope(x, cos, sin_signed):
    w = x.shape[-1]
    lane = lax.broadcasted_iota(jnp.int32, x.shape, 1)
    swapped = jnp.where((lane % 32) < 16, pltpu.roll(x, w - 16, 1), pltpu.roll(x, 16, 1))
    return x * cos + swapped * sin_signed


def _layer_row(ref, layer):
    return ref[layer:layer + 1, :]


def _proj_kernel(*refs, after_moe, layer):
    if after_moe:
        (x1_ref, moe_ref, modp_ref, l2g_ref, l2b_ref, mod_ref, w_ref, qg_ref, kg_ref, cos_ref, sin_ref, bd_ref,
         kc_in, vc_in, o_ref, kc_ref, vc_ref, xoc_ref, xol_ref) = refs
    else:
        (xc_ref, xl_ref, mod_ref, w_ref, qg_ref, kg_ref, cos_ref, sin_ref, bd_ref,
         kc_in, vc_in, o_ref, kc_ref, vc_ref) = refs
    del kc_in, vc_in
    m = mod_ref[0, 0]
    in_ctx = pl.program_id(0) < N_CTX_TILES
    bd = bd_ref[...]
    sub_rows = [slice(s * PROJ_SUB, (s + 1) * PROJ_SUB) for s in range(TM // PROJ_SUB)]
    projs = []
    xs = []
    for rows in sub_rows:
        if after_moe:
            moe = moe_ref[rows].reshape(PROJ_SUB, D_MODEL)
            x = (_ln(ALPHA * x1_ref[rows, :] + modp_ref[0, 0][5:6] * moe) * _layer_row(l2g_ref, layer - 1)
                 + _layer_row(l2b_ref, layer - 1))
            xs.append(x)
        else:
            x = jnp.where(in_ctx, xc_ref[rows, :], xl_ref[rows, :])
        h = _ln(x) * (1.0 + m[1:2]) + m[0:1]
        proj = _dot(h.astype(BF16), w_ref[...])
        o_ref[rows, ATT_W + KV_W:] = proj[:, ATT_W + KV_W:]
        projs.append(proj[:, :ATT_W + 2 * KV_W])
    keys = []
    for rows, proj in zip(sub_rows, projs):
        qn = _head_rms(proj[:, :ATT_W], bd, _layer_row(qg_ref, layer))
        kn = _head_rms(proj[:, ATT_W:ATT_W + KV_W], bd[:KV_W, :KV_W], _layer_row(kg_ref, layer))
        cos = cos_ref[rows, :]
        sin = sin_ref[rows, :]
        o_ref[rows, :ATT_W] = jnp.where(in_ctx, qn, _rope(qn, cos, sin))
        o_ref[rows, ATT_W:ATT_W + KV_W] = jnp.where(in_ctx, kn, _rope(kn, cos[:, :KV_W], sin[:, :KV_W]))
        keys.append(kn)

    @pl.when(in_ctx)
    def _():
        for s in range(len(sub_rows)):
            kc_ref[s] = keys[s]
            vc_ref[s] = projs[s][:, ATT_W + KV_W:]
        if after_moe:
            for rows, x in zip(sub_rows, xs):
                xoc_ref[rows, :] = x

    if after_moe:
        @pl.when(jnp.logical_not(in_ctx))
        def _():
            for rows, x in zip(sub_rows, xs):
                xol_ref[rows, :] = x


def _proj_call(layer, x, mod, w_in_b, qg, kg, cos_t, sin_t, bd, k_cache, v_cache, prev=None):
    assert PROJ_SUB == SEQ
    n_ctx = T_CTX // TM
    per_seq = DEC_SEQ // TM
    after_moe = prev is not None

    def rope_idx(i):
        return (jnp.where(i < n_ctx, 0, (i - n_ctx) % per_seq), 0)

    cache_spec = pl.BlockSpec((TM // SEQ, None, SEQ, KV_W), lambda i: (jnp.minimum(i, n_ctx - 1), layer, 0, 0))
    cache_shape = jax.ShapeDtypeStruct((BATCH, DEPTH, SEQ, KV_W), F32)
    out_shape = [jax.ShapeDtypeStruct((T_ALL, MIX_W), F32), cache_shape, cache_shape]
    out_specs = [pl.BlockSpec((TM, MIX_W), lambda i: (i, 0)), cache_spec, cache_spec]
    if after_moe:
        x1, moe, l2g, l2b = prev
        x_specs = [
            pl.BlockSpec((TM, D_MODEL), lambda i: (i, 0)),
            pl.BlockSpec((TM, D_CHUNKS, LANES), lambda i: (i, 0, 0)),
            pl.BlockSpec((1, 1, N_MOD, D_MODEL), lambda i: (layer - 1, _cond_row(i, TM), 0, 0)),
            pl.BlockSpec((DEPTH, D_MODEL), lambda i: (0, 0)),
            pl.BlockSpec((DEPTH, D_MODEL), lambda i: (0, 0)),
        ]
        x_args = [x1, moe, mod, l2g, l2b]
        out_shape += [jax.ShapeDtypeStruct((T_CTX, D_MODEL), F32), jax.ShapeDtypeStruct((T_LAT, D_MODEL), F32)]
        out_specs += _group_specs()
    else:
        x_specs, x_args = _group_specs(), list(x)
    n_in = len(x_args) + 9
    return pl.pallas_call(
        functools.partial(_proj_kernel, after_moe=after_moe, layer=layer),
        out_shape=tuple(out_shape),
        grid=(T_ALL // TM,),
        in_specs=x_specs + [
            pl.BlockSpec((1, 1, N_MOD, D_MODEL), lambda i: (layer, _cond_row(i, TM), 0, 0)),
            pl.BlockSpec((None, D_MODEL, MIX_W), lambda i: (layer, 0, 0)),
            pl.BlockSpec((DEPTH, ATT_W), lambda i: (0, 0)),
            pl.BlockSpec((DEPTH, KV_W), lambda i: (0, 0)),
            pl.BlockSpec((TM, ATT_W), rope_idx),
            pl.BlockSpec((TM, ATT_W), rope_idx),
            pl.BlockSpec((ATT_W, ATT_W), lambda i: (0, 0)),
            pl.BlockSpec(memory_space=pl.ANY),
            pl.BlockSpec(memory_space=pl.ANY),
        ],
        out_specs=tuple(out_specs),
        input_output_aliases={n_in - 2: 1, n_in - 1: 2},
        compiler_params=_cparams("arbitrary"),
        name="in_proj",
    )(*x_args, mod, w_in_b, qg, kg, cos_t, sin_t, bd, k_cache, v_cache)


def _attention_tile(q, k_parts, v_parts, phase_major):
    def kv_of(h, parts):
        g = h // Q_PER_KV
        return [p[:, g * HEAD_DIM:(g + 1) * HEAD_DIM] for p in parts]

    def scores(h):
        qh = q[:, h * HEAD_DIM:(h + 1) * HEAD_DIM]
        return [lax.dot_general(qh, k, (((1,), (1,)), ((), ())), preferred_element_type=F32)
                for k in kv_of(h, k_parts)]

    if phase_major:
        all_ss = [scores(h) for h in range(N_Q_HEADS)]
        all_m = [functools.reduce(jnp.maximum, [jnp.max(s, axis=-1, keepdims=True) for s in ss]) for ss in all_ss]
        all_es = [[jnp.exp(s - m) for s in ss] for ss, m in zip(all_ss, all_m)]
        all_den = [functools.reduce(jnp.add, [jnp.sum(e, axis=-1, keepdims=True) for e in es]) for es in all_es]
        outs = [functools.reduce(jnp.add, [_dot(e.astype(BF16), v) for e, v in zip(es, kv_of(h, v_parts))])
                * (1.0 / den) for h, (es, den) in enumerate(zip(all_es, all_den))]
        return jnp.concatenate(outs, axis=-1)
    outs = []
    ss_next = scores(0)
    for h in range(N_Q_HEADS):
        ss = ss_next
        if h + 1 < N_Q_HEADS:
            ss_next = scores(h + 1)
        m = functools.reduce(jnp.maximum, [jnp.max(s, axis=-1, keepdims=True) for s in ss])
        es = [jnp.exp(s - m) for s in ss]
        denom = functools.reduce(jnp.add, [jnp.sum(e, axis=-1, keepdims=True) for e in es])
        o = functools.reduce(jnp.add, [_dot(e.astype(BF16), v) for e, v in zip(es, kv_of(h, v_parts))])
        outs.append(o * (1.0 / denom))
    return jnp.concatenate(outs, axis=-1)


def _mixer_kernel(*refs, seq, seqs, latent, layer):
    if latent:
        (mix_ref, ck_ref, cv_ref, dft_ref, bdc_ref, bds_ref, ws_ref, bsm_ref, clg_ref, clb_ref, o_ref) = refs
    else:
        (mix_ref, dft_ref, bdc_ref, bds_ref, ws_ref, bsm_ref, clg_ref, clb_ref, o_ref) = refs
    tq = min(seq, TQ)
    c0 = ATT_W + 2 * KV_W + FOURIER_W
    lane = lax.broadcasted_iota(jnp.int32, (CHUNK, CHUNK_W), 1)

    for b in range(seqs):
        base = b * seq
        rows = slice(base, base + seq)

        k_new = mix_ref[rows, ATT_W:ATT_W + KV_W].astype(BF16)
        v_new = mix_ref[rows, ATT_W + KV_W:ATT_W + 2 * KV_W].astype(BF16)
        if latent:
            k_parts = [ck_ref[...].astype(BF16), k_new]
            v_parts = [cv_ref[...].astype(BF16), v_new]
        else:
            k_parts, v_parts = [k_new], [v_new]

        def q_tile(t, carry, base=base, k_parts=k_parts, v_parts=v_parts):
            r0 = base + (pl.multiple_of(t * tq, tq) if seq > tq else 0)
            q = (mix_ref[pl.ds(r0, tq), :ATT_W] * (HEAD_DIM ** -0.5)).astype(BF16)
            o_ref[pl.ds(r0, tq), :ATT_W] = _attention_tile(q, k_parts, v_parts, phase_major=not latent).astype(BF16)
            return carry

        if seq > tq:
            lax.fori_loop(0, seq // tq, q_tile, 0)
        else:
            q_tile(0, 0)

        f_hi, f_lo = _split_bf16(mix_ref[rows, ATT_W + 2 * KV_W:ATT_W + 2 * KV_W + FOURIER_W])
        bdc = bdc_ref[...]
        bds = bds_ref[...]
        y = jnp.concatenate([_dot(f_hi, bdc) + _dot(f_lo, bdc), _dot(f_hi, bds) + _dot(f_lo, bds)], axis=0)
        y_hi, y_lo = _split_bf16(y)
        dft = dft_ref[...]
        four = _dot(dft, y_hi) + _dot(dft, y_lo)
        o_ref[rows, ATT_W:ATT_W + FOURIER_W] = four.astype(BF16)

        u = _gelu_tanh(mix_ref[rows, c0:c0 + CHUNK_W])
        vn = (_ln(_gelu_tanh(mix_ref[rows, c0 + CHUNK_W:c0 + 2 * CHUNK_W])) * _layer_row(clg_ref, layer)
              + _layer_row(clb_ref, layer)).astype(BF16)
        bsm = bsm_ref[...]
        for c in range(seq // CHUNK):
            vc = vn[c * CHUNK:(c + 1) * CHUNK]
            sv = bsm
            for g in range(N_CHUNK_GROUPS):
                vg = jnp.where(lane // CHUNK_GROUP_W == g, vc, jnp.zeros_like(vc))
                sv = sv + _dot(ws_ref[g], vg)
            o_ref[base + c * CHUNK:base + (c + 1) * CHUNK, ATT_W + FOURIER_W:] = (
                u[c * CHUNK:(c + 1) * CHUNK] * sv).astype(BF16)


def _mixer_call(layer, mix, ctx_k, ctx_v, dft, bdc, bds, ws, bsm, clg, clb, *, latent):
    seq = DEC_SEQ if latent else SEQ
    seqs = 1 if latent else CTX_SEQS_PER_STEP
    nb = (DEC_BATCH if latent else BATCH) // seqs
    row0 = (T_CTX // seq) if latent else 0
    const2 = lambda b: (0, 0)
    per_layer = lambda b: (layer, 0, 0)
    in_specs = [pl.BlockSpec((seqs * seq, MIX_W), lambda b: (row0 + b, 0))]
    args = [mix]
    if latent:
        in_specs += [pl.BlockSpec((None, None, PAST_LEN, KV_W), lambda b: (b, layer, 0, 0))] * 2
        args += [ctx_k, ctx_v]
    in_specs += [
        pl.BlockSpec((seq, 2 * seq), const2),
        pl.BlockSpec((FOURIER_W, FOURIER_W), const2),
        pl.BlockSpec((FOURIER_W, FOURIER_W), const2),
        pl.BlockSpec((None, N_CHUNK_GROUPS, CHUNK, CHUNK), lambda b: (layer, 0, 0, 0)),
        pl.BlockSpec((None, CHUNK, CHUNK_W), per_layer),
        pl.BlockSpec((DEPTH, CHUNK_W), const2),
        pl.BlockSpec((DEPTH, CHUNK_W), const2),
    ]
    args += [dft, bdc, bds, ws, bsm, clg, clb]
    return pl.pallas_call(
        functools.partial(_mixer_kernel, seq=seq, seqs=seqs, latent=latent, layer=layer),
        out_shape=jax.ShapeDtypeStruct((nb * seqs * seq, D_MODEL), BF16),
        grid=(nb,),
        in_specs=in_specs,
        out_specs=pl.BlockSpec((seqs * seq, D_MODEL), lambda b: (b, 0)),
        compiler_params=_cparams("arbitrary"),
        name="mixer_latent" if latent else "mixer_context",
    )(*args)


def _route(logits, bias_t):
    lt = logits.T[:N_EXPERTS]
    ex = jnp.exp(lt - jnp.max(lt, axis=0, keepdims=True))
    probs = ex / jnp.sum(ex, axis=0, keepdims=True)
    sel = probs + bias_t
    p = [probs[e:e + 1] for e in range(N_EXPERTS)]
    s = [sel[e:e + 1] for e in range(N_EXPERTS)]
    n = EXPERTS_PER_GROUP
    scores = []
    for g in range(N_EXPERT_GROUPS):
        pair = [s[g * n + a] + s[g * n + b] for a in range(n) for b in range(a + 1, n)]
        scores.append(functools.reduce(jnp.maximum, pair))
    best = jnp.zeros_like(scores[0], dtype=jnp.int32)
    best_score = scores[0]
    for g in range(1, N_EXPERT_GROUPS):
        better = scores[g] > best_score
        best = jnp.where(better, g, best)
        best_score = jnp.where(better, scores[g], best_score)
    cls = jnp.zeros_like(best_score)
    w_a = jnp.zeros_like(best_score)
    w_b = jnp.zeros_like(best_score)
    for g in range(N_EXPERT_GROUPS):
        in_g = best == g
        chosen = []
        for a in range(n):
            rank = jnp.zeros_like(best)
            for b in range(n):
                if b == a:
                    continue
                ahead = (s[g * n + b] > s[g * n + a]) if b > a else (s[g * n + b] >= s[g * n + a])
                rank = rank + ahead.astype(jnp.int32)
            chosen.append(jnp.logical_and(in_g, rank < 2))
        for k in range(N_PAIRS):
            pa, pb = p[g * n + PAIR_A[k]], p[g * n + PAIR_B[k]]
            hit = jnp.logical_and(chosen[PAIR_A[k]], chosen[PAIR_B[k]])
            wsum = pa + pb
            cls = jnp.where(hit, float(g * N_PAIRS + k), cls)
            w_a = jnp.where(hit, pa / wsum, w_a)
            w_b = jnp.where(hit, pb / wsum, w_b)
    return jnp.concatenate([cls, w_a, w_b, jnp.zeros((ROUTE_ROWS - 3, lt.shape[1]), F32)], axis=0)


def _merge_kernel(xc_ref, xl_ref, mixc_ref, mixl_ref, mod_ref, win_ref, wpa_ref, wpf_ref, wpc_ref, wo_ref,
                  l1g_ref, l1b_ref, wrc_ref, bias_ref, x1_ref, ha_ref, rt_ref, *, layer):
    m = mod_ref[0, 0]
    in_ctx = pl.program_id(0) < N_CTX_TILES
    branch_w = ((0, ATT_W, wpa_ref), (ATT_W, FOURIER_W, wpf_ref), (ATT_W + FOURIER_W, CHUNK_W, wpc_ref))
    bounds = np.cumsum((0,) + MERGE_SUBS)
    sub_rows = [slice(int(a), int(b)) for a, b in zip(bounds[:-1], bounds[1:])]
    residual = []
    for rows in sub_rows:
        x = jnp.where(in_ctx, xc_ref[rows, :], xl_ref[rows, :])
        mixed = jnp.where(in_ctx, mixc_ref[rows, :], mixl_ref[rows, :])
        h = (_ln(x) * (1.0 + m[1:2]) + m[0:1]).astype(BF16)
        merged = None
        for b, (c0, width, w_ref) in enumerate(branch_w):
            gate = _sigmoid(_dot(h, win_ref[:, MIX_W + b * D_MODEL:MIX_W + (b + 1) * D_MODEL]))
            term = gate * _dot(mixed[:, c0:c0 + width], w_ref[...])
            merged = term if merged is None else merged + term
        mix = _dot(merged.astype(BF16), wo_ref[...])
        residual.append(ALPHA * x + m[2:3] * mix)
    for rows, pre in zip(sub_rows, residual):
        x1 = _ln(pre) * _layer_row(l1g_ref, layer) + _layer_row(l1b_ref, layer)
        x1_ref[rows, :] = x1
        h2 = _ln(x1) * (1.0 + m[4:5]) + m[3:4]
        h_hi, h_lo = _split_bf16(h2)
        both = _dot(h_hi, wrc_ref[...])
        logits = both[:, :ROUTER_PAD] + _dot(h_lo, wrc_ref[:, :ROUTER_PAD]) + both[:, ROUTER_PAD:]
        rt = _route(logits, bias_ref[:, rows])
        rt_ref[:, rows] = rt
        ha_ref[rows, :D_MODEL] = h2
        ha_ref[rows, D_MODEL:] = jnp.concatenate(
            [rt, jnp.zeros((LANES - ROUTE_ROWS, rows.stop - rows.start), F32)], axis=0).T


def _merge_call(layer, x, mixed_ctx, mixed_lat, mod, w_in_b, wpa, wpf, wpc, wo, l1g, l1b, wr_cat, bias_t):
    const2 = lambda i: (0, 0)
    per_layer = lambda i: (layer, 0, 0)
    tok = lambda i: (i, 0)
    return pl.pallas_call(
        functools.partial(_merge_kernel, layer=layer),
        out_shape=(jax.ShapeDtypeStruct((T_ALL, D_MODEL), F32),
                   jax.ShapeDtypeStruct((T_ALL, HA_W), F32),
                   jax.ShapeDtypeStruct((ROUTE_ROWS, T_ALL), F32)),
        grid=(T_ALL // TM,),
        in_specs=_group_specs() + _group_specs() + [
            pl.BlockSpec((1, 1, N_MOD, D_MODEL), lambda i: (layer, _cond_row(i, TM), 0, 0)),
            pl.BlockSpec((None, D_MODEL, IN_W), per_layer, pipeline_mode=pl.Buffered(1)),
            pl.BlockSpec((None, ATT_W, D_MODEL), per_layer, pipeline_mode=pl.Buffered(1)),
            pl.BlockSpec((None, FOURIER_W, D_MODEL), per_layer, pipeline_mode=pl.Buffered(1)),
            pl.BlockSpec((None, CHUNK_W, D_MODEL), per_layer, pipeline_mode=pl.Buffered(1)),
            pl.BlockSpec((None, D_MODEL, D_MODEL), per_layer, pipeline_mode=pl.Buffered(1)),
            pl.BlockSpec((DEPTH, D_MODEL), const2),
            pl.BlockSpec((DEPTH, D_MODEL), const2),
            pl.BlockSpec((D_MODEL, 2 * ROUTER_PAD), const2),
            pl.BlockSpec((N_EXPERTS, TM), const2),
        ],
        out_specs=(pl.BlockSpec((TM, D_MODEL), tok),
                   pl.BlockSpec((TM, HA_W), tok),
                   pl.BlockSpec((ROUTE_ROWS, TM), lambda i: (0, i))),
        compiler_params=_cparams("arbitrary"),
        name="merge_route",
    )(*x, mixed_ctx, mixed_lat, mod, w_in_b, wpa, wpf, wpc, wo, l1g, l1b, wr_cat, bias_t)


def _sort_kernel(rt_ref, tri_ref, low_ref, pos_ref, tile_ref):
    crow = lax.broadcasted_iota(jnp.int32, (CLASS_PAD, SORT_BLK), 0)
    tri = tri_ref[...]
    n_blk = T_ALL // SORT_BLK
    carry = jnp.zeros((CLASS_PAD, 1), F32)
    ranks = []
    for b in range(n_blk):
        hot = rt_ref[0:1, b * SORT_BLK:(b + 1) * SORT_BLK].astype(jnp.int32) == crow
        hot_f = jnp.where(hot, 1.0, 0.0)
        before = _dot(hot_f.astype(BF16), tri) + carry
        ranks.append(jnp.sum(jnp.where(hot, before, 0.0), axis=0, keepdims=True))
        carry = carry + jnp.sum(hot_f, axis=1, keepdims=True)
    padded = jnp.floor((carry + (TM_MOE - 1.0)) * (1.0 / TM_MOE)) * TM_MOE
    padded = jnp.broadcast_to(padded, (CLASS_PAD, LANES))
    offs = _dot(low_ref[...], padded.astype(BF16))
    for b in range(n_blk):
        hot = rt_ref[0:1, b * SORT_BLK:(b + 1) * SORT_BLK].astype(jnp.int32) == crow
        base = jnp.sum(jnp.where(hot, offs[:, 0:1], 0.0), axis=0, keepdims=True)
        pos_ref[:, b * SORT_BLK:(b + 1) * SORT_BLK] = (base + ranks[b]).astype(jnp.int32)

    start = lax.broadcasted_iota(jnp.int32, (1, LANES), 1).astype(F32) * TM_MOE
    is_class = lax.broadcasted_iota(jnp.int32, (CLASS_PAD, LANES), 0) < N_CLASSES
    ends = jnp.where(is_class, offs + padded, 0.0)
    total = jnp.max(ends, axis=0, keepdims=True)
    valid = start < total
    tcls = jnp.sum(jnp.where(jnp.logical_and(is_class, ends <= start), 1.0, 0.0), axis=0, keepdims=True)
    last = jnp.max(jnp.where(valid, tcls, 0.0), axis=1, keepdims=True)
    tcls = jnp.where(valid, tcls, last)
    grp = functools.reduce(jnp.add, [jnp.where(tcls >= g * N_PAIRS, 1.0, 0.0) for g in range(1, N_EXPERT_GROUPS)])
    pair = tcls - grp * N_PAIRS
    slot_a = functools.reduce(jnp.add, [jnp.where(pair == k, float(PAIR_A[k]), 0.0) for k in range(N_PAIRS)])
    slot_b = functools.reduce(jnp.add, [jnp.where(pair == k, float(PAIR_B[k]), 0.0) for k in range(N_PAIRS)])
    crow_t = lax.broadcasted_iota(jnp.int32, (CLASS_PAD, LANES), 0).astype(F32)
    real_end = jnp.sum(jnp.where(crow_t == tcls, offs + carry, 0.0), axis=0, keepdims=True)
    n_rows = jnp.where(valid, jnp.clip(real_end - start, 0.0, float(TM_MOE)), 0.0)
    rows = [grp * EXPERTS_PER_GROUP + slot_a, grp * EXPERTS_PER_GROUP + slot_b, n_rows]
    tile_ref[...] = jnp.concatenate(rows + [jnp.zeros((8 - len(rows), LANES), F32)], axis=0).astype(jnp.int32)


def _sort_call(rt, tri, low):
    return pl.pallas_call(
        _sort_kernel,
        out_shape=(jax.ShapeDtypeStruct((1, T_ALL), jnp.int32),
                   jax.ShapeDtypeStruct((8, LANES), jnp.int32)),
        name="route_sort",
    )(rt, tri, low)


def _moe_kernel(tile_ref, pos_ref, ha_hbm, wg_hbm, wu_hbm, wd_hbm, out_ref, hbuf, ybuf, src_ref, ssem,
                stg_g, stg_u, stg_d, act_g, act_u, act_d, wsem, ha_ref, hsem, staged_ref, *, layer):
    j = pl.program_id(0)
    slot = j % 2

    def n_rows(tile):
        inside = jnp.logical_and(tile >= 0, tile < N_TILES)
        return jnp.where(inside, tile_ref[TILE_ROWS, jnp.clip(tile, 0, N_TILES - 1)], 0)

    def token_of(tile, n, r, tail):
        return src_ref[tile * TM_MOE + (jnp.minimum(r, n - 1) if tail else r)]

    def scatter_copy(tile, n, buf_slot, r, tail):
        tok = token_of(tile, n, r, tail)
        dst = jnp.where(r < n, tok, T_ALL + buf_slot * TM_MOE + r) if tail else tok
        return pltpu.make_async_copy(ybuf.at[buf_slot, r], out_ref.at[dst], ssem.at[buf_slot])

    def for_groups(n, fn):
        full = lax.shift_right_logical(n, GROUP_SHIFT)

        def body(g, c):
            for k in range(COPY_GROUP):
                fn(g, k, False)
            return c
        lax.fori_loop(0, full, body, 0)

        @pl.when(jnp.bitwise_and(n, COPY_GROUP - 1) != 0)
        def _():
            for k in range(COPY_GROUP):
                fn(full, k, True)

    def retire(tile, buf_slot):
        n = n_rows(tile)
        one = scatter_copy(tile, n, buf_slot, 0, False)

        def body(g, c):
            for _ in range(COPY_GROUP):
                one.wait()
            return c
        lax.fori_loop(0, lax.shift_right_logical(n + COPY_GROUP - 1, GROUP_SHIFT), body, 0)

    def expert_of(x, tile):
        return tile_ref[TILE_SLOT_A + x, tile]

    def weight_copies(x, e):
        return [pltpu.make_async_copy(src.at[layer, e], stg.at[x], wsem.at[x])
                for src, stg in ((wg_hbm, stg_g), (wu_hbm, stg_u), (wd_hbm, stg_d))]

    @pl.when(j == 0)
    def _():
        load_inputs = pltpu.make_async_copy(ha_hbm, ha_ref, hsem)
        load_inputs.start()
        for x in range(2):
            for c in weight_copies(x, expert_of(x, 0)):
                c.start()
            staged_ref[x] = expert_of(x, 0)

        def place(t, c):
            src_ref[pos_ref[t]] = t
            return c
        lax.fori_loop(0, T_ALL, place, 0, unroll=8)
        load_inputs.wait()
        hbuf[...] = jnp.zeros(hbuf.shape, F32)
        ybuf[...] = jnp.zeros(ybuf.shape, F32)
        for s in range(2):
            park = pltpu.make_async_copy(ybuf.at[s], out_ref.at[pl.ds(T_ALL + s * TM_MOE, TM_MOE)], ssem.at[s])
            park.start()
            park.wait()

    @pl.when(n_rows(j - 2) > 0)
    def _():
        retire(j - 2, slot)

    def run_experts(rows):
        rec = hbuf[:rows // SUBLANES].reshape(rows, HA_W)
        h = rec[:, :D_MODEL].astype(BF16)

        def expert(x):
            a = _dot(h, act_g[x])
            u = _dot(h, act_u[x])
            w = rec[:, D_MODEL + 1 + x:D_MODEL + 2 + x]
            return _dot(((a * _sigmoid(a)) * u * w).astype(BF16), act_d[x])

        ybuf[slot, :rows] = (expert(0) + expert(1)).reshape(rows, D_CHUNKS, LANES)

    @pl.when(n_rows(j) > 0)
    def _():
        for x in range(2):
            prev = jnp.maximum(j - 1, 0)

            @pl.when(jnp.logical_or(j == 0, expert_of(x, j) != expert_of(x, prev)))
            def _():
                for c in weight_copies(x, expert_of(x, j)):
                    c.wait()
                act_g[x] = stg_g[x].astype(BF16)
                act_u[x] = stg_u[x].astype(BF16)
                act_d[x] = stg_d[x].astype(BF16)

        for x in range(2):
            cur = expert_of(x, j)

            def expert_or_cur(tile):
                return jnp.where(n_rows(tile) > 0, expert_of(x, jnp.minimum(tile, N_TILES - 1)), cur)
            e1, e2 = expert_or_cur(j + 1), expert_or_cur(j + 2)
            target = jnp.where(e1 != cur, e1, e2)

            @pl.when(jnp.logical_and(target != cur, target != staged_ref[x]))
            def _():
                for c in weight_copies(x, target):
                    c.start()
                staged_ref[x] = target

        n = n_rows(j)

        def gather_row(g, k, tail):
            tok = token_of(j, n, g * COPY_GROUP + k, tail)
            grp, sub = lax.shift_right_logical(tok, GROUP_SHIFT), jnp.bitwise_and(tok, SUBLANES - 1)
            hbuf[g, pl.ds(k, 1), :] = ha_ref[grp, pl.ds(sub, 1), :]
        for_groups(n, gather_row)

        @pl.when(n > SMALL_TILE)
        def _():
            run_experts(TM_MOE)

        @pl.when(n <= SMALL_TILE)
        def _():
            run_experts(SMALL_TILE)

        for_groups(n, lambda g, k, tail: scatter_copy(j, n, slot, g * COPY_GROUP + k, tail).start(priority=k % 2))

    @pl.when(j == N_TILES - 1)
    def _():
        @pl.when(n_rows(j - 1) > 0)
        def _():
            retire(j - 1, 1 - slot)

        @pl.when(n_rows(j) > 0)
        def _():
            retire(j, slot)


def _moe_call(layer, tiles, pos, ha, wge, wue, wde):
    up = (2, D_MODEL, D_EXPERT)
    down = (2, D_EXPERT, D_MODEL)
    grid_spec = pltpu.PrefetchScalarGridSpec(
        num_scalar_prefetch=2,
        grid=(N_TILES,),
        in_specs=[pl.BlockSpec(memory_space=pl.ANY)] * 4,
        out_specs=pl.BlockSpec(memory_space=pl.ANY),
        scratch_shapes=[
            pltpu.VMEM((TM_MOE // SUBLANES, SUBLANES, HA_W), F32),
            pltpu.VMEM((2, TM_MOE, D_CHUNKS, LANES), F32),
            pltpu.SMEM((R_ROWS,), jnp.int32),
            pltpu.SemaphoreType.DMA((2,)),
            pltpu.VMEM(up, F32), pltpu.VMEM(up, F32), pltpu.VMEM(down, F32),
            pltpu.VMEM(up, BF16), pltpu.VMEM(up, BF16), pltpu.VMEM(down, BF16),
            pltpu.SemaphoreType.DMA((2,)),
            pltpu.VMEM((T_ALL // SUBLANES, SUBLANES, HA_W), F32),
            pltpu.SemaphoreType.DMA(()),
            pltpu.SMEM((2,), jnp.int32),
        ],
    )
    return pl.pallas_call(
        functools.partial(_moe_kernel, layer=layer),
        out_shape=jax.ShapeDtypeStruct((T_ALL + N_PARK, D_CHUNKS, LANES), F32),
        grid_spec=grid_spec,
        compiler_params=pltpu.CompilerParams(dimension_semantics=("arbitrary",), vmem_limit_bytes=V7X_VMEM_LIMIT,
                                             has_side_effects=True),
        name="moe_pairs",
    )(tiles, pos,
      ha.reshape(T_ALL // SUBLANES, SUBLANES, HA_W), wge, wue, wde)


def _post_kernel(x1_ref, moe_ref, mod_ref, l2g_ref, l2b_ref, oc_ref, ol_ref, *, layer):
    m = mod_ref[0, 0]
    moe = moe_ref[...].reshape(TM, D_MODEL)
    y = _ln(ALPHA * x1_ref[...] + m[5:6] * moe) * _layer_row(l2g_ref, layer) + _layer_row(l2b_ref, layer)
    i = pl.program_id(0)

    @pl.when(i < N_CTX_TILES)
    def _():
        oc_ref[...] = y

    @pl.when(i >= N_CTX_TILES)
    def _():
        ol_ref[...] = y


def _post_call(layer, x1, moe, mod, l2g, l2b):
    tok = lambda i: (i, 0)
    return pl.pallas_call(
        functools.partial(_post_kernel, layer=layer),
        out_shape=(jax.ShapeDtypeStruct((T_CTX, D_MODEL), F32), jax.ShapeDtypeStruct((T_LAT, D_MODEL), F32)),
        grid=(T_ALL // TM,),
        in_specs=[
            pl.BlockSpec((TM, D_MODEL), tok),
            pl.BlockSpec((TM, D_CHUNKS, LANES), lambda i: (i, 0, 0)),
            pl.BlockSpec((1, 1, N_MOD, D_MODEL), lambda i: (layer, _cond_row(i, TM), 0, 0)),
            pl.BlockSpec((DEPTH, D_MODEL), lambda i: (0, 0)),
            pl.BlockSpec((DEPTH, D_MODEL), lambda i: (0, 0)),
        ],
        out_specs=tuple(_group_specs()),
        compiler_params=_cparams("arbitrary"),
        name="post_moe",
    )(x1, moe, mod, l2g, l2b)


def _rope_tables():
    pos = np.arange(DEC_SEQ)
    quarter = HEAD_DIM // 4
    inv = ROPE_THETA ** (-np.arange(quarter, dtype=np.float64) / quarter)
    ang_r = (pos // GRID_W)[:, None] * inv[None, :]
    ang_c = (pos % GRID_W)[:, None] * inv[None, :]
    cos = np.concatenate([np.cos(ang_r)] * 2 + [np.cos(ang_c)] * 2, axis=-1)
    sin = np.concatenate([-np.sin(ang_r), np.sin(ang_r), -np.sin(ang_c), np.sin(ang_c)], axis=-1)
    return (jnp.asarray(np.tile(cos, (1, N_Q_HEADS)).astype(np.float32)),
            jnp.asarray(np.tile(sin, (1, N_Q_HEADS)).astype(np.float32)))


def _dft_mats(n, scale):
    j = np.arange(n)
    ang = ((j[:, None] * j[None, :]) % n) * (2 * np.pi / n)
    return np.cos(ang) * scale, np.sin(ang) * scale


def _block_diag(m, reps):
    return np.kron(np.eye(reps), m)


def _const_bf16(a):
    return jnp.asarray(np.asarray(a, np.float32)).astype(BF16)


def _const_01(a):
    return jnp.asarray(np.asarray(a, np.float32).astype(BF16))


def kernel(x_prompt, x_sample, cache_k, cache_v, c, c_ctx, w_in, q_norm_g, k_norm_g, w_proj_att,
           w_proj_fourier, w_proj_chunk, w_out, chunk_ln_g, chunk_ln_b, chunk_ws, chunk_bs, w_ada, b_ada,
           ln1_g, ln1_b, ln2_g, ln2_b, w_router, router_bias, w_gate_e, w_up_e, w_down_e):
    x = (x_prompt.reshape(T_CTX, D_MODEL), x_sample.reshape(T_LAT, D_MODEL))
    cond = jnp.concatenate([c_ctx[None, :], c, jnp.zeros((N_COND - 1 - DEC_BATCH, D_MODEL), F32)], axis=0)
    mod = _mod_call(cond, w_ada, b_ada).reshape(DEPTH, N_COND, N_MOD, D_MODEL)

    cos_t, sin_t = _rope_tables()
    bd_heads = _const_01(_block_diag(np.ones((HEAD_DIM, HEAD_DIM)), N_Q_HEADS))
    c64, s64 = _dft_mats(FOURIER_GROUP_W, 1.0)
    bdc = _const_bf16(_block_diag(c64, FOURIER_W // FOURIER_GROUP_W))
    bds = _const_bf16(_block_diag(s64, FOURIER_W // FOURIER_GROUP_W))
    dft = {}
    for seq in (SEQ, DEC_SEQ):
        cs, ss = _dft_mats(seq, 1.0 / math.sqrt(seq * FOURIER_GROUP_W))
        dft[seq] = _const_bf16(np.concatenate([cs, -ss], axis=1))
    ctx_k = cache_k.reshape(DEC_BATCH, DEPTH, PAST_LEN, KV_W)
    ctx_v = cache_v.reshape(DEC_BATCH, DEPTH, PAST_LEN, KV_W)
    wr = jnp.pad(w_router, ((0, 0), (0, ROUTER_PAD - N_EXPERTS)))
    wr_hi = wr.astype(BF16)
    wr_cat = jnp.concatenate([wr_hi, (wr - wr_hi.astype(F32)).astype(BF16)], axis=1)
    bias_t = jnp.broadcast_to(router_bias[:, None], (N_EXPERTS, TM))
    tri = _const_01(np.triu(np.ones((SORT_BLK, SORT_BLK)), 1))
    low = _const_01(np.tril(np.ones((CLASS_PAD, CLASS_PAD)), -1))

    w_in_b = w_in.astype(BF16)
    wpa, wpf, wpc, wo = (w.astype(BF16) for w in (w_proj_att, w_proj_fourier, w_proj_chunk, w_out))
    wge, wue, wde = w_gate_e, w_up_e, w_down_e
    qg = jnp.tile(q_norm_g, (1, N_Q_HEADS))
    kg = jnp.tile(k_norm_g, (1, N_KV_HEADS))
    ws = chunk_ws.astype(BF16)
    bsm = jnp.repeat(jnp.swapaxes(chunk_bs, 1, 2), CHUNK_GROUP_W, axis=2)

    new_k = jnp.zeros((BATCH, DEPTH, SEQ, KV_W), F32)
    new_v = jnp.zeros((BATCH, DEPTH, SEQ, KV_W), F32)
    proj_args = (mod, w_in_b, qg, kg, cos_t, sin_t, bd_heads)
    for l in range(DEPTH):
        if l == 0:
            mix, new_k, new_v = _proj_call(l, x, *proj_args, new_k, new_v)
        else:
            mix, new_k, new_v, *x = _proj_call(l, None, *proj_args, new_k, new_v,
                                               prev=(x1, moe, ln2_g, ln2_b))
        mixer_args = (bdc, bds, ws, bsm, chunk_ln_g, chunk_ln_b)
        mixed_ctx = _mixer_call(l, mix, None, None, dft[SEQ], *mixer_args, latent=False)
        mixed_lat = _mixer_call(l, mix, ctx_k, ctx_v, dft[DEC_SEQ], *mixer_args, latent=True)
        x1, ha, rt = _merge_call(l, x, mixed_ctx, mixed_lat, mod, w_in_b, wpa, wpf, wpc, wo,
                                 ln1_g, ln1_b, wr_cat, bias_t)
        pos2d, tiles = _sort_call(rt, tri, low)
        moe = _moe_call(l, tiles, pos2d[0], ha, wge, wue, wde)
    x = _post_call(DEPTH - 1, x1, moe, mod, ln2_g, ln2_b)

    y_prompt = x[0].reshape(BATCH, SEQ, D_MODEL)
    y_sample = x[1].reshape(DEC_BATCH, DEC_SEQ, D_MODEL)
    cache_shape = (BATCH, DEPTH, SEQ, N_KV_HEADS, HEAD_DIM)
    return (y_prompt, y_sample, new_k.reshape(cache_shape), new_v.reshape(cache_shape))
```

```python
import functools
import math

import jax
import jax.numpy as jnp
import numpy as np
from jax import lax
from jax.experimental import pallas as pl
from jax.experimental.pallas import tpu as pltpu

F32 = jnp.float32
BF16 = jnp.bfloat16

D_MODEL = 1024
BATCH = 16
SEQ = 256
DEPTH = 2
DEC_BATCH = 2
DEC_SEQ = 1024
PAST_LEN = 256
GRID_W = 64
N_Q_HEADS = 8
N_KV_HEADS = 2
HEAD_DIM = 64
Q_PER_KV = N_Q_HEADS // N_KV_HEADS
ATT_W = N_Q_HEADS * HEAD_DIM
KV_W = N_KV_HEADS * HEAD_DIM
ROPE_THETA = 10000.0
FOURIER_GROUP_W = 64
FOURIER_W = 256
N_CHUNK_GROUPS = 4
CHUNK_GROUP_W = 64
CHUNK_W = 256
CHUNK = 128
N_EXPERTS = 16
N_EXPERT_GROUPS = 4
EXPERTS_PER_GROUP = 4
D_EXPERT = 512
ALPHA = (2 * DEPTH) ** 0.25
LN_EPS = 1e-6
RMS_EPS = 1e-6

LANES = 128
SUBLANES = 8
MXU_W = 256
T_CTX = BATCH * SEQ
T_LAT = DEC_BATCH * DEC_SEQ
T_ALL = T_CTX + T_LAT
IN_W = ATT_W + 2 * KV_W + FOURIER_W + 2 * CHUNK_W + 3 * D_MODEL
MIX_W = ATT_W + 2 * KV_W + FOURIER_W + 2 * CHUNK_W
N_MOD = 6
N_COND = 8
ROUTER_PAD = LANES
TM = 512
TQ = 512
MERGE_SUBS = (256, 256)
PROJ_SUB = 256
CTX_SEQS_PER_STEP = 2
V7X_VMEM_LIMIT = 56 * 1024 * 1024

PAIR_A = (0, 0, 0, 1, 1, 3)
PAIR_B = (1, 2, 3, 3, 2, 2)
N_PAIRS = len(PAIR_A)
N_CLASSES = N_EXPERT_GROUPS * N_PAIRS
CLASS_PAD = 32
ROUTE_ROWS = 8
TM_MOE = 256
N_TILES = T_ALL // TM_MOE + N_CLASSES
R_ROWS = N_TILES * TM_MOE
SORT_BLK = 512
D_CHUNKS = D_MODEL // LANES
HA_W = D_MODEL + LANES
TILE_SLOT_A, TILE_SLOT_B, TILE_ROWS = 0, 1, 2
SMALL_TILE = 64
N_PARK = 2 * TM_MOE
COPY_GROUP = SUBLANES
GROUP_SHIFT = COPY_GROUP.bit_length() - 1
assert 1 << GROUP_SHIFT == COPY_GROUP


def _cparams(*sem):
    return pltpu.CompilerParams(dimension_semantics=sem, vmem_limit_bytes=V7X_VMEM_LIMIT)


def _cond_row(i, tm):
    n_ctx = T_CTX // tm
    return jnp.where(i < n_ctx, 0, 1 + (i - n_ctx) // (DEC_SEQ // tm))


N_CTX_TILES = T_CTX // TM


def _group_specs():
    return [pl.BlockSpec((TM, D_MODEL), lambda i: (jnp.minimum(i, N_CTX_TILES - 1), 0)),
            pl.BlockSpec((TM, D_MODEL), lambda i: (jnp.maximum(i - N_CTX_TILES, 0), 0))]


def _ln(x):
    mu = jnp.mean(x, axis=-1, keepdims=True)
    xc = x - mu
    var = jnp.mean(xc * xc, axis=-1, keepdims=True)
    return xc * lax.rsqrt(var + LN_EPS)


def _split_bf16(x):
    hi = x.astype(BF16)
    lo = (x - hi.astype(F32)).astype(BF16)
    return hi, lo


def _dot(a, b):
    return jnp.dot(a, b, preferred_element_type=F32)


def _sigmoid(x):
    return 1.0 / (1.0 + jnp.exp(-x))


def _gelu_tanh(x):
    c = np.float32(np.sqrt(2 / np.pi))
    return x * (0.5 * (1.0 + jnp.tanh(c * (x + 0.044715 * (x * x * x)))))


def _mod_kernel(c_ref, w_ref, b_ref, o_ref):
    c = c_ref[...]
    s = c * _sigmoid(c)
    o_ref[0] = _dot(s.astype(BF16), w_ref[0].astype(BF16)) + b_ref[0]


def _mod_call(cond, w_ada, b_ada):
    tn = 1536
    n = N_MOD * D_MODEL
    return pl.pallas_call(
        _mod_kernel,
        out_shape=jax.ShapeDtypeStruct((DEPTH, N_COND, n), F32),
        grid=(DEPTH, n // tn),
        in_specs=[
            pl.BlockSpec((N_COND, D_MODEL), lambda l, j: (0, 0)),
            pl.BlockSpec((1, D_MODEL, tn), lambda l, j: (l, 0, j)),
            pl.BlockSpec((1, 1, tn), lambda l, j: (l, 0, j)),
        ],
        out_specs=pl.BlockSpec((1, N_COND, tn), lambda l, j: (l, 0, j)),
        compiler_params=_cparams("arbitrary", "arbitrary"),
        name="adaln_mod",
    )(cond, w_ada, b_ada.reshape(DEPTH, 1, n))


def _head_rms(q, bd, g):
    hi, lo = _split_bf16(q * q)
    blk = min(q.shape[-1], MXU_W)
    ssum = jnp.concatenate(
        [_dot(hi[:, c:c + blk], bd[c:c + blk, c:c + blk]) + _dot(lo[:, c:c + blk], bd[c:c + blk, c:c + blk])
         for c in range(0, q.shape[-1], blk)], axis=-1)
    return q * lax.rsqrt(ssum * (1.0 / HEAD_DIM) + RMS_EPS) * g


def _rope(x, cos, sin_signed):
    w = x.shape[-1]
    lane = lax.broadcasted_iota(jnp.int32, x.shape, 1)
    swapped = jnp.where((lane % 32) < 16, pltpu.roll(x, w - 16, 1), pltpu.roll(x, 16, 1))
    return x * cos + swapped * sin_signed


def _layer_row(ref, layer):
    return ref[layer:layer + 1, :]


def _proj_kernel(*refs, after_moe, layer):
    if after_moe:
        (x1_ref, moe_ref, modp_ref, l2g_ref, l2b_ref, mod_ref, w_ref, qg_ref, kg_ref, cos_ref, sin_ref, bd_ref,
         kc_in, vc_in, o_ref, kc_ref, vc_ref, xoc_ref, xol_ref) = refs
    else:
        (xc_ref, xl_ref, mod_ref, w_ref, qg_ref, kg_ref, cos_ref, sin_ref, bd_ref,
         kc_in, vc_in, o_ref, kc_ref, vc_ref) = refs
    del kc_in, vc_in
    m = mod_ref[0, 0]
    in_ctx = pl.program_id(0) < N_CTX_TILES
    bd = bd_ref[...]
    sub_rows = [slice(s * PROJ_SUB, (s + 1) * PROJ_SUB) for s in range(TM // PROJ_SUB)]
    projs = []
    xs = []
    for rows in sub_rows:
        if after_moe:
            moe = moe_ref[rows].reshape(PROJ_SUB, D_MODEL)
            x = (_ln(ALPHA * x1_ref[rows, :] + modp_ref[0, 0][5:6] * moe) * _layer_row(l2g_ref, layer - 1)
                 + _layer_row(l2b_ref, layer - 1))
            xs.append(x)
        else:
            x = jnp.where(in_ctx, xc_ref[rows, :], xl_ref[rows, :])
        h = _ln(x) * (1.0 + m[1:2]) + m[0:1]
        proj = _dot(h.astype(BF16), w_ref[...])
        o_ref[rows, ATT_W + KV_W:] = proj[:, ATT_W + KV_W:]
        projs.append(proj[:, :ATT_W + 2 * KV_W])
    keys = []
    for rows, proj in zip(sub_rows, projs):
        qn = _head_rms(proj[:, :ATT_W], bd, _layer_row(qg_ref, layer))
        kn = _head_rms(proj[:, ATT_W:ATT_W + KV_W], bd[:KV_W, :KV_W], _layer_row(kg_ref, layer))
        cos = cos_ref[rows, :]
        sin = sin_ref[rows, :]
        o_ref[rows, :ATT_W] = jnp.where(in_ctx, qn, _rope(qn, cos, sin))
        o_ref[rows, ATT_W:ATT_W + KV_W] = jnp.where(in_ctx, kn, _rope(kn, cos[:, :KV_W], sin[:, :KV_W]))
        keys.append(kn)

    @pl.when(in_ctx)
    def _():
        for s in range(len(sub_rows)):
            kc_ref[s] = keys[s]
            vc_ref[s] = projs[s][:, ATT_W + KV_W:]
        if after_moe:
            for rows, x in zip(sub_rows, xs):
                xoc_ref[rows, :] = x

    if after_moe:
        @pl.when(jnp.logical_not(in_ctx))
        def _():
            for rows, x in zip(sub_rows, xs):
                xol_ref[rows, :] = x


def _proj_call(layer, x, mod, w_in_b, qg, kg, cos_t, sin_t, bd, k_cache, v_cache, prev=None):
    assert PROJ_SUB == SEQ
    n_ctx = T_CTX // TM
    per_seq = DEC_SEQ // TM
    after_moe = prev is not None

    def rope_idx(i):
        return (jnp.where(i < n_ctx, 0, (i - n_ctx) % per_seq), 0)

    cache_spec = pl.BlockSpec((TM // SEQ, None, SEQ, KV_W), lambda i: (jnp.minimum(i, n_ctx - 1), layer, 0, 0))
    cache_shape = jax.ShapeDtypeStruct((BATCH, DEPTH, SEQ, KV_W), F32)
    out_shape = [jax.ShapeDtypeStruct((T_ALL, MIX_W), F32), cache_shape, cache_shape]
    out_specs = [pl.BlockSpec((TM, MIX_W), lambda i: (i, 0)), cache_spec, cache_spec]
    if after_moe:
        x1, moe, l2g, l2b = prev
        x_specs = [
            pl.BlockSpec((TM, D_MODEL), lambda i: (i, 0)),
            pl.BlockSpec((TM, D_CHUNKS, LANES), lambda i: (i, 0, 0)),
            pl.BlockSpec((1, 1, N_MOD, D_MODEL), lambda i: (layer - 1, _cond_row(i, TM), 0, 0)),
            pl.BlockSpec((DEPTH, D_MODEL), lambda i: (0, 0)),
            pl.BlockSpec((DEPTH, D_MODEL), lambda i: (0, 0)),
        ]
        x_args = [x1, moe, mod, l2g, l2b]
        out_shape += [jax.ShapeDtypeStruct((T_CTX, D_MODEL), F32), jax.ShapeDtypeStruct((T_LAT, D_MODEL), F32)]
        out_specs += _group_specs()
    else:
        x_specs, x_args = _group_specs(), list(x)
    n_in = len(x_args) + 9
    return pl.pallas_call(
        functools.partial(_proj_kernel, after_moe=after_moe, layer=layer),
        out_shape=tuple(out_shape),
        grid=(T_ALL // TM,),
        in_specs=x_specs + [
            pl.BlockSpec((1, 1, N_MOD, D_MODEL), lambda i: (layer, _cond_row(i, TM), 0, 0)),
            pl.BlockSpec((None, D_MODEL, MIX_W), lambda i: (layer, 0, 0)),
            pl.BlockSpec((DEPTH, ATT_W), lambda i: (0, 0)),
            pl.BlockSpec((DEPTH, KV_W), lambda i: (0, 0)),
            pl.BlockSpec((TM, ATT_W), rope_idx),
            pl.BlockSpec((TM, ATT_W), rope_idx),
            pl.BlockSpec((ATT_W, ATT_W), lambda i: (0, 0)),
            pl.BlockSpec(memory_space=pl.ANY),
            pl.BlockSpec(memory_space=pl.ANY),
        ],
        out_specs=tuple(out_specs),
        input_output_aliases={n_in - 2: 1, n_in - 1: 2},
        compiler_params=_cparams("arbitrary"),
        name="in_proj",
    )(*x_args, mod, w_in_b, qg, kg, cos_t, sin_t, bd, k_cache, v_cache)


def _attention_tile(q, k_parts, v_parts, phase_major):
    def kv_of(h, parts):
        g = h // Q_PER_KV
        return [p[:, g * HEAD_DIM:(g + 1) * HEAD_DIM] for p in parts]

    def scores(h):
        qh = q[:, h * HEAD_DIM:(h + 1) * HEAD_DIM]
        return [lax.dot_general(qh, k, (((1,), (1,)), ((), ())), preferred_element_type=F32)
                for k in kv_of(h, k_parts)]

    if phase_major:
        all_ss = [scores(h) for h in range(N_Q_HEADS)]
        all_m = [functools.reduce(jnp.maximum, [jnp.max(s, axis=-1, keepdims=True) for s in ss]) for ss in all_ss]
        all_es = [[jnp.exp(s - m) for s in ss] for ss, m in zip(all_ss, all_m)]
        all_den = [functools.reduce(jnp.add, [jnp.sum(e, axis=-1, keepdims=True) for e in es]) for es in all_es]
        outs = [functools.reduce(jnp.add, [_dot(e.astype(BF16), v) for e, v in zip(es, kv_of(h, v_parts))])
                * (1.0 / den) for h, (es, den) in enumerate(zip(all_es, all_den))]
        return jnp.concatenate(outs, axis=-1)
    outs = []
    ss_next = scores(0)
    for h in range(N_Q_HEADS):
        ss = ss_next
        if h + 1 < N_Q_HEADS:
            ss_next = scores(h + 1)
        m = functools.reduce(jnp.maximum, [jnp.max(s, axis=-1, keepdims=True) for s in ss])
        es = [jnp.exp(s - m) for s in ss]
        denom = functools.reduce(jnp.add, [jnp.sum(e, axis=-1, keepdims=True) for e in es])
        o = functools.reduce(jnp.add, [_dot(e.astype(BF16), v) for e, v in zip(es, kv_of(h, v_parts))])
        outs.append(o * (1.0 / denom))
    return jnp.concatenate(outs, axis=-1)


def _mixer_kernel(*refs, seq, seqs, latent, layer):
    if latent:
        (mix_ref, ck_ref, cv_ref, dft_ref, bdc_ref, bds_ref, ws_ref, bsm_ref, clg_ref, clb_ref, o_ref) = refs
    else:
        (mix_ref, dft_ref, bdc_ref, bds_ref, ws_ref, bsm_ref, clg_ref, clb_ref, o_ref) = refs
    tq = min(seq, TQ)
    c0 = ATT_W + 2 * KV_W + FOURIER_W
    lane = lax.broadcasted_iota(jnp.int32, (CHUNK, CHUNK_W), 1)

    for b in range(seqs):
        base = b * seq
        rows = slice(base, base + seq)

        k_new = mix_ref[rows, ATT_W:ATT_W + KV_W].astype(BF16)
        v_new = mix_ref[rows, ATT_W + KV_W:ATT_W + 2 * KV_W].astype(BF16)
        if latent:
            k_parts = [ck_ref[...].astype(BF16), k_new]
            v_parts = [cv_ref[...].astype(BF16), v_new]
        else:
            k_parts, v_parts = [k_new], [v_new]

        def q_tile(t, carry, base=base, k_parts=k_parts, v_parts=v_parts):
            r0 = base + (pl.multiple_of(t * tq, tq) if seq > tq else 0)
            q = (mix_ref[pl.ds(r0, tq), :ATT_W] * (HEAD_DIM ** -0.5)).astype(BF16)
            o_ref[pl.ds(r0, tq), :ATT_W] = _attention_tile(q, k_parts, v_parts, phase_major=not latent).astype(BF16)
            return carry

        if seq > tq:
            lax.fori_loop(0, seq // tq, q_tile, 0)
        else:
            q_tile(0, 0)

        f_hi, f_lo = _split_bf16(mix_ref[rows, ATT_W + 2 * KV_W:ATT_W + 2 * KV_W + FOURIER_W])
        bdc = bdc_ref[...]
        bds = bds_ref[...]
        y = jnp.concatenate([_dot(f_hi, bdc) + _dot(f_lo, bdc), _dot(f_hi, bds) + _dot(f_lo, bds)], axis=0)
        y_hi, y_lo = _split_bf16(y)
        dft = dft_ref[...]
        four = _dot(dft, y_hi) + _dot(dft, y_lo)
        o_ref[rows, ATT_W:ATT_W + FOURIER_W] = four.astype(BF16)

        u = _gelu_tanh(mix_ref[rows, c0:c0 + CHUNK_W])
        vn = (_ln(_gelu_tanh(mix_ref[rows, c0 + CHUNK_W:c0 + 2 * CHUNK_W])) * _layer_row(clg_ref, layer)
              + _layer_row(clb_ref, layer)).astype(BF16)
        bsm = bsm_ref[...]
        for c in range(seq // CHUNK):
            vc = vn[c * CHUNK:(c + 1) * CHUNK]
            sv = bsm
            for g in range(N_CHUNK_GROUPS):
                vg = jnp.where(lane // CHUNK_GROUP_W == g, vc, jnp.zeros_like(vc))
                sv = sv + _dot(ws_ref[g], vg)
            o_ref[base + c * CHUNK:base + (c + 1) * CHUNK, ATT_W + FOURIER_W:] = (
                u[c * CHUNK:(c + 1) * CHUNK] * sv).astype(BF16)


def _mixer_call(layer, mix, ctx_k, ctx_v, dft, bdc, bds, ws, bsm, clg, clb, *, latent):
    seq = DEC_SEQ if latent else SEQ
    seqs = 1 if latent else CTX_SEQS_PER_STEP
    nb = (DEC_BATCH if latent else BATCH) // seqs
    row0 = (T_CTX // seq) if latent else 0
    const2 = lambda b: (0, 0)
    per_layer = lambda b: (layer, 0, 0)
    in_specs = [pl.BlockSpec((seqs * seq, MIX_W), lambda b: (row0 + b, 0))]
    args = [mix]
    if latent:
        in_specs += [pl.BlockSpec((None, None, PAST_LEN, KV_W), lambda b: (b, layer, 0, 0))] * 2
        args += [ctx_k, ctx_v]
    in_specs += [
        pl.BlockSpec((seq, 2 * seq), const2),
        pl.BlockSpec((FOURIER_W, FOURIER_W), const2),
        pl.BlockSpec((FOURIER_W, FOURIER_W), const2),
        pl.BlockSpec((None, N_CHUNK_GROUPS, CHUNK, CHUNK), lambda b: (layer, 0, 0, 0)),
        pl.BlockSpec((None, CHUNK, CHUNK_W), per_layer),
        pl.BlockSpec((DEPTH, CHUNK_W), const2),
        pl.BlockSpec((DEPTH, CHUNK_W), const2),
    ]
    args += [dft, bdc, bds, ws, bsm, clg, clb]
    return pl.pallas_call(
        functools.partial(_mixer_kernel, seq=seq, seqs=seqs, latent=latent, layer=layer),
        out_shape=jax.ShapeDtypeStruct((nb * seqs * seq, D_MODEL), BF16),
        grid=(nb,),
        in_specs=in_specs,
        out_specs=pl.BlockSpec((seqs * seq, D_MODEL), lambda b: (b, 0)),
        compiler_params=_cparams("arbitrary"),
        name="mixer_latent" if latent else "mixer_context",
    )(*args)


def _route(logits, bias_t):
    lt = logits.T[:N_EXPERTS]
    ex = jnp.exp(lt - jnp.max(lt, axis=0, keepdims=True))
    probs = ex / jnp.sum(ex, axis=0, keepdims=True)
    sel = probs + bias_t
    p = [probs[e:e + 1] for e in range(N_EXPERTS)]
    s = [sel[e:e + 1] for e in range(N_EXPERTS)]
    n = EXPERTS_PER_GROUP
    scores = []
    for g in range(N_EXPERT_GROUPS):
        pair = [s[g * n + a] + s[g * n + b] for a in range(n) for b in range(a + 1, n)]
        scores.append(functools.reduce(jnp.maximum, pair))
    best = jnp.zeros_like(scores[0], dtype=jnp.int32)
    best_score = scores[0]
    for g in range(1, N_EXPERT_GROUPS):
        better = scores[g] > best_score
        best = jnp.where(better, g, best)
        best_score = jnp.where(better, scores[g], best_score)
    cls = jnp.zeros_like(best_score)
    w_a = jnp.zeros_like(best_score)
    w_b = jnp.zeros_like(best_score)
    for g in range(N_EXPERT_GROUPS):
        in_g = best == g
        chosen = []
        for a in range(n):
            rank = jnp.zeros_like(best)
            for b in range(n):
                if b == a:
                    continue
                ahead = (s[g * n + b] > s[g * n + a]) if b > a else (s[g * n + b] >= s[g * n + a])
                rank = rank + ahead.astype(jnp.int32)
            chosen.append(jnp.logical_and(in_g, rank < 2))
        for k in range(N_PAIRS):
            pa, pb = p[g * n + PAIR_A[k]], p[g * n + PAIR_B[k]]
            hit = jnp.logical_and(chosen[PAIR_A[k]], chosen[PAIR_B[k]])
            wsum = pa + pb
            cls = jnp.where(hit, float(g * N_PAIRS + k), cls)
            w_a = jnp.where(hit, pa / wsum, w_a)
            w_b = jnp.where(hit, pb / wsum, w_b)
    return jnp.concatenate([cls, w_a, w_b, jnp.zeros((ROUTE_ROWS - 3, lt.shape[1]), F32)], axis=0)


def _merge_kernel(xc_ref, xl_ref, mixc_ref, mixl_ref, mod_ref, win_ref, wpa_ref, wpf_ref, wpc_ref, wo_ref,
                  l1g_ref, l1b_ref, wrc_ref, bias_ref, x1_ref, ha_ref, rt_ref, *, layer):
    m = mod_ref[0, 0]
    in_ctx = pl.program_id(0) < N_CTX_TILES
    branch_w = ((0, ATT_W, wpa_ref), (ATT_W, FOURIER_W, wpf_ref), (ATT_W + FOURIER_W, CHUNK_W, wpc_ref))
    bounds = np.cumsum((0,) + MERGE_SUBS)
    sub_rows = [slice(int(a), int(b)) for a, b in zip(bounds[:-1], bounds[1:])]
    residual = []
    for rows in sub_rows:
        x = jnp.where(in_ctx, xc_ref[rows, :], xl_ref[rows, :])
        mixed = jnp.where(in_ctx, mixc_ref[rows, :], mixl_ref[rows, :])
        h = (_ln(x) * (1.0 + m[1:2]) + m[0:1]).astype(BF16)
        merged = None
        for b, (c0, width, w_ref) in enumerate(branch_w):
            gate = _sigmoid(_dot(h, win_ref[:, MIX_W + b * D_MODEL:MIX_W + (b + 1) * D_MODEL]))
            term = gate * _dot(mixed[:, c0:c0 + width], w_ref[...])
            merged = term if merged is None else merged + term
        mix = _dot(merged.astype(BF16), wo_ref[...])
        residual.append(ALPHA * x + m[2:3] * mix)
    for rows, pre in zip(sub_rows, residual):
        x1 = _ln(pre) * _layer_row(l1g_ref, layer) + _layer_row(l1b_ref, layer)
        x1_ref[rows, :] = x1
        h2 = _ln(x1) * (1.0 + m[4:5]) + m[3:4]
        h_hi, h_lo = _split_bf16(h2)
        both = _dot(h_hi, wrc_ref[...])
        logits = both[:, :ROUTER_PAD] + _dot(h_lo, wrc_ref[:, :ROUTER_PAD]) + both[:, ROUTER_PAD:]
        rt = _route(logits, bias_ref[:, rows])
        rt_ref[:, rows] = rt
        ha_ref[rows, :D_MODEL] = h2
        ha_ref[rows, D_MODEL:] = jnp.concatenate(
            [rt, jnp.zeros((LANES - ROUTE_ROWS, rows.stop - rows.start), F32)], axis=0).T


def _merge_call(layer, x, mixed_ctx, mixed_lat, mod, w_in_b, wpa, wpf, wpc, wo, l1g, l1b, wr_cat, bias_t):
    const2 = lambda i: (0, 0)
    per_layer = lambda i: (layer, 0, 0)
    tok = lambda i: (i, 0)
    return pl.pallas_call(
        functools.partial(_merge_kernel, layer=layer),
        out_shape=(jax.ShapeDtypeStruct((T_ALL, D_MODEL), F32),
                   jax.ShapeDtypeStruct((T_ALL, HA_W), F32),
                   jax.ShapeDtypeStruct((ROUTE_ROWS, T_ALL), F32)),
        grid=(T_ALL // TM,),
        in_specs=_group_specs() + _group_specs() + [
            pl.BlockSpec((1, 1, N_MOD, D_MODEL), lambda i: (layer, _cond_row(i, TM), 0, 0)),
            pl.BlockSpec((None, D_MODEL, IN_W), per_layer, pipeline_mode=pl.Buffered(1)),
            pl.BlockSpec((None, ATT_W, D_MODEL), per_layer, pipeline_mode=pl.Buffered(1)),
            pl.BlockSpec((None, FOURIER_W, D_MODEL), per_layer, pipeline_mode=pl.Buffered(1)),
            pl.BlockSpec((None, CHUNK_W, D_MODEL), per_layer, pipeline_mode=pl.Buffered(1)),
            pl.BlockSpec((None, D_MODEL, D_MODEL), per_layer, pipeline_mode=pl.Buffered(1)),
            pl.BlockSpec((DEPTH, D_MODEL), const2),
            pl.BlockSpec((DEPTH, D_MODEL), const2),
            pl.BlockSpec((D_MODEL, 2 * ROUTER_PAD), const2),
            pl.BlockSpec((N_EXPERTS, TM), const2),
        ],
        out_specs=(pl.BlockSpec((TM, D_MODEL), tok),
                   pl.BlockSpec((TM, HA_W), tok),
                   pl.BlockSpec((ROUTE_ROWS, TM), lambda i: (0, i))),
        compiler_params=_cparams("arbitrary"),
        name="merge_route",
    )(*x, mixed_ctx, mixed_lat, mod, w_in_b, wpa, wpf, wpc, wo, l1g, l1b, wr_cat, bias_t)


def _sort_kernel(rt_ref, tri_ref, low_ref, pos_ref, tile_ref):
    crow = lax.broadcasted_iota(jnp.int32, (CLASS_PAD, SORT_BLK), 0)
    tri = tri_ref[...]
    n_blk = T_ALL // SORT_BLK
    carry = jnp.zeros((CLASS_PAD, 1), F32)
    ranks = []
    for b in range(n_blk):
        hot = rt_ref[0:1, b * SORT_BLK:(b + 1) * SORT_BLK].astype(jnp.int32) == crow
        hot_f = jnp.where(hot, 1.0, 0.0)
        before = _dot(hot_f.astype(BF16), tri) + carry
        ranks.append(jnp.sum(jnp.where(hot, before, 0.0), axis=0, keepdims=True))
        carry = carry + jnp.sum(hot_f, axis=1, keepdims=True)
    padded = jnp.floor((carry + (TM_MOE - 1.0)) * (1.0 / TM_MOE)) * TM_MOE
    padded = jnp.broadcast_to(padded, (CLASS_PAD, LANES))
    offs = _dot(low_ref[...], padded.astype(BF16))
    for b in range(n_blk):
        hot = rt_ref[0:1, b * SORT_BLK:(b + 1) * SORT_BLK].astype(jnp.int32) == crow
        base = jnp.sum(jnp.where(hot, offs[:, 0:1], 0.0), axis=0, keepdims=True)
        pos_ref[:, b * SORT_BLK:(b + 1) * SORT_BLK] = (base + ranks[b]).astype(jnp.int32)

    start = lax.broadcasted_iota(jnp.int32, (1, LANES), 1).astype(F32) * TM_MOE
    is_class = lax.broadcasted_iota(jnp.int32, (CLASS_PAD, LANES), 0) < N_CLASSES
    ends = jnp.where(is_class, offs + padded, 0.0)
    total = jnp.max(ends, axis=0, keepdims=True)
    valid = start < total
    tcls = jnp.sum(jnp.where(jnp.logical_and(is_class, ends <= start), 1.0, 0.0), axis=0, keepdims=True)
    last = jnp.max(jnp.where(valid, tcls, 0.0), axis=1, keepdims=True)
    tcls = jnp.where(valid, tcls, last)
    grp = functools.reduce(jnp.add, [jnp.where(tcls >= g * N_PAIRS, 1.0, 0.0) for g in range(1, N_EXPERT_GROUPS)])
    pair = tcls - grp * N_PAIRS
    slot_a = functools.reduce(jnp.add, [jnp.where(pair == k, float(PAIR_A[k]), 0.0) for k in range(N_PAIRS)])
    slot_b = functools.reduce(jnp.add, [jnp.where(pair == k, float(PAIR_B[k]), 0.0) for k in range(N_PAIRS)])
    crow_t = lax.broadcasted_iota(jnp.int32, (CLASS_PAD, LANES), 0).astype(F32)
    real_end = jnp.sum(jnp.where(crow_t == tcls, offs + carry, 0.0), axis=0, keepdims=True)
    n_rows = jnp.where(valid, jnp.clip(real_end - start, 0.0, float(TM_MOE)), 0.0)
    rows = [grp * EXPERTS_PER_GROUP + slot_a, grp * EXPERTS_PER_GROUP + slot_b, n_rows]
    tile_ref[...] = jnp.concatenate(rows + [jnp.zeros((8 - len(rows), LANES), F32)], axis=0).astype(jnp.int32)


def _sort_call(rt, tri, low):
    return pl.pallas_call(
        _sort_kernel,
        out_shape=(jax.ShapeDtypeStruct((1, T_ALL), jnp.int32),
                   jax.ShapeDtypeStruct((8, LANES), jnp.int32)),
        name="route_sort",
    )(rt, tri, low)


def _moe_kernel(tile_ref, pos_ref, ha_hbm, wg_hbm, wu_hbm, wd_hbm, out_ref, hbuf, ybuf, src_ref, ssem,
                stg_g, stg_u, stg_d, act_g, act_u, act_d, wsem, ha_ref, hsem, staged_ref, *, layer):
    j = pl.program_id(0)
    slot = j % 2

    def n_rows(tile):
        inside = jnp.logical_and(tile >= 0, tile < N_TILES)
        return jnp.where(inside, tile_ref[TILE_ROWS, jnp.clip(tile, 0, N_TILES - 1)], 0)

    def token_of(tile, n, r, tail):
        return src_ref[tile * TM_MOE + (jnp.minimum(r, n - 1) if tail else r)]

    def scatter_copy(tile, n, buf_slot, r, tail):
        tok = token_of(tile, n, r, tail)
        dst = jnp.where(r < n, tok, T_ALL + buf_slot * TM_MOE + r) if tail else tok
        return pltpu.make_async_copy(ybuf.at[buf_slot, r], out_ref.at[dst], ssem.at[buf_slot])

    def for_groups(n, fn):
        full = lax.shift_right_logical(n, GROUP_SHIFT)

        def body(g, c):
            for k in range(COPY_GROUP):
                fn(g, k, False)
            return c
        lax.fori_loop(0, full, body, 0)

        @pl.when(jnp.bitwise_and(n, COPY_GROUP - 1) != 0)
        def _():
            for k in range(COPY_GROUP):
                fn(full, k, True)

    def retire(tile, buf_slot):
        n = n_rows(tile)
        one = scatter_copy(tile, n, buf_slot, 0, False)

        def body(g, c):
            for _ in range(COPY_GROUP):
                one.wait()
            return c
        lax.fori_loop(0, lax.shift_right_logical(n + COPY_GROUP - 1, GROUP_SHIFT), body, 0)

    def expert_of(x, tile):
        return tile_ref[TILE_SLOT_A + x, tile]

    def weight_copies(x, e):
        return [pltpu.make_async_copy(src.at[layer, e], stg.at[x], wsem.at[x])
                for src, stg in ((wg_hbm, stg_g), (wu_hbm, stg_u), (wd_hbm, stg_d))]

    @pl.when(j == 0)
    def _():
        load_inputs = pltpu.make_async_copy(ha_hbm, ha_ref, hsem)
        load_inputs.start()
        for x in range(2):
            for c in weight_copies(x, expert_of(x, 0)):
                c.start()
            staged_ref[x] = expert_of(x, 0)

        def place(t, c):
            src_ref[pos_ref[t]] = t
            return c
        lax.fori_loop(0, T_ALL, place, 0, unroll=8)
        load_inputs.wait()
        hbuf[...] = jnp.zeros(hbuf.shape, F32)
        ybuf[...] = jnp.zeros(ybuf.shape, F32)
        for s in range(2):
            park = pltpu.make_async_copy(ybuf.at[s], out_ref.at[pl.ds(T_ALL + s * TM_MOE, TM_MOE)], ssem.at[s])
            park.start()
            park.wait()

    @pl.when(n_rows(j - 2) > 0)
    def _():
        retire(j - 2, slot)

    def run_experts(rows):
        rec = hbuf[:rows // SUBLANES].reshape(rows, HA_W)
        h = rec[:, :D_MODEL].astype(BF16)

        def expert(x):
            a = _dot(h, act_g[x])
            u = _dot(h, act_u[x])
            w = rec[:, D_MODEL + 1 + x:D_MODEL + 2 + x]
            return _dot(((a * _sigmoid(a)) * u * w).astype(BF16), act_d[x])

        ybuf[slot, :rows] = (expert(0) + expert(1)).reshape(rows, D_CHUNKS, LANES)

    @pl.when(n_rows(j) > 0)
    def _():
        for x in range(2):
            prev = jnp.maximum(j - 1, 0)

            @pl.when(jnp.logical_or(j == 0, expert_of(x, j) != expert_of(x, prev)))
            def _():
                for c in weight_copies(x, expert_of(x, j)):
                    c.wait()
                act_g[x] = stg_g[x].astype(BF16)
                act_u[x] = stg_u[x].astype(BF16)
                act_d[x] = stg_d[x].astype(BF16)

        for x in range(2):
            cur = expert_of(x, j)

            def expert_or_cur(tile):
                return jnp.where(n_rows(tile) > 0, expert_of(x, jnp.minimum(tile, N_TILES - 1)), cur)
            e1, e2 = expert_or_cur(j + 1), expert_or_cur(j + 2)
            target = jnp.where(e1 != cur, e1, e2)

            @pl.when(jnp.logical_and(target != cur, target != staged_ref[x]))
            def _():
                for c in weight_copies(x, target):
                    c.start()
                staged_ref[x] = target

        n = n_rows(j)

        def gather_row(g, k, tail):
            tok = token_of(j, n, g * COPY_GROUP + k, tail)
            grp, sub = lax.shift_right_logical(tok, GROUP_SHIFT), jnp.bitwise_and(tok, SUBLANES - 1)
            hbuf[g, pl.ds(k, 1), :] = ha_ref[grp, pl.ds(sub, 1), :]
        for_groups(n, gather_row)

        @pl.when(n > SMALL_TILE)
        def _():
            run_experts(TM_MOE)

        @pl.when(n <= SMALL_TILE)
        def _():
            run_experts(SMALL_TILE)

        for_groups(n, lambda g, k, tail: scatter_copy(j, n, slot, g * COPY_GROUP + k, tail).start(priority=k % 2))

    @pl.when(j == N_TILES - 1)
    def _():
        @pl.when(n_rows(j - 1) > 0)
        def _():
            retire(j - 1, 1 - slot)

        @pl.when(n_rows(j) > 0)
        def _():
            retire(j, slot)


def _moe_call(layer, tiles, pos, ha, wge, wue, wde):
    up = (2, D_MODEL, D_EXPERT)
    down = (2, D_EXPERT, D_MODEL)
    grid_spec = pltpu.PrefetchScalarGridSpec(
        num_scalar_prefetch=2,
        grid=(N_TILES,),
        in_specs=[pl.BlockSpec(memory_space=pl.ANY)] * 4,
        out_specs=pl.BlockSpec(memory_space=pl.ANY),
        scratch_shapes=[
            pltpu.VMEM((TM_MOE // SUBLANES, SUBLANES, HA_W), F32),
            pltpu.VMEM((2, TM_MOE, D_CHUNKS, LANES), F32),
            pltpu.SMEM((R_ROWS,), jnp.int32),
            pltpu.SemaphoreType.DMA((2,)),
            pltpu.VMEM(up, F32), pltpu.VMEM(up, F32), pltpu.VMEM(down, F32),
            pltpu.VMEM(up, BF16), pltpu.VMEM(up, BF16), pltpu.VMEM(down, BF16),
            pltpu.SemaphoreType.DMA((2,)),
            pltpu.VMEM((T_ALL // SUBLANES, SUBLANES, HA_W), F32),
            pltpu.SemaphoreType.DMA(()),
            pltpu.SMEM((2,), jnp.int32),
        ],
    )
    return pl.pallas_call(
        functools.partial(_moe_kernel, layer=layer),
        out_shape=jax.ShapeDtypeStruct((T_ALL + N_PARK, D_CHUNKS, LANES), F32),
        grid_spec=grid_spec,
        compiler_params=pltpu.CompilerParams(dimension_semantics=("arbitrary",), vmem_limit_bytes=V7X_VMEM_LIMIT,
                                             has_side_effects=True),
        name="moe_pairs",
    )(tiles, pos,
      ha.reshape(T_ALL // SUBLANES, SUBLANES, HA_W), wge, wue, wde)


def _post_kernel(x1_ref, moe_ref, mod_ref, l2g_ref, l2b_ref, oc_ref, ol_ref, *, layer):
    m = mod_ref[0, 0]
    moe = moe_ref[...].reshape(TM, D_MODEL)
    y = _ln(ALPHA * x1_ref[...] + m[5:6] * moe) * _layer_row(l2g_ref, layer) + _layer_row(l2b_ref, layer)
    i = pl.program_id(0)

    @pl.when(i < N_CTX_TILES)
    def _():
        oc_ref[...] = y

    @pl.when(i >= N_CTX_TILES)
    def _():
        ol_ref[...] = y


def _post_call(layer, x1, moe, mod, l2g, l2b):
    tok = lambda i: (i, 0)
    return pl.pallas_call(
        functools.partial(_post_kernel, layer=layer),
        out_shape=(jax.ShapeDtypeStruct((T_CTX, D_MODEL), F32), jax.ShapeDtypeStruct((T_LAT, D_MODEL), F32)),
        grid=(T_ALL // TM,),
        in_specs=[
            pl.BlockSpec((TM, D_MODEL), tok),
            pl.BlockSpec((TM, D_CHUNKS, LANES), lambda i: (i, 0, 0)),
            pl.BlockSpec((1, 1, N_MOD, D_MODEL), lambda i: (layer, _cond_row(i, TM), 0, 0)),
            pl.BlockSpec((DEPTH, D_MODEL), lambda i: (0, 0)),
            pl.BlockSpec((DEPTH, D_MODEL), lambda i: (0, 0)),
        ],
        out_specs=tuple(_group_specs()),
        compiler_params=_cparams("arbitrary"),
        name="post_moe",
    )(x1, moe, mod, l2g, l2b)


def _rope_tables():
    pos = np.arange(DEC_SEQ)
    quarter = HEAD_DIM // 4
    inv = ROPE_THETA ** (-np.arange(quarter, dtype=np.float64) / quarter)
    ang_r = (pos // GRID_W)[:, None] * inv[None, :]
    ang_c = (pos % GRID_W)[:, None] * inv[None, :]
    cos = np.concatenate([np.cos(ang_r)] * 2 + [np.cos(ang_c)] * 2, axis=-1)
    sin = np.concatenate([-np.sin(ang_r), np.sin(ang_r), -np.sin(ang_c), np.sin(ang_c)], axis=-1)
    return (jnp.asarray(np.tile(cos, (1, N_Q_HEADS)).astype(np.float32)),
            jnp.asarray(np.tile(sin, (1, N_Q_HEADS)).astype(np.float32)))


def _dft_mats(n, scale):
    j = np.arange(n)
    ang = ((j[:, None] * j[None, :]) % n) * (2 * np.pi / n)
    return np.cos(ang) * scale, np.sin(ang) * scale


def _block_diag(m, reps):
    return np.kron(np.eye(reps), m)


def _const_bf16(a):
    return jnp.asarray(np.asarray(a, np.float32)).astype(BF16)


def _const_01(a):
    return jnp.asarray(np.asarray(a, np.float32).astype(BF16))


def kernel(x_prompt, x_sample, cache_k, cache_v, c, c_ctx, w_in, q_norm_g, k_norm_g, w_proj_att,
           w_proj_fourier, w_proj_chunk, w_out, chunk_ln_g, chunk_ln_b, chunk_ws, chunk_bs, w_ada, b_ada,
           ln1_g, ln1_b, ln2_g, ln2_b, w_router, router_bias, w_gate_e, w_up_e, w_down_e):
    x = (x_prompt.reshape(T_CTX, D_MODEL), x_sample.reshape(T_LAT, D_MODEL))
    cond = jnp.concatenate([c_ctx[None, :], c, jnp.zeros((N_COND - 1 - DEC_BATCH, D_MODEL), F32)], axis=0)
    mod = _mod_call(cond, w_ada, b_ada).reshape(DEPTH, N_COND, N_MOD, D_MODEL)

    cos_t, sin_t = _rope_tables()
    bd_heads = _const_01(_block_diag(np.ones((HEAD_DIM, HEAD_DIM)), N_Q_HEADS))
    c64, s64 = _dft_mats(FOURIER_GROUP_W, 1.0)
    bdc = _const_bf16(_block_diag(c64, FOURIER_W // FOURIER_GROUP_W))
    bds = _const_bf16(_block_diag(s64, FOURIER_W // FOURIER_GROUP_W))
    dft = {}
    for seq in (SEQ, DEC_SEQ):
        cs, ss = _dft_mats(seq, 1.0 / math.sqrt(seq * FOURIER_GROUP_W))
        dft[seq] = _const_bf16(np.concatenate([cs, -ss], axis=1))
    ctx_k = cache_k.reshape(DEC_BATCH, DEPTH, PAST_LEN, KV_W)
    ctx_v = cache_v.reshape(DEC_BATCH, DEPTH, PAST_LEN, KV_W)
    wr = jnp.pad(w_router, ((0, 0), (0, ROUTER_PAD - N_EXPERTS)))
    wr_hi = wr.astype(BF16)
    wr_cat = jnp.concatenate([wr_hi, (wr - wr_hi.astype(F32)).astype(BF16)], axis=1)
    bias_t = jnp.broadcast_to(router_bias[:, None], (N_EXPERTS, TM))
    tri = _const_01(np.triu(np.ones((SORT_BLK, SORT_BLK)), 1))
    low = _const_01(np.tril(np.ones((CLASS_PAD, CLASS_PAD)), -1))

    w_in_b = w_in.astype(BF16)
    wpa, wpf, wpc, wo = (w.astype(BF16) for w in (w_proj_att, w_proj_fourier, w_proj_chunk, w_out))
    wge, wue, wde = w_gate_e, w_up_e, w_down_e
    qg = jnp.tile(q_norm_g, (1, N_Q_HEADS))
    kg = jnp.tile(k_norm_g, (1, N_KV_HEADS))
    ws = chunk_ws.astype(BF16)
    bsm = jnp.repeat(jnp.swapaxes(chunk_bs, 1, 2), CHUNK_GROUP_W, axis=2)

    new_k = jnp.zeros((BATCH, DEPTH, SEQ, KV_W), F32)
    new_v = jnp.zeros((BATCH, DEPTH, SEQ, KV_W), F32)
    proj_args = (mod, w_in_b, qg, kg, cos_t, sin_t, bd_heads)
    for l in range(DEPTH):
        if l == 0:
            mix, new_k, new_v = _proj_call(l, x, *proj_args, new_k, new_v)
        else:
            mix, new_k, new_v, *x = _proj_call(l, None, *proj_args, new_k, new_v,
                                               prev=(x1, moe, ln2_g, ln2_b))
        mixer_args = (bdc, bds, ws, bsm, chunk_ln_g, chunk_ln_b)
        mixed_ctx = _mixer_call(l, mix, None, None, dft[SEQ], *mixer_args, latent=False)
        mixed_lat = _mixer_call(l, mix, ctx_k, ctx_v, dft[DEC_SEQ], *mixer_args, latent=True)
        x1, ha, rt = _merge_call(l, x, mixed_ctx, mixed_lat, mod, w_in_b, wpa, wpf, wpc, wo,
                                 ln1_g, ln1_b, wr_cat, bias_t)
        pos2d, tiles = _sort_call(rt, tri, low)
        moe = _moe_call(l, tiles, pos2d[0], ha, wge, wue, wde)
    x = _post_call(DEPTH - 1, x1, moe, mod, ln2_g, ln2_b)

    y_prompt = x[0].reshape(BATCH, SEQ, D_MODEL)
    y_sample = x[1].reshape(DEC_BATCH, DEC_SEQ, D_MODEL)
    cache_shape = (BATCH, DEPTH, SEQ, N_KV_HEADS, HEAD_DIM)
    return (y_prompt, y_sample, new_k.reshape(cache_shape), new_v.reshape(cache_shape))
```
